```python
import jax
import jax.numpy as jnp
from jax import lax
import numpy as np

D_MODEL = 1024
BATCH = 8
SEQ = 2048
DEPTH = 2

GRID_W = 64
CTX_LEN = 256
N_EVEN = (DEPTH + 1) // 2
N_ODD = DEPTH // 2
N_MOD = 6
EPS = 1e-6
ROPE_THETA = 10000.0
NEG_INF = -1e30

A_HEADS = 8
A_KV_HEADS = 2
A_HEAD_DIM = 64
WINDOW = 128
B_HEADS = 8
B_Q_RANK = 256
B_KV_RANK = 256
B_NOPE = 64
B_ROPE = 32
B_V_DIM = 64
Q_BLOCK = 128
C_HEADS = 8
C_HEAD = 64
C_DIM = C_HEADS * C_HEAD
C_DECAY_LORA = 64
C_AAA_LORA = 64
C_GATE_LORA = 128
C_GN_EPS = 64e-5
D_HEADS = 4
D_HEAD_DIM = 128
D_DIM = D_HEADS * D_HEAD_DIM
D_CONV = 5
D_CHUNK = 64
D_FF = 2816
FFN_CONV = 3

MIX_WIDTH = A_HEADS * A_HEAD_DIM + B_HEADS * B_V_DIM
AB_SIZES = (A_HEADS * A_HEAD_DIM, A_KV_HEADS * A_HEAD_DIM, A_KV_HEADS * A_HEAD_DIM, B_Q_RANK, B_KV_RANK, B_ROPE)
IN_AB = sum(AB_SIZES)
C_SIZES = (C_DIM, C_DIM, C_DIM, C_DECAY_LORA, C_DECAY_LORA, C_AAA_LORA, C_AAA_LORA, C_GATE_LORA)
IN_C = sum(C_SIZES)
D_SIZES = (3 * D_DIM, D_DIM, D_HEADS, D_HEADS, D_HEADS, D_HEADS)
IN_CD = IN_C + sum(D_SIZES)

kernel_name = 'hybrid_dit_swa_mla_rwkv7_gdn'

F32 = jnp.float32


def split_cols(p, sizes):
    return jnp.split(p, [int(s) for s in np.cumsum(sizes)[:-1]], axis=-1)


def rms_norm(x, gain=None, eps=EPS):
    xf = x.astype(F32)
    y = xf * lax.rsqrt(jnp.mean(xf * xf, axis=-1, keepdims=True) + eps)
    if gain is not None:
        y = y * gain.astype(F32)
    return y.astype(x.dtype)


def l2_normalize(x, eps=1e-6):
    xf = x.astype(F32)
    return (xf * lax.rsqrt(jnp.sum(xf * xf, axis=-1, keepdims=True) + eps)).astype(x.dtype)


def modulate(x, shift, scale):
    return rms_norm(x) * (1.0 + scale) + shift


def rope_1d(x, pos):
    half = x.shape[-1] // 2
    inv = jnp.power(ROPE_THETA, -jnp.arange(half, dtype=F32) / half)
    ang = pos.astype(F32)[:, None] * inv[None, :]
    cos = jnp.cos(ang)[None, :, None, :]
    sin = jnp.sin(ang)[None, :, None, :]
    x1, x2 = x[..., :half], x[..., half:]
    return jnp.concatenate([x1 * cos - x2 * sin, x1 * sin + x2 * cos], axis=-1).astype(x.dtype)


def rope_2d(x, row, col):
    h = x.shape[-1] // 2
    return jnp.concatenate([rope_1d(x[..., :h], row), rope_1d(x[..., h:], col)], axis=-1)


def dwconv_centred(x, w):
    k, ch = w.shape
    p = k // 2
    return lax.conv_general_dilated(x, w[:, None, :].astype(x.dtype), (1,), [(p, p)],
                                    dimension_numbers=('NWC', 'WIO', 'NWC'), feature_group_count=ch)


def flip_time(t, rev):
    return jnp.flip(t, axis=1) if rev else t


def window_attention(q, k, v, k_ctx, v_ctx, sink):
    b, s, h, d = q.shape
    kv = k.shape[2]
    g = h // kv
    nb = s // WINDOW
    n_ctx = k_ctx.shape[1]
    qb = q.reshape(b, nb, WINDOW, kv, g, d)

    def band(t):
        tb = t.reshape(b, nb, WINDOW, kv, d)
        zero = jnp.zeros_like(tb[:, :1])
        prev = jnp.concatenate([zero, tb[:, :-1]], axis=1)
        nxt = jnp.concatenate([tb[:, 1:], zero], axis=1)
        return jnp.concatenate([prev, tb, nxt], axis=2)

    kb, vb = band(k), band(v)
    scale = d ** -0.5
    s_loc = jnp.einsum('bnqhgd,bnjhd->bnhgqj', qb, kb).astype(F32) * scale
    blk = jnp.arange(nb)[:, None, None]
    qpos = blk * WINDOW + jnp.arange(WINDOW)[None, :, None]
    kpos = (blk - 1) * WINDOW + jnp.arange(3 * WINDOW)[None, None, :]
    valid = (jnp.abs(qpos - kpos) <= WINDOW) & (kpos >= 0) & (kpos < s)
    s_loc = jnp.where(valid[None, :, None, None], s_loc, NEG_INF)
    s_ctx = jnp.einsum('bnqhgd,bjhd->bnhgqj', qb, k_ctx).astype(F32) * scale
    s_sink = jnp.broadcast_to(sink.astype(F32).reshape(1, 1, kv, g, 1, 1), s_loc.shape[:-1] + (1,))
    p = jax.nn.softmax(jnp.concatenate([s_loc, s_ctx, s_sink], axis=-1), axis=-1)
    p_loc = p[..., :3 * WINDOW].astype(v.dtype)
    p_ctx = p[..., 3 * WINDOW:3 * WINDOW + n_ctx].astype(v.dtype)
    o = jnp.einsum('bnhgqj,bnjhd->bnqhgd', p_loc, vb) + jnp.einsum('bnhgqj,bjhd->bnqhgd', p_ctx, v_ctx)
    return o.reshape(b, s, h * d)


def context_gqa(q, k, v, sink):
    b, n, h, d = q.shape
    kv = k.shape[2]
    g = h // kv
    qg = q.reshape(b, n, kv, g, d)
    s = jnp.einsum('bqhgd,bjhd->bhgqj', qg, k).astype(F32) * d ** -0.5
    s_sink = jnp.broadcast_to(sink.astype(F32).reshape(1, kv, g, 1, 1), s.shape[:-1] + (1,))
    p = jax.nn.softmax(jnp.concatenate([s, s_sink], axis=-1), axis=-1)[..., :n].astype(v.dtype)
    return jnp.einsum('bhgqj,bjhd->bqhgd', p, v).reshape(b, n, h * d)


def mla_attend(qn, qr, kn, kr, v):
    s = (jnp.einsum('bqhd,bkhd->bhqk', qn, kn) + jnp.einsum('bqhd,bkd->bhqk', qr, kr)).astype(F32)
    p = jax.nn.softmax(s * (B_NOPE + B_ROPE) ** -0.5, axis=-1).astype(v.dtype)
    return jnp.einsum('bhqk,bkhd->bqhd', p, v)


def mla_latent(qn, qr, kn_all, kr_all, v_all):
    b, s = qn.shape[:2]
    nq = s // Q_BLOCK

    def blocks(t):
        return jnp.moveaxis(t.reshape(b, nq, Q_BLOCK, *t.shape[2:]), 1, 0)

    o = lax.map(lambda qs: mla_attend(qs[0], qs[1], kn_all, kr_all, v_all), (blocks(qn), blocks(qr)))
    return jnp.moveaxis(o, 0, 1).reshape(b, s, -1)


def ab_prep(p, a_q_norm, a_k_norm, b_cq_norm, b_ckv_norm, b_w_uq, b_w_uk, b_w_uv,
            b_qn_norm, b_qr_norm, b_kn_norm, b_kr_norm):
    b, t = p.shape[:2]
    qa, ka, va, cq, ckv, kr = split_cols(p, AB_SIZES)
    qa = rms_norm(qa.reshape(b, t, A_HEADS, A_HEAD_DIM), a_q_norm)
    ka = rms_norm(ka.reshape(b, t, A_KV_HEADS, A_HEAD_DIM), a_k_norm)
    va = va.reshape(b, t, A_KV_HEADS, A_HEAD_DIM)
    qb = (rms_norm(cq, b_cq_norm) @ b_w_uq).reshape(b, t, B_HEADS, B_NOPE + B_ROPE)
    qn = rms_norm(qb[..., :B_NOPE], b_qn_norm)
    qr = rms_norm(qb[..., B_NOPE:], b_qr_norm)
    ckv = rms_norm(ckv, b_ckv_norm)
    kn = rms_norm((ckv @ b_w_uk).reshape(b, t, B_HEADS, B_NOPE), b_kn_norm)
    vb = (ckv @ b_w_uv).reshape(b, t, B_HEADS, B_V_DIM)
    kr = rms_norm(kr, b_kr_norm)
    return qa, ka, va, qn, qr, kn, kr, vb


def mixer_ab(p_lat, p_ctx, row, col, a_q_norm, a_k_norm, a_sink, b_cq_norm, b_ckv_norm, b_w_uq, b_w_uk,
             b_w_uv, b_qn_norm, b_qr_norm, b_kn_norm, b_kr_norm, ctx_out):
    prm = (a_q_norm, a_k_norm, b_cq_norm, b_ckv_norm, b_w_uq, b_w_uk, b_w_uv,
           b_qn_norm, b_qr_norm, b_kn_norm, b_kr_norm)
    qa_c, ka_c, va_c, qn_c, qr_c, kn_c, kr_c, vb_c = ab_prep(p_ctx, *prm)
    qa_l, ka_l, va_l, qn_l, qr_l, kn_l, kr_l, vb_l = ab_prep(p_lat, *prm)
    qa_l = rope_2d(qa_l, row, col)
    ka_l = rope_2d(ka_l, row, col)
    qr_l = rope_2d(qr_l, row, col)
    kr_l = rope_2d(kr_l[:, :, None, :], row, col)[:, :, 0, :]
    o_a = window_attention(qa_l, ka_l, va_l, ka_c, va_c, a_sink)
    o_b = mla_latent(qn_l, qr_l, jnp.concatenate([kn_c, kn_l], axis=1),
                     jnp.concatenate([kr_c, kr_l], axis=1), jnp.concatenate([vb_c, vb_l], axis=1))
    y_lat = jnp.concatenate([o_a, o_b], axis=-1)
    y_ctx = None
    if ctx_out:
        b, n = p_ctx.shape[:2]
        y_ctx = jnp.concatenate([context_gqa(qa_c, ka_c, va_c, a_sink),
                                 mla_attend(qn_c, qr_c, kn_c, kr_c, vb_c).reshape(b, n, -1)], axis=-1)
    return y_lat, y_ctx


def token_shift_centred(p, mu_prev, mu_next):
    prev = jnp.pad(p, ((0, 0), (1, 0), (0, 0)))[:, :-1]
    nxt = jnp.pad(p, ((0, 0), (0, 1), (0, 0)))[:, 1:]
    return p + mu_prev * (prev - p) + mu_next * (nxt - p)


def rwkv_prep(pc, c_mu_prev, c_mu_next, c_w0, c_w2, c_a0, c_a2, c_g2, c_k_k, c_k_a):
    b, t = pc.shape[:2]

    def heads(z):
        return z.reshape(b, t, C_HEADS, C_HEAD)

    xs = token_shift_centred(pc, c_mu_prev, c_mu_next)
    r, k, v, wl_f, wl_b, al_f, al_b, gl = split_cols(xs, C_SIZES)

    def decay(wl, w0, w2):
        w = -jax.nn.softplus(-(w0 + jnp.tanh(wl) @ w2)) - 0.5
        return heads(jnp.exp(-jnp.exp(w.astype(F32))))

    decays = (decay(wl_f, c_w0[0], c_w2[0]), decay(wl_b, c_w0[1], c_w2[1]))
    iclr = (jax.nn.sigmoid(c_a0[0] + al_f @ c_a2[0]), jax.nn.sigmoid(c_a0[1] + al_b @ c_a2[1]))
    g = jax.nn.sigmoid(gl) @ c_g2
    kk = l2_normalize(heads(k * c_k_k))
    ks = tuple(heads(k * (1.0 + (a - 1.0) * c_k_a)) for a in iclr)
    return heads(r), heads(v), kk, g, decays, tuple(heads(a) for a in iclr), ks


def rwkv_scan(r, w, k, v, kk, a, s0, reverse):
    def step(state, inp):
        rt, wt, kt, vt, kkt, at = inp
        sa = jnp.einsum('bhvk,bhk->bhv', state, -kkt)
        state = state * wt[:, :, None, :] + sa[..., None] * (kkt * at)[:, :, None, :] + vt[..., None] * kt[:, :, None, :]
        return state, jnp.einsum('bhvk,bhk->bhv', state, rt)

    xs = tuple(jnp.moveaxis(z.astype(F32), 1, 0) for z in (r, w, k, v, kk, a))
    state, y = lax.scan(step, s0, xs, reverse=reverse)
    return jnp.moveaxis(y, 0, 1), state


def head_group_norm(y, w, b_):
    mean = jnp.mean(y, axis=-1, keepdims=True)
    var = jnp.mean(jnp.square(y - mean), axis=-1, keepdims=True)
    yn = (y - mean) * lax.rsqrt(var + C_GN_EPS)
    return yn * w.astype(F32).reshape(C_HEADS, C_HEAD) + b_.astype(F32).reshape(C_HEADS, C_HEAD)


def rwkv_output(y, r, ks, v, g, c_r_k, c_ln_w, c_ln_b):
    b, t = y.shape[:2]
    yn = head_group_norm(y, c_ln_w, c_ln_b)
    bonus = sum(jnp.sum(r * kd * c_r_k, axis=-1, keepdims=True) * v for kd in ks)
    return (yn + bonus).reshape(b, t, C_DIM) * g


def gdn_prep(pd, d_conv_w, d_A_log, d_dt_bias):
    b, t = pd.shape[:2]
    qkv, z, bf, bb, af, ab = split_cols(pd, D_SIZES)
    qkv = jax.nn.silu(dwconv_centred(qkv, d_conv_w))
    q, k, v = jnp.split(qkv, 3, axis=-1)
    q = l2_normalize(q.reshape(b, t, D_HEADS, D_HEAD_DIM)) * D_HEAD_DIM ** -0.5
    k = l2_normalize(k.reshape(b, t, D_HEADS, D_HEAD_DIM))
    v = v.reshape(b, t, D_HEADS, D_HEAD_DIM)
    betas = (jax.nn.sigmoid(bf), jax.nn.sigmoid(bb))
    gs = tuple(-jnp.exp(d_A_log[i].astype(F32)) * jax.nn.softplus((al + d_dt_bias[i]).astype(F32))
               for i, al in enumerate((af, ab)))
    return q, k, v, z, betas, gs


def chunk_gated_delta(q, k, v, beta, g, s0):
    b, t, h, dk = q.shape
    dv = v.shape[-1]
    n = t // D_CHUNK

    def to_chunks(z):
        z = z.astype(F32).reshape(b, n, D_CHUNK, h, *z.shape[3:])
        return jnp.moveaxis(z, (1, 3), (0, 2))

    qc, kc, vc, bc, gc = (to_chunks(z) for z in (q, k, v, beta, g))
    gcum = jnp.cumsum(gc, axis=-1)
    idx = jnp.arange(D_CHUNK)
    causal = idx[:, None] >= idx[None, :]
    strict = idx[:, None] > idx[None, :]
    diff = gcum[..., :, None] - gcum[..., None, :]
    decay = jnp.where(causal, jnp.exp(jnp.where(causal, diff, 0.0)), 0.0)
    kb = kc * bc[..., None]
    vb = vc * bc[..., None]
    lmat = jnp.where(strict, jnp.einsum('nbhid,nbhjd->nbhij', kb, kc) * decay, 0.0)
    eye = jnp.broadcast_to(jnp.eye(D_CHUNK, dtype=F32), lmat.shape)
    tinv = lax.linalg.triangular_solve(lmat + eye, eye, left_side=True, lower=True, unit_diagonal=True)
    u = tinv @ vb
    wk = tinv @ (kb * jnp.exp(gcum)[..., None])
    a_intra = jnp.einsum('nbhid,nbhjd->nbhij', qc, kc) * decay

    def step(state, inp):
        qi, ki, ui, wi, ai, gi = inp
        v_new = ui - wi @ state
        o = (qi * jnp.exp(gi)[..., None]) @ state + ai @ v_new
        g_last = gi[..., -1:]
        state = state * jnp.exp(g_last)[..., None] + jnp.einsum(
            'bhcd,bhce->bhde', ki * jnp.exp(g_last - gi)[..., None], v_new)
        return state, o

    state, o = lax.scan(step, s0, (qc, kc, u, wk, a_intra, gcum))
    return jnp.moveaxis(o, (0, 2), (1, 3)).reshape(b, t, h, dv), state


def gdn_output(o, z, d_o_norm):
    b, t = o.shape[:2]
    gate = jax.nn.silu(z.reshape(b, t, D_HEADS, D_HEAD_DIM).astype(F32))
    return (rms_norm(o, d_o_norm) * gate).reshape(b, t, D_DIM)


def mixer_cd(p_lat, p_ctx, c_mu_prev, c_mu_next, c_w0, c_w2, c_a0, c_a2, c_g2, c_k_k, c_k_a, c_r_k,
             c_ln_w, c_ln_b, d_conv_w, d_A_log, d_dt_bias, d_o_norm, ctx_out):
    b = p_lat.shape[0]
    rw = (c_mu_prev, c_mu_next, c_w0, c_w2, c_a0, c_a2, c_g2, c_k_k, c_k_a)
    r_c, v_c, kk_c, g_c, dec_c, a_c, k_c = rwkv_prep(p_ctx[..., :IN_C], *rw)
    r_l, v_l, kk_l, g_l, dec_l, a_l, k_l = rwkv_prep(p_lat[..., :IN_C], *rw)
    s0 = jnp.zeros((b, C_HEADS, C_HEAD, C_HEAD), F32)
    y_c = 0.0
    y_l = 0.0
    for i, rev in enumerate((False, True)):
        yc, st = rwkv_scan(r_c, dec_c[i], k_c[i], v_c, kk_c, a_c[i], s0, rev)
        yl, _ = rwkv_scan(r_l, dec_l[i], k_l[i], v_l, kk_l, a_l[i], st, rev)
        y_c = y_c + yc
        y_l = y_l + yl

    q_c, kd_c, vd_c, z_c, beta_c, gd_c = gdn_prep(p_ctx[..., IN_C:], d_conv_w, d_A_log, d_dt_bias)
    q_l, kd_l, vd_l, z_l, beta_l, gd_l = gdn_prep(p_lat[..., IN_C:], d_conv_w, d_A_log, d_dt_bias)
    h0 = jnp.zeros((b, D_HEADS, D_HEAD_DIM, D_HEAD_DIM), F32)
    o_c = 0.0
    o_l = 0.0
    for i, rev in enumerate((False, True)):
        oc, st = chunk_gated_delta(flip_time(q_c, rev), flip_time(kd_c, rev), flip_time(vd_c, rev),
                                   flip_time(beta_c[i], rev), flip_time(gd_c[i], rev), h0)
        ol, _ = chunk_gated_delta(flip_time(q_l, rev), flip_time(kd_l, rev), flip_time(vd_l, rev),
                                  flip_time(beta_l[i], rev), flip_time(gd_l[i], rev), st)
        o_c = o_c + flip_time(oc, rev)
        o_l = o_l + flip_time(ol, rev)

    y_lat = jnp.concatenate([rwkv_output(y_l, r_l, k_l, v_l, g_l, c_r_k, c_ln_w, c_ln_b),
                             gdn_output(o_l, z_l, d_o_norm)], axis=-1).astype(p_lat.dtype)
    y_ctx = None
    if ctx_out:
        y_ctx = jnp.concatenate([rwkv_output(y_c, r_c, k_c, v_c, g_c, c_r_k, c_ln_w, c_ln_b),
                                 gdn_output(o_c, z_c, d_o_norm)], axis=-1).astype(p_ctx.dtype)
    return y_lat, y_ctx


def conv_ffn(h, w_up, conv_w, conv_b, w_down):
    u = dwconv_centred(h @ w_up, conv_w) + conv_b
    val, gate = jnp.split(u, 2, axis=-1)
    return (jax.nn.silu(gate) * val) @ w_down


def setup_inputs(seed: int = 0) -> dict:
    key = jax.random.key(seed)
    keys = iter(jax.random.split(key, 64))
    d = D_MODEL

    def nrm(shape, scale):
        return jax.random.normal(next(keys), shape, F32) * scale

    def gain(shape):
        return 1.0 + 0.05 * jax.random.normal(next(keys), shape, F32)

    def uni(shape, lo, hi):
        return jax.random.uniform(next(keys), shape, F32, lo, hi)

    dt = jnp.exp(uni((N_ODD, 2, D_HEADS), float(np.log(1e-3)), float(np.log(1e-1))))
    return {
        'x': nrm((BATCH, SEQ, d), 1.0),
        'c': nrm((BATCH, d), 1.0),
        'ctx': nrm((BATCH, CTX_LEN, d), 1.0),
        'c_ctx': nrm((d,), 1.0),
        'ada_w': nrm((DEPTH, d, N_MOD * d), 0.5 * d ** -0.5),
        'ada_b': nrm((DEPTH, N_MOD * d), 0.02),
        'ffn_w_up': nrm((DEPTH, d, 2 * D_FF), d ** -0.5),
        'ffn_conv_w': nrm((DEPTH, FFN_CONV, 2 * D_FF), FFN_CONV ** -0.5),
        'ffn_conv_b': nrm((DEPTH, 2 * D_FF), 0.02),
        'ffn_w_down': nrm((DEPTH, D_FF, d), D_FF ** -0.5),
        'ab_w_in': nrm((N_EVEN, d, IN_AB), d ** -0.5),
        'ab_w_out': nrm((N_EVEN, MIX_WIDTH, d), MIX_WIDTH ** -0.5),
        'a_q_norm': gain((N_EVEN, A_HEAD_DIM)),
        'a_k_norm': gain((N_EVEN, A_HEAD_DIM)),
        'a_sink': nrm((N_EVEN, A_HEADS), 1.0),
        'b_cq_norm': gain((N_EVEN, B_Q_RANK)),
        'b_ckv_norm': gain((N_EVEN, B_KV_RANK)),
        'b_w_uq': nrm((N_EVEN, B_Q_RANK, B_HEADS * (B_NOPE + B_ROPE)), B_Q_RANK ** -0.5),
        'b_w_uk': nrm((N_EVEN, B_KV_RANK, B_HEADS * B_NOPE), B_KV_RANK ** -0.5),
        'b_w_uv': nrm((N_EVEN, B_KV_RANK, B_HEADS * B_V_DIM), B_KV_RANK ** -0.5),
        'b_qn_norm': gain((N_EVEN, B_NOPE)),
        'b_qr_norm': gain((N_EVEN, B_ROPE)),
        'b_kn_norm': gain((N_EVEN, B_NOPE)),
        'b_kr_norm': gain((N_EVEN, B_ROPE)),
        'cd_w_in': nrm((N_ODD, d, IN_CD), d ** -0.5),
        'cd_w_out': nrm((N_ODD, MIX_WIDTH, d), MIX_WIDTH ** -0.5),
        'c_mu_prev': uni((N_ODD, IN_C), 0.0, 0.5),
        'c_mu_next': uni((N_ODD, IN_C), 0.0, 0.5),
        'c_w0': uni((N_ODD, 2, C_DIM), -6.0, -1.0),
        'c_w2': nrm((N_ODD, 2, C_DECAY_LORA, C_DIM), 0.1),
        'c_a0': nrm((N_ODD, 2, C_DIM), 0.1),
        'c_a2': nrm((N_ODD, 2, C_AAA_LORA, C_DIM), C_AAA_LORA ** -0.5),
        'c_g2': nrm((N_ODD, C_GATE_LORA, C_DIM), C_GATE_LORA ** -0.5),
        'c_k_k': 0.85 + nrm((N_ODD, C_DIM), 0.05),
        'c_k_a': gain((N_ODD, C_DIM)),
        'c_r_k': nrm((N_ODD, C_HEADS, C_HEAD), 0.1),
        'c_ln_w': gain((N_ODD, C_DIM)),
        'c_ln_b': nrm((N_ODD, C_DIM), 0.02),
        'd_conv_w': nrm((N_ODD, D_CONV, 3 * D_DIM), D_CONV ** -0.5),
        'd_A_log': jnp.log(uni((N_ODD, 2, D_HEADS), 1.0, 16.0)),
        'd_dt_bias': dt + jnp.log(-jnp.expm1(-dt)),
        'd_o_norm': gain((N_ODD, D_HEAD_DIM)),
    }


def reference(x, c, ctx, c_ctx, ada_w, ada_b, ffn_w_up, ffn_conv_w, ffn_conv_b, ffn_w_down,
              ab_w_in, ab_w_out, a_q_norm, a_k_norm, a_sink, b_cq_norm, b_ckv_norm, b_w_uq, b_w_uk, b_w_uv,
              b_qn_norm, b_qr_norm, b_kn_norm, b_kr_norm, cd_w_in, cd_w_out, c_mu_prev, c_mu_next, c_w0, c_w2,
              c_a0, c_a2, c_g2, c_k_k, c_k_a, c_r_k, c_ln_w, c_ln_b, d_conv_w, d_A_log, d_dt_bias, d_o_norm):
    seq = x.shape[1]
    rows = seq // GRID_W
    row = jnp.repeat(jnp.arange(rows, dtype=jnp.int32), GRID_W)
    col = jnp.tile(jnp.arange(GRID_W, dtype=jnp.int32), rows)
    silu_c = jax.nn.silu(c)
    silu_cc = jax.nn.silu(c_ctx)
    for l in range(DEPTH):
        last = l == DEPTH - 1
        i = l // 2
        mod_l = jnp.split((silu_c @ ada_w[l] + ada_b[l])[:, None, :], N_MOD, axis=-1)
        mod_c = jnp.split(silu_cc @ ada_w[l] + ada_b[l], N_MOD, axis=-1)
        h_l = modulate(x, mod_l[0], mod_l[1])
        h_c = modulate(ctx, mod_c[0], mod_c[1])
        if l % 2 == 0:
            y_l, y_c = mixer_ab(h_l @ ab_w_in[i], h_c @ ab_w_in[i], row, col, a_q_norm[i], a_k_norm[i],
                                a_sink[i], b_cq_norm[i], b_ckv_norm[i], b_w_uq[i], b_w_uk[i], b_w_uv[i],
                                b_qn_norm[i], b_qr_norm[i], b_kn_norm[i], b_kr_norm[i], not last)
            w_out = ab_w_out[i]
        else:
            y_l, y_c = mixer_cd(h_l @ cd_w_in[i], h_c @ cd_w_in[i], c_mu_prev[i], c_mu_next[i], c_w0[i],
                                c_w2[i], c_a0[i], c_a2[i], c_g2[i], c_k_k[i], c_k_a[i], c_r_k[i], c_ln_w[i],
                                c_ln_b[i], d_conv_w[i], d_A_log[i], d_dt_bias[i], d_o_norm[i], not last)
            w_out = cd_w_out[i]
        x = x + mod_l[2] * (y_l @ w_out)
        x = x + mod_l[5] * conv_ffn(modulate(x, mod_l[3], mod_l[4]), ffn_w_up[l], ffn_conv_w[l],
                                    ffn_conv_b[l], ffn_w_down[l])
        if not last:
            ctx = ctx + mod_c[2] * (y_c @ w_out)
            ctx = ctx + mod_c[5] * conv_ffn(modulate(ctx, mod_c[3], mod_c[4]), ffn_w_up[l], ffn_conv_w[l],
                                            ffn_conv_b[l], ffn_w_down[l])
    return x
```

```python
import functools

import jax
import jax.numpy as jnp
from jax import lax
import numpy as np
from jax.experimental import pallas as pl
from jax.experimental.pallas import tpu as pltpu

D_MODEL = 1024
DEPTH = 2
GRID_W = 64
N_MOD = 6
EPS = 1e-6
ROPE_THETA = 10000.0
NEG_INF = -1e30

A_HEADS = 8
A_KV_HEADS = 2
A_HEAD_DIM = 64
WINDOW = 128
B_HEADS = 8
B_Q_RANK = 256
B_KV_RANK = 256
B_NOPE = 64
B_ROPE = 32
B_V_DIM = 64
Q_BLOCK = 128
C_HEADS = 8
C_HEAD = 64
C_DIM = C_HEADS * C_HEAD
C_DECAY_LORA = 64
C_AAA_LORA = 64
C_GATE_LORA = 128
C_GN_EPS = 64e-5
D_HEADS = 4
D_HEAD_DIM = 128
D_DIM = D_HEADS * D_HEAD_DIM
D_CONV = 5
D_CHUNK = 64
D_FF = 2816
FFN_CONV = 3

AB_SIZES = (A_HEADS * A_HEAD_DIM, A_KV_HEADS * A_HEAD_DIM, A_KV_HEADS * A_HEAD_DIM, B_Q_RANK, B_KV_RANK, B_ROPE)
C_SIZES = (C_DIM, C_DIM, C_DIM, C_DECAY_LORA, C_DECAY_LORA, C_AAA_LORA, C_AAA_LORA, C_GATE_LORA)
IN_C = sum(C_SIZES)
D_SIZES = (3 * D_DIM, D_DIM, D_HEADS, D_HEADS, D_HEADS, D_HEADS)

F32 = jnp.float32
BF16 = jnp.bfloat16

V7X_VMEM_LIMIT_BYTES = 48 * 1024 * 1024
LANE = 128
MXU_N = 256


def _mm_kernel(a_ref, b_ref, o_ref):
    o_ref[...] = jnp.dot(a_ref[...].astype(BF16), b_ref[...], preferred_element_type=F32)


def _pick_tile(n, candidates):
    for c in candidates:
        if n % c == 0:
            return c
    raise ValueError(f"no tile for {n}")


def _mm(a, w):
    lead = a.shape[:-1]
    k = a.shape[-1]
    n = w.shape[-1]
    a2 = a.reshape(-1, k)
    m = a2.shape[0]
    n_pad = -(-n // MXU_N) * MXU_N
    wb = w.astype(BF16)
    if n_pad != n:
        wb = jnp.pad(wb, ((0, 0), (0, n_pad - n)))
    tm = _pick_tile(m, (1024, 512, 256, 128, 8))
    tn = _pick_tile(n_pad, (1024, 768, 512, 256))
    out = pl.pallas_call(
        _mm_kernel,
        grid=(m // tm, n_pad // tn),
        in_specs=[pl.BlockSpec((tm, k), lambda i, j: (i, 0)),
                  pl.BlockSpec((k, tn), lambda i, j: (0, j))],
        out_specs=pl.BlockSpec((tm, tn), lambda i, j: (i, j)),
        out_shape=jax.ShapeDtypeStruct((m, n_pad), F32),
        compiler_params=pltpu.CompilerParams(
            dimension_semantics=("arbitrary", "arbitrary"),
            vmem_limit_bytes=V7X_VMEM_LIMIT_BYTES),
        name="mm",
    )(a2, wb)
    if n_pad != n:
        out = out[:, :n]
    return out.reshape(*lead, n)


def split_cols(p, sizes):
    return jnp.split(p, [int(s) for s in np.cumsum(sizes)[:-1]], axis=-1)


def rms_norm(x, gain=None, eps=EPS):
    xf = x.astype(F32)
    y = xf * lax.rsqrt(jnp.mean(xf * xf, axis=-1, keepdims=True) + eps)
    if gain is not None:
        y = y * gain.astype(F32)
    return y.astype(x.dtype)


def l2_normalize(x, eps=1e-6):
    xf = x.astype(F32)
    return (xf * lax.rsqrt(jnp.sum(xf * xf, axis=-1, keepdims=True) + eps)).astype(x.dtype)


def modulate(x, shift, scale):
    return rms_norm(x) * (1.0 + scale) + shift


def rope_1d(x, pos):
    half = x.shape[-1] // 2
    inv = jnp.power(ROPE_THETA, -jnp.arange(half, dtype=F32) / half)
    ang = pos.astype(F32)[:, None] * inv[None, :]
    cos = jnp.cos(ang)[None, :, None, :]
    sin = jnp.sin(ang)[None, :, None, :]
    x1, x2 = x[..., :half], x[..., half:]
    return jnp.concatenate([x1 * cos - x2 * sin, x1 * sin + x2 * cos], axis=-1).astype(x.dtype)


def rope_2d(x, row, col):
    h = x.shape[-1] // 2
    return jnp.concatenate([rope_1d(x[..., :h], row), rope_1d(x[..., h:], col)], axis=-1)


def dwconv_centred(x, w):
    k, ch = w.shape
    p = k // 2
    return lax.conv_general_dilated(x, w[:, None, :].astype(x.dtype), (1,), [(p, p)],
                                    dimension_numbers=('NWC', 'WIO', 'NWC'), feature_group_count=ch)


def flip_time(t, rev):
    return jnp.flip(t, axis=1) if rev else t


def window_attention(q, k, v, k_ctx, v_ctx, sink):
    b, s, h, d = q.shape
    kv = k.shape[2]
    g = h // kv
    nb = s // WINDOW
    n_ctx = k_ctx.shape[1]
    qb = q.reshape(b, nb, WINDOW, kv, g, d)

    def band(t):
        tb = t.reshape(b, nb, WINDOW, kv, d)
        zero = jnp.zeros_like(tb[:, :1])
        prev = jnp.concatenate([zero, tb[:, :-1]], axis=1)
        nxt = jnp.concatenate([tb[:, 1:], zero], axis=1)
        return jnp.concatenate([prev, tb, nxt], axis=2)

    kb, vb = band(k), band(v)
    scale = d ** -0.5
    s_loc = jnp.einsum('bnqhgd,bnjhd->bnhgqj', qb, kb).astype(F32) * scale
    blk = jnp.arange(nb)[:, None, None]
    qpos = blk * WINDOW + jnp.arange(WINDOW)[None, :, None]
    kpos = (blk - 1) * WINDOW + jnp.arange(3 * WINDOW)[None, None, :]
    valid = (jnp.abs(qpos - kpos) <= WINDOW) & (kpos >= 0) & (kpos < s)
    s_loc = jnp.where(valid[None, :, None, None], s_loc, NEG_INF)
    s_ctx = jnp.einsum('bnqhgd,bjhd->bnhgqj', qb, k_ctx).astype(F32) * scale
    s_sink = jnp.broadcast_to(sink.astype(F32).reshape(1, 1, kv, g, 1, 1), s_loc.shape[:-1] + (1,))
    p = jax.nn.softmax(jnp.concatenate([s_loc, s_ctx, s_sink], axis=-1), axis=-1)
    p_loc = p[..., :3 * WINDOW].astype(v.dtype)
    p_ctx = p[..., 3 * WINDOW:3 * WINDOW + n_ctx].astype(v.dtype)
    o = jnp.einsum('bnhgqj,bnjhd->bnqhgd', p_loc, vb) + jnp.einsum('bnhgqj,bjhd->bnqhgd', p_ctx, v_ctx)
    return o.reshape(b, s, h * d)


def context_gqa(q, k, v, sink):
    b, n, h, d = q.shape
    kv = k.shape[2]
    g = h // kv
    qg = q.reshape(b, n, kv, g, d)
    s = jnp.einsum('bqhgd,bjhd->bhgqj', qg, k).astype(F32) * d ** -0.5
    s_sink = jnp.broadcast_to(sink.astype(F32).reshape(1, kv, g, 1, 1), s.shape[:-1] + (1,))
    p = jax.nn.softmax(jnp.concatenate([s, s_sink], axis=-1), axis=-1)[..., :n].astype(v.dtype)
    return jnp.einsum('bhgqj,bjhd->bqhgd', p, v).reshape(b, n, h * d)


def mla_attend(qn, qr, kn, kr, v):
    s = (jnp.einsum('bqhd,bkhd->bhqk', qn, kn) + jnp.einsum('bqhd,bkd->bhqk', qr, kr)).astype(F32)
    p = jax.nn.softmax(s * (B_NOPE + B_ROPE) ** -0.5, axis=-1).astype(v.dtype)
    return jnp.einsum('bhqk,bkhd->bqhd', p, v)


def mla_latent(qn, qr, kn_all, kr_all, v_all):
    b, s = qn.shape[:2]
    nq = s // Q_BLOCK

    def blocks(t):
        return jnp.moveaxis(t.reshape(b, nq, Q_BLOCK, *t.shape[2:]), 1, 0)

    o = lax.map(lambda qs: mla_attend(qs[0], qs[1], kn_all, kr_all, v_all), (blocks(qn), blocks(qr)))
    return jnp.moveaxis(o, 0, 1).reshape(b, s, -1)


def ab_prep(p, a_q_norm, a_k_norm, b_cq_norm, b_ckv_norm, b_w_uq, b_w_uk, b_w_uv,
            b_qn_norm, b_qr_norm, b_kn_norm, b_kr_norm):
    b, t = p.shape[:2]
    qa, ka, va, cq, ckv, kr = split_cols(p, AB_SIZES)
    qa = rms_norm(qa.reshape(b, t, A_HEADS, A_HEAD_DIM), a_q_norm)
    ka = rms_norm(ka.reshape(b, t, A_KV_HEADS, A_HEAD_DIM), a_k_norm)
    va = va.reshape(b, t, A_KV_HEADS, A_HEAD_DIM)
    qb = _mm(rms_norm(cq, b_cq_norm), b_w_uq).reshape(b, t, B_HEADS, B_NOPE + B_ROPE)
    qn = rms_norm(qb[..., :B_NOPE], b_qn_norm)
    qr = rms_norm(qb[..., B_NOPE:], b_qr_norm)
    ckv = rms_norm(ckv, b_ckv_norm)
    kn = rms_norm(_mm(ckv, b_w_uk).reshape(b, t, B_HEADS, B_NOPE), b_kn_norm)
    vb = _mm(ckv, b_w_uv).reshape(b, t, B_HEADS, B_V_DIM)
    kr = rms_norm(kr, b_kr_norm)
    return qa, ka, va, qn, qr, kn, kr, vb


def mixer_ab(p_lat, p_ctx, row, col, a_q_norm, a_k_norm, a_sink, b_cq_norm, b_ckv_norm, b_w_uq, b_w_uk,
             b_w_uv, b_qn_norm, b_qr_norm, b_kn_norm, b_kr_norm, ctx_out):
    prm = (a_q_norm, a_k_norm, b_cq_norm, b_ckv_norm, b_w_uq, b_w_uk, b_w_uv,
           b_qn_norm, b_qr_norm, b_kn_norm, b_kr_norm)
    qa_c, ka_c, va_c, qn_c, qr_c, kn_c, kr_c, vb_c = ab_prep(p_ctx, *prm)
    qa_l, ka_l, va_l, qn_l, qr_l, kn_l, kr_l, vb_l = ab_prep(p_lat, *prm)
    qa_l = rope_2d(qa_l, row, col)
    ka_l = rope_2d(ka_l, row, col)
    qr_l = rope_2d(qr_l, row, col)
    kr_l = rope_2d(kr_l[:, :, None, :], row, col)[:, :, 0, :]
    o_a = window_attention(qa_l, ka_l, va_l, ka_c, va_c, a_sink)
    o_b = mla_latent(qn_l, qr_l, jnp.concatenate([kn_c, kn_l], axis=1),
                     jnp.concatenate([kr_c, kr_l], axis=1), jnp.concatenate([vb_c, vb_l], axis=1))
    y_lat = jnp.concatenate([o_a, o_b], axis=-1)
    y_ctx = None
    if ctx_out:
        b, n = p_ctx.shape[:2]
        y_ctx = jnp.concatenate([context_gqa(qa_c, ka_c, va_c, a_sink),
                                 mla_attend(qn_c, qr_c, kn_c, kr_c, vb_c).reshape(b, n, -1)], axis=-1)
    return y_lat, y_ctx


def token_shift_centred(p, mu_prev, mu_next):
    prev = jnp.pad(p, ((0, 0), (1, 0), (0, 0)))[:, :-1]
    nxt = jnp.pad(p, ((0, 0), (0, 1), (0, 0)))[:, 1:]
    return p + mu_prev * (prev - p) + mu_next * (nxt - p)


def rwkv_prep(pc, c_mu_prev, c_mu_next, c_w0, c_w2, c_a0, c_a2, c_g2, c_k_k, c_k_a):
    b, t = pc.shape[:2]

    def heads(z):
        return z.reshape(b, t, C_HEADS, C_HEAD)

    xs = token_shift_centred(pc, c_mu_prev, c_mu_next)
    r, k, v, wl_f, wl_b, al_f, al_b, gl = split_cols(xs, C_SIZES)

    def decay(wl, w0, w2):
        w = -jax.nn.softplus(-(w0 + _mm(jnp.tanh(wl), w2))) - 0.5
        return heads(jnp.exp(-jnp.exp(w.astype(F32))))

    decays = (decay(wl_f, c_w0[0], c_w2[0]), decay(wl_b, c_w0[1], c_w2[1]))
    iclr = (jax.nn.sigmoid(c_a0[0] + _mm(al_f, c_a2[0])), jax.nn.sigmoid(c_a0[1] + _mm(al_b, c_a2[1])))
    g = _mm(jax.nn.sigmoid(gl), c_g2)
    kk = l2_normalize(heads(k * c_k_k))
    ks = tuple(heads(k * (1.0 + (a - 1.0) * c_k_a)) for a in iclr)
    return heads(r), heads(v), kk, g, decays, tuple(heads(a) for a in iclr), ks


def rwkv_scan(r, w, k, v, kk, a, s0, reverse):
    def step(state, inp):
        rt, wt, kt, vt, kkt, at = inp
        sa = jnp.einsum('bhvk,bhk->bhv', state, -kkt)
        state = state * wt[:, :, None, :] + sa[..., None] * (kkt * at)[:, :, None, :] + vt[..., None] * kt[:, :, None, :]
        return state, jnp.einsum('bhvk,bhk->bhv', state, rt)

    xs = tuple(jnp.moveaxis(z.astype(F32), 1, 0) for z in (r, w, k, v, kk, a))
    state, y = lax.scan(step, s0, xs, reverse=reverse)
    return jnp.moveaxis(y, 0, 1), state


def head_group_norm(y, w, b_):
    mean = jnp.mean(y, axis=-1, keepdims=True)
    var = jnp.mean(jnp.square(y - mean), axis=-1, keepdims=True)
    yn = (y - mean) * lax.rsqrt(var + C_GN_EPS)
    return yn * w.astype(F32).reshape(C_HEADS, C_HEAD) + b_.astype(F32).reshape(C_HEADS, C_HEAD)


def rwkv_output(y, r, ks, v, g, c_r_k, c_ln_w, c_ln_b):
    b, t = y.shape[:2]
    yn = head_group_norm(y, c_ln_w, c_ln_b)
    bonus = sum(jnp.sum(r * kd * c_r_k, axis=-1, keepdims=True) * v for kd in ks)
    return (yn + bonus).reshape(b, t, C_DIM) * g


def gdn_prep(pd, d_conv_w, d_A_log, d_dt_bias):
    b, t = pd.shape[:2]
    qkv, z, bf, bb, af, ab = split_cols(pd, D_SIZES)
    qkv = jax.nn.silu(dwconv_centred(qkv, d_conv_w))
    q, k, v = jnp.split(qkv, 3, axis=-1)
    q = l2_normalize(q.reshape(b, t, D_HEADS, D_HEAD_DIM)) * D_HEAD_DIM ** -0.5
    k = l2_normalize(k.reshape(b, t, D_HEADS, D_HEAD_DIM))
    v = v.reshape(b, t, D_HEADS, D_HEAD_DIM)
    betas = (jax.nn.sigmoid(bf), jax.nn.sigmoid(bb))
    gs = tuple(-jnp.exp(d_A_log[i].astype(F32)) * jax.nn.softplus((al + d_dt_bias[i]).astype(F32))
               for i, al in enumerate((af, ab)))
    return q, k, v, z, betas, gs


def chunk_gated_delta(q, k, v, beta, g, s0):
    b, t, h, dk = q.shape
    dv = v.shape[-1]
    n = t // D_CHUNK

    def to_chunks(z):
        z = z.astype(F32).reshape(b, n, D_CHUNK, h, *z.shape[3:])
        return jnp.moveaxis(z, (1, 3), (0, 2))

    qc, kc, vc, bc, gc = (to_chunks(z) for z in (q, k, v, beta, g))
    gcum = jnp.cumsum(gc, axis=-1)
    idx = jnp.arange(D_CHUNK)
    causal = idx[:, None] >= idx[None, :]
    strict = idx[:, None] > idx[None, :]
    diff = gcum[..., :, None] - gcum[..., None, :]
    decay = jnp.where(causal, jnp.exp(jnp.where(causal, diff, 0.0)), 0.0)
    kb = kc * bc[..., None]
    vb = vc * bc[..., None]
    lmat = jnp.where(strict, jnp.einsum('nbhid,nbhjd->nbhij', kb, kc) * decay, 0.0)
    eye = jnp.broadcast_to(jnp.eye(D_CHUNK, dtype=F32), lmat.shape)
    tinv = lax.linalg.triangular_solve(lmat + eye, eye, left_side=True, lower=True, unit_diagonal=True)
    u = tinv @ vb
    wk = tinv @ (kb * jnp.exp(gcum)[..., None])
    a_intra = jnp.einsum('nbhid,nbhjd->nbhij', qc, kc) * decay

    def step(state, inp):
        qi, ki, ui, wi, ai, gi = inp
        v_new = ui - wi @ state
        o = (qi * jnp.exp(gi)[..., None]) @ state + ai @ v_new
        g_last = gi[..., -1:]
        state = state * jnp.exp(g_last)[..., None] + jnp.einsum(
            'bhcd,bhce->bhde', ki * jnp.exp(g_last - gi)[..., None], v_new)
        return state, o

    state, o = lax.scan(step, s0, (qc, kc, u, wk, a_intra, gcum))
    return jnp.moveaxis(o, (0, 2), (1, 3)).reshape(b, t, h, dv), state


def gdn_output(o, z, d_o_norm):
    b, t = o.shape[:2]
    gate = jax.nn.silu(z.reshape(b, t, D_HEADS, D_HEAD_DIM).astype(F32))
    return (rms_norm(o, d_o_norm) * gate).reshape(b, t, D_DIM)


def mixer_cd(p_lat, p_ctx, c_mu_prev, c_mu_next, c_w0, c_w2, c_a0, c_a2, c_g2, c_k_k, c_k_a, c_r_k,
             c_ln_w, c_ln_b, d_conv_w, d_A_log, d_dt_bias, d_o_norm, ctx_out):
    b = p_lat.shape[0]
    rw = (c_mu_prev, c_mu_next, c_w0, c_w2, c_a0, c_a2, c_g2, c_k_k, c_k_a)
    r_c, v_c, kk_c, g_c, dec_c, a_c, k_c = rwkv_prep(p_ctx[..., :IN_C], *rw)
    r_l, v_l, kk_l, g_l, dec_l, a_l, k_l = rwkv_prep(p_lat[..., :IN_C], *rw)
    s0 = jnp.zeros((b, C_HEADS, C_HEAD, C_HEAD), F32)
    y_c = 0.0
    y_l = 0.0
    for i, rev in enumerate((False, True)):
        yc, st = rwkv_scan(r_c, dec_c[i], k_c[i], v_c, kk_c, a_c[i], s0, rev)
        yl, _ = rwkv_scan(r_l, dec_l[i], k_l[i], v_l, kk_l, a_l[i], st, rev)
        y_c = y_c + yc
        y_l = y_l + yl

    q_c, kd_c, vd_c, z_c, beta_c, gd_c = gdn_prep(p_ctx[..., IN_C:], d_conv_w, d_A_log, d_dt_bias)
    q_l, kd_l, vd_l, z_l, beta_l, gd_l = gdn_prep(p_lat[..., IN_C:], d_conv_w, d_A_log, d_dt_bias)
    h0 = jnp.zeros((b, D_HEADS, D_HEAD_DIM, D_HEAD_DIM), F32)
    o_c = 0.0
    o_l = 0.0
    for i, rev in enumerate((False, True)):
        oc, st = chunk_gated_delta(flip_time(q_c, rev), flip_time(kd_c, rev), flip_time(vd_c, rev),
                                   flip_time(beta_c[i], rev), flip_time(gd_c[i], rev), h0)
        ol, _ = chunk_gated_delta(flip_time(q_l, rev), flip_time(kd_l, rev), flip_time(vd_l, rev),
                                  flip_time(beta_l[i], rev), flip_time(gd_l[i], rev), st)
        o_c = o_c + flip_time(oc, rev)
        o_l = o_l + flip_time(ol, rev)

    y_lat = jnp.concatenate([rwkv_output(y_l, r_l, k_l, v_l, g_l, c_r_k, c_ln_w, c_ln_b),
                             gdn_output(o_l, z_l, d_o_norm)], axis=-1).astype(p_lat.dtype)
    y_ctx = None
    if ctx_out:
        y_ctx = jnp.concatenate([rwkv_output(y_c, r_c, k_c, v_c, g_c, c_r_k, c_ln_w, c_ln_b),
                                 gdn_output(o_c, z_c, d_o_norm)], axis=-1).astype(p_ctx.dtype)
    return y_lat, y_ctx


def conv_ffn(h, w_up, conv_w, conv_b, w_down):
    u = dwconv_centred(_mm(h, w_up), conv_w) + conv_b
    val, gate = jnp.split(u, 2, axis=-1)
    return _mm(jax.nn.silu(gate) * val, w_down)


def kernel(x, c, ctx, c_ctx, ada_w, ada_b, ffn_w_up, ffn_conv_w, ffn_conv_b, ffn_w_down,
           ab_w_in, ab_w_out, a_q_norm, a_k_norm, a_sink, b_cq_norm, b_ckv_norm, b_w_uq, b_w_uk, b_w_uv,
           b_qn_norm, b_qr_norm, b_kn_norm, b_kr_norm, cd_w_in, cd_w_out, c_mu_prev, c_mu_next, c_w0, c_w2,
           c_a0, c_a2, c_g2, c_k_k, c_k_a, c_r_k, c_ln_w, c_ln_b, d_conv_w, d_A_log, d_dt_bias, d_o_norm):
    seq = x.shape[1]
    rows = seq // GRID_W
    row = jnp.repeat(jnp.arange(rows, dtype=jnp.int32), GRID_W)
    col = jnp.tile(jnp.arange(GRID_W, dtype=jnp.int32), rows)
    silu_c = jax.nn.silu(c)
    silu_cc = jax.nn.silu(c_ctx)
    for l in range(DEPTH):
        last = l == DEPTH - 1
        i = l // 2
        mod_l = jnp.split((silu_c @ ada_w[l] + ada_b[l])[:, None, :], N_MOD, axis=-1)
        mod_c = jnp.split(silu_cc @ ada_w[l] + ada_b[l], N_MOD, axis=-1)
        h_l = modulate(x, mod_l[0], mod_l[1])
        h_c = modulate(ctx, mod_c[0], mod_c[1])
        if l % 2 == 0:
            y_l, y_c = mixer_ab(_mm(h_l, ab_w_in[i]), _mm(h_c, ab_w_in[i]), row, col, a_q_norm[i], a_k_norm[i],
                                a_sink[i], b_cq_norm[i], b_ckv_norm[i], b_w_uq[i], b_w_uk[i], b_w_uv[i],
                                b_qn_norm[i], b_qr_norm[i], b_kn_norm[i], b_kr_norm[i], not last)
            w_out = ab_w_out[i]
        else:
            y_l, y_c = mixer_cd(_mm(h_l, cd_w_in[i]), _mm(h_c, cd_w_in[i]), c_mu_prev[i], c_mu_next[i], c_w0[i],
                                c_w2[i], c_a0[i], c_a2[i], c_g2[i], c_k_k[i], c_k_a[i], c_r_k[i], c_ln_w[i],
                                c_ln_b[i], d_conv_w[i], d_A_log[i], d_dt_bias[i], d_o_norm[i], not last)
            w_out = cd_w_out[i]
        x = x + mod_l[2] * _mm(y_l, w_out)
        x = x + mod_l[5] * conv_ffn(modulate(x, mod_l[3], mod_l[4]), ffn_w_up[l], ffn_conv_w[l],
                                    ffn_conv_b[l], ffn_w_down[l])
        if not last:
            ctx = ctx + mod_c[2] * _mm(y_c, w_out)
            ctx = ctx + mod_c[5] * conv_ffn(modulate(ctx, mod_c[3], mod_c[4]), ffn_w_up[l], ffn_conv_w[l],
                                            ffn_conv_b[l], ffn_w_down[l])
    return x
```

```python
import functools

import jax
import jax.numpy as jnp
from jax import lax
import numpy as np
from jax.experimental import pallas as pl
from jax.experimental.pallas import tpu as pltpu

D_MODEL = 1024
DEPTH = 2
GRID_W = 64
N_MOD = 6
EPS = 1e-6
ROPE_THETA = 10000.0
NEG_INF = -1e30

A_HEADS = 8
A_KV_HEADS = 2
A_HEAD_DIM = 64
WINDOW = 128
B_HEADS = 8
B_Q_RANK = 256
B_KV_RANK = 256
B_NOPE = 64
B_ROPE = 32
B_V_DIM = 64
Q_BLOCK = 128
C_HEADS = 8
C_HEAD = 64
C_DIM = C_HEADS * C_HEAD
C_DECAY_LORA = 64
C_AAA_LORA = 64
C_GATE_LORA = 128
C_GN_EPS = 64e-5
D_HEADS = 4
D_HEAD_DIM = 128
D_DIM = D_HEADS * D_HEAD_DIM
D_CONV = 5
D_CHUNK = 64
D_FF = 2816
FFN_CONV = 3

AB_SIZES = (A_HEADS * A_HEAD_DIM, A_KV_HEADS * A_HEAD_DIM, A_KV_HEADS * A_HEAD_DIM, B_Q_RANK, B_KV_RANK, B_ROPE)
C_SIZES = (C_DIM, C_DIM, C_DIM, C_DECAY_LORA, C_DECAY_LORA, C_AAA_LORA, C_AAA_LORA, C_GATE_LORA)
IN_C = sum(C_SIZES)
D_SIZES = (3 * D_DIM, D_DIM, D_HEADS, D_HEADS, D_HEADS, D_HEADS)

F32 = jnp.float32
BF16 = jnp.bfloat16

V7X_VMEM_LIMIT_BYTES = 48 * 1024 * 1024
LANE = 128
MXU_N = 256


def _mm_kernel(a_ref, b_ref, o_ref):
    o_ref[...] = jnp.dot(a_ref[...].astype(BF16), b_ref[...], preferred_element_type=F32)


def _pick_tile(n, candidates):
    for c in candidates:
        if n % c == 0:
            return c
    raise ValueError(f"no tile for {n}")


def _mm(a, w):
    lead = a.shape[:-1]
    k = a.shape[-1]
    n = w.shape[-1]
    a2 = a.reshape(-1, k)
    m = a2.shape[0]
    n_pad = -(-n // MXU_N) * MXU_N
    wb = w.astype(BF16)
    if n_pad != n:
        wb = jnp.pad(wb, ((0, 0), (0, n_pad - n)))
    tm = _pick_tile(m, (1024, 512, 256, 128, 8))
    tn = _pick_tile(n_pad, (1024, 768, 512, 256))
    out = pl.pallas_call(
        _mm_kernel,
        grid=(m // tm, n_pad // tn),
        in_specs=[pl.BlockSpec((tm, k), lambda i, j: (i, 0)),
                  pl.BlockSpec((k, tn), lambda i, j: (0, j))],
        out_specs=pl.BlockSpec((tm, tn), lambda i, j: (i, j)),
        out_shape=jax.ShapeDtypeStruct((m, n_pad), F32),
        compiler_params=pltpu.CompilerParams(
            dimension_semantics=("arbitrary", "arbitrary"),
            vmem_limit_bytes=V7X_VMEM_LIMIT_BYTES),
        name="mm",
    )(a2, wb)
    if n_pad != n:
        out = out[:, :n]
    return out.reshape(*lead, n)


def split_cols(p, sizes):
    return jnp.split(p, [int(s) for s in np.cumsum(sizes)[:-1]], axis=-1)


def rms_norm(x, gain=None, eps=EPS):
    xf = x.astype(F32)
    y = xf * lax.rsqrt(jnp.mean(xf * xf, axis=-1, keepdims=True) + eps)
    if gain is not None:
        y = y * gain.astype(F32)
    return y.astype(x.dtype)


def l2_normalize(x, eps=1e-6):
    xf = x.astype(F32)
    return (xf * lax.rsqrt(jnp.sum(xf * xf, axis=-1, keepdims=True) + eps)).astype(x.dtype)


def modulate(x, shift, scale):
    return rms_norm(x) * (1.0 + scale) + shift


def rope_1d(x, pos):
    half = x.shape[-1] // 2
    inv = jnp.power(ROPE_THETA, -jnp.arange(half, dtype=F32) / half)
    ang = pos.astype(F32)[:, None] * inv[None, :]
    cos = jnp.cos(ang)[None, :, None, :]
    sin = jnp.sin(ang)[None, :, None, :]
    x1, x2 = x[..., :half], x[..., half:]
    return jnp.concatenate([x1 * cos - x2 * sin, x1 * sin + x2 * cos], axis=-1).astype(x.dtype)


def rope_2d(x, row, col):
    h = x.shape[-1] // 2
    return jnp.concatenate([rope_1d(x[..., :h], row), rope_1d(x[..., h:], col)], axis=-1)


def dwconv_centred(x, w):
    k, ch = w.shape
    p = k // 2
    return lax.conv_general_dilated(x, w[:, None, :].astype(x.dtype), (1,), [(p, p)],
                                    dimension_numbers=('NWC', 'WIO', 'NWC'), feature_group_count=ch)


def flip_time(t, rev):
    return jnp.flip(t, axis=1) if rev else t


def window_attention(q, k, v, k_ctx, v_ctx, sink):
    b, s, h, d = q.shape
    kv = k.shape[2]
    g = h // kv
    nb = s // WINDOW
    n_ctx = k_ctx.shape[1]
    qb = q.reshape(b, nb, WINDOW, kv, g, d)

    def band(t):
        tb = t.reshape(b, nb, WINDOW, kv, d)
        zero = jnp.zeros_like(tb[:, :1])
        prev = jnp.concatenate([zero, tb[:, :-1]], axis=1)
        nxt = jnp.concatenate([tb[:, 1:], zero], axis=1)
        return jnp.concatenate([prev, tb, nxt], axis=2)

    kb, vb = band(k), band(v)
    scale = d ** -0.5
    s_loc = jnp.einsum('bnqhgd,bnjhd->bnhgqj', qb, kb).astype(F32) * scale
    blk = jnp.arange(nb)[:, None, None]
    qpos = blk * WINDOW + jnp.arange(WINDOW)[None, :, None]
    kpos = (blk - 1) * WINDOW + jnp.arange(3 * WINDOW)[None, None, :]
    valid = (jnp.abs(qpos - kpos) <= WINDOW) & (kpos >= 0) & (kpos < s)
    s_loc = jnp.where(valid[None, :, None, None], s_loc, NEG_INF)
    s_ctx = jnp.einsum('bnqhgd,bjhd->bnhgqj', qb, k_ctx).astype(F32) * scale
    s_sink = jnp.broadcast_to(sink.astype(F32).reshape(1, 1, kv, g, 1, 1), s_loc.shape[:-1] + (1,))
    p = jax.nn.softmax(jnp.concatenate([s_loc, s_ctx, s_sink], axis=-1), axis=-1)
    p_loc = p[..., :3 * WINDOW].astype(v.dtype)
    p_ctx = p[..., 3 * WINDOW:3 * WINDOW + n_ctx].astype(v.dtype)
    o = jnp.einsum('bnhgqj,bnjhd->bnqhgd', p_loc, vb) + jnp.einsum('bnhgqj,bjhd->bnqhgd', p_ctx, v_ctx)
    return o.reshape(b, s, h * d)


def context_gqa(q, k, v, sink):
    b, n, h, d = q.shape
    kv = k.shape[2]
    g = h // kv
    qg = q.reshape(b, n, kv, g, d)
    s = jnp.einsum('bqhgd,bjhd->bhgqj', qg, k).astype(F32) * d ** -0.5
    s_sink = jnp.broadcast_to(sink.astype(F32).reshape(1, kv, g, 1, 1), s.shape[:-1] + (1,))
    p = jax.nn.softmax(jnp.concatenate([s, s_sink], axis=-1), axis=-1)[..., :n].astype(v.dtype)
    return jnp.einsum('bhgqj,bjhd->bqhgd', p, v).reshape(b, n, h * d)


def mla_attend(qn, qr, kn, kr, v):
    s = (jnp.einsum('bqhd,bkhd->bhqk', qn, kn) + jnp.einsum('bqhd,bkd->bhqk', qr, kr)).astype(F32)
    p = jax.nn.softmax(s * (B_NOPE + B_ROPE) ** -0.5, axis=-1).astype(v.dtype)
    return jnp.einsum('bhqk,bkhd->bqhd', p, v)


def mla_latent(qn, qr, kn_all, kr_all, v_all):
    b, s = qn.shape[:2]
    nq = s // Q_BLOCK

    def blocks(t):
        return jnp.moveaxis(t.reshape(b, nq, Q_BLOCK, *t.shape[2:]), 1, 0)

    o = lax.map(lambda qs: mla_attend(qs[0], qs[1], kn_all, kr_all, v_all), (blocks(qn), blocks(qr)))
    return jnp.moveaxis(o, 0, 1).reshape(b, s, -1)


def ab_prep(p, a_q_norm, a_k_norm, b_cq_norm, b_ckv_norm, b_w_uq, b_w_uk, b_w_uv,
            b_qn_norm, b_qr_norm, b_kn_norm, b_kr_norm):
    b, t = p.shape[:2]
    qa, ka, va, cq, ckv, kr = split_cols(p, AB_SIZES)
    qa = rms_norm(qa.reshape(b, t, A_HEADS, A_HEAD_DIM), a_q_norm)
    ka = rms_norm(ka.reshape(b, t, A_KV_HEADS, A_HEAD_DIM), a_k_norm)
    va = va.reshape(b, t, A_KV_HEADS, A_HEAD_DIM)
    qb = _mm(rms_norm(cq, b_cq_norm), b_w_uq).reshape(b, t, B_HEADS, B_NOPE + B_ROPE)
    qn = rms_norm(qb[..., :B_NOPE], b_qn_norm)
    qr = rms_norm(qb[..., B_NOPE:], b_qr_norm)
    ckv = rms_norm(ckv, b_ckv_norm)
    kn = rms_norm(_mm(ckv, b_w_uk).reshape(b, t, B_HEADS, B_NOPE), b_kn_norm)
    vb = _mm(ckv, b_w_uv).reshape(b, t, B_HEADS, B_V_DIM)
    kr = rms_norm(kr, b_kr_norm)
    return qa, ka, va, qn, qr, kn, kr, vb


def mixer_ab(p_lat, p_ctx, row, col, a_q_norm, a_k_norm, a_sink, b_cq_norm, b_ckv_norm, b_w_uq, b_w_uk,
             b_w_uv, b_qn_norm, b_qr_norm, b_kn_norm, b_kr_norm, ctx_out):
    prm = (a_q_norm, a_k_norm, b_cq_norm, b_ckv_norm, b_w_uq, b_w_uk, b_w_uv,
           b_qn_norm, b_qr_norm, b_kn_norm, b_kr_norm)
    qa_c, ka_c, va_c, qn_c, qr_c, kn_c, kr_c, vb_c = ab_prep(p_ctx, *prm)
    qa_l, ka_l, va_l, qn_l, qr_l, kn_l, kr_l, vb_l = ab_prep(p_lat, *prm)
    qa_l = rope_2d(qa_l, row, col)
    ka_l = rope_2d(ka_l, row, col)
    qr_l = rope_2d(qr_l, row, col)
    kr_l = rope_2d(kr_l[:, :, None, :], row, col)[:, :, 0, :]
    o_a = window_attention(qa_l, ka_l, va_l, ka_c, va_c, a_sink)
    o_b = mla_latent(qn_l, qr_l, jnp.concatenate([kn_c, kn_l], axis=1),
                     jnp.concatenate([kr_c, kr_l], axis=1), jnp.concatenate([vb_c, vb_l], axis=1))
    y_lat = jnp.concatenate([o_a, o_b], axis=-1)
    y_ctx = None
    if ctx_out:
        b, n = p_ctx.shape[:2]
        y_ctx = jnp.concatenate([context_gqa(qa_c, ka_c, va_c, a_sink),
                                 mla_attend(qn_c, qr_c, kn_c, kr_c, vb_c).reshape(b, n, -1)], axis=-1)
    return y_lat, y_ctx


NN = ((1,), (0,))
NT = ((1,), (1,))
TN = ((0,), (0,))
RW_CHUNK = 64
HEAD_PAIR = 2 * C_HEAD


def _dot(a, b, dims, passes=1):
    def dg(x, y):
        return lax.dot_general(x, y, (dims, ((), ())), preferred_element_type=F32)

    ah, bh = a.astype(BF16), b.astype(BF16)
    if passes == 1:
        return dg(ah, bh)
    al = (a - ah.astype(F32)).astype(BF16)
    bl = (b - bh.astype(F32)).astype(BF16)
    return dg(ah, bh) + (dg(ah, bl) + dg(al, bh))


def _dot_exact(a, b, dims):
    return lax.dot_general(a, b, (dims, ((), ())), precision=lax.Precision.HIGHEST, preferred_element_type=F32)


def _unit_lower_inverse(x, eye):
    p = eye + x
    xp = x
    for _ in range(int(np.log2(RW_CHUNK)) - 1):
        xp = _dot(xp, xp, NN, passes=3)
        p = p + _dot(p, xp, NN, passes=3)
    return p


def _rwkv_kernel(r_ref, v_ref, kk_ref, lw_ref, a_ref, k_ref, y_ref, st_ref):
    d = pl.program_id(1)
    s = pl.program_id(2)

    @pl.when(s == 0)
    def _():
        st_ref[...] = jnp.zeros_like(st_ref)

    c_len = RW_CHUNK
    sign = 1 - 2 * d
    ii = lax.broadcasted_iota(jnp.int32, (c_len, c_len), 0)
    jj = lax.broadcasted_iota(jnp.int32, (c_len, c_len), 1)
    tri = jnp.where((ii - jj) * sign >= 0, 1.0, 0.0).astype(F32)
    lw_all = lw_ref[0, 0]
    cum_all = _dot_exact(tri, lw_all, NN)
    tot_all = jnp.sum(lw_all, axis=0, keepdims=True)

    r2 = lax.broadcasted_iota(jnp.int32, (HEAD_PAIR, HEAD_PAIR), 0)
    c2 = lax.broadcasted_iota(jnp.int32, (HEAD_PAIR, HEAD_PAIR), 1)
    dlt = ((r2 % c_len) - (c2 % c_len)) * sign
    m_strict = dlt > 0
    m_incl = dlt >= 0
    eye = jnp.where(r2 == c2, 1.0, 0.0).astype(F32)
    lane = lax.broadcasted_iota(jnp.int32, (c_len, HEAD_PAIR), 1)
    m0 = lane < C_HEAD

    def stack_heads(x):
        return jnp.concatenate([jnp.where(m0, x, 0.0), jnp.where(m0, 0.0, x)], axis=0)

    for p in range(C_DIM // HEAD_PAIR):
        sl = slice(p * HEAD_PAIR, (p + 1) * HEAD_PAIR)
        lw = lw_all[:, sl]
        cum = cum_all[:, sl]
        tot = tot_all[:, sl]
        kk = kk_ref[0, :, sl]
        kd = k_ref[0, 0, :, sl]
        bb = kk * a_ref[0, 0, :, sl]
        e_neg = jnp.exp(-cum)
        e_end = jnp.exp(tot - cum)
        abar = stack_heads(-kk * jnp.exp(cum - lw))
        rbar = stack_heads(r_ref[0, :, sl] * jnp.exp(cum))
        ktil = stack_heads(kd * e_neg)
        btil = stack_heads(bb * e_neg)
        khat = stack_heads(kd * e_end)
        bhat = stack_heads(bb * e_end)
        vs = stack_heads(v_ref[0, :, sl])

        g = _dot(jnp.concatenate([abar, rbar], axis=0), jnp.concatenate([ktil, btil], axis=0), NT)
        a_ak = jnp.where(m_strict, g[:HEAD_PAIR, :HEAD_PAIR], 0.0)
        a_ab = jnp.where(m_strict, g[:HEAD_PAIR, HEAD_PAIR:], 0.0)
        a_rk = jnp.where(m_incl, g[HEAD_PAIR:, :HEAD_PAIR], 0.0)
        a_rb = jnp.where(m_incl, g[HEAD_PAIR:, HEAD_PAIR:], 0.0)
        tinv = _unit_lower_inverse(a_ab, eye)
        wm = _dot(tinv, abar, NN)
        u0 = _dot(tinv, _dot(a_ak, vs, NN), NN)

        st = st_ref[p]
        u = _dot(wm, st, NT) + u0
        ys = _dot(rbar, st, NT) + _dot(a_rk, vs, NN) + _dot(a_rb, u, NN)
        y_ref[0, 0, :, sl] = ys[:c_len] + ys[c_len:]
        st_ref[p] = st * jnp.exp(tot) + _dot(vs, khat, TN) + _dot(u, bhat, TN)


def rwkv_chunked(r, v, kk, lw2, a2, k2, n_ctx, interpret=False):
    b, t, cd = r.shape
    nc = t // RW_CHUNK
    nctx = n_ctx // RW_CHUNK

    def chunk(d, s):
        rev = jnp.where(s < nctx, nctx - 1 - s, nc + nctx - 1 - s)
        return jnp.where(d == 0, s, rev)

    shared = pl.BlockSpec((1, RW_CHUNK, cd), lambda i, d, s: (i, chunk(d, s), 0))
    per_dir = pl.BlockSpec((1, 1, RW_CHUNK, cd), lambda i, d, s: (d, i, chunk(d, s), 0))
    return pl.pallas_call(
        _rwkv_kernel,
        grid=(b, 2, nc),
        in_specs=[shared, shared, shared, per_dir, per_dir, per_dir],
        out_specs=per_dir,
        out_shape=jax.ShapeDtypeStruct((2, b, t, cd), F32),
        scratch_shapes=[pltpu.VMEM((cd // HEAD_PAIR, HEAD_PAIR, HEAD_PAIR), F32)],
        compiler_params=pltpu.CompilerParams(
            dimension_semantics=("arbitrary", "arbitrary", "arbitrary"),
            vmem_limit_bytes=V7X_VMEM_LIMIT_BYTES),
        name="rwkv7_chunked",
        interpret=interpret,
    )(r, v, kk, lw2, a2, k2)


def token_shift_centred(p, mu_prev, mu_next):
    prev = jnp.pad(p, ((0, 0), (1, 0), (0, 0)))[:, :-1]
    nxt = jnp.pad(p, ((0, 0), (0, 1), (0, 0)))[:, 1:]
    return p + mu_prev * (prev - p) + mu_next * (nxt - p)


def rwkv_prep(pc, c_mu_prev, c_mu_next, c_w0, c_w2, c_a0, c_a2, c_g2, c_k_k, c_k_a):
    b, t = pc.shape[:2]

    def heads(z):
        return z.reshape(b, t, C_HEADS, C_HEAD)

    xs = token_shift_centred(pc, c_mu_prev, c_mu_next)
    r, k, v, wl_f, wl_b, al_f, al_b, gl = split_cols(xs, C_SIZES)

    def decay(wl, w0, w2):
        w = -jax.nn.softplus(-(w0 + _mm(jnp.tanh(wl), w2))) - 0.5
        return heads(-jnp.exp(w.astype(F32)))

    decays = (decay(wl_f, c_w0[0], c_w2[0]), decay(wl_b, c_w0[1], c_w2[1]))
    iclr = (jax.nn.sigmoid(c_a0[0] + _mm(al_f, c_a2[0])), jax.nn.sigmoid(c_a0[1] + _mm(al_b, c_a2[1])))
    g = _mm(jax.nn.sigmoid(gl), c_g2)
    kk = l2_normalize(heads(k * c_k_k))
    ks = tuple(heads(k * (1.0 + (a - 1.0) * c_k_a)) for a in iclr)
    return heads(r), heads(v), kk, g, decays, tuple(heads(a) for a in iclr), ks


def head_group_norm(y, w, b_):
    mean = jnp.mean(y, axis=-1, keepdims=True)
    var = jnp.mean(jnp.square(y - mean), axis=-1, keepdims=True)
    yn = (y - mean) * lax.rsqrt(var + C_GN_EPS)
    return yn * w.astype(F32).reshape(C_HEADS, C_HEAD) + b_.astype(F32).reshape(C_HEADS, C_HEAD)


def rwkv_output(y, r, ks, v, g, c_r_k, c_ln_w, c_ln_b):
    b, t = y.shape[:2]
    yn = head_group_norm(y, c_ln_w, c_ln_b)
    bonus = sum(jnp.sum(r * kd * c_r_k, axis=-1, keepdims=True) * v for kd in ks)
    return (yn + bonus).reshape(b, t, C_DIM) * g


def gdn_prep(pd, d_conv_w, d_A_log, d_dt_bias):
    b, t = pd.shape[:2]
    qkv, z, bf, bb, af, ab = split_cols(pd, D_SIZES)
    qkv = jax.nn.silu(dwconv_centred(qkv, d_conv_w))
    q, k, v = jnp.split(qkv, 3, axis=-1)
    q = l2_normalize(q.reshape(b, t, D_HEADS, D_HEAD_DIM)) * D_HEAD_DIM ** -0.5
    k = l2_normalize(k.reshape(b, t, D_HEADS, D_HEAD_DIM))
    v = v.reshape(b, t, D_HEADS, D_HEAD_DIM)
    betas = (jax.nn.sigmoid(bf), jax.nn.sigmoid(bb))
    gs = tuple(-jnp.exp(d_A_log[i].astype(F32)) * jax.nn.softplus((al + d_dt_bias[i]).astype(F32))
               for i, al in enumerate((af, ab)))
    return q, k, v, z, betas, gs


def chunk_gated_delta(q, k, v, beta, g, s0):
    b, t, h, dk = q.shape
    dv = v.shape[-1]
    n = t // D_CHUNK

    def to_chunks(z):
        z = z.astype(F32).reshape(b, n, D_CHUNK, h, *z.shape[3:])
        return jnp.moveaxis(z, (1, 3), (0, 2))

    qc, kc, vc, bc, gc = (to_chunks(z) for z in (q, k, v, beta, g))
    gcum = jnp.cumsum(gc, axis=-1)
    idx = jnp.arange(D_CHUNK)
    causal = idx[:, None] >= idx[None, :]
    strict = idx[:, None] > idx[None, :]
    diff = gcum[..., :, None] - gcum[..., None, :]
    decay = jnp.where(causal, jnp.exp(jnp.where(causal, diff, 0.0)), 0.0)
    kb = kc * bc[..., None]
    vb = vc * bc[..., None]
    lmat = jnp.where(strict, jnp.einsum('nbhid,nbhjd->nbhij', kb, kc) * decay, 0.0)
    eye = jnp.broadcast_to(jnp.eye(D_CHUNK, dtype=F32), lmat.shape)
    tinv = lax.linalg.triangular_solve(lmat + eye, eye, left_side=True, lower=True, unit_diagonal=True)
    u = tinv @ vb
    wk = tinv @ (kb * jnp.exp(gcum)[..., None])
    a_intra = jnp.einsum('nbhid,nbhjd->nbhij', qc, kc) * decay

    def step(state, inp):
        qi, ki, ui, wi, ai, gi = inp
        v_new = ui - wi @ state
        o = (qi * jnp.exp(gi)[..., None]) @ state + ai @ v_new
        g_last = gi[..., -1:]
        state = state * jnp.exp(g_last)[..., None] + jnp.einsum(
            'bhcd,bhce->bhde', ki * jnp.exp(g_last - gi)[..., None], v_new)
        return state, o

    state, o = lax.scan(step, s0, (qc, kc, u, wk, a_intra, gcum))
    return jnp.moveaxis(o, (0, 2), (1, 3)).reshape(b, t, h, dv), state


def gdn_output(o, z, d_o_norm):
    b, t = o.shape[:2]
    gate = jax.nn.silu(z.reshape(b, t, D_HEADS, D_HEAD_DIM).astype(F32))
    return (rms_norm(o, d_o_norm) * gate).reshape(b, t, D_DIM)


def mixer_cd(p_lat, p_ctx, c_mu_prev, c_mu_next, c_w0, c_w2, c_a0, c_a2, c_g2, c_k_k, c_k_a, c_r_k,
             c_ln_w, c_ln_b, d_conv_w, d_A_log, d_dt_bias, d_o_norm, ctx_out):
    b = p_lat.shape[0]
    rw = (c_mu_prev, c_mu_next, c_w0, c_w2, c_a0, c_a2, c_g2, c_k_k, c_k_a)
    r_c, v_c, kk_c, g_c, dec_c, a_c, k_c = rwkv_prep(p_ctx[..., :IN_C], *rw)
    r_l, v_l, kk_l, g_l, dec_l, a_l, k_l = rwkv_prep(p_lat[..., :IN_C], *rw)
    n_ctx = p_ctx.shape[1]

    def seq(zc, zl):
        return jnp.concatenate([zc, zl], axis=1).reshape(b, -1, C_DIM)

    y2 = rwkv_chunked(seq(r_c, r_l), seq(v_c, v_l), seq(kk_c, kk_l),
                      jnp.stack([seq(dec_c[i], dec_l[i]) for i in range(2)]),
                      jnp.stack([seq(a_c[i], a_l[i]) for i in range(2)]),
                      jnp.stack([seq(k_c[i], k_l[i]) for i in range(2)]), n_ctx)
    y_all = (y2[0] + y2[1]).reshape(b, -1, C_HEADS, C_HEAD)
    y_c, y_l = y_all[:, :n_ctx], y_all[:, n_ctx:]

    q_c, kd_c, vd_c, z_c, beta_c, gd_c = gdn_prep(p_ctx[..., IN_C:], d_conv_w, d_A_log, d_dt_bias)
    q_l, kd_l, vd_l, z_l, beta_l, gd_l = gdn_prep(p_lat[..., IN_C:], d_conv_w, d_A_log, d_dt_bias)
    h0 = jnp.zeros((b, D_HEADS, D_HEAD_DIM, D_HEAD_DIM), F32)
    o_c = 0.0
    o_l = 0.0
    for i, rev in enumerate((False, True)):
        oc, st = chunk_gated_delta(flip_time(q_c, rev), flip_time(kd_c, rev), flip_time(vd_c, rev),
                                   flip_time(beta_c[i], rev), flip_time(gd_c[i], rev), h0)
        ol, _ = chunk_gated_delta(flip_time(q_l, rev), flip_time(kd_l, rev), flip_time(vd_l, rev),
                                  flip_time(beta_l[i], rev), flip_time(gd_l[i], rev), st)
        o_c = o_c + flip_time(oc, rev)
        o_l = o_l + flip_time(ol, rev)

    y_lat = jnp.concatenate([rwkv_output(y_l, r_l, k_l, v_l, g_l, c_r_k, c_ln_w, c_ln_b),
                             gdn_output(o_l, z_l, d_o_norm)], axis=-1).astype(p_lat.dtype)
    y_ctx = None
    if ctx_out:
        y_ctx = jnp.concatenate([rwkv_output(y_c, r_c, k_c, v_c, g_c, c_r_k, c_ln_w, c_ln_b),
                                 gdn_output(o_c, z_c, d_o_norm)], axis=-1).astype(p_ctx.dtype)
    return y_lat, y_ctx


def conv_ffn(h, w_up, conv_w, conv_b, w_down):
    u = dwconv_centred(_mm(h, w_up), conv_w) + conv_b
    val, gate = jnp.split(u, 2, axis=-1)
    return _mm(jax.nn.silu(gate) * val, w_down)


def kernel(x, c, ctx, c_ctx, ada_w, ada_b, ffn_w_up, ffn_conv_w, ffn_conv_b, ffn_w_down,
           ab_w_in, ab_w_out, a_q_norm, a_k_norm, a_sink, b_cq_norm, b_ckv_norm, b_w_uq, b_w_uk, b_w_uv,
           b_qn_norm, b_qr_norm, b_kn_norm, b_kr_norm, cd_w_in, cd_w_out, c_mu_prev, c_mu_next, c_w0, c_w2,
           c_a0, c_a2, c_g2, c_k_k, c_k_a, c_r_k, c_ln_w, c_ln_b, d_conv_w, d_A_log, d_dt_bias, d_o_norm):
    seq = x.shape[1]
    rows = seq // GRID_W
    row = jnp.repeat(jnp.arange(rows, dtype=jnp.int32), GRID_W)
    col = jnp.tile(jnp.arange(GRID_W, dtype=jnp.int32), rows)
    silu_c = jax.nn.silu(c)
    silu_cc = jax.nn.silu(c_ctx)
    for l in range(DEPTH):
        last = l == DEPTH - 1
        i = l // 2
        mod_l = jnp.split((silu_c @ ada_w[l] + ada_b[l])[:, None, :], N_MOD, axis=-1)
        mod_c = jnp.split(silu_cc @ ada_w[l] + ada_b[l], N_MOD, axis=-1)
        h_l = modulate(x, mod_l[0], mod_l[1])
        h_c = modulate(ctx, mod_c[0], mod_c[1])
        if l % 2 == 0:
            y_l, y_c = mixer_ab(_mm(h_l, ab_w_in[i]), _mm(h_c, ab_w_in[i]), row, col, a_q_norm[i], a_k_norm[i],
                                a_sink[i], b_cq_norm[i], b_ckv_norm[i], b_w_uq[i], b_w_uk[i], b_w_uv[i],
                                b_qn_norm[i], b_qr_norm[i], b_kn_norm[i], b_kr_norm[i], not last)
            w_out = ab_w_out[i]
        else:
            y_l, y_c = mixer_cd(_mm(h_l, cd_w_in[i]), _mm(h_c, cd_w_in[i]), c_mu_prev[i], c_mu_next[i], c_w0[i],
                                c_w2[i], c_a0[i], c_a2[i], c_g2[i], c_k_k[i], c_k_a[i], c_r_k[i], c_ln_w[i],
                                c_ln_b[i], d_conv_w[i], d_A_log[i], d_dt_bias[i], d_o_norm[i], not last)
            w_out = cd_w_out[i]
        x = x + mod_l[2] * _mm(y_l, w_out)
        x = x + mod_l[5] * conv_ffn(modulate(x, mod_l[3], mod_l[4]), ffn_w_up[l], ffn_conv_w[l],
                                    ffn_conv_b[l], ffn_w_down[l])
        if not last:
            ctx = ctx + mod_c[2] * _mm(y_c, w_out)
            ctx = ctx + mod_c[5] * conv_ffn(modulate(ctx, mod_c[3], mod_c[4]), ffn_w_up[l], ffn_conv_w[l],
                                            ffn_conv_b[l], ffn_w_down[l])
    return x
```

```python
import functools

import jax
import jax.numpy as jnp
from jax import lax
import numpy as np
from jax.experimental import pallas as pl
from jax.experimental.pallas import tpu as pltpu

D_MODEL = 1024
DEPTH = 2
GRID_W = 64
N_MOD = 6
EPS = 1e-6
ROPE_THETA = 10000.0
NEG_INF = -1e30

A_HEADS = 8
A_KV_HEADS = 2
A_HEAD_DIM = 64
WINDOW = 128
B_HEADS = 8
B_Q_RANK = 256
B_KV_RANK = 256
B_NOPE = 64
B_ROPE = 32
B_V_DIM = 64
Q_BLOCK = 128
C_HEADS = 8
C_HEAD = 64
C_DIM = C_HEADS * C_HEAD
C_DECAY_LORA = 64
C_AAA_LORA = 64
C_GATE_LORA = 128
C_GN_EPS = 64e-5
D_HEADS = 4
D_HEAD_DIM = 128
D_DIM = D_HEADS * D_HEAD_DIM
D_CONV = 5
D_CHUNK = 64
D_FF = 2816
FFN_CONV = 3

AB_SIZES = (A_HEADS * A_HEAD_DIM, A_KV_HEADS * A_HEAD_DIM, A_KV_HEADS * A_HEAD_DIM, B_Q_RANK, B_KV_RANK, B_ROPE)
C_SIZES = (C_DIM, C_DIM, C_DIM, C_DECAY_LORA, C_DECAY_LORA, C_AAA_LORA, C_AAA_LORA, C_GATE_LORA)
IN_C = sum(C_SIZES)
D_SIZES = (3 * D_DIM, D_DIM, D_HEADS, D_HEADS, D_HEADS, D_HEADS)

F32 = jnp.float32
BF16 = jnp.bfloat16

V7X_VMEM_LIMIT_BYTES = 48 * 1024 * 1024
LANE = 128
MXU_N = 256


def _mm_kernel(a_ref, b_ref, o_ref):
    o_ref[...] = jnp.dot(a_ref[...].astype(BF16), b_ref[...], preferred_element_type=F32)


def _pick_tile(n, candidates):
    for c in candidates:
        if n % c == 0:
            return c
    raise ValueError(f"no tile for {n}")


def _mm(a, w):
    lead = a.shape[:-1]
    k = a.shape[-1]
    n = w.shape[-1]
    a2 = a.reshape(-1, k)
    m = a2.shape[0]
    n_pad = -(-n // MXU_N) * MXU_N
    wb = w.astype(BF16)
    if n_pad != n:
        wb = jnp.pad(wb, ((0, 0), (0, n_pad - n)))
    tm = _pick_tile(m, (1024, 512, 256, 128, 8))
    tn = _pick_tile(n_pad, (1024, 768, 512, 256))
    out = pl.pallas_call(
        _mm_kernel,
        grid=(m // tm, n_pad // tn),
        in_specs=[pl.BlockSpec((tm, k), lambda i, j: (i, 0)),
                  pl.BlockSpec((k, tn), lambda i, j: (0, j))],
        out_specs=pl.BlockSpec((tm, tn), lambda i, j: (i, j)),
        out_shape=jax.ShapeDtypeStruct((m, n_pad), F32),
        compiler_params=pltpu.CompilerParams(
            dimension_semantics=("arbitrary", "arbitrary"),
            vmem_limit_bytes=V7X_VMEM_LIMIT_BYTES),
        name="mm",
    )(a2, wb)
    if n_pad != n:
        out = out[:, :n]
    return out.reshape(*lead, n)


def split_cols(p, sizes):
    return jnp.split(p, [int(s) for s in np.cumsum(sizes)[:-1]], axis=-1)


def rms_norm(x, gain=None, eps=EPS):
    xf = x.astype(F32)
    y = xf * lax.rsqrt(jnp.mean(xf * xf, axis=-1, keepdims=True) + eps)
    if gain is not None:
        y = y * gain.astype(F32)
    return y.astype(x.dtype)


def l2_normalize(x, eps=1e-6):
    xf = x.astype(F32)
    return (xf * lax.rsqrt(jnp.sum(xf * xf, axis=-1, keepdims=True) + eps)).astype(x.dtype)


def modulate(x, shift, scale):
    return rms_norm(x) * (1.0 + scale) + shift


def rope_1d(x, pos):
    half = x.shape[-1] // 2
    inv = jnp.power(ROPE_THETA, -jnp.arange(half, dtype=F32) / half)
    ang = pos.astype(F32)[:, None] * inv[None, :]
    cos = jnp.cos(ang)[None, :, None, :]
    sin = jnp.sin(ang)[None, :, None, :]
    x1, x2 = x[..., :half], x[..., half:]
    return jnp.concatenate([x1 * cos - x2 * sin, x1 * sin + x2 * cos], axis=-1).astype(x.dtype)


def rope_2d(x, row, col):
    h = x.shape[-1] // 2
    return jnp.concatenate([rope_1d(x[..., :h], row), rope_1d(x[..., h:], col)], axis=-1)


def dwconv_centred(x, w):
    k, ch = w.shape
    p = k // 2
    return lax.conv_general_dilated(x, w[:, None, :].astype(x.dtype), (1,), [(p, p)],
                                    dimension_numbers=('NWC', 'WIO', 'NWC'), feature_group_count=ch)


def flip_time(t, rev):
    return jnp.flip(t, axis=1) if rev else t


def window_attention(q, k, v, k_ctx, v_ctx, sink):
    b, s, h, d = q.shape
    kv = k.shape[2]
    g = h // kv
    nb = s // WINDOW
    n_ctx = k_ctx.shape[1]
    qb = q.reshape(b, nb, WINDOW, kv, g, d)

    def band(t):
        tb = t.reshape(b, nb, WINDOW, kv, d)
        zero = jnp.zeros_like(tb[:, :1])
        prev = jnp.concatenate([zero, tb[:, :-1]], axis=1)
        nxt = jnp.concatenate([tb[:, 1:], zero], axis=1)
        return jnp.concatenate([prev, tb, nxt], axis=2)

    kb, vb = band(k), band(v)
    scale = d ** -0.5
    s_loc = jnp.einsum('bnqhgd,bnjhd->bnhgqj', qb, kb).astype(F32) * scale
    blk = jnp.arange(nb)[:, None, None]
    qpos = blk * WINDOW + jnp.arange(WINDOW)[None, :, None]
    kpos = (blk - 1) * WINDOW + jnp.arange(3 * WINDOW)[None, None, :]
    valid = (jnp.abs(qpos - kpos) <= WINDOW) & (kpos >= 0) & (kpos < s)
    s_loc = jnp.where(valid[None, :, None, None], s_loc, NEG_INF)
    s_ctx = jnp.einsum('bnqhgd,bjhd->bnhgqj', qb, k_ctx).astype(F32) * scale
    s_sink = jnp.broadcast_to(sink.astype(F32).reshape(1, 1, kv, g, 1, 1), s_loc.shape[:-1] + (1,))
    p = jax.nn.softmax(jnp.concatenate([s_loc, s_ctx, s_sink], axis=-1), axis=-1)
    p_loc = p[..., :3 * WINDOW].astype(v.dtype)
    p_ctx = p[..., 3 * WINDOW:3 * WINDOW + n_ctx].astype(v.dtype)
    o = jnp.einsum('bnhgqj,bnjhd->bnqhgd', p_loc, vb) + jnp.einsum('bnhgqj,bjhd->bnqhgd', p_ctx, v_ctx)
    return o.reshape(b, s, h * d)


def context_gqa(q, k, v, sink):
    b, n, h, d = q.shape
    kv = k.shape[2]
    g = h // kv
    qg = q.reshape(b, n, kv, g, d)
    s = jnp.einsum('bqhgd,bjhd->bhgqj', qg, k).astype(F32) * d ** -0.5
    s_sink = jnp.broadcast_to(sink.astype(F32).reshape(1, kv, g, 1, 1), s.shape[:-1] + (1,))
    p = jax.nn.softmax(jnp.concatenate([s, s_sink], axis=-1), axis=-1)[..., :n].astype(v.dtype)
    return jnp.einsum('bhgqj,bjhd->bqhgd', p, v).reshape(b, n, h * d)


def mla_attend(qn, qr, kn, kr, v):
    s = (jnp.einsum('bqhd,bkhd->bhqk', qn, kn) + jnp.einsum('bqhd,bkd->bhqk', qr, kr)).astype(F32)
    p = jax.nn.softmax(s * (B_NOPE + B_ROPE) ** -0.5, axis=-1).astype(v.dtype)
    return jnp.einsum('bhqk,bkhd->bqhd', p, v)


def mla_latent(qn, qr, kn_all, kr_all, v_all):
    b, s = qn.shape[:2]
    nq = s // Q_BLOCK

    def blocks(t):
        return jnp.moveaxis(t.reshape(b, nq, Q_BLOCK, *t.shape[2:]), 1, 0)

    o = lax.map(lambda qs: mla_attend(qs[0], qs[1], kn_all, kr_all, v_all), (blocks(qn), blocks(qr)))
    return jnp.moveaxis(o, 0, 1).reshape(b, s, -1)


def ab_prep(p, a_q_norm, a_k_norm, b_cq_norm, b_ckv_norm, b_w_uq, b_w_uk, b_w_uv,
            b_qn_norm, b_qr_norm, b_kn_norm, b_kr_norm):
    b, t = p.shape[:2]
    qa, ka, va, cq, ckv, kr = split_cols(p, AB_SIZES)
    qa = rms_norm(qa.reshape(b, t, A_HEADS, A_HEAD_DIM), a_q_norm)
    ka = rms_norm(ka.reshape(b, t, A_KV_HEADS, A_HEAD_DIM), a_k_norm)
    va = va.reshape(b, t, A_KV_HEADS, A_HEAD_DIM)
    qb = _mm(rms_norm(cq, b_cq_norm), b_w_uq).reshape(b, t, B_HEADS, B_NOPE + B_ROPE)
    qn = rms_norm(qb[..., :B_NOPE], b_qn_norm)
    qr = rms_norm(qb[..., B_NOPE:], b_qr_norm)
    ckv = rms_norm(ckv, b_ckv_norm)
    kn = rms_norm(_mm(ckv, b_w_uk).reshape(b, t, B_HEADS, B_NOPE), b_kn_norm)
    vb = _mm(ckv, b_w_uv).reshape(b, t, B_HEADS, B_V_DIM)
    kr = rms_norm(kr, b_kr_norm)
    return qa, ka, va, qn, qr, kn, kr, vb


def mixer_ab(p_lat, p_ctx, row, col, a_q_norm, a_k_norm, a_sink, b_cq_norm, b_ckv_norm, b_w_uq, b_w_uk,
             b_w_uv, b_qn_norm, b_qr_norm, b_kn_norm, b_kr_norm, ctx_out):
    prm = (a_q_norm, a_k_norm, b_cq_norm, b_ckv_norm, b_w_uq, b_w_uk, b_w_uv,
           b_qn_norm, b_qr_norm, b_kn_norm, b_kr_norm)
    qa_c, ka_c, va_c, qn_c, qr_c, kn_c, kr_c, vb_c = ab_prep(p_ctx, *prm)
    qa_l, ka_l, va_l, qn_l, qr_l, kn_l, kr_l, vb_l = ab_prep(p_lat, *prm)
    qa_l = rope_2d(qa_l, row, col)
    ka_l = rope_2d(ka_l, row, col)
    qr_l = rope_2d(qr_l, row, col)
    kr_l = rope_2d(kr_l[:, :, None, :], row, col)[:, :, 0, :]
    o_a = window_attention(qa_l, ka_l, va_l, ka_c, va_c, a_sink)
    o_b = mla_latent(qn_l, qr_l, jnp.concatenate([kn_c, kn_l], axis=1),
                     jnp.concatenate([kr_c, kr_l], axis=1), jnp.concatenate([vb_c, vb_l], axis=1))
    y_lat = jnp.concatenate([o_a, o_b], axis=-1)
    y_ctx = None
    if ctx_out:
        b, n = p_ctx.shape[:2]
        y_ctx = jnp.concatenate([context_gqa(qa_c, ka_c, va_c, a_sink),
                                 mla_attend(qn_c, qr_c, kn_c, kr_c, vb_c).reshape(b, n, -1)], axis=-1)
    return y_lat, y_ctx


NN = ((1,), (0,))
NT = ((1,), (1,))
TN = ((0,), (0,))
RW_CHUNK = 64
HEAD_PAIR = 2 * C_HEAD


def _dot(a, b, dims, passes=1):
    def dg(x, y):
        return lax.dot_general(x, y, (dims, ((), ())), preferred_element_type=F32)

    ah, bh = a.astype(BF16), b.astype(BF16)
    if passes == 1:
        return dg(ah, bh)
    al = (a - ah.astype(F32)).astype(BF16)
    bl = (b - bh.astype(F32)).astype(BF16)
    return dg(ah, bh) + (dg(ah, bl) + dg(al, bh))


def _dot_exact(a, b, dims):
    return lax.dot_general(a, b, (dims, ((), ())), precision=lax.Precision.HIGHEST, preferred_element_type=F32)


def _unit_lower_inverse(x, eye):
    p = eye + x
    xp = x
    for _ in range(int(np.log2(RW_CHUNK)) - 1):
        xp = _dot(xp, xp, NN, passes=3)
        p = p + _dot(p, xp, NN, passes=3)
    return p


def _rwkv_kernel(r_ref, v_ref, kk_ref, lw_ref, a_ref, k_ref, y_ref, st_ref):
    d = pl.program_id(1)
    s = pl.program_id(2)

    @pl.when(s == 0)
    def _():
        st_ref[...] = jnp.zeros_like(st_ref)

    c_len = RW_CHUNK
    sign = 1 - 2 * d
    ii = lax.broadcasted_iota(jnp.int32, (c_len, c_len), 0)
    jj = lax.broadcasted_iota(jnp.int32, (c_len, c_len), 1)
    tri = jnp.where((ii - jj) * sign >= 0, 1.0, 0.0).astype(F32)
    lw_all = lw_ref[0, 0]
    cum_all = _dot_exact(tri, lw_all, NN)
    tot_all = jnp.sum(lw_all, axis=0, keepdims=True)

    r2 = lax.broadcasted_iota(jnp.int32, (HEAD_PAIR, HEAD_PAIR), 0)
    c2 = lax.broadcasted_iota(jnp.int32, (HEAD_PAIR, HEAD_PAIR), 1)
    dlt = ((r2 % c_len) - (c2 % c_len)) * sign
    m_strict = dlt > 0
    m_incl = dlt >= 0
    eye = jnp.where(r2 == c2, 1.0, 0.0).astype(F32)
    lane = lax.broadcasted_iota(jnp.int32, (c_len, HEAD_PAIR), 1)
    m0 = lane < C_HEAD

    def stack_heads(x):
        return jnp.concatenate([jnp.where(m0, x, 0.0), jnp.where(m0, 0.0, x)], axis=0)

    for p in range(C_DIM // HEAD_PAIR):
        sl = slice(p * HEAD_PAIR, (p + 1) * HEAD_PAIR)
        lw = lw_all[:, sl]
        cum = cum_all[:, sl]
        tot = tot_all[:, sl]
        kk = kk_ref[0, :, sl]
        kd = k_ref[0, 0, :, sl]
        bb = kk * a_ref[0, 0, :, sl]
        e_neg = jnp.exp(-cum)
        e_end = jnp.exp(tot - cum)
        abar = stack_heads(-kk * jnp.exp(cum - lw))
        rbar = stack_heads(r_ref[0, :, sl] * jnp.exp(cum))
        ktil = stack_heads(kd * e_neg)
        btil = stack_heads(bb * e_neg)
        khat = stack_heads(kd * e_end)
        bhat = stack_heads(bb * e_end)
        vs = stack_heads(v_ref[0, :, sl])

        g = _dot(jnp.concatenate([abar, rbar], axis=0), jnp.concatenate([ktil, btil], axis=0), NT)
        a_ak = jnp.where(m_strict, g[:HEAD_PAIR, :HEAD_PAIR], 0.0)
        a_ab = jnp.where(m_strict, g[:HEAD_PAIR, HEAD_PAIR:], 0.0)
        a_rk = jnp.where(m_incl, g[HEAD_PAIR:, :HEAD_PAIR], 0.0)
        a_rb = jnp.where(m_incl, g[HEAD_PAIR:, HEAD_PAIR:], 0.0)
        tinv = _unit_lower_inverse(a_ab, eye)
        wm = _dot(tinv, abar, NN)
        u0 = _dot(tinv, _dot(a_ak, vs, NN), NN)

        st = st_ref[p]
        u = _dot(wm, st, NT) + u0
        ys = _dot(rbar, st, NT) + _dot(a_rk, vs, NN) + _dot(a_rb, u, NN)
        y_ref[0, 0, :, sl] = ys[:c_len] + ys[c_len:]
        st_ref[p] = st * jnp.exp(tot) + _dot(vs, khat, TN) + _dot(u, bhat, TN)


def rwkv_chunked(r, v, kk, lw2, a2, k2, n_ctx, interpret=False):
    b, t, cd = r.shape
    nc = t // RW_CHUNK
    nctx = n_ctx // RW_CHUNK

    def chunk(d, s):
        rev = jnp.where(s < nctx, nctx - 1 - s, nc + nctx - 1 - s)
        return jnp.where(d == 0, s, rev)

    shared = pl.BlockSpec((1, RW_CHUNK, cd), lambda i, d, s: (i, chunk(d, s), 0))
    per_dir = pl.BlockSpec((1, 1, RW_CHUNK, cd), lambda i, d, s: (d, i, chunk(d, s), 0))
    return pl.pallas_call(
        _rwkv_kernel,
        grid=(b, 2, nc),
        in_specs=[shared, shared, shared, per_dir, per_dir, per_dir],
        out_specs=per_dir,
        out_shape=jax.ShapeDtypeStruct((2, b, t, cd), F32),
        scratch_shapes=[pltpu.VMEM((cd // HEAD_PAIR, HEAD_PAIR, HEAD_PAIR), F32)],
        compiler_params=pltpu.CompilerParams(
            dimension_semantics=("arbitrary", "arbitrary", "arbitrary"),
            vmem_limit_bytes=V7X_VMEM_LIMIT_BYTES),
        name="rwkv7_chunked",
        interpret=interpret,
    )(r, v, kk, lw2, a2, k2)


def token_shift_centred(p, mu_prev, mu_next):
    prev = jnp.pad(p, ((0, 0), (1, 0), (0, 0)))[:, :-1]
    nxt = jnp.pad(p, ((0, 0), (0, 1), (0, 0)))[:, 1:]
    return p + mu_prev * (prev - p) + mu_next * (nxt - p)


def rwkv_prep(pc, c_mu_prev, c_mu_next, c_w0, c_w2, c_a0, c_a2, c_g2, c_k_k, c_k_a):
    b, t = pc.shape[:2]

    def heads(z):
        return z.reshape(b, t, C_HEADS, C_HEAD)

    xs = token_shift_centred(pc, c_mu_prev, c_mu_next)
    r, k, v, wl_f, wl_b, al_f, al_b, gl = split_cols(xs, C_SIZES)

    def decay(wl, w0, w2):
        w = -jax.nn.softplus(-(w0 + _mm(jnp.tanh(wl), w2))) - 0.5
        return heads(-jnp.exp(w.astype(F32)))

    decays = (decay(wl_f, c_w0[0], c_w2[0]), decay(wl_b, c_w0[1], c_w2[1]))
    iclr = (jax.nn.sigmoid(c_a0[0] + _mm(al_f, c_a2[0])), jax.nn.sigmoid(c_a0[1] + _mm(al_b, c_a2[1])))
    g = _mm(jax.nn.sigmoid(gl), c_g2)
    kk = l2_normalize(heads(k * c_k_k))
    ks = tuple(heads(k * (1.0 + (a - 1.0) * c_k_a)) for a in iclr)
    return heads(r), heads(v), kk, g, decays, tuple(heads(a) for a in iclr), ks


def head_group_norm(y, w, b_):
    mean = jnp.mean(y, axis=-1, keepdims=True)
    var = jnp.mean(jnp.square(y - mean), axis=-1, keepdims=True)
    yn = (y - mean) * lax.rsqrt(var + C_GN_EPS)
    return yn * w.astype(F32).reshape(C_HEADS, C_HEAD) + b_.astype(F32).reshape(C_HEADS, C_HEAD)


def rwkv_output(y, r, ks, v, g, c_r_k, c_ln_w, c_ln_b):
    b, t = y.shape[:2]
    yn = head_group_norm(y, c_ln_w, c_ln_b)
    bonus = sum(jnp.sum(r * kd * c_r_k, axis=-1, keepdims=True) * v for kd in ks)
    return (yn + bonus).reshape(b, t, C_DIM) * g


def gdn_prep(pd, d_conv_w, d_A_log, d_dt_bias):
    b, t = pd.shape[:2]
    qkv, z, bf, bb, af, ab = split_cols(pd, D_SIZES)
    qkv = jax.nn.silu(dwconv_centred(qkv, d_conv_w))
    q, k, v = jnp.split(qkv, 3, axis=-1)
    q = l2_normalize(q.reshape(b, t, D_HEADS, D_HEAD_DIM)) * D_HEAD_DIM ** -0.5
    k = l2_normalize(k.reshape(b, t, D_HEADS, D_HEAD_DIM))
    v = v.reshape(b, t, D_HEADS, D_HEAD_DIM)
    betas = (jax.nn.sigmoid(bf), jax.nn.sigmoid(bb))
    gs = tuple(-jnp.exp(d_A_log[i].astype(F32)) * jax.nn.softplus((al + d_dt_bias[i]).astype(F32))
               for i, al in enumerate((af, ab)))
    return q, k, v, z, betas, gs


def _gdn_kernel(q_ref, k_ref, v_ref, beta_ref, g_ref, gt_ref, o_ref, st_ref):
    d = pl.program_id(1)
    s = pl.program_id(2)

    @pl.when(s == 0)
    def _():
        st_ref[...] = jnp.zeros_like(st_ref)

    c_len = D_CHUNK
    n2 = 2 * c_len
    sign = 1 - 2 * d
    r2 = lax.broadcasted_iota(jnp.int32, (n2, n2), 0)
    c2 = lax.broadcasted_iota(jnp.int32, (n2, n2), 1)
    same = (r2 // c_len) == (c2 // c_len)
    dlt = ((r2 % c_len) - (c2 % c_len)) * sign
    m_strict = same & (dlt > 0)
    m_incl = same & (dlt >= 0)
    tri = jnp.where(m_incl, 1.0, 0.0).astype(F32)
    ones_bd = jnp.where(same, 1.0, 0.0).astype(F32)
    eye = jnp.where(r2 == c2, 1.0, 0.0).astype(F32)

    for p in range(D_HEADS // 2):
        h0, h1 = 2 * p, 2 * p + 1

        def rows(ref):
            blk = ref[0]
            return jnp.concatenate([blk[:, h0 * D_HEAD_DIM:(h0 + 1) * D_HEAD_DIM],
                                    blk[:, h1 * D_HEAD_DIM:(h1 + 1) * D_HEAD_DIM]], axis=0)

        q_s = rows(q_ref)
        k_s = rows(k_ref)
        v_s = rows(v_ref)
        g_blk = g_ref[0, 0, 0]
        b_blk = beta_ref[0, 0, 0]
        gt_blk = gt_ref[0, 0, 0]
        g_col = jnp.concatenate([g_blk[:, h0:h0 + 1], g_blk[:, h1:h1 + 1]], axis=0)
        b_col = jnp.concatenate([b_blk[:, h0:h0 + 1], b_blk[:, h1:h1 + 1]], axis=0)
        g_row = jnp.concatenate([gt_blk[h0:h0 + 1, :], gt_blk[h1:h1 + 1, :]], axis=1)
        g_colb = jnp.broadcast_to(g_col, (n2, n2))
        g_rowb = jnp.broadcast_to(g_row, (n2, n2))
        gc_col = _dot_exact(tri, g_colb, NN)
        gc_row = _dot_exact(g_rowb, tri, NT)
        g_end = _dot_exact(ones_bd, g_colb, NN)
        decay = jnp.where(m_incl, jnp.exp(jnp.where(m_incl, gc_col - gc_row, 0.0)), 0.0)
        beta = jnp.broadcast_to(b_col, (n2, n2))
        kb = k_s * beta
        vb = v_s * beta

        g_mat = _dot(jnp.concatenate([kb, q_s], axis=0), k_s, NT)
        l_mat = jnp.where(m_strict, g_mat[:n2] * decay, 0.0)
        a_intra = g_mat[n2:] * decay
        tinv = _unit_lower_inverse(-l_mat, eye)
        u = _dot(tinv, vb, NN)
        wk = _dot(tinv, kb * jnp.exp(gc_col), NN)
        q_e = q_s * jnp.exp(gc_col)
        k_e = k_s * jnp.exp(g_end - gc_col)

        v_new = []
        o_st = []
        for j, h in enumerate((h0, h1)):
            rs = slice(j * c_len, (j + 1) * c_len)
            st = st_ref[h]
            v_new.append(u[rs] - _dot(wk[rs], st, NN))
            o_st.append(_dot(q_e[rs], st, NN))
        v_new_s = jnp.concatenate(v_new, axis=0)
        o_s = jnp.concatenate(o_st, axis=0) + _dot(a_intra, v_new_s, NN)
        for j, h in enumerate((h0, h1)):
            rs = slice(j * c_len, (j + 1) * c_len)
            o_ref[0, 0, :, h * D_HEAD_DIM:(h + 1) * D_HEAD_DIM] = o_s[rs]
            st_ref[h] = st_ref[h] * jnp.exp(g_end[j * c_len:j * c_len + 1, :]) + _dot(k_e[rs], v_new_s[rs], TN)


def gdn_chunked(q, k, v, beta2, g2, n_ctx, interpret=False):
    b, t, cd = q.shape
    nc = t // D_CHUNK
    nctx = n_ctx // D_CHUNK
    g2c = g2.reshape(2, b, nc, D_CHUNK, D_HEADS)
    gt2 = jnp.swapaxes(g2c, -1, -2)
    beta2c = beta2.reshape(2, b, nc, D_CHUNK, D_HEADS)

    def chunk(d, s):
        rev = jnp.where(s < nctx, nctx - 1 - s, nc + nctx - 1 - s)
        return jnp.where(d == 0, s, rev)

    shared = pl.BlockSpec((1, D_CHUNK, cd), lambda i, d, s: (i, chunk(d, s), 0))
    per_dir = pl.BlockSpec((1, 1, D_CHUNK, cd), lambda i, d, s: (d, i, chunk(d, s), 0))
    small = pl.BlockSpec((1, 1, 1, D_CHUNK, D_HEADS), lambda i, d, s: (d, i, chunk(d, s), 0, 0))
    small_t = pl.BlockSpec((1, 1, 1, D_HEADS, D_CHUNK), lambda i, d, s: (d, i, chunk(d, s), 0, 0))
    return pl.pallas_call(
        _gdn_kernel,
        grid=(b, 2, nc),
        in_specs=[shared, shared, shared, small, small, small_t],
        out_specs=per_dir,
        out_shape=jax.ShapeDtypeStruct((2, b, t, cd), F32),
        scratch_shapes=[pltpu.VMEM((D_HEADS, D_HEAD_DIM, D_HEAD_DIM), F32)],
        compiler_params=pltpu.CompilerParams(
            dimension_semantics=("arbitrary", "arbitrary", "arbitrary"),
            vmem_limit_bytes=V7X_VMEM_LIMIT_BYTES),
        name="gdn_chunked",
        interpret=interpret,
    )(q, k, v, beta2c, g2c, gt2)


def gdn_output(o, z, d_o_norm):
    b, t = o.shape[:2]
    gate = jax.nn.silu(z.reshape(b, t, D_HEADS, D_HEAD_DIM).astype(F32))
    return (rms_norm(o, d_o_norm) * gate).reshape(b, t, D_DIM)


def mixer_cd(p_lat, p_ctx, c_mu_prev, c_mu_next, c_w0, c_w2, c_a0, c_a2, c_g2, c_k_k, c_k_a, c_r_k,
             c_ln_w, c_ln_b, d_conv_w, d_A_log, d_dt_bias, d_o_norm, ctx_out):
    b = p_lat.shape[0]
    rw = (c_mu_prev, c_mu_next, c_w0, c_w2, c_a0, c_a2, c_g2, c_k_k, c_k_a)
    r_c, v_c, kk_c, g_c, dec_c, a_c, k_c = rwkv_prep(p_ctx[..., :IN_C], *rw)
    r_l, v_l, kk_l, g_l, dec_l, a_l, k_l = rwkv_prep(p_lat[..., :IN_C], *rw)
    n_ctx = p_ctx.shape[1]

    def seq(zc, zl):
        return jnp.concatenate([zc, zl], axis=1).reshape(b, -1, C_DIM)

    y2 = rwkv_chunked(seq(r_c, r_l), seq(v_c, v_l), seq(kk_c, kk_l),
                      jnp.stack([seq(dec_c[i], dec_l[i]) for i in range(2)]),
                      jnp.stack([seq(a_c[i], a_l[i]) for i in range(2)]),
                      jnp.stack([seq(k_c[i], k_l[i]) for i in range(2)]), n_ctx)
    y_all = (y2[0] + y2[1]).reshape(b, -1, C_HEADS, C_HEAD)
    y_c, y_l = y_all[:, :n_ctx], y_all[:, n_ctx:]

    q_c, kd_c, vd_c, z_c, beta_c, gd_c = gdn_prep(p_ctx[..., IN_C:], d_conv_w, d_A_log, d_dt_bias)
    q_l, kd_l, vd_l, z_l, beta_l, gd_l = gdn_prep(p_lat[..., IN_C:], d_conv_w, d_A_log, d_dt_bias)

    def seq_d(zc, zl):
        return jnp.concatenate([zc, zl], axis=1).reshape(b, zc.shape[1] + zl.shape[1], -1)

    o2 = gdn_chunked(seq_d(q_c, q_l), seq_d(kd_c, kd_l), seq_d(vd_c, vd_l),
                     jnp.stack([seq_d(beta_c[i], beta_l[i]) for i in range(2)]),
                     jnp.stack([seq_d(gd_c[i], gd_l[i]) for i in range(2)]), n_ctx)
    o_all = (o2[0] + o2[1]).reshape(b, -1, D_HEADS, D_HEAD_DIM)
    o_c, o_l = o_all[:, :n_ctx], o_all[:, n_ctx:]

    y_lat = jnp.concatenate([rwkv_output(y_l, r_l, k_l, v_l, g_l, c_r_k, c_ln_w, c_ln_b),
                             gdn_output(o_l, z_l, d_o_norm)], axis=-1).astype(p_lat.dtype)
    y_ctx = None
    if ctx_out:
        y_ctx = jnp.concatenate([rwkv_output(y_c, r_c, k_c, v_c, g_c, c_r_k, c_ln_w, c_ln_b),
                                 gdn_output(o_c, z_c, d_o_norm)], axis=-1).astype(p_ctx.dtype)
    return y_lat, y_ctx


def conv_ffn(h, w_up, conv_w, conv_b, w_down):
    u = dwconv_centred(_mm(h, w_up), conv_w) + conv_b
    val, gate = jnp.split(u, 2, axis=-1)
    return _mm(jax.nn.silu(gate) * val, w_down)


def kernel(x, c, ctx, c_ctx, ada_w, ada_b, ffn_w_up, ffn_conv_w, ffn_conv_b, ffn_w_down,
           ab_w_in, ab_w_out, a_q_norm, a_k_norm, a_sink, b_cq_norm, b_ckv_norm, b_w_uq, b_w_uk, b_w_uv,
           b_qn_norm, b_qr_norm, b_kn_norm, b_kr_norm, cd_w_in, cd_w_out, c_mu_prev, c_mu_next, c_w0, c_w2,
           c_a0, c_a2, c_g2, c_k_k, c_k_a, c_r_k, c_ln_w, c_ln_b, d_conv_w, d_A_log, d_dt_bias, d_o_norm):
    seq = x.shape[1]
    rows = seq // GRID_W
    row = jnp.repeat(jnp.arange(rows, dtype=jnp.int32), GRID_W)
    col = jnp.tile(jnp.arange(GRID_W, dtype=jnp.int32), rows)
    silu_c = jax.nn.silu(c)
    silu_cc = jax.nn.silu(c_ctx)
    for l in range(DEPTH):
        last = l == DEPTH - 1
        i = l // 2
        mod_l = jnp.split((silu_c @ ada_w[l] + ada_b[l])[:, None, :], N_MOD, axis=-1)
        mod_c = jnp.split(silu_cc @ ada_w[l] + ada_b[l], N_MOD, axis=-1)
        h_l = modulate(x, mod_l[0], mod_l[1])
        h_c = modulate(ctx, mod_c[0], mod_c[1])
        if l % 2 == 0:
            y_l, y_c = mixer_ab(_mm(h_l, ab_w_in[i]), _mm(h_c, ab_w_in[i]), row, col, a_q_norm[i], a_k_norm[i],
                                a_sink[i], b_cq_norm[i], b_ckv_norm[i], b_w_uq[i], b_w_uk[i], b_w_uv[i],
                                b_qn_norm[i], b_qr_norm[i], b_kn_norm[i], b_kr_norm[i], not last)
            w_out = ab_w_out[i]
        else:
            y_l, y_c = mixer_cd(_mm(h_l, cd_w_in[i]), _mm(h_c, cd_w_in[i]), c_mu_prev[i], c_mu_next[i], c_w0[i],
                                c_w2[i], c_a0[i], c_a2[i], c_g2[i], c_k_k[i], c_k_a[i], c_r_k[i], c_ln_w[i],
                                c_ln_b[i], d_conv_w[i], d_A_log[i], d_dt_bias[i], d_o_norm[i], not last)
            w_out = cd_w_out[i]
        x = x + mod_l[2] * _mm(y_l, w_out)
        x = x + mod_l[5] * conv_ffn(modulate(x, mod_l[3], mod_l[4]), ffn_w_up[l], ffn_conv_w[l],
                                    ffn_conv_b[l], ffn_w_down[l])
        if not last:
            ctx = ctx + mod_c[2] * _mm(y_c, w_out)
            ctx = ctx + mod_c[5] * conv_ffn(modulate(ctx, mod_c[3], mod_c[4]), ffn_w_up[l], ffn_conv_w[l],
                                            ffn_conv_b[l], ffn_w_down[l])
    return x
```

```python
import functools

import jax
import jax.numpy as jnp
from jax import lax
import numpy as np
from jax.experimental import pallas as pl
from jax.experimental.pallas import tpu as pltpu

D_MODEL = 1024
DEPTH = 2
GRID_W = 64
N_MOD = 6
EPS = 1e-6
ROPE_THETA = 10000.0
NEG_INF = -1e30

A_HEADS = 8
A_KV_HEADS = 2
A_HEAD_DIM = 64
WINDOW = 128
B_HEADS = 8
B_Q_RANK = 256
B_KV_RANK = 256
B_NOPE = 64
B_ROPE = 32
B_V_DIM = 64
Q_BLOCK = 128
C_HEADS = 8
C_HEAD = 64
C_DIM = C_HEADS * C_HEAD
C_DECAY_LORA = 64
C_AAA_LORA = 64
C_GATE_LORA = 128
C_GN_EPS = 64e-5
D_HEADS = 4
D_HEAD_DIM = 128
D_DIM = D_HEADS * D_HEAD_DIM
D_CONV = 5
D_CHUNK = 64
D_FF = 2816
FFN_CONV = 3

AB_SIZES = (A_HEADS * A_HEAD_DIM, A_KV_HEADS * A_HEAD_DIM, A_KV_HEADS * A_HEAD_DIM, B_Q_RANK, B_KV_RANK, B_ROPE)
C_SIZES = (C_DIM, C_DIM, C_DIM, C_DECAY_LORA, C_DECAY_LORA, C_AAA_LORA, C_AAA_LORA, C_GATE_LORA)
IN_C = sum(C_SIZES)
D_SIZES = (3 * D_DIM, D_DIM, D_HEADS, D_HEADS, D_HEADS, D_HEADS)

F32 = jnp.float32
BF16 = jnp.bfloat16

V7X_VMEM_LIMIT_BYTES = 48 * 1024 * 1024
LANE = 128
MXU_N = 256


def _mm_kernel(a_ref, b_ref, o_ref):
    o_ref[...] = jnp.dot(a_ref[...].astype(BF16), b_ref[...], preferred_element_type=F32)


def _pick_tile(n, candidates):
    for c in candidates:
        if n % c == 0:
            return c
    raise ValueError(f"no tile for {n}")


def _mm(a, w):
    lead = a.shape[:-1]
    k = a.shape[-1]
    n = w.shape[-1]
    a2 = a.reshape(-1, k)
    m = a2.shape[0]
    n_pad = -(-n // MXU_N) * MXU_N
    wb = w.astype(BF16)
    if n_pad != n:
        wb = jnp.pad(wb, ((0, 0), (0, n_pad - n)))
    tm = _pick_tile(m, (1024, 512, 256, 128, 8))
    tn = _pick_tile(n_pad, (1024, 768, 512, 256))
    out = pl.pallas_call(
        _mm_kernel,
        grid=(m // tm, n_pad // tn),
        in_specs=[pl.BlockSpec((tm, k), lambda i, j: (i, 0)),
                  pl.BlockSpec((k, tn), lambda i, j: (0, j))],
        out_specs=pl.BlockSpec((tm, tn), lambda i, j: (i, j)),
        out_shape=jax.ShapeDtypeStruct((m, n_pad), F32),
        compiler_params=pltpu.CompilerParams(
            dimension_semantics=("arbitrary", "arbitrary"),
            vmem_limit_bytes=V7X_VMEM_LIMIT_BYTES),
        name="mm",
    )(a2, wb)
    if n_pad != n:
        out = out[:, :n]
    return out.reshape(*lead, n)


def split_cols(p, sizes):
    return jnp.split(p, [int(s) for s in np.cumsum(sizes)[:-1]], axis=-1)


def rms_norm(x, gain=None, eps=EPS):
    xf = x.astype(F32)
    y = xf * lax.rsqrt(jnp.mean(xf * xf, axis=-1, keepdims=True) + eps)
    if gain is not None:
        y = y * gain.astype(F32)
    return y.astype(x.dtype)


def l2_normalize(x, eps=1e-6):
    xf = x.astype(F32)
    return (xf * lax.rsqrt(jnp.sum(xf * xf, axis=-1, keepdims=True) + eps)).astype(x.dtype)


def modulate(x, shift, scale):
    return rms_norm(x) * (1.0 + scale) + shift


def rope_1d(x, pos):
    half = x.shape[-1] // 2
    inv = jnp.power(ROPE_THETA, -jnp.arange(half, dtype=F32) / half)
    ang = pos.astype(F32)[:, None] * inv[None, :]
    cos = jnp.cos(ang)[None, :, None, :]
    sin = jnp.sin(ang)[None, :, None, :]
    x1, x2 = x[..., :half], x[..., half:]
    return jnp.concatenate([x1 * cos - x2 * sin, x1 * sin + x2 * cos], axis=-1).astype(x.dtype)


def rope_2d(x, row, col):
    h = x.shape[-1] // 2
    return jnp.concatenate([rope_1d(x[..., :h], row), rope_1d(x[..., h:], col)], axis=-1)


def dwconv_centred(x, w):
    k, ch = w.shape
    p = k // 2
    return lax.conv_general_dilated(x, w[:, None, :].astype(x.dtype), (1,), [(p, p)],
                                    dimension_numbers=('NWC', 'WIO', 'NWC'), feature_group_count=ch)


def flip_time(t, rev):
    return jnp.flip(t, axis=1) if rev else t


NN = ((1,), (0,))
NT = ((1,), (1,))
TN = ((0,), (0,))


def _dot(a, b, dims, passes=1):
    def dg(x, y):
        return lax.dot_general(x, y, (dims, ((), ())), preferred_element_type=F32)

    ah, bh = a.astype(BF16), b.astype(BF16)
    if passes == 1:
        return dg(ah, bh)
    al = (a - ah.astype(F32)).astype(BF16)
    bl = (b - bh.astype(F32)).astype(BF16)
    return dg(ah, bh) + (dg(ah, bl) + dg(al, bh))


def _dot_exact(a, b, dims):
    return lax.dot_general(a, b, (dims, ((), ())), precision=lax.Precision.HIGHEST, preferred_element_type=F32)


def _softmax_pv(s, v, sink):
    m = jnp.max(s, axis=-1, keepdims=True)
    if sink is not None:
        m = jnp.maximum(m, sink)
    p = jnp.exp(s - m)
    den = jnp.sum(p, axis=-1, keepdims=True)
    if sink is not None:
        den = den + jnp.exp(sink - m)
    return _dot(p, v, NN) / den


def _attn_full_kernel(q_ref, k_ref, v_ref, sink_ref, o_ref, *, scale, use_sink):
    s = _dot(q_ref[0, 0], k_ref[0, 0], NT) * scale
    sink = sink_ref[0] if use_sink else None
    o_ref[0, 0] = _softmax_pv(s, v_ref[0, 0], sink)


def attn_full(q, k, v, scale, sink=None):
    b, h, sq, d = q.shape
    hk, sk = k.shape[1], k.shape[2]
    dv = v.shape[-1]
    g = h // hk
    tq = min(sq, 256)
    use_sink = sink is not None
    sink_arr = (sink if use_sink else jnp.zeros((h,), F32)).astype(F32).reshape(h, 1, 1)
    return pl.pallas_call(
        functools.partial(_attn_full_kernel, scale=scale, use_sink=use_sink),
        grid=(b, h, sq // tq),
        in_specs=[pl.BlockSpec((1, 1, tq, d), lambda i, j, t: (i, j, t, 0)),
                  pl.BlockSpec((1, 1, sk, d), lambda i, j, t: (i, j // g, 0, 0)),
                  pl.BlockSpec((1, 1, sk, dv), lambda i, j, t: (i, j // g, 0, 0)),
                  pl.BlockSpec((1, 1, 1), lambda i, j, t: (j, 0, 0))],
        out_specs=pl.BlockSpec((1, 1, tq, dv), lambda i, j, t: (i, j, t, 0)),
        out_shape=jax.ShapeDtypeStruct((b, h, sq, dv), F32),
        compiler_params=pltpu.CompilerParams(
            dimension_semantics=("arbitrary", "arbitrary", "arbitrary"),
            vmem_limit_bytes=V7X_VMEM_LIMIT_BYTES),
        name="attn_full",
    )(q, k, v, sink_arr)


def _attn_window_kernel(q_ref, kp_ref, k0_ref, kn_ref, kc_ref, vp_ref, v0_ref, vn_ref, vc_ref, sink_ref, o_ref,
                        *, scale, group, n_blocks):
    n = pl.program_id(2)
    w = WINDOW
    d = q_ref.shape[-1]
    q = q_ref[0].reshape(group * w, d)
    keys = jnp.concatenate([kp_ref[0, 0], k0_ref[0, 0], kn_ref[0, 0], kc_ref[0, 0]], axis=0)
    vals = jnp.concatenate([vp_ref[0, 0], v0_ref[0, 0], vn_ref[0, 0], vc_ref[0, 0]], axis=0)
    s = _dot(q, keys, NT) * scale
    nk = keys.shape[0]
    qi = lax.broadcasted_iota(jnp.int32, (group * w, nk), 0) % w
    kj = lax.broadcasted_iota(jnp.int32, (group * w, nk), 1)
    rel = qi + w - kj
    band_ok = (jnp.abs(rel) <= w) & ((kj >= w) | (n > 0)) & ((kj < 2 * w) | (n < n_blocks - 1))
    s = jnp.where((kj >= 3 * w) | band_ok, s, NEG_INF)
    sink = jnp.concatenate([jnp.broadcast_to(sink_ref[0, hh], (w, 1)) for hh in range(group)], axis=0)
    o = _softmax_pv(s, vals, sink)
    o_ref[0] = o.reshape(group, w, o.shape[-1])


def attn_window(q, k, v, k_ctx, v_ctx, sink):
    b, h, s, d = q.shape
    hk = k.shape[1]
    g = h // hk
    nb = s // WINDOW
    lc = k_ctx.shape[2]
    sink_arr = sink.astype(F32).reshape(hk, g, 1, 1)

    def blk(f):
        return pl.BlockSpec((1, 1, WINDOW, d), f)

    prev = blk(lambda i, j, n: (i, j, jnp.maximum(n - 1, 0), 0))
    own = blk(lambda i, j, n: (i, j, n, 0))
    nxt = blk(lambda i, j, n: (i, j, jnp.minimum(n + 1, nb - 1), 0))
    ctx = pl.BlockSpec((1, 1, lc, d), lambda i, j, n: (i, j, 0, 0))
    return pl.pallas_call(
        functools.partial(_attn_window_kernel, scale=d ** -0.5, group=g, n_blocks=nb),
        grid=(b, hk, nb),
        in_specs=[pl.BlockSpec((1, g, WINDOW, d), lambda i, j, n: (i, j, n, 0)),
                  prev, own, nxt, ctx, prev, own, nxt, ctx,
                  pl.BlockSpec((1, g, 1, 1), lambda i, j, n: (j, 0, 0, 0))],
        out_specs=pl.BlockSpec((1, g, WINDOW, d), lambda i, j, n: (i, j, n, 0)),
        out_shape=jax.ShapeDtypeStruct((b, h, s, d), F32),
        compiler_params=pltpu.CompilerParams(
            dimension_semantics=("arbitrary", "arbitrary", "arbitrary"),
            vmem_limit_bytes=V7X_VMEM_LIMIT_BYTES),
        name="attn_window",
    )(q, k, k, k, k_ctx, v, v, v, v_ctx, sink_arr)


def _heads_major(z):
    return jnp.swapaxes(z, 1, 2)


def mla_operands(qn, qr, kn, kr, vb):
    b, s, h = qn.shape[:3]
    kt = kn.shape[1]
    pad = LANE - B_NOPE - B_ROPE
    q_cat = jnp.concatenate([qn, qr, jnp.zeros((b, s, h, pad), F32)], axis=-1)
    k_cat = jnp.concatenate([kn, jnp.broadcast_to(kr[:, :, None, :], (b, kt, h, B_ROPE)),
                             jnp.zeros((b, kt, h, pad), F32)], axis=-1)
    return _heads_major(q_cat), _heads_major(k_cat), _heads_major(vb)


def mla_attention(qn, qr, kn, kr, vb):
    b, s = qn.shape[:2]
    o = attn_full(*mla_operands(qn, qr, kn, kr, vb), (B_NOPE + B_ROPE) ** -0.5)
    return _heads_major(o).reshape(b, s, -1)


def ab_prep(p, a_q_norm, a_k_norm, b_cq_norm, b_ckv_norm, b_w_uq, b_w_uk, b_w_uv,
            b_qn_norm, b_qr_norm, b_kn_norm, b_kr_norm):
    b, t = p.shape[:2]
    qa, ka, va, cq, ckv, kr = split_cols(p, AB_SIZES)
    qa = rms_norm(qa.reshape(b, t, A_HEADS, A_HEAD_DIM), a_q_norm)
    ka = rms_norm(ka.reshape(b, t, A_KV_HEADS, A_HEAD_DIM), a_k_norm)
    va = va.reshape(b, t, A_KV_HEADS, A_HEAD_DIM)
    qb = _mm(rms_norm(cq, b_cq_norm), b_w_uq).reshape(b, t, B_HEADS, B_NOPE + B_ROPE)
    qn = rms_norm(qb[..., :B_NOPE], b_qn_norm)
    qr = rms_norm(qb[..., B_NOPE:], b_qr_norm)
    ckv = rms_norm(ckv, b_ckv_norm)
    kn = rms_norm(_mm(ckv, b_w_uk).reshape(b, t, B_HEADS, B_NOPE), b_kn_norm)
    vb = _mm(ckv, b_w_uv).reshape(b, t, B_HEADS, B_V_DIM)
    kr = rms_norm(kr, b_kr_norm)
    return qa, ka, va, qn, qr, kn, kr, vb


def mixer_ab(p_lat, p_ctx, row, col, a_q_norm, a_k_norm, a_sink, b_cq_norm, b_ckv_norm, b_w_uq, b_w_uk,
             b_w_uv, b_qn_norm, b_qr_norm, b_kn_norm, b_kr_norm, ctx_out):
    prm = (a_q_norm, a_k_norm, b_cq_norm, b_ckv_norm, b_w_uq, b_w_uk, b_w_uv,
           b_qn_norm, b_qr_norm, b_kn_norm, b_kr_norm)
    qa_c, ka_c, va_c, qn_c, qr_c, kn_c, kr_c, vb_c = ab_prep(p_ctx, *prm)
    qa_l, ka_l, va_l, qn_l, qr_l, kn_l, kr_l, vb_l = ab_prep(p_lat, *prm)
    qa_l = rope_2d(qa_l, row, col)
    ka_l = rope_2d(ka_l, row, col)
    qr_l = rope_2d(qr_l, row, col)
    kr_l = rope_2d(kr_l[:, :, None, :], row, col)[:, :, 0, :]
    b, s = p_lat.shape[:2]
    n = p_ctx.shape[1]
    hm = _heads_major
    o_a = hm(attn_window(hm(qa_l), hm(ka_l), hm(va_l), hm(ka_c), hm(va_c), a_sink)).reshape(b, s, -1)
    o_b = mla_attention(qn_l, qr_l, jnp.concatenate([kn_c, kn_l], axis=1),
                        jnp.concatenate([kr_c, kr_l], axis=1), jnp.concatenate([vb_c, vb_l], axis=1))
    y_lat = jnp.concatenate([o_a, o_b], axis=-1)
    y_ctx = None
    if ctx_out:
        o_ac = hm(attn_full(hm(qa_c), hm(ka_c), hm(va_c), A_HEAD_DIM ** -0.5, sink=a_sink)).reshape(b, n, -1)
        y_ctx = jnp.concatenate([o_ac, mla_attention(qn_c, qr_c, kn_c, kr_c, vb_c)], axis=-1)
    return y_lat, y_ctx


RW_CHUNK = 64
HEAD_PAIR = 2 * C_HEAD


def _unit_lower_inverse(x, eye):
    p = eye + x
    xp = x
    for _ in range(int(np.log2(RW_CHUNK)) - 1):
        xp = _dot(xp, xp, NN, passes=3)
        p = p + _dot(p, xp, NN, passes=3)
    return p


def _rwkv_kernel(r_ref, v_ref, kk_ref, lw_ref, a_ref, k_ref, y_ref, st_ref):
    d = pl.program_id(1)
    s = pl.program_id(2)

    @pl.when(s == 0)
    def _():
        st_ref[...] = jnp.zeros_like(st_ref)

    c_len = RW_CHUNK
    sign = 1 - 2 * d
    ii = lax.broadcasted_iota(jnp.int32, (c_len, c_len), 0)
    jj = lax.broadcasted_iota(jnp.int32, (c_len, c_len), 1)
    tri = jnp.where((ii - jj) * sign >= 0, 1.0, 0.0).astype(F32)
    lw_all = lw_ref[0, 0]
    cum_all = _dot_exact(tri, lw_all, NN)
    tot_all = jnp.sum(lw_all, axis=0, keepdims=True)

    r2 = lax.broadcasted_iota(jnp.int32, (HEAD_PAIR, HEAD_PAIR), 0)
    c2 = lax.broadcasted_iota(jnp.int32, (HEAD_PAIR, HEAD_PAIR), 1)
    dlt = ((r2 % c_len) - (c2 % c_len)) * sign
    m_strict = dlt > 0
    m_incl = dlt >= 0
    eye = jnp.where(r2 == c2, 1.0, 0.0).astype(F32)
    lane = lax.broadcasted_iota(jnp.int32, (c_len, HEAD_PAIR), 1)
    m0 = lane < C_HEAD

    def stack_heads(x):
        return jnp.concatenate([jnp.where(m0, x, 0.0), jnp.where(m0, 0.0, x)], axis=0)

    for p in range(C_DIM // HEAD_PAIR):
        sl = slice(p * HEAD_PAIR, (p + 1) * HEAD_PAIR)
        lw = lw_all[:, sl]
        cum = cum_all[:, sl]
        tot = tot_all[:, sl]
        kk = kk_ref[0, :, sl]
        kd = k_ref[0, 0, :, sl]
        bb = kk * a_ref[0, 0, :, sl]
        e_neg = jnp.exp(-cum)
        e_end = jnp.exp(tot - cum)
        abar = stack_heads(-kk * jnp.exp(cum - lw))
        rbar = stack_heads(r_ref[0, :, sl] * jnp.exp(cum))
        ktil = stack_heads(kd * e_neg)
        btil = stack_heads(bb * e_neg)
        khat = stack_heads(kd * e_end)
        bhat = stack_heads(bb * e_end)
        vs = stack_heads(v_ref[0, :, sl])

        g = _dot(jnp.concatenate([abar, rbar], axis=0), jnp.concatenate([ktil, btil], axis=0), NT)
        a_ak = jnp.where(m_strict, g[:HEAD_PAIR, :HEAD_PAIR], 0.0)
        a_ab = jnp.where(m_strict, g[:HEAD_PAIR, HEAD_PAIR:], 0.0)
        a_rk = jnp.where(m_incl, g[HEAD_PAIR:, :HEAD_PAIR], 0.0)
        a_rb = jnp.where(m_incl, g[HEAD_PAIR:, HEAD_PAIR:], 0.0)
        tinv = _unit_lower_inverse(a_ab, eye)
        wm = _dot(tinv, abar, NN)
        u0 = _dot(tinv, _dot(a_ak, vs, NN), NN)

        st = st_ref[p]
        u = _dot(wm, st, NT) + u0
        ys = _dot(rbar, st, NT) + _dot(a_rk, vs, NN) + _dot(a_rb, u, NN)
        y_ref[0, 0, :, sl] = ys[:c_len] + ys[c_len:]
        st_ref[p] = st * jnp.exp(tot) + _dot(vs, khat, TN) + _dot(u, bhat, TN)


def rwkv_chunked(r, v, kk, lw2, a2, k2, n_ctx, interpret=False):
    b, t, cd = r.shape
    nc = t // RW_CHUNK
    nctx = n_ctx // RW_CHUNK

    def chunk(d, s):
        rev = jnp.where(s < nctx, nctx - 1 - s, nc + nctx - 1 - s)
        return jnp.where(d == 0, s, rev)

    shared = pl.BlockSpec((1, RW_CHUNK, cd), lambda i, d, s: (i, chunk(d, s), 0))
    per_dir = pl.BlockSpec((1, 1, RW_CHUNK, cd), lambda i, d, s: (d, i, chunk(d, s), 0))
    return pl.pallas_call(
        _rwkv_kernel,
        grid=(b, 2, nc),
        in_specs=[shared, shared, shared, per_dir, per_dir, per_dir],
        out_specs=per_dir,
        out_shape=jax.ShapeDtypeStruct((2, b, t, cd), F32),
        scratch_shapes=[pltpu.VMEM((cd // HEAD_PAIR, HEAD_PAIR, HEAD_PAIR), F32)],
        compiler_params=pltpu.CompilerParams(
            dimension_semantics=("arbitrary", "arbitrary", "arbitrary"),
            vmem_limit_bytes=V7X_VMEM_LIMIT_BYTES),
        name="rwkv7_chunked",
        interpret=interpret,
    )(r, v, kk, lw2, a2, k2)


def token_shift_centred(p, mu_prev, mu_next):
    prev = jnp.pad(p, ((0, 0), (1, 0), (0, 0)))[:, :-1]
    nxt = jnp.pad(p, ((0, 0), (0, 1), (0, 0)))[:, 1:]
    return p + mu_prev * (prev - p) + mu_next * (nxt - p)


def rwkv_prep(pc, c_mu_prev, c_mu_next, c_w0, c_w2, c_a0, c_a2, c_g2, c_k_k, c_k_a):
    b, t = pc.shape[:2]

    def heads(z):
        return z.reshape(b, t, C_HEADS, C_HEAD)

    xs = token_shift_centred(pc, c_mu_prev, c_mu_next)
    r, k, v, wl_f, wl_b, al_f, al_b, gl = split_cols(xs, C_SIZES)

    def decay(wl, w0, w2):
        w = -jax.nn.softplus(-(w0 + _mm(jnp.tanh(wl), w2))) - 0.5
        return heads(-jnp.exp(w.astype(F32)))

    decays = (decay(wl_f, c_w0[0], c_w2[0]), decay(wl_b, c_w0[1], c_w2[1]))
    iclr = (jax.nn.sigmoid(c_a0[0] + _mm(al_f, c_a2[0])), jax.nn.sigmoid(c_a0[1] + _mm(al_b, c_a2[1])))
    g = _mm(jax.nn.sigmoid(gl), c_g2)
    kk = l2_normalize(heads(k * c_k_k))
    ks = tuple(heads(k * (1.0 + (a - 1.0) * c_k_a)) for a in iclr)
    return heads(r), heads(v), kk, g, decays, tuple(heads(a) for a in iclr), ks


def head_group_norm(y, w, b_):
    mean = jnp.mean(y, axis=-1, keepdims=True)
    var = jnp.mean(jnp.square(y - mean), axis=-1, keepdims=True)
    yn = (y - mean) * lax.rsqrt(var + C_GN_EPS)
    return yn * w.astype(F32).reshape(C_HEADS, C_HEAD) + b_.astype(F32).reshape(C_HEADS, C_HEAD)


def rwkv_output(y, r, ks, v, g, c_r_k, c_ln_w, c_ln_b):
    b, t = y.shape[:2]
    yn = head_group_norm(y, c_ln_w, c_ln_b)
    bonus = sum(jnp.sum(r * kd * c_r_k, axis=-1, keepdims=True) * v for kd in ks)
    return (yn + bonus).reshape(b, t, C_DIM) * g


def gdn_prep(pd, d_conv_w, d_A_log, d_dt_bias):
    b, t = pd.shape[:2]
    qkv, z, bf, bb, af, ab = split_cols(pd, D_SIZES)
    qkv = jax.nn.silu(dwconv_centred(qkv, d_conv_w))
    q, k, v = jnp.split(qkv, 3, axis=-1)
    q = l2_normalize(q.reshape(b, t, D_HEADS, D_HEAD_DIM)) * D_HEAD_DIM ** -0.5
    k = l2_normalize(k.reshape(b, t, D_HEADS, D_HEAD_DIM))
    v = v.reshape(b, t, D_HEADS, D_HEAD_DIM)
    betas = (jax.nn.sigmoid(bf), jax.nn.sigmoid(bb))
    gs = tuple(-jnp.exp(d_A_log[i].astype(F32)) * jax.nn.softplus((al + d_dt_bias[i]).astype(F32))
               for i, al in enumerate((af, ab)))
    return q, k, v, z, betas, gs


def _gdn_kernel(q_ref, k_ref, v_ref, beta_ref, g_ref, gt_ref, o_ref, st_ref):
    d = pl.program_id(1)
    s = pl.program_id(2)

    @pl.when(s == 0)
    def _():
        st_ref[...] = jnp.zeros_like(st_ref)

    c_len = D_CHUNK
    n2 = 2 * c_len
    sign = 1 - 2 * d
    r2 = lax.broadcasted_iota(jnp.int32, (n2, n2), 0)
    c2 = lax.broadcasted_iota(jnp.int32, (n2, n2), 1)
    same = (r2 // c_len) == (c2 // c_len)
    dlt = ((r2 % c_len) - (c2 % c_len)) * sign
    m_strict = same & (dlt > 0)
    m_incl = same & (dlt >= 0)
    tri = jnp.where(m_incl, 1.0, 0.0).astype(F32)
    ones_bd = jnp.where(same, 1.0, 0.0).astype(F32)
    eye = jnp.where(r2 == c2, 1.0, 0.0).astype(F32)

    for p in range(D_HEADS // 2):
        h0, h1 = 2 * p, 2 * p + 1

        def rows(ref):
            blk = ref[0]
            return jnp.concatenate([blk[:, h0 * D_HEAD_DIM:(h0 + 1) * D_HEAD_DIM],
                                    blk[:, h1 * D_HEAD_DIM:(h1 + 1) * D_HEAD_DIM]], axis=0)

        q_s = rows(q_ref)
        k_s = rows(k_ref)
        v_s = rows(v_ref)
        g_blk = g_ref[0, 0, 0]
        b_blk = beta_ref[0, 0, 0]
        gt_blk = gt_ref[0, 0, 0]
        g_col = jnp.concatenate([g_blk[:, h0:h0 + 1], g_blk[:, h1:h1 + 1]], axis=0)
        b_col = jnp.concatenate([b_blk[:, h0:h0 + 1], b_blk[:, h1:h1 + 1]], axis=0)
        g_row = jnp.concatenate([gt_blk[h0:h0 + 1, :], gt_blk[h1:h1 + 1, :]], axis=1)
        g_colb = jnp.broadcast_to(g_col, (n2, n2))
        g_rowb = jnp.broadcast_to(g_row, (n2, n2))
        gc_col = _dot_exact(tri, g_colb, NN)
        gc_row = _dot_exact(g_rowb, tri, NT)
        g_end = _dot_exact(ones_bd, g_colb, NN)
        decay = jnp.where(m_incl, jnp.exp(jnp.where(m_incl, gc_col - gc_row, 0.0)), 0.0)
        beta = jnp.broadcast_to(b_col, (n2, n2))
        kb = k_s * beta
        vb = v_s * beta

        g_mat = _dot(jnp.concatenate([kb, q_s], axis=0), k_s, NT)
        l_mat = jnp.where(m_strict, g_mat[:n2] * decay, 0.0)
        a_intra = g_mat[n2:] * decay
        tinv = _unit_lower_inverse(-l_mat, eye)
        u = _dot(tinv, vb, NN)
        wk = _dot(tinv, kb * jnp.exp(gc_col), NN)
        q_e = q_s * jnp.exp(gc_col)
        k_e = k_s * jnp.exp(g_end - gc_col)

        v_new = []
        o_st = []
        for j, h in enumerate((h0, h1)):
            rs = slice(j * c_len, (j + 1) * c_len)
            st = st_ref[h]
            v_new.append(u[rs] - _dot(wk[rs], st, NN))
            o_st.append(_dot(q_e[rs], st, NN))
        v_new_s = jnp.concatenate(v_new, axis=0)
        o_s = jnp.concatenate(o_st, axis=0) + _dot(a_intra, v_new_s, NN)
        for j, h in enumerate((h0, h1)):
            rs = slice(j * c_len, (j + 1) * c_len)
            o_ref[0, 0, :, h * D_HEAD_DIM:(h + 1) * D_HEAD_DIM] = o_s[rs]
            st_ref[h] = st_ref[h] * jnp.exp(g_end[j * c_len:j * c_len + 1, :]) + _dot(k_e[rs], v_new_s[rs], TN)


def gdn_chunked(q, k, v, beta2, g2, n_ctx, interpret=False):
    b, t, cd = q.shape
    nc = t // D_CHUNK
    nctx = n_ctx // D_CHUNK
    g2c = g2.reshape(2, b, nc, D_CHUNK, D_HEADS)
    gt2 = jnp.swapaxes(g2c, -1, -2)
    beta2c = beta2.reshape(2, b, nc, D_CHUNK, D_HEADS)

    def chunk(d, s):
        rev = jnp.where(s < nctx, nctx - 1 - s, nc + nctx - 1 - s)
        return jnp.where(d == 0, s, rev)

    shared = pl.BlockSpec((1, D_CHUNK, cd), lambda i, d, s: (i, chunk(d, s), 0))
    per_dir = pl.BlockSpec((1, 1, D_CHUNK, cd), lambda i, d, s: (d, i, chunk(d, s), 0))
    small = pl.BlockSpec((1, 1, 1, D_CHUNK, D_HEADS), lambda i, d, s: (d, i, chunk(d, s), 0, 0))
    small_t = pl.BlockSpec((1, 1, 1, D_HEADS, D_CHUNK), lambda i, d, s: (d, i, chunk(d, s), 0, 0))
    return pl.pallas_call(
        _gdn_kernel,
        grid=(b, 2, nc),
        in_specs=[shared, shared, shared, small, small, small_t],
        out_specs=per_dir,
        out_shape=jax.ShapeDtypeStruct((2, b, t, cd), F32),
        scratch_shapes=[pltpu.VMEM((D_HEADS, D_HEAD_DIM, D_HEAD_DIM), F32)],
        compiler_params=pltpu.CompilerParams(
            dimension_semantics=("arbitrary", "arbitrary", "arbitrary"),
            vmem_limit_bytes=V7X_VMEM_LIMIT_BYTES),
        name="gdn_chunked",
        interpret=interpret,
    )(q, k, v, beta2c, g2c, gt2)


def gdn_output(o, z, d_o_norm):
    b, t = o.shape[:2]
    gate = jax.nn.silu(z.reshape(b, t, D_HEADS, D_HEAD_DIM).astype(F32))
    return (rms_norm(o, d_o_norm) * gate).reshape(b, t, D_DIM)


def mixer_cd(p_lat, p_ctx, c_mu_prev, c_mu_next, c_w0, c_w2, c_a0, c_a2, c_g2, c_k_k, c_k_a, c_r_k,
             c_ln_w, c_ln_b, d_conv_w, d_A_log, d_dt_bias, d_o_norm, ctx_out):
    b = p_lat.shape[0]
    rw = (c_mu_prev, c_mu_next, c_w0, c_w2, c_a0, c_a2, c_g2, c_k_k, c_k_a)
    r_c, v_c, kk_c, g_c, dec_c, a_c, k_c = rwkv_prep(p_ctx[..., :IN_C], *rw)
    r_l, v_l, kk_l, g_l, dec_l, a_l, k_l = rwkv_prep(p_lat[..., :IN_C], *rw)
    n_ctx = p_ctx.shape[1]

    def seq(zc, zl):
        return jnp.concatenate([zc, zl], axis=1).reshape(b, -1, C_DIM)

    y2 = rwkv_chunked(seq(r_c, r_l), seq(v_c, v_l), seq(kk_c, kk_l),
                      jnp.stack([seq(dec_c[i], dec_l[i]) for i in range(2)]),
                      jnp.stack([seq(a_c[i], a_l[i]) for i in range(2)]),
                      jnp.stack([seq(k_c[i], k_l[i]) for i in range(2)]), n_ctx)
    y_all = (y2[0] + y2[1]).reshape(b, -1, C_HEADS, C_HEAD)
    y_c, y_l = y_all[:, :n_ctx], y_all[:, n_ctx:]

    q_c, kd_c, vd_c, z_c, beta_c, gd_c = gdn_prep(p_ctx[..., IN_C:], d_conv_w, d_A_log, d_dt_bias)
    q_l, kd_l, vd_l, z_l, beta_l, gd_l = gdn_prep(p_lat[..., IN_C:], d_conv_w, d_A_log, d_dt_bias)

    def seq_d(zc, zl):
        return jnp.concatenate([zc, zl], axis=1).reshape(b, zc.shape[1] + zl.shape[1], -1)

    o2 = gdn_chunked(seq_d(q_c, q_l), seq_d(kd_c, kd_l), seq_d(vd_c, vd_l),
                     jnp.stack([seq_d(beta_c[i], beta_l[i]) for i in range(2)]),
                     jnp.stack([seq_d(gd_c[i], gd_l[i]) for i in range(2)]), n_ctx)
    o_all = (o2[0] + o2[1]).reshape(b, -1, D_HEADS, D_HEAD_DIM)
    o_c, o_l = o_all[:, :n_ctx], o_all[:, n_ctx:]

    y_lat = jnp.concatenate([rwkv_output(y_l, r_l, k_l, v_l, g_l, c_r_k, c_ln_w, c_ln_b),
                             gdn_output(o_l, z_l, d_o_norm)], axis=-1).astype(p_lat.dtype)
    y_ctx = None
    if ctx_out:
        y_ctx = jnp.concatenate([rwkv_output(y_c, r_c, k_c, v_c, g_c, c_r_k, c_ln_w, c_ln_b),
                                 gdn_output(o_c, z_c, d_o_norm)], axis=-1).astype(p_ctx.dtype)
    return y_lat, y_ctx


def conv_ffn(h, w_up, conv_w, conv_b, w_down):
    u = dwconv_centred(_mm(h, w_up), conv_w) + conv_b
    val, gate = jnp.split(u, 2, axis=-1)
    return _mm(jax.nn.silu(gate) * val, w_down)


def kernel(x, c, ctx, c_ctx, ada_w, ada_b, ffn_w_up, ffn_conv_w, ffn_conv_b, ffn_w_down,
           ab_w_in, ab_w_out, a_q_norm, a_k_norm, a_sink, b_cq_norm, b_ckv_norm, b_w_uq, b_w_uk, b_w_uv,
           b_qn_norm, b_qr_norm, b_kn_norm, b_kr_norm, cd_w_in, cd_w_out, c_mu_prev, c_mu_next, c_w0, c_w2,
           c_a0, c_a2, c_g2, c_k_k, c_k_a, c_r_k, c_ln_w, c_ln_b, d_conv_w, d_A_log, d_dt_bias, d_o_norm):
    seq = x.shape[1]
    rows = seq // GRID_W
    row = jnp.repeat(jnp.arange(rows, dtype=jnp.int32), GRID_W)
    col = jnp.tile(jnp.arange(GRID_W, dtype=jnp.int32), rows)
    silu_c = jax.nn.silu(c)
    silu_cc = jax.nn.silu(c_ctx)
    for l in range(DEPTH):
        last = l == DEPTH - 1
        i = l // 2
        mod_l = jnp.split((silu_c @ ada_w[l] + ada_b[l])[:, None, :], N_MOD, axis=-1)
        mod_c = jnp.split(silu_cc @ ada_w[l] + ada_b[l], N_MOD, axis=-1)
        h_l = modulate(x, mod_l[0], mod_l[1])
        h_c = modulate(ctx, mod_c[0], mod_c[1])
        if l % 2 == 0:
            y_l, y_c = mixer_ab(_mm(h_l, ab_w_in[i]), _mm(h_c, ab_w_in[i]), row, col, a_q_norm[i], a_k_norm[i],
                                a_sink[i], b_cq_norm[i], b_ckv_norm[i], b_w_uq[i], b_w_uk[i], b_w_uv[i],
                                b_qn_norm[i], b_qr_norm[i], b_kn_norm[i], b_kr_norm[i], not last)
            w_out = ab_w_out[i]
        else:
            y_l, y_c = mixer_cd(_mm(h_l, cd_w_in[i]), _mm(h_c, cd_w_in[i]), c_mu_prev[i], c_mu_next[i], c_w0[i],
                                c_w2[i], c_a0[i], c_a2[i], c_g2[i], c_k_k[i], c_k_a[i], c_r_k[i], c_ln_w[i],
                                c_ln_b[i], d_conv_w[i], d_A_log[i], d_dt_bias[i], d_o_norm[i], not last)
            w_out = cd_w_out[i]
        x = x + mod_l[2] * _mm(y_l, w_out)
        x = x + mod_l[5] * conv_ffn(modulate(x, mod_l[3], mod_l[4]), ffn_w_up[l], ffn_conv_w[l],
                                    ffn_conv_b[l], ffn_w_down[l])
        if not last:
            ctx = ctx + mod_c[2] * _mm(y_c, w_out)
            ctx = ctx + mod_c[5] * conv_ffn(modulate(ctx, mod_c[3], mod_c[4]), ffn_w_up[l], ffn_conv_w[l],
                                            ffn_conv_b[l], ffn_w_down[l])
    return x
```

```python
import functools

import jax
import jax.numpy as jnp
from jax import lax
import numpy as np
from jax.experimental import pallas as pl
from jax.experimental.pallas import tpu as pltpu

D_MODEL = 1024
DEPTH = 2
GRID_W = 64
N_MOD = 6
EPS = 1e-6
ROPE_THETA = 10000.0
NEG_INF = -1e30

A_HEADS = 8
A_KV_HEADS = 2
A_HEAD_DIM = 64
WINDOW = 128
B_HEADS = 8
B_Q_RANK = 256
B_KV_RANK = 256
B_NOPE = 64
B_ROPE = 32
B_V_DIM = 64
Q_BLOCK = 128
C_HEADS = 8
C_HEAD = 64
C_DIM = C_HEADS * C_HEAD
C_DECAY_LORA = 64
C_AAA_LORA = 64
C_GATE_LORA = 128
C_GN_EPS = 64e-5
D_HEADS = 4
D_HEAD_DIM = 128
D_DIM = D_HEADS * D_HEAD_DIM
D_CONV = 5
D_CHUNK = 64
D_FF = 2816
FFN_CONV = 3

AB_SIZES = (A_HEADS * A_HEAD_DIM, A_KV_HEADS * A_HEAD_DIM, A_KV_HEADS * A_HEAD_DIM, B_Q_RANK, B_KV_RANK, B_ROPE)
C_SIZES = (C_DIM, C_DIM, C_DIM, C_DECAY_LORA, C_DECAY_LORA, C_AAA_LORA, C_AAA_LORA, C_GATE_LORA)
IN_C = sum(C_SIZES)
D_SIZES = (3 * D_DIM, D_DIM, D_HEADS, D_HEADS, D_HEADS, D_HEADS)

F32 = jnp.float32
BF16 = jnp.bfloat16

V7X_VMEM_LIMIT_BYTES = 48 * 1024 * 1024
LANE = 128
MXU_N = 256


def _mm_kernel(a_ref, b_ref, o_ref):
    o_ref[...] = jnp.dot(a_ref[...].astype(BF16), b_ref[...], preferred_element_type=F32)


def _pick_tile(n, candidates):
    for c in candidates:
        if n % c == 0:
            return c
    raise ValueError(f"no tile for {n}")


def _mm(a, w):
    lead = a.shape[:-1]
    k = a.shape[-1]
    n = w.shape[-1]
    a2 = a.reshape(-1, k)
    m = a2.shape[0]
    n_pad = -(-n // MXU_N) * MXU_N
    wb = w.astype(BF16)
    if n_pad != n:
        wb = jnp.pad(wb, ((0, 0), (0, n_pad - n)))
    tm = _pick_tile(m, (1024, 512, 256, 128, 8))
    tn = _pick_tile(n_pad, (1024, 768, 512, 256))
    out = pl.pallas_call(
        _mm_kernel,
        grid=(m // tm, n_pad // tn),
        in_specs=[pl.BlockSpec((tm, k), lambda i, j: (i, 0)),
                  pl.BlockSpec((k, tn), lambda i, j: (0, j))],
        out_specs=pl.BlockSpec((tm, tn), lambda i, j: (i, j)),
        out_shape=jax.ShapeDtypeStruct((m, n_pad), F32),
        compiler_params=pltpu.CompilerParams(
            dimension_semantics=("arbitrary", "arbitrary"),
            vmem_limit_bytes=V7X_VMEM_LIMIT_BYTES),
        name="mm",
    )(a2, wb)
    if n_pad != n:
        out = out[:, :n]
    return out.reshape(*lead, n)


def split_cols(p, sizes):
    return jnp.split(p, [int(s) for s in np.cumsum(sizes)[:-1]], axis=-1)


def rms_norm(x, gain=None, eps=EPS):
    xf = x.astype(F32)
    y = xf * lax.rsqrt(jnp.mean(xf * xf, axis=-1, keepdims=True) + eps)
    if gain is not None:
        y = y * gain.astype(F32)
    return y.astype(x.dtype)


def l2_normalize(x, eps=1e-6):
    xf = x.astype(F32)
    return (xf * lax.rsqrt(jnp.sum(xf * xf, axis=-1, keepdims=True) + eps)).astype(x.dtype)


def modulate(x, shift, scale):
    return rms_norm(x) * (1.0 + scale) + shift


def rope_1d(x, pos):
    half = x.shape[-1] // 2
    inv = jnp.power(ROPE_THETA, -jnp.arange(half, dtype=F32) / half)
    ang = pos.astype(F32)[:, None] * inv[None, :]
    cos = jnp.cos(ang)[None, :, None, :]
    sin = jnp.sin(ang)[None, :, None, :]
    x1, x2 = x[..., :half], x[..., half:]
    return jnp.concatenate([x1 * cos - x2 * sin, x1 * sin + x2 * cos], axis=-1).astype(x.dtype)


def rope_2d(x, row, col):
    h = x.shape[-1] // 2
    return jnp.concatenate([rope_1d(x[..., :h], row), rope_1d(x[..., h:], col)], axis=-1)


def dwconv_centred(x, w):
    k, ch = w.shape
    p = k // 2
    return lax.conv_general_dilated(x, w[:, None, :].astype(x.dtype), (1,), [(p, p)],
                                    dimension_numbers=('NWC', 'WIO', 'NWC'), feature_group_count=ch)


def flip_time(t, rev):
    return jnp.flip(t, axis=1) if rev else t


NN = ((1,), (0,))
NT = ((1,), (1,))
TN = ((0,), (0,))


def _dot(a, b, dims, passes=1):
    def dg(x, y):
        return lax.dot_general(x, y, (dims, ((), ())), preferred_element_type=F32)

    ah, bh = a.astype(BF16), b.astype(BF16)
    if passes == 1:
        return dg(ah, bh)
    al = (a - ah.astype(F32)).astype(BF16)
    bl = (b - bh.astype(F32)).astype(BF16)
    return dg(ah, bh) + (dg(ah, bl) + dg(al, bh))


def _dot_exact(a, b, dims):
    return lax.dot_general(a, b, (dims, ((), ())), precision=lax.Precision.HIGHEST, preferred_element_type=F32)


def _softmax_pv(s, v, sink):
    m = jnp.max(s, axis=-1, keepdims=True)
    if sink is not None:
        m = jnp.maximum(m, sink)
    p = jnp.exp(s - m)
    den = jnp.sum(p, axis=-1, keepdims=True)
    if sink is not None:
        den = den + jnp.exp(sink - m)
    return _dot(p, v, NN) / den


def _attn_full_kernel(q_ref, k_ref, v_ref, sink_ref, o_ref, *, scale, use_sink):
    s = _dot(q_ref[0, 0], k_ref[0, 0], NT) * scale
    sink = sink_ref[0] if use_sink else None
    o_ref[0, 0] = _softmax_pv(s, v_ref[0, 0], sink)


def attn_full(q, k, v, scale, sink=None):
    b, h, sq, d = q.shape
    hk, sk = k.shape[1], k.shape[2]
    dv = v.shape[-1]
    g = h // hk
    tq = min(sq, 256)
    use_sink = sink is not None
    sink_arr = (sink if use_sink else jnp.zeros((h,), F32)).astype(F32).reshape(h, 1, 1)
    return pl.pallas_call(
        functools.partial(_attn_full_kernel, scale=scale, use_sink=use_sink),
        grid=(b, h, sq // tq),
        in_specs=[pl.BlockSpec((1, 1, tq, d), lambda i, j, t: (i, j, t, 0)),
                  pl.BlockSpec((1, 1, sk, d), lambda i, j, t: (i, j // g, 0, 0)),
                  pl.BlockSpec((1, 1, sk, dv), lambda i, j, t: (i, j // g, 0, 0)),
                  pl.BlockSpec((1, 1, 1), lambda i, j, t: (j, 0, 0))],
        out_specs=pl.BlockSpec((1, 1, tq, dv), lambda i, j, t: (i, j, t, 0)),
        out_shape=jax.ShapeDtypeStruct((b, h, sq, dv), F32),
        compiler_params=pltpu.CompilerParams(
            dimension_semantics=("arbitrary", "arbitrary", "arbitrary"),
            vmem_limit_bytes=V7X_VMEM_LIMIT_BYTES),
        name="attn_full",
    )(q, k, v, sink_arr)


def _attn_window_kernel(q_ref, kp_ref, k0_ref, kn_ref, kc_ref, vp_ref, v0_ref, vn_ref, vc_ref, sink_ref, o_ref,
                        *, scale, group, n_blocks):
    n = pl.program_id(2)
    w = WINDOW
    d = q_ref.shape[-1]
    q = q_ref[0].reshape(group * w, d)
    keys = jnp.concatenate([kp_ref[0, 0], k0_ref[0, 0], kn_ref[0, 0], kc_ref[0, 0]], axis=0)
    vals = jnp.concatenate([vp_ref[0, 0], v0_ref[0, 0], vn_ref[0, 0], vc_ref[0, 0]], axis=0)
    s = _dot(q, keys, NT) * scale
    nk = keys.shape[0]
    qi = lax.broadcasted_iota(jnp.int32, (group * w, nk), 0) % w
    kj = lax.broadcasted_iota(jnp.int32, (group * w, nk), 1)
    rel = qi + w - kj
    band_ok = (jnp.abs(rel) <= w) & ((kj >= w) | (n > 0)) & ((kj < 2 * w) | (n < n_blocks - 1))
    s = jnp.where((kj >= 3 * w) | band_ok, s, NEG_INF)
    sink = jnp.concatenate([jnp.broadcast_to(sink_ref[0, hh], (w, 1)) for hh in range(group)], axis=0)
    o = _softmax_pv(s, vals, sink)
    o_ref[0] = o.reshape(group, w, o.shape[-1])


def attn_window(q, k, v, k_ctx, v_ctx, sink):
    b, h, s, d = q.shape
    hk = k.shape[1]
    g = h // hk
    nb = s // WINDOW
    lc = k_ctx.shape[2]
    sink_arr = sink.astype(F32).reshape(hk, g, 1, 1)

    def blk(f):
        return pl.BlockSpec((1, 1, WINDOW, d), f)

    prev = blk(lambda i, j, n: (i, j, jnp.maximum(n - 1, 0), 0))
    own = blk(lambda i, j, n: (i, j, n, 0))
    nxt = blk(lambda i, j, n: (i, j, jnp.minimum(n + 1, nb - 1), 0))
    ctx = pl.BlockSpec((1, 1, lc, d), lambda i, j, n: (i, j, 0, 0))
    return pl.pallas_call(
        functools.partial(_attn_window_kernel, scale=d ** -0.5, group=g, n_blocks=nb),
        grid=(b, hk, nb),
        in_specs=[pl.BlockSpec((1, g, WINDOW, d), lambda i, j, n: (i, j, n, 0)),
                  prev, own, nxt, ctx, prev, own, nxt, ctx,
                  pl.BlockSpec((1, g, 1, 1), lambda i, j, n: (j, 0, 0, 0))],
        out_specs=pl.BlockSpec((1, g, WINDOW, d), lambda i, j, n: (i, j, n, 0)),
        out_shape=jax.ShapeDtypeStruct((b, h, s, d), F32),
        compiler_params=pltpu.CompilerParams(
            dimension_semantics=("arbitrary", "arbitrary", "arbitrary"),
            vmem_limit_bytes=V7X_VMEM_LIMIT_BYTES),
        name="attn_window",
    )(q, k, k, k, k_ctx, v, v, v, v_ctx, sink_arr)


def _heads_major(z):
    return jnp.swapaxes(z, 1, 2)


def mla_operands(qn, qr, kn, kr, vb):
    b, s, h = qn.shape[:3]
    kt = kn.shape[1]
    pad = LANE - B_NOPE - B_ROPE
    q_cat = jnp.concatenate([qn, qr, jnp.zeros((b, s, h, pad), F32)], axis=-1)
    k_cat = jnp.concatenate([kn, jnp.broadcast_to(kr[:, :, None, :], (b, kt, h, B_ROPE)),
                             jnp.zeros((b, kt, h, pad), F32)], axis=-1)
    return _heads_major(q_cat), _heads_major(k_cat), _heads_major(vb)


def mla_attention(qn, qr, kn, kr, vb):
    b, s = qn.shape[:2]
    o = attn_full(*mla_operands(qn, qr, kn, kr, vb), (B_NOPE + B_ROPE) ** -0.5)
    return _heads_major(o).reshape(b, s, -1)


def ab_prep(p, a_q_norm, a_k_norm, b_cq_norm, b_ckv_norm, b_w_uq, b_w_uk, b_w_uv,
            b_qn_norm, b_qr_norm, b_kn_norm, b_kr_norm):
    b, t = p.shape[:2]
    qa, ka, va, cq, ckv, kr = split_cols(p, AB_SIZES)
    qa = rms_norm(qa.reshape(b, t, A_HEADS, A_HEAD_DIM), a_q_norm)
    ka = rms_norm(ka.reshape(b, t, A_KV_HEADS, A_HEAD_DIM), a_k_norm)
    va = va.reshape(b, t, A_KV_HEADS, A_HEAD_DIM)
    qb = _mm(rms_norm(cq, b_cq_norm), b_w_uq).reshape(b, t, B_HEADS, B_NOPE + B_ROPE)
    qn = rms_norm(qb[..., :B_NOPE], b_qn_norm)
    qr = rms_norm(qb[..., B_NOPE:], b_qr_norm)
    ckv = rms_norm(ckv, b_ckv_norm)
    kn = rms_norm(_mm(ckv, b_w_uk).reshape(b, t, B_HEADS, B_NOPE), b_kn_norm)
    vb = _mm(ckv, b_w_uv).reshape(b, t, B_HEADS, B_V_DIM)
    kr = rms_norm(kr, b_kr_norm)
    return qa, ka, va, qn, qr, kn, kr, vb


def mixer_ab(p_lat, p_ctx, row, col, a_q_norm, a_k_norm, a_sink, b_cq_norm, b_ckv_norm, b_w_uq, b_w_uk,
             b_w_uv, b_qn_norm, b_qr_norm, b_kn_norm, b_kr_norm, ctx_out):
    prm = (a_q_norm, a_k_norm, b_cq_norm, b_ckv_norm, b_w_uq, b_w_uk, b_w_uv,
           b_qn_norm, b_qr_norm, b_kn_norm, b_kr_norm)
    qa_c, ka_c, va_c, qn_c, qr_c, kn_c, kr_c, vb_c = ab_prep(p_ctx, *prm)
    qa_l, ka_l, va_l, qn_l, qr_l, kn_l, kr_l, vb_l = ab_prep(p_lat, *prm)
    qa_l = rope_2d(qa_l, row, col)
    ka_l = rope_2d(ka_l, row, col)
    qr_l = rope_2d(qr_l, row, col)
    kr_l = rope_2d(kr_l[:, :, None, :], row, col)[:, :, 0, :]
    b, s = p_lat.shape[:2]
    n = p_ctx.shape[1]
    hm = _heads_major
    o_a = hm(attn_window(hm(qa_l), hm(ka_l), hm(va_l), hm(ka_c), hm(va_c), a_sink)).reshape(b, s, -1)
    o_b = mla_attention(qn_l, qr_l, jnp.concatenate([kn_c, kn_l], axis=1),
                        jnp.concatenate([kr_c, kr_l], axis=1), jnp.concatenate([vb_c, vb_l], axis=1))
    y_lat = jnp.concatenate([o_a, o_b], axis=-1)
    y_ctx = None
    if ctx_out:
        o_ac = hm(attn_full(hm(qa_c), hm(ka_c), hm(va_c), A_HEAD_DIM ** -0.5, sink=a_sink)).reshape(b, n, -1)
        y_ctx = jnp.concatenate([o_ac, mla_attention(qn_c, qr_c, kn_c, kr_c, vb_c)], axis=-1)
    return y_lat, y_ctx


RW_CHUNK = 64
HEAD_PAIR = 2 * C_HEAD


def _unit_lower_inverse(x, eye):
    p = eye + x
    xp = x
    for _ in range(int(np.log2(RW_CHUNK)) - 1):
        xp = _dot(xp, xp, NN, passes=3)
        p = p + _dot(p, xp, NN, passes=3)
    return p


def _rwkv_kernel(r_ref, v_ref, kk_ref, lw_ref, a_ref, k_ref, y_ref, st_ref):
    d = pl.program_id(1)
    s = pl.program_id(2)

    @pl.when(s == 0)
    def _():
        st_ref[...] = jnp.zeros_like(st_ref)

    c_len = RW_CHUNK
    sign = 1 - 2 * d
    ii = lax.broadcasted_iota(jnp.int32, (c_len, c_len), 0)
    jj = lax.broadcasted_iota(jnp.int32, (c_len, c_len), 1)
    tri = jnp.where((ii - jj) * sign >= 0, 1.0, 0.0).astype(F32)
    lw_all = lw_ref[0, 0]
    cum_all = _dot_exact(tri, lw_all, NN)
    tot_all = jnp.sum(lw_all, axis=0, keepdims=True)

    r2 = lax.broadcasted_iota(jnp.int32, (HEAD_PAIR, HEAD_PAIR), 0)
    c2 = lax.broadcasted_iota(jnp.int32, (HEAD_PAIR, HEAD_PAIR), 1)
    dlt = ((r2 % c_len) - (c2 % c_len)) * sign
    m_strict = dlt > 0
    m_incl = dlt >= 0
    eye = jnp.where(r2 == c2, 1.0, 0.0).astype(F32)
    lane = lax.broadcasted_iota(jnp.int32, (c_len, HEAD_PAIR), 1)
    m0 = lane < C_HEAD

    def stack_heads(x):
        return jnp.concatenate([jnp.where(m0, x, 0.0), jnp.where(m0, 0.0, x)], axis=0)

    for p in range(C_DIM // HEAD_PAIR):
        sl = slice(p * HEAD_PAIR, (p + 1) * HEAD_PAIR)
        lw = lw_all[:, sl]
        cum = cum_all[:, sl]
        tot = tot_all[:, sl]
        kk = kk_ref[0, :, sl]
        kd = k_ref[0, 0, :, sl]
        bb = kk * a_ref[0, 0, :, sl]
        e_neg = jnp.exp(-cum)
        e_end = jnp.exp(tot - cum)
        abar = stack_heads(-kk * jnp.exp(cum - lw))
        rbar = stack_heads(r_ref[0, :, sl] * jnp.exp(cum))
        ktil = stack_heads(kd * e_neg)
        btil = stack_heads(bb * e_neg)
        khat = stack_heads(kd * e_end)
        bhat = stack_heads(bb * e_end)
        vs = stack_heads(v_ref[0, :, sl])

        g = _dot(jnp.concatenate([abar, rbar], axis=0), jnp.concatenate([ktil, btil], axis=0), NT)
        a_ak = jnp.where(m_strict, g[:HEAD_PAIR, :HEAD_PAIR], 0.0)
        a_ab = jnp.where(m_strict, g[:HEAD_PAIR, HEAD_PAIR:], 0.0)
        a_rk = jnp.where(m_incl, g[HEAD_PAIR:, :HEAD_PAIR], 0.0)
        a_rb = jnp.where(m_incl, g[HEAD_PAIR:, HEAD_PAIR:], 0.0)
        tinv = _unit_lower_inverse(a_ab, eye)
        wm = _dot(tinv, abar, NN)
        u0 = _dot(tinv, _dot(a_ak, vs, NN), NN)

        st = st_ref[p]
        u = _dot(wm, st, NT) + u0
        ys = _dot(rbar, st, NT) + _dot(a_rk, vs, NN) + _dot(a_rb, u, NN)
        y_ref[0, 0, :, sl] = ys[:c_len] + ys[c_len:]
        st_ref[p] = st * jnp.exp(tot) + _dot(vs, khat, TN) + _dot(u, bhat, TN)


def rwkv_chunked(r, v, kk, lw2, a2, k2, n_ctx, interpret=False):
    b, t, cd = r.shape
    nc = t // RW_CHUNK
    nctx = n_ctx // RW_CHUNK

    def chunk(d, s):
        rev = jnp.where(s < nctx, nctx - 1 - s, nc + nctx - 1 - s)
        return jnp.where(d == 0, s, rev)

    shared = pl.BlockSpec((1, RW_CHUNK, cd), lambda i, d, s: (i, chunk(d, s), 0))
    per_dir = pl.BlockSpec((1, 1, RW_CHUNK, cd), lambda i, d, s: (d, i, chunk(d, s), 0))
    return pl.pallas_call(
        _rwkv_kernel,
        grid=(b, 2, nc),
        in_specs=[shared, shared, shared, per_dir, per_dir, per_dir],
        out_specs=per_dir,
        out_shape=jax.ShapeDtypeStruct((2, b, t, cd), F32),
        scratch_shapes=[pltpu.VMEM((cd // HEAD_PAIR, HEAD_PAIR, HEAD_PAIR), F32)],
        compiler_params=pltpu.CompilerParams(
            dimension_semantics=("arbitrary", "arbitrary", "arbitrary"),
            vmem_limit_bytes=V7X_VMEM_LIMIT_BYTES),
        name="rwkv7_chunked",
        interpret=interpret,
    )(r, v, kk, lw2, a2, k2)


def token_shift_centred(p, mu_prev, mu_next):
    prev = jnp.pad(p, ((0, 0), (1, 0), (0, 0)))[:, :-1]
    nxt = jnp.pad(p, ((0, 0), (0, 1), (0, 0)))[:, 1:]
    return p + mu_prev * (prev - p) + mu_next * (nxt - p)


def rwkv_prep(pc, c_mu_prev, c_mu_next, c_w0, c_w2, c_a0, c_a2, c_g2, c_k_k, c_k_a):
    b, t = pc.shape[:2]

    def heads(z):
        return z.reshape(b, t, C_HEADS, C_HEAD)

    xs = token_shift_centred(pc, c_mu_prev, c_mu_next)
    r, k, v, wl_f, wl_b, al_f, al_b, gl = split_cols(xs, C_SIZES)

    def decay(wl, w0, w2):
        w = -jax.nn.softplus(-(w0 + _mm(jnp.tanh(wl), w2))) - 0.5
        return heads(-jnp.exp(w.astype(F32)))

    decays = (decay(wl_f, c_w0[0], c_w2[0]), decay(wl_b, c_w0[1], c_w2[1]))
    iclr = (jax.nn.sigmoid(c_a0[0] + _mm(al_f, c_a2[0])), jax.nn.sigmoid(c_a0[1] + _mm(al_b, c_a2[1])))
    g = _mm(jax.nn.sigmoid(gl), c_g2)
    kk = l2_normalize(heads(k * c_k_k))
    ks = tuple(heads(k * (1.0 + (a - 1.0) * c_k_a)) for a in iclr)
    return heads(r), heads(v), kk, g, decays, tuple(heads(a) for a in iclr), ks


def head_group_norm(y, w, b_):
    mean = jnp.mean(y, axis=-1, keepdims=True)
    var = jnp.mean(jnp.square(y - mean), axis=-1, keepdims=True)
    yn = (y - mean) * lax.rsqrt(var + C_GN_EPS)
    return yn * w.astype(F32).reshape(C_HEADS, C_HEAD) + b_.astype(F32).reshape(C_HEADS, C_HEAD)


def rwkv_output(y, r, ks, v, g, c_r_k, c_ln_w, c_ln_b):
    b, t = y.shape[:2]
    yn = head_group_norm(y, c_ln_w, c_ln_b)
    bonus = sum(jnp.sum(r * kd * c_r_k, axis=-1, keepdims=True) * v for kd in ks)
    return (yn + bonus).reshape(b, t, C_DIM) * g


def gdn_prep(pd, d_conv_w, d_A_log, d_dt_bias):
    b, t = pd.shape[:2]
    qkv, z, bf, bb, af, ab = split_cols(pd, D_SIZES)
    qkv = jax.nn.silu(dwconv_centred(qkv, d_conv_w))
    q, k, v = jnp.split(qkv, 3, axis=-1)
    q = l2_normalize(q.reshape(b, t, D_HEADS, D_HEAD_DIM)) * D_HEAD_DIM ** -0.5
    k = l2_normalize(k.reshape(b, t, D_HEADS, D_HEAD_DIM))
    v = v.reshape(b, t, D_HEADS, D_HEAD_DIM)
    betas = (jax.nn.sigmoid(bf), jax.nn.sigmoid(bb))
    gs = tuple(-jnp.exp(d_A_log[i].astype(F32)) * jax.nn.softplus((al + d_dt_bias[i]).astype(F32))
               for i, al in enumerate((af, ab)))
    return q, k, v, z, betas, gs


def _gdn_kernel(q_ref, k_ref, v_ref, beta_ref, g_ref, gt_ref, o_ref, st_ref):
    d = pl.program_id(1)
    s = pl.program_id(2)

    @pl.when(s == 0)
    def _():
        st_ref[...] = jnp.zeros_like(st_ref)

    c_len = D_CHUNK
    n2 = 2 * c_len
    sign = 1 - 2 * d
    r2 = lax.broadcasted_iota(jnp.int32, (n2, n2), 0)
    c2 = lax.broadcasted_iota(jnp.int32, (n2, n2), 1)
    same = (r2 // c_len) == (c2 // c_len)
    dlt = ((r2 % c_len) - (c2 % c_len)) * sign
    m_strict = same & (dlt > 0)
    m_incl = same & (dlt >= 0)
    tri = jnp.where(m_incl, 1.0, 0.0).astype(F32)
    ones_bd = jnp.where(same, 1.0, 0.0).astype(F32)
    eye = jnp.where(r2 == c2, 1.0, 0.0).astype(F32)

    for p in range(D_HEADS // 2):
        h0, h1 = 2 * p, 2 * p + 1

        def rows(ref):
            blk = ref[0]
            return jnp.concatenate([blk[:, h0 * D_HEAD_DIM:(h0 + 1) * D_HEAD_DIM],
                                    blk[:, h1 * D_HEAD_DIM:(h1 + 1) * D_HEAD_DIM]], axis=0)

        q_s = rows(q_ref)
        k_s = rows(k_ref)
        v_s = rows(v_ref)
        g_blk = g_ref[0, 0, 0]
        b_blk = beta_ref[0, 0, 0]
        gt_blk = gt_ref[0, 0, 0]
        g_col = jnp.concatenate([g_blk[:, h0:h0 + 1], g_blk[:, h1:h1 + 1]], axis=0)
        b_col = jnp.concatenate([b_blk[:, h0:h0 + 1], b_blk[:, h1:h1 + 1]], axis=0)
        g_row = jnp.concatenate([gt_blk[h0:h0 + 1, :], gt_blk[h1:h1 + 1, :]], axis=1)
        g_colb = jnp.broadcast_to(g_col, (n2, n2))
        g_rowb = jnp.broadcast_to(g_row, (n2, n2))
        gc_col = _dot_exact(tri, g_colb, NN)
        gc_row = _dot_exact(g_rowb, tri, NT)
        g_end = _dot_exact(ones_bd, g_colb, NN)
        decay = jnp.where(m_incl, jnp.exp(jnp.where(m_incl, gc_col - gc_row, 0.0)), 0.0)
        beta = jnp.broadcast_to(b_col, (n2, n2))
        kb = k_s * beta
        vb = v_s * beta

        g_mat = _dot(jnp.concatenate([kb, q_s], axis=0), k_s, NT)
        l_mat = jnp.where(m_strict, g_mat[:n2] * decay, 0.0)
        a_intra = g_mat[n2:] * decay
        tinv = _unit_lower_inverse(-l_mat, eye)
        u = _dot(tinv, vb, NN)
        wk = _dot(tinv, kb * jnp.exp(gc_col), NN)
        q_e = q_s * jnp.exp(gc_col)
        k_e = k_s * jnp.exp(g_end - gc_col)

        v_new = []
        o_st = []
        for j, h in enumerate((h0, h1)):
            rs = slice(j * c_len, (j + 1) * c_len)
            st = st_ref[h]
            v_new.append(u[rs] - _dot(wk[rs], st, NN))
            o_st.append(_dot(q_e[rs], st, NN))
        v_new_s = jnp.concatenate(v_new, axis=0)
        o_s = jnp.concatenate(o_st, axis=0) + _dot(a_intra, v_new_s, NN)
        for j, h in enumerate((h0, h1)):
            rs = slice(j * c_len, (j + 1) * c_len)
            o_ref[0, 0, :, h * D_HEAD_DIM:(h + 1) * D_HEAD_DIM] = o_s[rs]
            st_ref[h] = st_ref[h] * jnp.exp(g_end[j * c_len:j * c_len + 1, :]) + _dot(k_e[rs], v_new_s[rs], TN)


def gdn_chunked(q, k, v, beta2, g2, n_ctx, interpret=False):
    b, t, cd = q.shape
    nc = t // D_CHUNK
    nctx = n_ctx // D_CHUNK
    g2c = g2.reshape(2, b, nc, D_CHUNK, D_HEADS)
    gt2 = jnp.swapaxes(g2c, -1, -2)
    beta2c = beta2.reshape(2, b, nc, D_CHUNK, D_HEADS)

    def chunk(d, s):
        rev = jnp.where(s < nctx, nctx - 1 - s, nc + nctx - 1 - s)
        return jnp.where(d == 0, s, rev)

    shared = pl.BlockSpec((1, D_CHUNK, cd), lambda i, d, s: (i, chunk(d, s), 0))
    per_dir = pl.BlockSpec((1, 1, D_CHUNK, cd), lambda i, d, s: (d, i, chunk(d, s), 0))
    small = pl.BlockSpec((1, 1, 1, D_CHUNK, D_HEADS), lambda i, d, s: (d, i, chunk(d, s), 0, 0))
    small_t = pl.BlockSpec((1, 1, 1, D_HEADS, D_CHUNK), lambda i, d, s: (d, i, chunk(d, s), 0, 0))
    return pl.pallas_call(
        _gdn_kernel,
        grid=(b, 2, nc),
        in_specs=[shared, shared, shared, small, small, small_t],
        out_specs=per_dir,
        out_shape=jax.ShapeDtypeStruct((2, b, t, cd), F32),
        scratch_shapes=[pltpu.VMEM((D_HEADS, D_HEAD_DIM, D_HEAD_DIM), F32)],
        compiler_params=pltpu.CompilerParams(
            dimension_semantics=("arbitrary", "arbitrary", "arbitrary"),
            vmem_limit_bytes=V7X_VMEM_LIMIT_BYTES),
        name="gdn_chunked",
        interpret=interpret,
    )(q, k, v, beta2c, g2c, gt2)


def gdn_output(o, z, d_o_norm):
    b, t = o.shape[:2]
    gate = jax.nn.silu(z.reshape(b, t, D_HEADS, D_HEAD_DIM).astype(F32))
    return (rms_norm(o, d_o_norm) * gate).reshape(b, t, D_DIM)


def mixer_cd(p_lat, p_ctx, c_mu_prev, c_mu_next, c_w0, c_w2, c_a0, c_a2, c_g2, c_k_k, c_k_a, c_r_k,
             c_ln_w, c_ln_b, d_conv_w, d_A_log, d_dt_bias, d_o_norm, ctx_out):
    b = p_lat.shape[0]
    rw = (c_mu_prev, c_mu_next, c_w0, c_w2, c_a0, c_a2, c_g2, c_k_k, c_k_a)
    r_c, v_c, kk_c, g_c, dec_c, a_c, k_c = rwkv_prep(p_ctx[..., :IN_C], *rw)
    r_l, v_l, kk_l, g_l, dec_l, a_l, k_l = rwkv_prep(p_lat[..., :IN_C], *rw)
    n_ctx = p_ctx.shape[1]

    def seq(zc, zl):
        return jnp.concatenate([zc, zl], axis=1).reshape(b, -1, C_DIM)

    y2 = rwkv_chunked(seq(r_c, r_l), seq(v_c, v_l), seq(kk_c, kk_l),
                      jnp.stack([seq(dec_c[i], dec_l[i]) for i in range(2)]),
                      jnp.stack([seq(a_c[i], a_l[i]) for i in range(2)]),
                      jnp.stack([seq(k_c[i], k_l[i]) for i in range(2)]), n_ctx)
    y_all = (y2[0] + y2[1]).reshape(b, -1, C_HEADS, C_HEAD)
    y_c, y_l = y_all[:, :n_ctx], y_all[:, n_ctx:]

    q_c, kd_c, vd_c, z_c, beta_c, gd_c = gdn_prep(p_ctx[..., IN_C:], d_conv_w, d_A_log, d_dt_bias)
    q_l, kd_l, vd_l, z_l, beta_l, gd_l = gdn_prep(p_lat[..., IN_C:], d_conv_w, d_A_log, d_dt_bias)

    def seq_d(zc, zl):
        return jnp.concatenate([zc, zl], axis=1).reshape(b, zc.shape[1] + zl.shape[1], -1)

    o2 = gdn_chunked(seq_d(q_c, q_l), seq_d(kd_c, kd_l), seq_d(vd_c, vd_l),
                     jnp.stack([seq_d(beta_c[i], beta_l[i]) for i in range(2)]),
                     jnp.stack([seq_d(gd_c[i], gd_l[i]) for i in range(2)]), n_ctx)
    o_all = (o2[0] + o2[1]).reshape(b, -1, D_HEADS, D_HEAD_DIM)
    o_c, o_l = o_all[:, :n_ctx], o_all[:, n_ctx:]

    y_lat = jnp.concatenate([rwkv_output(y_l, r_l, k_l, v_l, g_l, c_r_k, c_ln_w, c_ln_b),
                             gdn_output(o_l, z_l, d_o_norm)], axis=-1).astype(p_lat.dtype)
    y_ctx = None
    if ctx_out:
        y_ctx = jnp.concatenate([rwkv_output(y_c, r_c, k_c, v_c, g_c, c_r_k, c_ln_w, c_ln_b),
                                 gdn_output(o_c, z_c, d_o_norm)], axis=-1).astype(p_ctx.dtype)
    return y_lat, y_ctx


FFN_TM = 256
FFN_HALO = 8
FFN_FC = 256
FFN_VMEM_LIMIT_BYTES = 56 * 1024 * 1024


def _modulated(x, shift, scale):
    y = x * lax.rsqrt(jnp.mean(x * x, axis=-1, keepdims=True) + EPS)
    return y * (1.0 + scale) + shift


def _ffn_kernel(xp_ref, x_ref, xn_ref, shift_ref, scale_ref, gate_ref, wup_ref, cw_ref, cb_ref, wdn_ref, o_ref, acc_ref,
                *, n_tiles):
    i = pl.program_id(1)
    tm, halo = FFN_TM, FFN_HALO
    rows = tm + 2 * halo
    x = x_ref[0]
    xe = jnp.concatenate([xp_ref[0], x, xn_ref[0]], axis=0)
    h = _modulated(xe, shift_ref[0], scale_ref[0]).astype(BF16)
    r = lax.broadcasted_iota(jnp.int32, (rows, 1), 0)
    keep_prev = jnp.where((r == halo) & (i == 0), 0.0, 1.0)
    keep_next = jnp.where((r == halo + tm - 1) & (i == n_tiles - 1), 0.0, 1.0)
    acc_ref[...] = jnp.zeros_like(acc_ref)

    def conv(u, c0):
        w = cw_ref[:, c0:c0 + FFN_FC]
        um = pltpu.roll(u, 1, 0) * keep_prev
        up = pltpu.roll(u, rows - 1, 0) * keep_next
        y = um * w[0:1] + u * w[1:2] + up * w[2:3] + cb_ref[:, c0:c0 + FFN_FC]
        return y[halo:halo + tm]

    for c in range(D_FF // FFN_FC):
        cv, cg = c * FFN_FC, D_FF + c * FFN_FC
        val = conv(jnp.dot(h, wup_ref[:, cv:cv + FFN_FC], preferred_element_type=F32), cv)
        gat = conv(jnp.dot(h, wup_ref[:, cg:cg + FFN_FC], preferred_element_type=F32), cg)
        act = (gat * jax.nn.sigmoid(gat) * val).astype(BF16)
        acc_ref[...] += jnp.dot(act, wdn_ref[cv:cv + FFN_FC, :], preferred_element_type=F32)
    o_ref[0] = x + gate_ref[0] * acc_ref[...]


def ffn_fused(x, shift, scale, gate, w_up, conv_w, conv_b, w_down):
    b, t, d = x.shape
    nt = t // FFN_TM
    hb = FFN_TM // FFN_HALO
    f2 = w_up.shape[1]
    mod = pl.BlockSpec((1, 1, d), lambda bi, i: (bi, 0, 0))

    def full(shp):
        return pl.BlockSpec(shp, lambda bi, i: (0,) * len(shp))

    return pl.pallas_call(
        functools.partial(_ffn_kernel, n_tiles=nt),
        grid=(b, nt),
        in_specs=[pl.BlockSpec((1, FFN_HALO, d), lambda bi, i: (bi, jnp.maximum(i * hb - 1, 0), 0)),
                  pl.BlockSpec((1, FFN_TM, d), lambda bi, i: (bi, i, 0)),
                  pl.BlockSpec((1, FFN_HALO, d), lambda bi, i: (bi, jnp.minimum((i + 1) * hb, nt * hb - 1), 0)),
                  mod, mod, mod,
                  full((d, f2)), full((FFN_CONV, f2)), full((1, f2)), full((f2 // 2, d))],
        out_specs=pl.BlockSpec((1, FFN_TM, d), lambda bi, i: (bi, i, 0)),
        out_shape=jax.ShapeDtypeStruct((b, t, d), F32),
        scratch_shapes=[pltpu.VMEM((FFN_TM, d), F32)],
        compiler_params=pltpu.CompilerParams(
            dimension_semantics=("arbitrary", "arbitrary"),
            vmem_limit_bytes=FFN_VMEM_LIMIT_BYTES),
        name="conv_ffn",
    )(x, x, x, shift, scale, gate, w_up.astype(BF16), conv_w, conv_b.reshape(1, f2), w_down.astype(BF16))


def kernel(x, c, ctx, c_ctx, ada_w, ada_b, ffn_w_up, ffn_conv_w, ffn_conv_b, ffn_w_down,
           ab_w_in, ab_w_out, a_q_norm, a_k_norm, a_sink, b_cq_norm, b_ckv_norm, b_w_uq, b_w_uk, b_w_uv,
           b_qn_norm, b_qr_norm, b_kn_norm, b_kr_norm, cd_w_in, cd_w_out, c_mu_prev, c_mu_next, c_w0, c_w2,
           c_a0, c_a2, c_g2, c_k_k, c_k_a, c_r_k, c_ln_w, c_ln_b, d_conv_w, d_A_log, d_dt_bias, d_o_norm):
    seq = x.shape[1]
    rows = seq // GRID_W
    row = jnp.repeat(jnp.arange(rows, dtype=jnp.int32), GRID_W)
    col = jnp.tile(jnp.arange(GRID_W, dtype=jnp.int32), rows)
    silu_c = jax.nn.silu(c)
    silu_cc = jax.nn.silu(c_ctx)
    for l in range(DEPTH):
        last = l == DEPTH - 1
        i = l // 2
        mod_l = jnp.split((silu_c @ ada_w[l] + ada_b[l])[:, None, :], N_MOD, axis=-1)
        mod_c = jnp.split(silu_cc @ ada_w[l] + ada_b[l], N_MOD, axis=-1)
        h_l = modulate(x, mod_l[0], mod_l[1])
        h_c = modulate(ctx, mod_c[0], mod_c[1])
        if l % 2 == 0:
            y_l, y_c = mixer_ab(_mm(h_l, ab_w_in[i]), _mm(h_c, ab_w_in[i]), row, col, a_q_norm[i], a_k_norm[i],
                                a_sink[i], b_cq_norm[i], b_ckv_norm[i], b_w_uq[i], b_w_uk[i], b_w_uv[i],
                                b_qn_norm[i], b_qr_norm[i], b_kn_norm[i], b_kr_norm[i], not last)
            w_out = ab_w_out[i]
        else:
            y_l, y_c = mixer_cd(_mm(h_l, cd_w_in[i]), _mm(h_c, cd_w_in[i]), c_mu_prev[i], c_mu_next[i], c_w0[i],
                                c_w2[i], c_a0[i], c_a2[i], c_g2[i], c_k_k[i], c_k_a[i], c_r_k[i], c_ln_w[i],
                                c_ln_b[i], d_conv_w[i], d_A_log[i], d_dt_bias[i], d_o_norm[i], not last)
            w_out = cd_w_out[i]
        ffn_w = (ffn_w_up[l], ffn_conv_w[l], ffn_conv_b[l], ffn_w_down[l])
        x = x + mod_l[2] * _mm(y_l, w_out)
        x = ffn_fused(x, mod_l[3], mod_l[4], mod_l[5], *ffn_w)
        if not last:
            ctx = ctx + mod_c[2] * _mm(y_c, w_out)
            rep = lambda m: jnp.broadcast_to(m[None, None, :], (ctx.shape[0], 1, D_MODEL))
            ctx = ffn_fused(ctx, rep(mod_c[3]), rep(mod_c[4]), rep(mod_c[5]), *ffn_w)
    return x
```

```python
import functools

import jax
import jax.numpy as jnp
from jax import lax
import numpy as np
from jax.experimental import pallas as pl
from jax.experimental.pallas import tpu as pltpu

D_MODEL = 1024
DEPTH = 2
GRID_W = 64
N_MOD = 6
EPS = 1e-6
ROPE_THETA = 10000.0
NEG_INF = -1e30

A_HEADS = 8
A_KV_HEADS = 2
A_HEAD_DIM = 64
WINDOW = 128
B_HEADS = 8
B_Q_RANK = 256
B_KV_RANK = 256
B_NOPE = 64
B_ROPE = 32
B_V_DIM = 64
Q_BLOCK = 128
C_HEADS = 8
C_HEAD = 64
C_DIM = C_HEADS * C_HEAD
C_DECAY_LORA = 64
C_AAA_LORA = 64
C_GATE_LORA = 128
C_GN_EPS = 64e-5
D_HEADS = 4
D_HEAD_DIM = 128
D_DIM = D_HEADS * D_HEAD_DIM
D_CONV = 5
D_CHUNK = 64
D_FF = 2816
FFN_CONV = 3

AB_SIZES = (A_HEADS * A_HEAD_DIM, A_KV_HEADS * A_HEAD_DIM, A_KV_HEADS * A_HEAD_DIM, B_Q_RANK, B_KV_RANK, B_ROPE)
C_SIZES = (C_DIM, C_DIM, C_DIM, C_DECAY_LORA, C_DECAY_LORA, C_AAA_LORA, C_AAA_LORA, C_GATE_LORA)
IN_C = sum(C_SIZES)
D_SIZES = (3 * D_DIM, D_DIM, D_HEADS, D_HEADS, D_HEADS, D_HEADS)

F32 = jnp.float32
BF16 = jnp.bfloat16

V7X_VMEM_LIMIT_BYTES = 48 * 1024 * 1024
LANE = 128
MXU_N = 256


def _mm_kernel(a_ref, b_ref, o_ref):
    o_ref[...] = jnp.dot(a_ref[...].astype(BF16), b_ref[...], preferred_element_type=F32)


def _pick_tile(n, candidates):
    for c in candidates:
        if n % c == 0:
            return c
    raise ValueError(f"no tile for {n}")


def _mm(a, w):
    lead = a.shape[:-1]
    k = a.shape[-1]
    n = w.shape[-1]
    a2 = a.reshape(-1, k)
    m = a2.shape[0]
    n_pad = -(-n // MXU_N) * MXU_N
    wb = w.astype(BF16)
    if n_pad != n:
        wb = jnp.pad(wb, ((0, 0), (0, n_pad - n)))
    tm = _pick_tile(m, (1024, 512, 256, 128, 8))
    tn = _pick_tile(n_pad, (1024, 768, 512, 256))
    out = pl.pallas_call(
        _mm_kernel,
        grid=(m // tm, n_pad // tn),
        in_specs=[pl.BlockSpec((tm, k), lambda i, j: (i, 0)),
                  pl.BlockSpec((k, tn), lambda i, j: (0, j))],
        out_specs=pl.BlockSpec((tm, tn), lambda i, j: (i, j)),
        out_shape=jax.ShapeDtypeStruct((m, n_pad), F32),
        compiler_params=pltpu.CompilerParams(
            dimension_semantics=("arbitrary", "arbitrary"),
            vmem_limit_bytes=V7X_VMEM_LIMIT_BYTES),
        name="mm",
    )(a2, wb)
    if n_pad != n:
        out = out[:, :n]
    return out.reshape(*lead, n)


def split_cols(p, sizes):
    return jnp.split(p, [int(s) for s in np.cumsum(sizes)[:-1]], axis=-1)


def rms_norm(x, gain=None, eps=EPS):
    xf = x.astype(F32)
    y = xf * lax.rsqrt(jnp.mean(xf * xf, axis=-1, keepdims=True) + eps)
    if gain is not None:
        y = y * gain.astype(F32)
    return y.astype(x.dtype)


def l2_normalize(x, eps=1e-6):
    xf = x.astype(F32)
    return (xf * lax.rsqrt(jnp.sum(xf * xf, axis=-1, keepdims=True) + eps)).astype(x.dtype)


def modulate(x, shift, scale):
    return rms_norm(x) * (1.0 + scale) + shift


def rope_1d(x, pos):
    half = x.shape[-1] // 2
    inv = jnp.power(ROPE_THETA, -jnp.arange(half, dtype=F32) / half)
    ang = pos.astype(F32)[:, None] * inv[None, :]
    cos = jnp.cos(ang)[None, :, None, :]
    sin = jnp.sin(ang)[None, :, None, :]
    x1, x2 = x[..., :half], x[..., half:]
    return jnp.concatenate([x1 * cos - x2 * sin, x1 * sin + x2 * cos], axis=-1).astype(x.dtype)


def rope_2d(x, row, col):
    h = x.shape[-1] // 2
    return jnp.concatenate([rope_1d(x[..., :h], row), rope_1d(x[..., h:], col)], axis=-1)


def dwconv_centred(x, w):
    k, ch = w.shape
    p = k // 2
    return lax.conv_general_dilated(x, w[:, None, :].astype(x.dtype), (1,), [(p, p)],
                                    dimension_numbers=('NWC', 'WIO', 'NWC'), feature_group_count=ch)


def flip_time(t, rev):
    return jnp.flip(t, axis=1) if rev else t


NN = ((1,), (0,))
NT = ((1,), (1,))
TN = ((0,), (0,))


def _dot(a, b, dims, passes=1):
    def dg(x, y):
        return lax.dot_general(x, y, (dims, ((), ())), preferred_element_type=F32)

    ah, bh = a.astype(BF16), b.astype(BF16)
    if passes == 1:
        return dg(ah, bh)
    al = (a - ah.astype(F32)).astype(BF16)
    bl = (b - bh.astype(F32)).astype(BF16)
    return dg(ah, bh) + (dg(ah, bl) + dg(al, bh))


def _dot_exact(a, b, dims):
    return lax.dot_general(a, b, (dims, ((), ())), precision=lax.Precision.HIGHEST, preferred_element_type=F32)


def _softmax_pv(s, v, sink):
    m = jnp.max(s, axis=-1, keepdims=True)
    if sink is not None:
        m = jnp.maximum(m, sink)
    p = jnp.exp(s - m)
    den = jnp.sum(p, axis=-1, keepdims=True)
    if sink is not None:
        den = den + jnp.exp(sink - m)
    return _dot(p, v, NN) / den


def _attn_full_kernel(q_ref, k_ref, v_ref, sink_ref, o_ref, *, scale, use_sink):
    s = _dot(q_ref[0, 0], k_ref[0, 0], NT) * scale
    sink = sink_ref[0] if use_sink else None
    o_ref[0, 0] = _softmax_pv(s, v_ref[0, 0], sink)


def attn_full(q, k, v, scale, sink=None):
    b, h, sq, d = q.shape
    hk, sk = k.shape[1], k.shape[2]
    dv = v.shape[-1]
    g = h // hk
    tq = min(sq, 256)
    use_sink = sink is not None
    sink_arr = (sink if use_sink else jnp.zeros((h,), F32)).astype(F32).reshape(h, 1, 1)
    return pl.pallas_call(
        functools.partial(_attn_full_kernel, scale=scale, use_sink=use_sink),
        grid=(b, h, sq // tq),
        in_specs=[pl.BlockSpec((1, 1, tq, d), lambda i, j, t: (i, j, t, 0)),
                  pl.BlockSpec((1, 1, sk, d), lambda i, j, t: (i, j // g, 0, 0)),
                  pl.BlockSpec((1, 1, sk, dv), lambda i, j, t: (i, j // g, 0, 0)),
                  pl.BlockSpec((1, 1, 1), lambda i, j, t: (j, 0, 0))],
        out_specs=pl.BlockSpec((1, 1, tq, dv), lambda i, j, t: (i, j, t, 0)),
        out_shape=jax.ShapeDtypeStruct((b, h, sq, dv), F32),
        compiler_params=pltpu.CompilerParams(
            dimension_semantics=("arbitrary", "arbitrary", "arbitrary"),
            vmem_limit_bytes=V7X_VMEM_LIMIT_BYTES),
        name="attn_full",
    )(q, k, v, sink_arr)


def _attn_window_kernel(q_ref, kp_ref, k0_ref, kn_ref, kc_ref, vp_ref, v0_ref, vn_ref, vc_ref, sink_ref, o_ref,
                        *, scale, group, n_blocks):
    n = pl.program_id(2)
    w = WINDOW
    d = q_ref.shape[-1]
    q = q_ref[0].reshape(group * w, d)
    keys = jnp.concatenate([kp_ref[0, 0], k0_ref[0, 0], kn_ref[0, 0], kc_ref[0, 0]], axis=0)
    vals = jnp.concatenate([vp_ref[0, 0], v0_ref[0, 0], vn_ref[0, 0], vc_ref[0, 0]], axis=0)
    s = _dot(q, keys, NT) * scale
    nk = keys.shape[0]
    qi = lax.broadcasted_iota(jnp.int32, (group * w, nk), 0) % w
    kj = lax.broadcasted_iota(jnp.int32, (group * w, nk), 1)
    rel = qi + w - kj
    band_ok = (jnp.abs(rel) <= w) & ((kj >= w) | (n > 0)) & ((kj < 2 * w) | (n < n_blocks - 1))
    s = jnp.where((kj >= 3 * w) | band_ok, s, NEG_INF)
    sink = jnp.concatenate([jnp.broadcast_to(sink_ref[0, hh], (w, 1)) for hh in range(group)], axis=0)
    o = _softmax_pv(s, vals, sink)
    o_ref[0] = o.reshape(group, w, o.shape[-1])


def attn_window(q, k, v, k_ctx, v_ctx, sink):
    b, h, s, d = q.shape
    hk = k.shape[1]
    g = h // hk
    nb = s // WINDOW
    lc = k_ctx.shape[2]
    sink_arr = sink.astype(F32).reshape(hk, g, 1, 1)

    def blk(f):
        return pl.BlockSpec((1, 1, WINDOW, d), f)

    prev = blk(lambda i, j, n: (i, j, jnp.maximum(n - 1, 0), 0))
    own = blk(lambda i, j, n: (i, j, n, 0))
    nxt = blk(lambda i, j, n: (i, j, jnp.minimum(n + 1, nb - 1), 0))
    ctx = pl.BlockSpec((1, 1, lc, d), lambda i, j, n: (i, j, 0, 0))
    return pl.pallas_call(
        functools.partial(_attn_window_kernel, scale=d ** -0.5, group=g, n_blocks=nb),
        grid=(b, hk, nb),
        in_specs=[pl.BlockSpec((1, g, WINDOW, d), lambda i, j, n: (i, j, n, 0)),
                  prev, own, nxt, ctx, prev, own, nxt, ctx,
                  pl.BlockSpec((1, g, 1, 1), lambda i, j, n: (j, 0, 0, 0))],
        out_specs=pl.BlockSpec((1, g, WINDOW, d), lambda i, j, n: (i, j, n, 0)),
        out_shape=jax.ShapeDtypeStruct((b, h, s, d), F32),
        compiler_params=pltpu.CompilerParams(
            dimension_semantics=("arbitrary", "arbitrary", "arbitrary"),
            vmem_limit_bytes=V7X_VMEM_LIMIT_BYTES),
        name="attn_window",
    )(q, k, k, k, k_ctx, v, v, v, v_ctx, sink_arr)


def _heads_major(z):
    return jnp.swapaxes(z, 1, 2)


def mla_operands(qn, qr, kn, kr, vb):
    b, s, h = qn.shape[:3]
    kt = kn.shape[1]
    pad = LANE - B_NOPE - B_ROPE
    q_cat = jnp.concatenate([qn, qr, jnp.zeros((b, s, h, pad), F32)], axis=-1)
    k_cat = jnp.concatenate([kn, jnp.broadcast_to(kr[:, :, None, :], (b, kt, h, B_ROPE)),
                             jnp.zeros((b, kt, h, pad), F32)], axis=-1)
    return _heads_major(q_cat), _heads_major(k_cat), _heads_major(vb)


def mla_attention(qn, qr, kn, kr, vb):
    b, s = qn.shape[:2]
    o = attn_full(*mla_operands(qn, qr, kn, kr, vb), (B_NOPE + B_ROPE) ** -0.5)
    return _heads_major(o).reshape(b, s, -1)


def ab_prep(p, a_q_norm, a_k_norm, b_cq_norm, b_ckv_norm, b_w_uq, b_w_uk, b_w_uv,
            b_qn_norm, b_qr_norm, b_kn_norm, b_kr_norm):
    b, t = p.shape[:2]
    qa, ka, va, cq, ckv, kr = split_cols(p, AB_SIZES)
    qa = rms_norm(qa.reshape(b, t, A_HEADS, A_HEAD_DIM), a_q_norm)
    ka = rms_norm(ka.reshape(b, t, A_KV_HEADS, A_HEAD_DIM), a_k_norm)
    va = va.reshape(b, t, A_KV_HEADS, A_HEAD_DIM)
    qb = _mm(rms_norm(cq, b_cq_norm), b_w_uq).reshape(b, t, B_HEADS, B_NOPE + B_ROPE)
    qn = rms_norm(qb[..., :B_NOPE], b_qn_norm)
    qr = rms_norm(qb[..., B_NOPE:], b_qr_norm)
    ckv = rms_norm(ckv, b_ckv_norm)
    kn = rms_norm(_mm(ckv, b_w_uk).reshape(b, t, B_HEADS, B_NOPE), b_kn_norm)
    vb = _mm(ckv, b_w_uv).reshape(b, t, B_HEADS, B_V_DIM)
    kr = rms_norm(kr, b_kr_norm)
    return qa, ka, va, qn, qr, kn, kr, vb


def mixer_ab(p_lat, p_ctx, row, col, a_q_norm, a_k_norm, a_sink, b_cq_norm, b_ckv_norm, b_w_uq, b_w_uk,
             b_w_uv, b_qn_norm, b_qr_norm, b_kn_norm, b_kr_norm, ctx_out):
    prm = (a_q_norm, a_k_norm, b_cq_norm, b_ckv_norm, b_w_uq, b_w_uk, b_w_uv,
           b_qn_norm, b_qr_norm, b_kn_norm, b_kr_norm)
    qa_c, ka_c, va_c, qn_c, qr_c, kn_c, kr_c, vb_c = ab_prep(p_ctx, *prm)
    qa_l, ka_l, va_l, qn_l, qr_l, kn_l, kr_l, vb_l = ab_prep(p_lat, *prm)
    qa_l = rope_2d(qa_l, row, col)
    ka_l = rope_2d(ka_l, row, col)
    qr_l = rope_2d(qr_l, row, col)
    kr_l = rope_2d(kr_l[:, :, None, :], row, col)[:, :, 0, :]
    b, s = p_lat.shape[:2]
    n = p_ctx.shape[1]
    hm = _heads_major
    o_a = hm(attn_window(hm(qa_l), hm(ka_l), hm(va_l), hm(ka_c), hm(va_c), a_sink)).reshape(b, s, -1)
    o_b = mla_attention(qn_l, qr_l, jnp.concatenate([kn_c, kn_l], axis=1),
                        jnp.concatenate([kr_c, kr_l], axis=1), jnp.concatenate([vb_c, vb_l], axis=1))
    y_lat = jnp.concatenate([o_a, o_b], axis=-1)
    y_ctx = None
    if ctx_out:
        o_ac = hm(attn_full(hm(qa_c), hm(ka_c), hm(va_c), A_HEAD_DIM ** -0.5, sink=a_sink)).reshape(b, n, -1)
        y_ctx = jnp.concatenate([o_ac, mla_attention(qn_c, qr_c, kn_c, kr_c, vb_c)], axis=-1)
    return y_lat, y_ctx


RW_CHUNK = 64
HEAD_PAIR = 2 * C_HEAD


def _unit_lower_inverse(xs, eye):
    ps = [eye + x for x in xs]
    xps = list(xs)
    for _ in range(int(np.log2(RW_CHUNK)) - 1):
        xps = [_dot(xp, xp, NN, passes=3) for xp in xps]
        ps = [p + _dot(p, xp, NN, passes=3) for p, xp in zip(ps, xps)]
    return ps


def _rwkv_kernel(r_ref, v_ref, kk_ref, lw_ref, a_ref, k_ref, y_ref, st_ref):
    d = pl.program_id(1)
    s = pl.program_id(2)

    @pl.when(s == 0)
    def _():
        st_ref[...] = jnp.zeros_like(st_ref)

    c_len = RW_CHUNK
    sign = 1 - 2 * d
    ii = lax.broadcasted_iota(jnp.int32, (c_len, c_len), 0)
    jj = lax.broadcasted_iota(jnp.int32, (c_len, c_len), 1)
    tri = jnp.where((ii - jj) * sign >= 0, 1.0, 0.0).astype(F32)
    lw_all = lw_ref[0, 0]
    cum_all = _dot_exact(tri, lw_all, NN)
    tot_all = jnp.sum(lw_all, axis=0, keepdims=True)

    r2 = lax.broadcasted_iota(jnp.int32, (HEAD_PAIR, HEAD_PAIR), 0)
    c2 = lax.broadcasted_iota(jnp.int32, (HEAD_PAIR, HEAD_PAIR), 1)
    dlt = ((r2 % c_len) - (c2 % c_len)) * sign
    m_strict = dlt > 0
    m_incl = dlt >= 0
    eye = jnp.where(r2 == c2, 1.0, 0.0).astype(F32)
    lane = lax.broadcasted_iota(jnp.int32, (c_len, HEAD_PAIR), 1)
    m0 = lane < C_HEAD

    def stack_heads(x):
        return jnp.concatenate([jnp.where(m0, x, 0.0), jnp.where(m0, 0.0, x)], axis=0)

    pairs = range(C_DIM // HEAD_PAIR)
    sls = [slice(p * HEAD_PAIR, (p + 1) * HEAD_PAIR) for p in pairs]
    kk_all, kd_all, a_all, r_all, v_all = kk_ref[0], k_ref[0, 0], a_ref[0, 0], r_ref[0], v_ref[0]
    st = [st_ref[p] for p in pairs]
    cum = [cum_all[:, sl] for sl in sls]
    tot = [tot_all[:, sl] for sl in sls]
    kk = [kk_all[:, sl] for sl in sls]
    kd = [kd_all[:, sl] for sl in sls]
    bb = [kk[p] * a_all[:, sls[p]] for p in pairs]
    e_neg = [jnp.exp(-cum[p]) for p in pairs]
    e_end = [jnp.exp(tot[p] - cum[p]) for p in pairs]
    abar = [stack_heads(-kk[p] * jnp.exp(cum[p] - lw_all[:, sls[p]])) for p in pairs]
    rbar = [stack_heads(r_all[:, sls[p]] * jnp.exp(cum[p])) for p in pairs]
    ktil = [stack_heads(kd[p] * e_neg[p]) for p in pairs]
    btil = [stack_heads(bb[p] * e_neg[p]) for p in pairs]
    khat = [stack_heads(kd[p] * e_end[p]) for p in pairs]
    bhat = [stack_heads(bb[p] * e_end[p]) for p in pairs]
    vs = [stack_heads(v_all[:, sl]) for sl in sls]

    g = [_dot(jnp.concatenate([abar[p], rbar[p]], axis=0), jnp.concatenate([ktil[p], btil[p]], axis=0), NT)
         for p in pairs]
    a_ak = [jnp.where(m_strict, g[p][:HEAD_PAIR, :HEAD_PAIR], 0.0) for p in pairs]
    a_ab = [jnp.where(m_strict, g[p][:HEAD_PAIR, HEAD_PAIR:], 0.0) for p in pairs]
    a_rk = [jnp.where(m_incl, g[p][HEAD_PAIR:, :HEAD_PAIR], 0.0) for p in pairs]
    a_rb = [jnp.where(m_incl, g[p][HEAD_PAIR:, HEAD_PAIR:], 0.0) for p in pairs]
    tinv = _unit_lower_inverse(a_ab, eye)
    wm = [_dot(tinv[p], abar[p], NN) for p in pairs]
    akv = [_dot(a_ak[p], vs[p], NN) for p in pairs]
    u0 = [_dot(tinv[p], akv[p], NN) for p in pairs]
    u = [_dot(wm[p], st[p], NT) + u0[p] for p in pairs]
    ys = [_dot(rbar[p], st[p], NT) + _dot(a_rk[p], vs[p], NN) + _dot(a_rb[p], u[p], NN) for p in pairs]
    st_new = [st[p] * jnp.exp(tot[p]) + _dot(vs[p], khat[p], TN) + _dot(u[p], bhat[p], TN) for p in pairs]
    y_ref[0, 0] = jnp.concatenate([y[:c_len] + y[c_len:] for y in ys], axis=1)
    for p in pairs:
        st_ref[p] = st_new[p]


def rwkv_chunked(r, v, kk, lw2, a2, k2, n_ctx, interpret=False):
    b, t, cd = r.shape
    nc = t // RW_CHUNK
    nctx = n_ctx // RW_CHUNK

    def chunk(d, s):
        rev = jnp.where(s < nctx, nctx - 1 - s, nc + nctx - 1 - s)
        return jnp.where(d == 0, s, rev)

    shared = pl.BlockSpec((1, RW_CHUNK, cd), lambda i, d, s: (i, chunk(d, s), 0))
    per_dir = pl.BlockSpec((1, 1, RW_CHUNK, cd), lambda i, d, s: (d, i, chunk(d, s), 0))
    return pl.pallas_call(
        _rwkv_kernel,
        grid=(b, 2, nc),
        in_specs=[shared, shared, shared, per_dir, per_dir, per_dir],
        out_specs=per_dir,
        out_shape=jax.ShapeDtypeStruct((2, b, t, cd), F32),
        scratch_shapes=[pltpu.VMEM((cd // HEAD_PAIR, HEAD_PAIR, HEAD_PAIR), F32)],
        compiler_params=pltpu.CompilerParams(
            dimension_semantics=("arbitrary", "arbitrary", "arbitrary"),
            vmem_limit_bytes=V7X_VMEM_LIMIT_BYTES),
        name="rwkv7_chunked",
        interpret=interpret,
    )(r, v, kk, lw2, a2, k2)


def token_shift_centred(p, mu_prev, mu_next):
    prev = jnp.pad(p, ((0, 0), (1, 0), (0, 0)))[:, :-1]
    nxt = jnp.pad(p, ((0, 0), (0, 1), (0, 0)))[:, 1:]
    return p + mu_prev * (prev - p) + mu_next * (nxt - p)


def rwkv_prep(pc, c_mu_prev, c_mu_next, c_w0, c_w2, c_a0, c_a2, c_g2, c_k_k, c_k_a):
    b, t = pc.shape[:2]

    def heads(z):
        return z.reshape(b, t, C_HEADS, C_HEAD)

    xs = token_shift_centred(pc, c_mu_prev, c_mu_next)
    r, k, v, wl_f, wl_b, al_f, al_b, gl = split_cols(xs, C_SIZES)

    def decay(wl, w0, w2):
        w = -jax.nn.softplus(-(w0 + _mm(jnp.tanh(wl), w2))) - 0.5
        return heads(-jnp.exp(w.astype(F32)))

    decays = (decay(wl_f, c_w0[0], c_w2[0]), decay(wl_b, c_w0[1], c_w2[1]))
    iclr = (jax.nn.sigmoid(c_a0[0] + _mm(al_f, c_a2[0])), jax.nn.sigmoid(c_a0[1] + _mm(al_b, c_a2[1])))
    g = _mm(jax.nn.sigmoid(gl), c_g2)
    kk = l2_normalize(heads(k * c_k_k))
    ks = tuple(heads(k * (1.0 + (a - 1.0) * c_k_a)) for a in iclr)
    return heads(r), heads(v), kk, g, decays, tuple(heads(a) for a in iclr), ks


def head_group_norm(y, w, b_):
    mean = jnp.mean(y, axis=-1, keepdims=True)
    var = jnp.mean(jnp.square(y - mean), axis=-1, keepdims=True)
    yn = (y - mean) * lax.rsqrt(var + C_GN_EPS)
    return yn * w.astype(F32).reshape(C_HEADS, C_HEAD) + b_.astype(F32).reshape(C_HEADS, C_HEAD)


def rwkv_output(y, r, ks, v, g, c_r_k, c_ln_w, c_ln_b):
    b, t = y.shape[:2]
    yn = head_group_norm(y, c_ln_w, c_ln_b)
    bonus = sum(jnp.sum(r * kd * c_r_k, axis=-1, keepdims=True) * v for kd in ks)
    return (yn + bonus).reshape(b, t, C_DIM) * g


def gdn_prep(pd, d_conv_w, d_A_log, d_dt_bias):
    b, t = pd.shape[:2]
    qkv, z, bf, bb, af, ab = split_cols(pd, D_SIZES)
    qkv = jax.nn.silu(dwconv_centred(qkv, d_conv_w))
    q, k, v = jnp.split(qkv, 3, axis=-1)
    q = l2_normalize(q.reshape(b, t, D_HEADS, D_HEAD_DIM)) * D_HEAD_DIM ** -0.5
    k = l2_normalize(k.reshape(b, t, D_HEADS, D_HEAD_DIM))
    v = v.reshape(b, t, D_HEADS, D_HEAD_DIM)
    betas = (jax.nn.sigmoid(bf), jax.nn.sigmoid(bb))
    gs = tuple(-jnp.exp(d_A_log[i].astype(F32)) * jax.nn.softplus((al + d_dt_bias[i]).astype(F32))
               for i, al in enumerate((af, ab)))
    return q, k, v, z, betas, gs


def _gdn_kernel(q_ref, k_ref, v_ref, beta_ref, g_ref, gt_ref, o_ref, st_ref):
    d = pl.program_id(1)
    s = pl.program_id(2)

    @pl.when(s == 0)
    def _():
        st_ref[...] = jnp.zeros_like(st_ref)

    c_len = D_CHUNK
    n2 = 2 * c_len
    sign = 1 - 2 * d
    r2 = lax.broadcasted_iota(jnp.int32, (n2, n2), 0)
    c2 = lax.broadcasted_iota(jnp.int32, (n2, n2), 1)
    same = (r2 // c_len) == (c2 // c_len)
    dlt = ((r2 % c_len) - (c2 % c_len)) * sign
    m_strict = same & (dlt > 0)
    m_incl = same & (dlt >= 0)
    tri = jnp.where(m_incl, 1.0, 0.0).astype(F32)
    ones_bd = jnp.where(same, 1.0, 0.0).astype(F32)
    eye = jnp.where(r2 == c2, 1.0, 0.0).astype(F32)

    pairs = range(D_HEADS // 2)
    heads = range(D_HEADS)
    q_blk, k_blk, v_blk = q_ref[0], k_ref[0], v_ref[0]
    g_blk = g_ref[0, 0, 0]
    b_blk = beta_ref[0, 0, 0]
    gt_blk = gt_ref[0, 0, 0]
    st = [st_ref[h] for h in heads]

    def rows(blk, p):
        return jnp.concatenate([blk[:, h * D_HEAD_DIM:(h + 1) * D_HEAD_DIM] for h in (2 * p, 2 * p + 1)], axis=0)

    def col(blk, p):
        c = jnp.concatenate([blk[:, h:h + 1] for h in (2 * p, 2 * p + 1)], axis=0)
        return jnp.broadcast_to(c, (n2, n2))

    q_s = [rows(q_blk, p) for p in pairs]
    k_s = [rows(k_blk, p) for p in pairs]
    v_s = [rows(v_blk, p) for p in pairs]
    g_colb = [col(g_blk, p) for p in pairs]
    beta = [col(b_blk, p) for p in pairs]
    g_rowb = [jnp.broadcast_to(jnp.concatenate([gt_blk[h:h + 1, :] for h in (2 * p, 2 * p + 1)], axis=1), (n2, n2))
              for p in pairs]
    gc_col = [_dot_exact(tri, g_colb[p], NN) for p in pairs]
    gc_row = [_dot_exact(g_rowb[p], tri, NT) for p in pairs]
    g_end = [_dot_exact(ones_bd, g_colb[p], NN) for p in pairs]
    decay = [jnp.where(m_incl, jnp.exp(jnp.where(m_incl, gc_col[p] - gc_row[p], 0.0)), 0.0) for p in pairs]
    kb = [k_s[p] * beta[p] for p in pairs]
    vb = [v_s[p] * beta[p] for p in pairs]
    g_mat = [_dot(jnp.concatenate([kb[p], q_s[p]], axis=0), k_s[p], NT) for p in pairs]
    l_mat = [jnp.where(m_strict, g_mat[p][:n2] * decay[p], 0.0) for p in pairs]
    a_intra = [g_mat[p][n2:] * decay[p] for p in pairs]
    tinv = _unit_lower_inverse([-l for l in l_mat], eye)
    u = [_dot(tinv[p], vb[p], NN) for p in pairs]
    wk = [_dot(tinv[p], kb[p] * jnp.exp(gc_col[p]), NN) for p in pairs]
    q_e = [q_s[p] * jnp.exp(gc_col[p]) for p in pairs]
    k_e = [k_s[p] * jnp.exp(g_end[p] - gc_col[p]) for p in pairs]

    def head_rows(x, h):
        j = h % 2
        return x[h // 2][j * c_len:(j + 1) * c_len]

    v_new = [head_rows(u, h) - _dot(head_rows(wk, h), st[h], NN) for h in heads]
    o_st = [_dot(head_rows(q_e, h), st[h], NN) for h in heads]
    v_new_s = [jnp.concatenate([v_new[2 * p], v_new[2 * p + 1]], axis=0) for p in pairs]
    o_s = [jnp.concatenate([o_st[2 * p], o_st[2 * p + 1]], axis=0) + _dot(a_intra[p], v_new_s[p], NN) for p in pairs]
    st_new = [st[h] * jnp.exp(head_rows(g_end, h)[0:1, :]) + _dot(head_rows(k_e, h), v_new[h], TN) for h in heads]
    o_ref[0, 0] = jnp.concatenate([head_rows(o_s, h) for h in heads], axis=1)
    for h in heads:
        st_ref[h] = st_new[h]


def gdn_chunked(q, k, v, beta2, g2, n_ctx, interpret=False):
    b, t, cd = q.shape
    nc = t // D_CHUNK
    nctx = n_ctx // D_CHUNK
    g2c = g2.reshape(2, b, nc, D_CHUNK, D_HEADS)
    gt2 = jnp.swapaxes(g2c, -1, -2)
    beta2c = beta2.reshape(2, b, nc, D_CHUNK, D_HEADS)

    def chunk(d, s):
        rev = jnp.where(s < nctx, nctx - 1 - s, nc + nctx - 1 - s)
        return jnp.where(d == 0, s, rev)

    shared = pl.BlockSpec((1, D_CHUNK, cd), lambda i, d, s: (i, chunk(d, s), 0))
    per_dir = pl.BlockSpec((1, 1, D_CHUNK, cd), lambda i, d, s: (d, i, chunk(d, s), 0))
    small = pl.BlockSpec((1, 1, 1, D_CHUNK, D_HEADS), lambda i, d, s: (d, i, chunk(d, s), 0, 0))
    small_t = pl.BlockSpec((1, 1, 1, D_HEADS, D_CHUNK), lambda i, d, s: (d, i, chunk(d, s), 0, 0))
    return pl.pallas_call(
        _gdn_kernel,
        grid=(b, 2, nc),
        in_specs=[shared, shared, shared, small, small, small_t],
        out_specs=per_dir,
        out_shape=jax.ShapeDtypeStruct((2, b, t, cd), F32),
        scratch_shapes=[pltpu.VMEM((D_HEADS, D_HEAD_DIM, D_HEAD_DIM), F32)],
        compiler_params=pltpu.CompilerParams(
            dimension_semantics=("arbitrary", "arbitrary", "arbitrary"),
            vmem_limit_bytes=V7X_VMEM_LIMIT_BYTES),
        name="gdn_chunked",
        interpret=interpret,
    )(q, k, v, beta2c, g2c, gt2)


def gdn_output(o, z, d_o_norm):
    b, t = o.shape[:2]
    gate = jax.nn.silu(z.reshape(b, t, D_HEADS, D_HEAD_DIM).astype(F32))
    return (rms_norm(o, d_o_norm) * gate).reshape(b, t, D_DIM)


def mixer_cd(p_lat, p_ctx, c_mu_prev, c_mu_next, c_w0, c_w2, c_a0, c_a2, c_g2, c_k_k, c_k_a, c_r_k,
             c_ln_w, c_ln_b, d_conv_w, d_A_log, d_dt_bias, d_o_norm, ctx_out):
    b = p_lat.shape[0]
    rw = (c_mu_prev, c_mu_next, c_w0, c_w2, c_a0, c_a2, c_g2, c_k_k, c_k_a)
    r_c, v_c, kk_c, g_c, dec_c, a_c, k_c = rwkv_prep(p_ctx[..., :IN_C], *rw)
    r_l, v_l, kk_l, g_l, dec_l, a_l, k_l = rwkv_prep(p_lat[..., :IN_C], *rw)
    n_ctx = p_ctx.shape[1]

    def seq(zc, zl):
        return jnp.concatenate([zc, zl], axis=1).reshape(b, -1, C_DIM)

    y2 = rwkv_chunked(seq(r_c, r_l), seq(v_c, v_l), seq(kk_c, kk_l),
                      jnp.stack([seq(dec_c[i], dec_l[i]) for i in range(2)]),
                      jnp.stack([seq(a_c[i], a_l[i]) for i in range(2)]),
                      jnp.stack([seq(k_c[i], k_l[i]) for i in range(2)]), n_ctx)
    y_all = (y2[0] + y2[1]).reshape(b, -1, C_HEADS, C_HEAD)
    y_c, y_l = y_all[:, :n_ctx], y_all[:, n_ctx:]

    q_c, kd_c, vd_c, z_c, beta_c, gd_c = gdn_prep(p_ctx[..., IN_C:], d_conv_w, d_A_log, d_dt_bias)
    q_l, kd_l, vd_l, z_l, beta_l, gd_l = gdn_prep(p_lat[..., IN_C:], d_conv_w, d_A_log, d_dt_bias)

    def seq_d(zc, zl):
        return jnp.concatenate([zc, zl], axis=1).reshape(b, zc.shape[1] + zl.shape[1], -1)

    o2 = gdn_chunked(seq_d(q_c, q_l), seq_d(kd_c, kd_l), seq_d(vd_c, vd_l),
                     jnp.stack([seq_d(beta_c[i], beta_l[i]) for i in range(2)]),
                     jnp.stack([seq_d(gd_c[i], gd_l[i]) for i in range(2)]), n_ctx)
    o_all = (o2[0] + o2[1]).reshape(b, -1, D_HEADS, D_HEAD_DIM)
    o_c, o_l = o_all[:, :n_ctx], o_all[:, n_ctx:]

    y_lat = jnp.concatenate([rwkv_output(y_l, r_l, k_l, v_l, g_l, c_r_k, c_ln_w, c_ln_b),
                             gdn_output(o_l, z_l, d_o_norm)], axis=-1).astype(p_lat.dtype)
    y_ctx = None
    if ctx_out:
        y_ctx = jnp.concatenate([rwkv_output(y_c, r_c, k_c, v_c, g_c, c_r_k, c_ln_w, c_ln_b),
                                 gdn_output(o_c, z_c, d_o_norm)], axis=-1).astype(p_ctx.dtype)
    return y_lat, y_ctx


FFN_TM = 256
FFN_HALO = 8
FFN_FC = 256
FFN_VMEM_LIMIT_BYTES = 56 * 1024 * 1024


def _modulated(x, shift, scale):
    y = x * lax.rsqrt(jnp.mean(x * x, axis=-1, keepdims=True) + EPS)
    return y * (1.0 + scale) + shift


def _ffn_kernel(xp_ref, x_ref, xn_ref, shift_ref, scale_ref, gate_ref, wup_ref, cw_ref, cb_ref, wdn_ref, o_ref, acc_ref,
                *, n_tiles):
    i = pl.program_id(1)
    tm, halo = FFN_TM, FFN_HALO
    rows = tm + 2 * halo
    x = x_ref[0]
    xe = jnp.concatenate([xp_ref[0], x, xn_ref[0]], axis=0)
    h = _modulated(xe, shift_ref[0], scale_ref[0]).astype(BF16)
    r = lax.broadcasted_iota(jnp.int32, (rows, 1), 0)
    keep_prev = jnp.where((r == halo) & (i == 0), 0.0, 1.0)
    keep_next = jnp.where((r == halo + tm - 1) & (i == n_tiles - 1), 0.0, 1.0)
    acc_ref[...] = jnp.zeros_like(acc_ref)

    def conv(u, c0):
        w = cw_ref[:, c0:c0 + FFN_FC]
        um = pltpu.roll(u, 1, 0) * keep_prev
        up = pltpu.roll(u, rows - 1, 0) * keep_next
        y = um * w[0:1] + u * w[1:2] + up * w[2:3] + cb_ref[:, c0:c0 + FFN_FC]
        return y[halo:halo + tm]

    for c in range(D_FF // FFN_FC):
        cv, cg = c * FFN_FC, D_FF + c * FFN_FC
        val = conv(jnp.dot(h, wup_ref[:, cv:cv + FFN_FC], preferred_element_type=F32), cv)
        gat = conv(jnp.dot(h, wup_ref[:, cg:cg + FFN_FC], preferred_element_type=F32), cg)
        act = (gat * jax.nn.sigmoid(gat) * val).astype(BF16)
        acc_ref[...] += jnp.dot(act, wdn_ref[cv:cv + FFN_FC, :], preferred_element_type=F32)
    o_ref[0] = x + gate_ref[0] * acc_ref[...]


def ffn_fused(x, shift, scale, gate, w_up, conv_w, conv_b, w_down):
    b, t, d = x.shape
    nt = t // FFN_TM
    hb = FFN_TM // FFN_HALO
    f2 = w_up.shape[1]
    mod = pl.BlockSpec((1, 1, d), lambda bi, i: (bi, 0, 0))

    def full(shp):
        return pl.BlockSpec(shp, lambda bi, i: (0,) * len(shp))

    return pl.pallas_call(
        functools.partial(_ffn_kernel, n_tiles=nt),
        grid=(b, nt),
        in_specs=[pl.BlockSpec((1, FFN_HALO, d), lambda bi, i: (bi, jnp.maximum(i * hb - 1, 0), 0)),
                  pl.BlockSpec((1, FFN_TM, d), lambda bi, i: (bi, i, 0)),
                  pl.BlockSpec((1, FFN_HALO, d), lambda bi, i: (bi, jnp.minimum((i + 1) * hb, nt * hb - 1), 0)),
                  mod, mod, mod,
                  full((d, f2)), full((FFN_CONV, f2)), full((1, f2)), full((f2 // 2, d))],
        out_specs=pl.BlockSpec((1, FFN_TM, d), lambda bi, i: (bi, i, 0)),
        out_shape=jax.ShapeDtypeStruct((b, t, d), F32),
        scratch_shapes=[pltpu.VMEM((FFN_TM, d), F32)],
        compiler_params=pltpu.CompilerParams(
            dimension_semantics=("arbitrary", "arbitrary"),
            vmem_limit_bytes=FFN_VMEM_LIMIT_BYTES),
        name="conv_ffn",
    )(x, x, x, shift, scale, gate, w_up.astype(BF16), conv_w, conv_b.reshape(1, f2), w_down.astype(BF16))


def kernel(x, c, ctx, c_ctx, ada_w, ada_b, ffn_w_up, ffn_conv_w, ffn_conv_b, ffn_w_down,
           ab_w_in, ab_w_out, a_q_norm, a_k_norm, a_sink, b_cq_norm, b_ckv_norm, b_w_uq, b_w_uk, b_w_uv,
           b_qn_norm, b_qr_norm, b_kn_norm, b_kr_norm, cd_w_in, cd_w_out, c_mu_prev, c_mu_next, c_w0, c_w2,
           c_a0, c_a2, c_g2, c_k_k, c_k_a, c_r_k, c_ln_w, c_ln_b, d_conv_w, d_A_log, d_dt_bias, d_o_norm):
    seq = x.shape[1]
    rows = seq // GRID_W
    row = jnp.repeat(jnp.arange(rows, dtype=jnp.int32), GRID_W)
    col = jnp.tile(jnp.arange(GRID_W, dtype=jnp.int32), rows)
    silu_c = jax.nn.silu(c)
    silu_cc = jax.nn.silu(c_ctx)
    for l in range(DEPTH):
        last = l == DEPTH - 1
        i = l // 2
        mod_l = jnp.split((silu_c @ ada_w[l] + ada_b[l])[:, None, :], N_MOD, axis=-1)
        mod_c = jnp.split(silu_cc @ ada_w[l] + ada_b[l], N_MOD, axis=-1)
        h_l = modulate(x, mod_l[0], mod_l[1])
        h_c = modulate(ctx, mod_c[0], mod_c[1])
        if l % 2 == 0:
            y_l, y_c = mixer_ab(_mm(h_l, ab_w_in[i]), _mm(h_c, ab_w_in[i]), row, col, a_q_norm[i], a_k_norm[i],
                                a_sink[i], b_cq_norm[i], b_ckv_norm[i], b_w_uq[i], b_w_uk[i], b_w_uv[i],
                                b_qn_norm[i], b_qr_norm[i], b_kn_norm[i], b_kr_norm[i], not last)
            w_out = ab_w_out[i]
        else:
            y_l, y_c = mixer_cd(_mm(h_l, cd_w_in[i]), _mm(h_c, cd_w_in[i]), c_mu_prev[i], c_mu_next[i], c_w0[i],
                                c_w2[i], c_a0[i], c_a2[i], c_g2[i], c_k_k[i], c_k_a[i], c_r_k[i], c_ln_w[i],
                                c_ln_b[i], d_conv_w[i], d_A_log[i], d_dt_bias[i], d_o_norm[i], not last)
            w_out = cd_w_out[i]
        ffn_w = (ffn_w_up[l], ffn_conv_w[l], ffn_conv_b[l], ffn_w_down[l])
        x = x + mod_l[2] * _mm(y_l, w_out)
        x = ffn_fused(x, mod_l[3], mod_l[4], mod_l[5], *ffn_w)
        if not last:
            ctx = ctx + mod_c[2] * _mm(y_c, w_out)
            rep = lambda m: jnp.broadcast_to(m[None, None, :], (ctx.shape[0], 1, D_MODEL))
            ctx = ffn_fused(ctx, rep(mod_c[3]), rep(mod_c[4]), rep(mod_c[5]), *ffn_w)
    return x
```

```python
import functools

import jax
import jax.numpy as jnp
from jax import lax
import numpy as np
from jax.experimental import pallas as pl
from jax.experimental.pallas import tpu as pltpu

D_MODEL = 1024
DEPTH = 2
GRID_W = 64
N_MOD = 6
EPS = 1e-6
ROPE_THETA = 10000.0
NEG_INF = -1e30

A_HEADS = 8
A_KV_HEADS = 2
A_HEAD_DIM = 64
WINDOW = 128
B_HEADS = 8
B_Q_RANK = 256
B_KV_RANK = 256
B_NOPE = 64
B_ROPE = 32
B_V_DIM = 64
C_HEADS = 8
C_HEAD = 64
C_DIM = C_HEADS * C_HEAD
C_DECAY_LORA = 64
C_AAA_LORA = 64
C_GATE_LORA = 128
C_GN_EPS = 64e-5
D_HEADS = 4
D_HEAD_DIM = 128
D_DIM = D_HEADS * D_HEAD_DIM
D_CONV = 5
D_CHUNK = 64
D_FF = 2816
FFN_CONV = 3

AB_SIZES = (A_HEADS * A_HEAD_DIM, A_KV_HEADS * A_HEAD_DIM, A_KV_HEADS * A_HEAD_DIM, B_Q_RANK, B_KV_RANK, B_ROPE)
C_SIZES = (C_DIM, C_DIM, C_DIM, C_DECAY_LORA, C_DECAY_LORA, C_AAA_LORA, C_AAA_LORA, C_GATE_LORA)
IN_C = sum(C_SIZES)
D_SIZES = (3 * D_DIM, D_DIM, D_HEADS, D_HEADS, D_HEADS, D_HEADS)

F32 = jnp.float32
BF16 = jnp.bfloat16

V7X_VMEM_LIMIT_BYTES = 48 * 1024 * 1024
LANE = 128
MXU_N = 256

NN = ((1,), (0,))
NT = ((1,), (1,))
TN = ((0,), (0,))


def _mm_kernel(a_ref, b_ref, o_ref):
    o_ref[...] = jnp.dot(a_ref[...].astype(BF16), b_ref[...], preferred_element_type=F32)


def _pick_tile(n, candidates):
    for c in candidates:
        if n % c == 0:
            return c
    raise ValueError(f"no tile for {n}")


def _mm(a, w):
    lead = a.shape[:-1]
    k = a.shape[-1]
    n = w.shape[-1]
    a2 = a.reshape(-1, k)
    m = a2.shape[0]
    n_pad = -(-n // MXU_N) * MXU_N
    wb = w.astype(BF16)
    if n_pad != n:
        wb = jnp.pad(wb, ((0, 0), (0, n_pad - n)))
    tm = _pick_tile(m, (1024, 512, 256, 128, 8))
    tn = _pick_tile(n_pad, (1024, 768, 512, 256))
    out = pl.pallas_call(
        _mm_kernel,
        grid=(m // tm, n_pad // tn),
        in_specs=[pl.BlockSpec((tm, k), lambda i, j: (i, 0)),
                  pl.BlockSpec((k, tn), lambda i, j: (0, j))],
        out_specs=pl.BlockSpec((tm, tn), lambda i, j: (i, j)),
        out_shape=jax.ShapeDtypeStruct((m, n_pad), F32),
        compiler_params=pltpu.CompilerParams(
            dimension_semantics=("arbitrary", "arbitrary"),
            vmem_limit_bytes=V7X_VMEM_LIMIT_BYTES),
        name="mm",
    )(a2, wb)
    if n_pad != n:
        out = out[:, :n]
    return out.reshape(*lead, n)


def _dot(a, b, dims, passes=1):
    def dg(x, y):
        return lax.dot_general(x, y, (dims, ((), ())), preferred_element_type=F32)

    ah, bh = a.astype(BF16), b.astype(BF16)
    if passes == 1:
        return dg(ah, bh)
    al = (a - ah.astype(F32)).astype(BF16)
    bl = (b - bh.astype(F32)).astype(BF16)
    return dg(ah, bh) + (dg(ah, bl) + dg(al, bh))


def _dot_exact(a, b, dims):
    return lax.dot_general(a, b, (dims, ((), ())), precision=lax.Precision.HIGHEST, preferred_element_type=F32)


def split_cols(p, sizes):
    return jnp.split(p, [int(s) for s in np.cumsum(sizes)[:-1]], axis=-1)


def rms_norm(x, gain=None, eps=EPS):
    y = x * lax.rsqrt(jnp.mean(x * x, axis=-1, keepdims=True) + eps)
    if gain is not None:
        y = y * gain
    return y


def l2_normalize(x, eps=1e-6):
    return x * lax.rsqrt(jnp.sum(x * x, axis=-1, keepdims=True) + eps)


def rope_1d(x, pos):
    half = x.shape[-1] // 2
    inv = jnp.power(ROPE_THETA, -jnp.arange(half, dtype=F32) / half)
    ang = pos.astype(F32)[:, None] * inv[None, :]
    cos = jnp.cos(ang)[None, :, None, :]
    sin = jnp.sin(ang)[None, :, None, :]
    x1, x2 = x[..., :half], x[..., half:]
    return jnp.concatenate([x1 * cos - x2 * sin, x1 * sin + x2 * cos], axis=-1)


def rope_2d(x, row, col):
    h = x.shape[-1] // 2
    return jnp.concatenate([rope_1d(x[..., :h], row), rope_1d(x[..., h:], col)], axis=-1)


def _token_shifted(p, offset, n_ctx):
    t = p.shape[1]
    lo, hi = max(-offset, 0), max(offset, 0)
    shifted = jnp.pad(p, ((0, 0), (lo, hi), (0, 0)))[:, hi:hi + t] if offset else p
    idx = jnp.arange(t)
    src = idx + offset
    ok = (src >= 0) & (src < t) & ((src >= n_ctx) == (idx >= n_ctx))
    return jnp.where(ok[None, :, None], shifted, 0.0)


def dwconv_centred(x, w, n_ctx):
    k = w.shape[0]
    return sum(_token_shifted(x, j - k // 2, n_ctx) * w[j] for j in range(k))


def _heads_major(z):
    return jnp.swapaxes(z, 1, 2)


def _softmax_pv(s, v, sink):
    m = jnp.max(s, axis=-1, keepdims=True)
    if sink is not None:
        m = jnp.maximum(m, sink)
    p = jnp.exp(s - m)
    den = jnp.sum(p, axis=-1, keepdims=True)
    if sink is not None:
        den = den + jnp.exp(sink - m)
    return _dot(p, v, NN) / den


def _attn_full_kernel(q_ref, k_ref, v_ref, sink_ref, o_ref, *, scale, use_sink):
    s = _dot(q_ref[0, 0], k_ref[0, 0], NT) * scale
    sink = sink_ref[0] if use_sink else None
    o_ref[0, 0] = _softmax_pv(s, v_ref[0, 0], sink)


def attn_full(q, k, v, scale, q_start, n_q, n_keys, sink=None):
    b, h, _, d = q.shape
    hk = k.shape[1]
    dv = v.shape[-1]
    g = h // hk
    tq = min(n_q, 256)
    q0 = q_start // tq
    use_sink = sink is not None
    sink_arr = (sink if use_sink else jnp.zeros((h,), F32)).astype(F32).reshape(h, 1, 1)
    return pl.pallas_call(
        functools.partial(_attn_full_kernel, scale=scale, use_sink=use_sink),
        grid=(b, h, n_q // tq),
        in_specs=[pl.BlockSpec((1, 1, tq, d), lambda i, j, t: (i, j, t + q0, 0)),
                  pl.BlockSpec((1, 1, n_keys, d), lambda i, j, t: (i, j // g, 0, 0)),
                  pl.BlockSpec((1, 1, n_keys, dv), lambda i, j, t: (i, j // g, 0, 0)),
                  pl.BlockSpec((1, 1, 1), lambda i, j, t: (j, 0, 0))],
        out_specs=pl.BlockSpec((1, 1, tq, dv), lambda i, j, t: (i, j, t, 0)),
        out_shape=jax.ShapeDtypeStruct((b, h, n_q, dv), F32),
        compiler_params=pltpu.CompilerParams(
            dimension_semantics=("arbitrary", "arbitrary", "arbitrary"),
            vmem_limit_bytes=V7X_VMEM_LIMIT_BYTES),
        name="attn_full",
    )(q, k, v, sink_arr)


def _attn_window_kernel(q_ref, kp_ref, k0_ref, kn_ref, kc_ref, vp_ref, v0_ref, vn_ref, vc_ref, sink_ref, o_ref,
                        *, scale, group, n_blocks):
    n = pl.program_id(2)
    w = WINDOW
    d = q_ref.shape[-1]
    q = q_ref[0].reshape(group * w, d)
    keys = jnp.concatenate([kp_ref[0, 0], k0_ref[0, 0], kn_ref[0, 0], kc_ref[0, 0]], axis=0)
    vals = jnp.concatenate([vp_ref[0, 0], v0_ref[0, 0], vn_ref[0, 0], vc_ref[0, 0]], axis=0)
    s = _dot(q, keys, NT) * scale
    nk = keys.shape[0]
    qi = lax.broadcasted_iota(jnp.int32, (group * w, nk), 0) % w
    kj = lax.broadcasted_iota(jnp.int32, (group * w, nk), 1)
    rel = qi + w - kj
    band_ok = (jnp.abs(rel) <= w) & ((kj >= w) | (n > 0)) & ((kj < 2 * w) | (n < n_blocks - 1))
    s = jnp.where((kj >= 3 * w) | band_ok, s, NEG_INF)
    sink = jnp.concatenate([jnp.broadcast_to(sink_ref[0, hh], (w, 1)) for hh in range(group)], axis=0)
    o = _softmax_pv(s, vals, sink)
    o_ref[0] = o.reshape(group, w, o.shape[-1])


def attn_window(q, k, v, sink, n_ctx):
    b, h, t, d = q.shape
    hk = k.shape[1]
    g = h // hk
    off = n_ctx // WINDOW
    nb = (t - n_ctx) // WINDOW
    sink_arr = sink.astype(F32).reshape(hk, g, 1, 1)

    def blk(f):
        return pl.BlockSpec((1, 1, WINDOW, d), f)

    prev = blk(lambda i, j, n: (i, j, jnp.maximum(n - 1, 0) + off, 0))
    own = blk(lambda i, j, n: (i, j, n + off, 0))
    nxt = blk(lambda i, j, n: (i, j, jnp.minimum(n + 1, nb - 1) + off, 0))
    ctx = pl.BlockSpec((1, 1, n_ctx, d), lambda i, j, n: (i, j, 0, 0))
    return pl.pallas_call(
        functools.partial(_attn_window_kernel, scale=d ** -0.5, group=g, n_blocks=nb),
        grid=(b, hk, nb),
        in_specs=[pl.BlockSpec((1, g, WINDOW, d), lambda i, j, n: (i, j, n + off, 0)),
                  prev, own, nxt, ctx, prev, own, nxt, ctx,
                  pl.BlockSpec((1, g, 1, 1), lambda i, j, n: (j, 0, 0, 0))],
        out_specs=pl.BlockSpec((1, g, WINDOW, d), lambda i, j, n: (i, j, n, 0)),
        out_shape=jax.ShapeDtypeStruct((b, h, t - n_ctx, d), F32),
        compiler_params=pltpu.CompilerParams(
            dimension_semantics=("arbitrary", "arbitrary", "arbitrary"),
            vmem_limit_bytes=V7X_VMEM_LIMIT_BYTES),
        name="attn_window",
    )(q, k, k, k, k, v, v, v, v, sink_arr)


def ab_prep(p, a_q_norm, a_k_norm, b_cq_norm, b_ckv_norm, b_w_uq, b_w_uk, b_w_uv,
            b_qn_norm, b_qr_norm, b_kn_norm, b_kr_norm):
    b, t = p.shape[:2]
    qa, ka, va, cq, ckv, kr = split_cols(p, AB_SIZES)
    qa = rms_norm(qa.reshape(b, t, A_HEADS, A_HEAD_DIM), a_q_norm)
    ka = rms_norm(ka.reshape(b, t, A_KV_HEADS, A_HEAD_DIM), a_k_norm)
    va = va.reshape(b, t, A_KV_HEADS, A_HEAD_DIM)
    qb = _mm(rms_norm(cq, b_cq_norm), b_w_uq).reshape(b, t, B_HEADS, B_NOPE + B_ROPE)
    qn = rms_norm(qb[..., :B_NOPE], b_qn_norm)
    qr = rms_norm(qb[..., B_NOPE:], b_qr_norm)
    ckv = rms_norm(ckv, b_ckv_norm)
    kn = rms_norm(_mm(ckv, b_w_uk).reshape(b, t, B_HEADS, B_NOPE), b_kn_norm)
    vb = _mm(ckv, b_w_uv).reshape(b, t, B_HEADS, B_V_DIM)
    kr = rms_norm(kr, b_kr_norm)
    return qa, ka, va, qn, qr, kn, kr, vb


def mixer_ab(p, n_ctx, row, col, a_q_norm, a_k_norm, a_sink, b_cq_norm, b_ckv_norm, b_w_uq, b_w_uk,
             b_w_uv, b_qn_norm, b_qr_norm, b_kn_norm, b_kr_norm, ctx_out):
    b, t = p.shape[:2]
    qa, ka, va, qn, qr, kn, kr, vb = ab_prep(p, a_q_norm, a_k_norm, b_cq_norm, b_ckv_norm, b_w_uq, b_w_uk, b_w_uv,
                                             b_qn_norm, b_qr_norm, b_kn_norm, b_kr_norm)
    qa = rope_2d(qa, row, col)
    ka = rope_2d(ka, row, col)
    qr = rope_2d(qr, row, col)
    kr = rope_2d(kr[:, :, None, :], row, col)
    hm = _heads_major
    qa_t, ka_t, va_t = hm(qa), hm(ka), hm(va)
    pad = jnp.zeros((b, t, B_HEADS, LANE - B_NOPE - B_ROPE), F32)
    q_cat = hm(jnp.concatenate([qn, qr, pad], axis=-1))
    k_cat = hm(jnp.concatenate([kn, jnp.broadcast_to(kr, (b, t, B_HEADS, B_ROPE)), pad], axis=-1))
    vb_t = hm(vb)
    b_scale = (B_NOPE + B_ROPE) ** -0.5

    o_a = attn_window(qa_t, ka_t, va_t, a_sink, n_ctx)
    o_b = attn_full(q_cat, k_cat, vb_t, b_scale, n_ctx, t - n_ctx, t)
    if ctx_out:
        o_a_c = attn_full(qa_t, ka_t, va_t, A_HEAD_DIM ** -0.5, 0, n_ctx, n_ctx, sink=a_sink)
        o_b_c = attn_full(q_cat, k_cat, vb_t, b_scale, 0, n_ctx, n_ctx)
    else:
        o_a_c = jnp.zeros((b, A_HEADS, n_ctx, A_HEAD_DIM), F32)
        o_b_c = jnp.zeros((b, B_HEADS, n_ctx, B_V_DIM), F32)
    o_a = hm(jnp.concatenate([o_a_c, o_a], axis=2)).reshape(b, t, -1)
    o_b = hm(jnp.concatenate([o_b_c, o_b], axis=2)).reshape(b, t, -1)
    return jnp.concatenate([o_a, o_b], axis=-1)


RW_CHUNK = 64
HEAD_PAIR = 2 * C_HEAD


def _unit_lower_inverse(xs, eye):
    ps = [eye + x for x in xs]
    xps = list(xs)
    for _ in range(int(np.log2(RW_CHUNK)) - 1):
        xps = [_dot(xp, xp, NN, passes=3) for xp in xps]
        ps = [p + _dot(p, xp, NN, passes=3) for p, xp in zip(ps, xps)]
    return ps


def _chunk_maps(nc, nctx):
    def fwd(s):
        return s

    def bwd(s):
        return jnp.where(s < nctx, nctx - 1 - s, nc + nctx - 1 - s)

    return fwd, bwd


def _rwkv_kernel(rf_ref, vf_ref, kkf_ref, lwf_ref, af_ref, kf_ref,
                 rb_ref, vb_ref, kkb_ref, lwb_ref, ab_ref, kb_ref, yf_ref, yb_ref, st_ref):
    @pl.when(pl.program_id(1) == 0)
    def _():
        st_ref[...] = jnp.zeros_like(st_ref)

    c_len = RW_CHUNK
    ii = lax.broadcasted_iota(jnp.int32, (c_len, c_len), 0)
    jj = lax.broadcasted_iota(jnp.int32, (c_len, c_len), 1)
    r2 = lax.broadcasted_iota(jnp.int32, (HEAD_PAIR, HEAD_PAIR), 0)
    c2 = lax.broadcasted_iota(jnp.int32, (HEAD_PAIR, HEAD_PAIR), 1)
    dlt = (r2 % c_len) - (c2 % c_len)
    eye = jnp.where(r2 == c2, 1.0, 0.0).astype(F32)
    lane = lax.broadcasted_iota(jnp.int32, (c_len, HEAD_PAIR), 1)
    m0 = lane < C_HEAD

    def stack_heads(x):
        return jnp.concatenate([jnp.where(m0, x, 0.0), jnp.where(m0, 0.0, x)], axis=0)

    n_pairs = C_DIM // HEAD_PAIR
    ch = []
    for d, refs in enumerate(((rf_ref, vf_ref, kkf_ref, lwf_ref, af_ref, kf_ref),
                              (rb_ref, vb_ref, kkb_ref, lwb_ref, ab_ref, kb_ref))):
        sign = 1 - 2 * d
        r_all, v_all, kk_all, lw_all, a_all, kd_all = (ref[0] for ref in refs)
        tri = jnp.where((ii - jj) * sign >= 0, 1.0, 0.0).astype(F32)
        cum_all = _dot_exact(tri, lw_all, NN)
        tot_all = jnp.sum(lw_all, axis=0, keepdims=True)
        for p in range(n_pairs):
            sl = slice(p * HEAD_PAIR, (p + 1) * HEAD_PAIR)
            ch.append(dict(strict=dlt * sign > 0, incl=dlt * sign >= 0, st=st_ref[d, p],
                           lw=lw_all[:, sl], cum=cum_all[:, sl], tot=tot_all[:, sl], kk=kk_all[:, sl],
                           kd=kd_all[:, sl], bb=kk_all[:, sl] * a_all[:, sl], r=r_all[:, sl], v=v_all[:, sl]))
    for c in ch:
        e_neg = jnp.exp(-c['cum'])
        e_end = jnp.exp(c['tot'] - c['cum'])
        c['abar'] = stack_heads(-c['kk'] * jnp.exp(c['cum'] - c['lw']))
        c['rbar'] = stack_heads(c['r'] * jnp.exp(c['cum']))
        c['ktil'] = stack_heads(c['kd'] * e_neg)
        c['btil'] = stack_heads(c['bb'] * e_neg)
        c['khat'] = stack_heads(c['kd'] * e_end)
        c['bhat'] = stack_heads(c['bb'] * e_end)
        c['vs'] = stack_heads(c['v'])
    g = [_dot(jnp.concatenate([c['abar'], c['rbar']], axis=0), jnp.concatenate([c['ktil'], c['btil']], axis=0), NT)
         for c in ch]
    a_ak = [jnp.where(c['strict'], x[:HEAD_PAIR, :HEAD_PAIR], 0.0) for c, x in zip(ch, g)]
    a_ab = [jnp.where(c['strict'], x[:HEAD_PAIR, HEAD_PAIR:], 0.0) for c, x in zip(ch, g)]
    a_rk = [jnp.where(c['incl'], x[HEAD_PAIR:, :HEAD_PAIR], 0.0) for c, x in zip(ch, g)]
    a_rb = [jnp.where(c['incl'], x[HEAD_PAIR:, HEAD_PAIR:], 0.0) for c, x in zip(ch, g)]
    tinv = _unit_lower_inverse(a_ab, eye)
    n = range(len(ch))
    wm = [_dot(tinv[i], ch[i]['abar'], NN) for i in n]
    akv = [_dot(a_ak[i], ch[i]['vs'], NN) for i in n]
    u0 = [_dot(tinv[i], akv[i], NN) for i in n]
    u = [_dot(wm[i], ch[i]['st'], NT) + u0[i] for i in n]
    ys = [_dot(ch[i]['rbar'], ch[i]['st'], NT) + _dot(a_rk[i], ch[i]['vs'], NN) + _dot(a_rb[i], u[i], NN) for i in n]
    st_new = [ch[i]['st'] * jnp.exp(ch[i]['tot']) + _dot(ch[i]['vs'], ch[i]['khat'], TN) + _dot(u[i], ch[i]['bhat'], TN)
              for i in n]
    for d, y_ref in enumerate((yf_ref, yb_ref)):
        y_ref[0] = jnp.concatenate([ys[d * n_pairs + p][:c_len] + ys[d * n_pairs + p][c_len:] for p in range(n_pairs)],
                                   axis=1)
        for p in range(n_pairs):
            st_ref[d, p] = st_new[d * n_pairs + p]


def rwkv_chunked(r, v, kk, lw_f, a_f, k_f, lw_b, a_b, k_b, n_ctx, interpret=False):
    b, t, cd = r.shape
    nc = t // RW_CHUNK
    fwd, bwd = _chunk_maps(nc, n_ctx // RW_CHUNK)
    sf = pl.BlockSpec((1, RW_CHUNK, cd), lambda i, s: (i, fwd(s), 0))
    sb = pl.BlockSpec((1, RW_CHUNK, cd), lambda i, s: (i, bwd(s), 0))
    out = jax.ShapeDtypeStruct((b, t, cd), F32)
    return pl.pallas_call(
        _rwkv_kernel,
        grid=(b, nc),
        in_specs=[sf] * 6 + [sb] * 6,
        out_specs=[sf, sb],
        out_shape=[out, out],
        scratch_shapes=[pltpu.VMEM((2, cd // HEAD_PAIR, HEAD_PAIR, HEAD_PAIR), F32)],
        compiler_params=pltpu.CompilerParams(
            dimension_semantics=("arbitrary", "arbitrary"),
            vmem_limit_bytes=V7X_VMEM_LIMIT_BYTES),
        name="rwkv7_chunked",
        interpret=interpret,
    )(r, v, kk, lw_f, a_f, k_f, r, v, kk, lw_b, a_b, k_b)


def _gdn_kernel(qf_ref, kf_ref, vf_ref, bf_ref, gf_ref, gtf_ref,
                qb_ref, kb_ref, vb_ref, bb_ref, gb_ref, gtb_ref, of_ref, ob_ref, st_ref):
    @pl.when(pl.program_id(1) == 0)
    def _():
        st_ref[...] = jnp.zeros_like(st_ref)

    c_len = D_CHUNK
    n2 = 2 * c_len
    r2 = lax.broadcasted_iota(jnp.int32, (n2, n2), 0)
    c2 = lax.broadcasted_iota(jnp.int32, (n2, n2), 1)
    same = (r2 // c_len) == (c2 // c_len)
    dlt = (r2 % c_len) - (c2 % c_len)
    ones_bd = jnp.where(same, 1.0, 0.0).astype(F32)
    eye = jnp.where(r2 == c2, 1.0, 0.0).astype(F32)

    def rows(blk, p):
        return jnp.concatenate([blk[:, h * D_HEAD_DIM:(h + 1) * D_HEAD_DIM] for h in (2 * p, 2 * p + 1)], axis=0)

    def col(blk, p):
        c = jnp.concatenate([blk[:, h:h + 1] for h in (2 * p, 2 * p + 1)], axis=0)
        return jnp.broadcast_to(c, (n2, n2))

    n_pairs = D_HEADS // 2
    ch = []
    for d, refs in enumerate(((qf_ref, kf_ref, vf_ref, bf_ref, gf_ref, gtf_ref),
                              (qb_ref, kb_ref, vb_ref, bb_ref, gb_ref, gtb_ref))):
        sign = 1 - 2 * d
        q_blk, k_blk, v_blk = refs[0][0], refs[1][0], refs[2][0]
        b_blk, g_blk, gt_blk = refs[3][0, 0], refs[4][0, 0], refs[5][0, 0]
        incl = same & (dlt * sign >= 0)
        tri = jnp.where(incl, 1.0, 0.0).astype(F32)
        for p in range(n_pairs):
            g_row = jnp.concatenate([gt_blk[h:h + 1, :] for h in (2 * p, 2 * p + 1)], axis=1)
            ch.append(dict(d=d, p=p, strict=same & (dlt * sign > 0), incl=incl, tri=tri,
                           q=rows(q_blk, p), k=rows(k_blk, p), v=rows(v_blk, p), beta=col(b_blk, p),
                           g_colb=col(g_blk, p), g_rowb=jnp.broadcast_to(g_row, (n2, n2)),
                           st=[st_ref[d, 2 * p], st_ref[d, 2 * p + 1]]))
    n = range(len(ch))
    gc_col = [_dot_exact(c['tri'], c['g_colb'], NN) for c in ch]
    gc_row = [_dot_exact(c['g_rowb'], c['tri'], NT) for c in ch]
    g_end = [_dot_exact(ones_bd, c['g_colb'], NN) for c in ch]
    decay = [jnp.where(ch[i]['incl'], jnp.exp(jnp.where(ch[i]['incl'], gc_col[i] - gc_row[i], 0.0)), 0.0) for i in n]
    kb = [c['k'] * c['beta'] for c in ch]
    vb = [c['v'] * c['beta'] for c in ch]
    g_mat = [_dot(jnp.concatenate([kb[i], ch[i]['q']], axis=0), ch[i]['k'], NT) for i in n]
    l_mat = [jnp.where(ch[i]['strict'], g_mat[i][:n2] * decay[i], 0.0) for i in n]
    a_intra = [g_mat[i][n2:] * decay[i] for i in n]
    tinv = _unit_lower_inverse([-l for l in l_mat], eye)
    u = [_dot(tinv[i], vb[i], NN) for i in n]
    wk = [_dot(tinv[i], kb[i] * jnp.exp(gc_col[i]), NN) for i in n]
    q_e = [ch[i]['q'] * jnp.exp(gc_col[i]) for i in n]
    k_e = [ch[i]['k'] * jnp.exp(g_end[i] - gc_col[i]) for i in n]
    halves = (slice(0, c_len), slice(c_len, n2))
    v_new = [[u[i][hs] - _dot(wk[i][hs], ch[i]['st'][j], NN) for j, hs in enumerate(halves)] for i in n]
    o_st = [[_dot(q_e[i][hs], ch[i]['st'][j], NN) for j, hs in enumerate(halves)] for i in n]
    v_new_s = [jnp.concatenate(v_new[i], axis=0) for i in n]
    o_s = [jnp.concatenate(o_st[i], axis=0) + _dot(a_intra[i], v_new_s[i], NN) for i in n]
    st_new = [[ch[i]['st'][j] * jnp.exp(g_end[i][hs][0:1, :]) + _dot(k_e[i][hs], v_new[i][j], TN)
               for j, hs in enumerate(halves)] for i in n]
    for d, o_ref in enumerate((of_ref, ob_ref)):
        o_ref[0] = jnp.concatenate([o_s[d * n_pairs + p][hs] for p in range(n_pairs) for hs in halves], axis=1)
        for p in range(n_pairs):
            for j in range(2):
                st_ref[d, 2 * p + j] = st_new[d * n_pairs + p][j]


def gdn_chunked(q, k, v, beta_f, g_f, beta_b, g_b, n_ctx, interpret=False):
    b, t, cd = q.shape
    nc = t // D_CHUNK
    fwd, bwd = _chunk_maps(nc, n_ctx // D_CHUNK)

    def chunks(z):
        return z.reshape(b, nc, D_CHUNK, D_HEADS)

    def specs(cmap):
        big = pl.BlockSpec((1, D_CHUNK, cd), lambda i, s: (i, cmap(s), 0))
        small = pl.BlockSpec((1, 1, D_CHUNK, D_HEADS), lambda i, s: (i, cmap(s), 0, 0))
        small_t = pl.BlockSpec((1, 1, D_HEADS, D_CHUNK), lambda i, s: (i, cmap(s), 0, 0))
        return big, small, small_t

    bf, sf, stf = specs(fwd)
    bb, sb, stb = specs(bwd)
    out = jax.ShapeDtypeStruct((b, t, cd), F32)
    return pl.pallas_call(
        _gdn_kernel,
        grid=(b, nc),
        in_specs=[bf, bf, bf, sf, sf, stf, bb, bb, bb, sb, sb, stb],
        out_specs=[bf, bb],
        out_shape=[out, out],
        scratch_shapes=[pltpu.VMEM((2, D_HEADS, D_HEAD_DIM, D_HEAD_DIM), F32)],
        compiler_params=pltpu.CompilerParams(
            dimension_semantics=("arbitrary", "arbitrary"),
            vmem_limit_bytes=V7X_VMEM_LIMIT_BYTES),
        name="gdn_chunked",
        interpret=interpret,
    )(q, k, v, chunks(beta_f), chunks(g_f), jnp.swapaxes(chunks(g_f), -1, -2),
      q, k, v, chunks(beta_b), chunks(g_b), jnp.swapaxes(chunks(g_b), -1, -2))


def rwkv_prep(pc, n_ctx, c_mu_prev, c_mu_next, c_w0, c_w2, c_a0, c_a2, c_g2, c_k_k, c_k_a):
    b, t = pc.shape[:2]

    def heads(z):
        return z.reshape(b, t, C_HEADS, C_HEAD)

    prev = _token_shifted(pc, -1, n_ctx)
    nxt = _token_shifted(pc, 1, n_ctx)
    xs = pc + c_mu_prev * (prev - pc) + c_mu_next * (nxt - pc)
    r, k, v, wl_f, wl_b, al_f, al_b, gl = split_cols(xs, C_SIZES)

    def log_decay(wl, w0, w2):
        w = -jax.nn.softplus(-(w0 + _mm(jnp.tanh(wl), w2))) - 0.5
        return -jnp.exp(w)

    decays = (log_decay(wl_f, c_w0[0], c_w2[0]), log_decay(wl_b, c_w0[1], c_w2[1]))
    iclr = (jax.nn.sigmoid(c_a0[0] + _mm(al_f, c_a2[0])), jax.nn.sigmoid(c_a0[1] + _mm(al_b, c_a2[1])))
    g = _mm(jax.nn.sigmoid(gl), c_g2)
    kk = l2_normalize(heads(k * c_k_k)).reshape(b, t, C_DIM)
    ks = tuple(k * (1.0 + (a - 1.0) * c_k_a) for a in iclr)
    return r, v, kk, g, decays, iclr, ks


def head_group_norm(y, w, b_):
    mean = jnp.mean(y, axis=-1, keepdims=True)
    var = jnp.mean(jnp.square(y - mean), axis=-1, keepdims=True)
    yn = (y - mean) * lax.rsqrt(var + C_GN_EPS)
    return yn * w.reshape(C_HEADS, C_HEAD) + b_.reshape(C_HEADS, C_HEAD)


def rwkv_output(y, r, ks, v, g, c_r_k, c_ln_w, c_ln_b):
    b, t = y.shape[:2]

    def heads(z):
        return z.reshape(b, t, C_HEADS, C_HEAD)

    yn = head_group_norm(heads(y), c_ln_w, c_ln_b)
    bonus = sum(jnp.sum(heads(r) * heads(kd) * c_r_k, axis=-1, keepdims=True) * heads(v) for kd in ks)
    return (yn + bonus).reshape(b, t, C_DIM) * g


def gdn_prep(pd, n_ctx, d_conv_w, d_A_log, d_dt_bias):
    b, t = pd.shape[:2]
    qkv, z, bf, bb, af, ab = split_cols(pd, D_SIZES)
    qkv = jax.nn.silu(dwconv_centred(qkv, d_conv_w, n_ctx))
    q, k, v = jnp.split(qkv, 3, axis=-1)

    def heads(x):
        return x.reshape(b, t, D_HEADS, D_HEAD_DIM)

    q = (l2_normalize(heads(q)) * D_HEAD_DIM ** -0.5).reshape(b, t, D_DIM)
    k = l2_normalize(heads(k)).reshape(b, t, D_DIM)
    betas = (jax.nn.sigmoid(bf), jax.nn.sigmoid(bb))
    gs = tuple(-jnp.exp(d_A_log[i]) * jax.nn.softplus(al + d_dt_bias[i]) for i, al in enumerate((af, ab)))
    return q, k, v, z, betas, gs


def gdn_output(o, z, d_o_norm):
    b, t = o.shape[:2]

    def heads(x):
        return x.reshape(b, t, D_HEADS, D_HEAD_DIM)

    return (rms_norm(heads(o), d_o_norm) * jax.nn.silu(heads(z))).reshape(b, t, D_DIM)


def mixer_cd(p, n_ctx, c_mu_prev, c_mu_next, c_w0, c_w2, c_a0, c_a2, c_g2, c_k_k, c_k_a, c_r_k,
             c_ln_w, c_ln_b, d_conv_w, d_A_log, d_dt_bias, d_o_norm):
    r, v, kk, g, lw, a, ks = rwkv_prep(p[..., :IN_C], n_ctx, c_mu_prev, c_mu_next, c_w0, c_w2, c_a0, c_a2, c_g2,
                                       c_k_k, c_k_a)
    y_f, y_b = rwkv_chunked(r, v, kk, lw[0], a[0], ks[0], lw[1], a[1], ks[1], n_ctx)
    q, kd, vd, z, beta, gd = gdn_prep(p[..., IN_C:], n_ctx, d_conv_w, d_A_log, d_dt_bias)
    o_f, o_b = gdn_chunked(q, kd, vd, beta[0], gd[0], beta[1], gd[1], n_ctx)
    return jnp.concatenate([rwkv_output(y_f + y_b, r, ks, v, g, c_r_k, c_ln_w, c_ln_b),
                            gdn_output(o_f + o_b, z, d_o_norm)], axis=-1)


FFN_TM = 256
FFN_HALO = 8
FFN_FC = 256
FFN_VMEM_LIMIT_BYTES = 56 * 1024 * 1024


def _modulated(x, shift, scale):
    y = x * lax.rsqrt(jnp.mean(x * x, axis=-1, keepdims=True) + EPS)
    return y * (1.0 + scale) + shift


def _ffn_kernel(xp_ref, x_ref, xn_ref, shift_ref, scale_ref, gate_ref, wup_ref, cw_ref, cb_ref, wdn_ref, o_ref, acc_ref,
                *, n_tiles, ctx_tiles):
    i = pl.program_id(1)
    tm, halo = FFN_TM, FFN_HALO
    rows = tm + 2 * halo
    x = x_ref[0]
    xe = jnp.concatenate([xp_ref[0], x, xn_ref[0]], axis=0)
    h = _modulated(xe, shift_ref[0, 0], scale_ref[0, 0]).astype(BF16)
    r = lax.broadcasted_iota(jnp.int32, (rows, 1), 0)
    seq_first = (i == 0) | (i == ctx_tiles)
    seq_last = (i == ctx_tiles - 1) | (i == n_tiles - 1)
    keep_prev = jnp.where((r == halo) & seq_first, 0.0, 1.0)
    keep_next = jnp.where((r == halo + tm - 1) & seq_last, 0.0, 1.0)
    acc_ref[...] = jnp.zeros_like(acc_ref)

    def conv(u, c0):
        w = cw_ref[:, c0:c0 + FFN_FC]
        um = pltpu.roll(u, 1, 0) * keep_prev
        up = pltpu.roll(u, rows - 1, 0) * keep_next
        y = um * w[0:1] + u * w[1:2] + up * w[2:3] + cb_ref[:, c0:c0 + FFN_FC]
        return y[halo:halo + tm]

    for c in range(D_FF // FFN_FC):
        cv, cg = c * FFN_FC, D_FF + c * FFN_FC
        val = conv(jnp.dot(h, wup_ref[:, cv:cv + FFN_FC], preferred_element_type=F32), cv)
        gat = conv(jnp.dot(h, wup_ref[:, cg:cg + FFN_FC], preferred_element_type=F32), cg)
        act = (gat * jax.nn.sigmoid(gat) * val).astype(BF16)
        acc_ref[...] += jnp.dot(act, wdn_ref[cv:cv + FFN_FC, :], preferred_element_type=F32)
    o_ref[0] = x + gate_ref[0, 0] * acc_ref[...]


def ffn_fused(x, shift, scale, gate, w_up, conv_w, conv_b, w_down, n_ctx, interpret=False):
    b, t, d = x.shape
    nt = t // FFN_TM
    ctx_tiles = n_ctx // FFN_TM
    hb = FFN_TM // FFN_HALO
    f2 = w_up.shape[1]
    mod = pl.BlockSpec((1, 1, 1, d), lambda bi, i: (bi, jnp.where(i >= ctx_tiles, 1, 0), 0, 0))

    def full(shp):
        return pl.BlockSpec(shp, lambda bi, i: (0,) * len(shp))

    def mods(m):
        return m.reshape(b, 2, 1, d)

    return pl.pallas_call(
        functools.partial(_ffn_kernel, n_tiles=nt, ctx_tiles=ctx_tiles),
        grid=(b, nt),
        in_specs=[pl.BlockSpec((1, FFN_HALO, d), lambda bi, i: (bi, jnp.maximum(i * hb - 1, 0), 0)),
                  pl.BlockSpec((1, FFN_TM, d), lambda bi, i: (bi, i, 0)),
                  pl.BlockSpec((1, FFN_HALO, d), lambda bi, i: (bi, jnp.minimum((i + 1) * hb, nt * hb - 1), 0)),
                  mod, mod, mod,
                  full((d, f2)), full((FFN_CONV, f2)), full((1, f2)), full((f2 // 2, d))],
        out_specs=pl.BlockSpec((1, FFN_TM, d), lambda bi, i: (bi, i, 0)),
        out_shape=jax.ShapeDtypeStruct((b, t, d), F32),
        scratch_shapes=[pltpu.VMEM((FFN_TM, d), F32)],
        compiler_params=pltpu.CompilerParams(
            dimension_semantics=("arbitrary", "arbitrary"),
            vmem_limit_bytes=FFN_VMEM_LIMIT_BYTES),
        name="conv_ffn",
        interpret=interpret,
    )(x, x, x, mods(shift), mods(scale), mods(gate), w_up.astype(BF16), conv_w, conv_b.reshape(1, f2),
      w_down.astype(BF16))


def kernel(x, c, ctx, c_ctx, ada_w, ada_b, ffn_w_up, ffn_conv_w, ffn_conv_b, ffn_w_down,
           ab_w_in, ab_w_out, a_q_norm, a_k_norm, a_sink, b_cq_norm, b_ckv_norm, b_w_uq, b_w_uk, b_w_uv,
           b_qn_norm, b_qr_norm, b_kn_norm, b_kr_norm, cd_w_in, cd_w_out, c_mu_prev, c_mu_next, c_w0, c_w2,
           c_a0, c_a2, c_g2, c_k_k, c_k_a, c_r_k, c_ln_w, c_ln_b, d_conv_w, d_A_log, d_dt_bias, d_o_norm):
    bsz, seq = x.shape[:2]
    n_ctx = ctx.shape[1]
    rows = seq // GRID_W
    zeros = jnp.zeros((n_ctx,), jnp.int32)
    row = jnp.concatenate([zeros, jnp.repeat(jnp.arange(rows, dtype=jnp.int32), GRID_W)])
    col = jnp.concatenate([zeros, jnp.tile(jnp.arange(GRID_W, dtype=jnp.int32), rows)])
    is_ctx = (jnp.arange(n_ctx + seq) < n_ctx)[None, :, None]
    silu_c = jax.nn.silu(c)
    silu_cc = jax.nn.silu(c_ctx)
    xa = jnp.concatenate([ctx, x], axis=1)
    for l in range(DEPTH):
        last = l == DEPTH - 1
        i = l // 2
        mod_l = jnp.split(silu_c @ ada_w[l] + ada_b[l], N_MOD, axis=-1)
        mod_c = jnp.split(silu_cc @ ada_w[l] + ada_b[l], N_MOD, axis=-1)
        mods = [jnp.stack([jnp.broadcast_to(mc, ml.shape), ml], axis=1) for mc, ml in zip(mod_c, mod_l)]

        def per_token(m):
            return jnp.where(is_ctx, m[:, 0:1, :], m[:, 1:2, :])

        h = rms_norm(xa) * (1.0 + per_token(mods[1])) + per_token(mods[0])
        if l % 2 == 0:
            y = mixer_ab(_mm(h, ab_w_in[i]), n_ctx, row, col, a_q_norm[i], a_k_norm[i],
                         a_sink[i], b_cq_norm[i], b_ckv_norm[i], b_w_uq[i], b_w_uk[i], b_w_uv[i],
                         b_qn_norm[i], b_qr_norm[i], b_kn_norm[i], b_kr_norm[i], not last)
            w_out = ab_w_out[i]
        else:
            y = mixer_cd(_mm(h, cd_w_in[i]), n_ctx, c_mu_prev[i], c_mu_next[i], c_w0[i],
                         c_w2[i], c_a0[i], c_a2[i], c_g2[i], c_k_k[i], c_k_a[i], c_r_k[i], c_ln_w[i],
                         c_ln_b[i], d_conv_w[i], d_A_log[i], d_dt_bias[i], d_o_norm[i])
            w_out = cd_w_out[i]
        xa = xa + per_token(mods[2]) * _mm(y, w_out)
        xa = ffn_fused(xa, mods[3], mods[4], mods[5], ffn_w_up[l], ffn_conv_w[l], ffn_conv_b[l], ffn_w_down[l],
                       n_ctx)
    return xa[:, n_ctx:]
```

```python
import functools

import jax
import jax.numpy as jnp
from jax import lax
import numpy as np
from jax.experimental import pallas as pl
from jax.experimental.pallas import tpu as pltpu

D_MODEL = 1024
DEPTH = 2
GRID_W = 64
N_MOD = 6
EPS = 1e-6
ROPE_THETA = 10000.0
NEG_INF = -1e30

A_HEADS = 8
A_KV_HEADS = 2
A_HEAD_DIM = 64
WINDOW = 128
B_HEADS = 8
B_Q_RANK = 256
B_KV_RANK = 256
B_NOPE = 64
B_ROPE = 32
B_V_DIM = 64
C_HEADS = 8
C_HEAD = 64
C_DIM = C_HEADS * C_HEAD
C_DECAY_LORA = 64
C_AAA_LORA = 64
C_GATE_LORA = 128
C_GN_EPS = 64e-5
D_HEADS = 4
D_HEAD_DIM = 128
D_DIM = D_HEADS * D_HEAD_DIM
D_CONV = 5
D_CHUNK = 64
D_FF = 2816
FFN_CONV = 3

AB_SIZES = (A_HEADS * A_HEAD_DIM, A_KV_HEADS * A_HEAD_DIM, A_KV_HEADS * A_HEAD_DIM, B_Q_RANK, B_KV_RANK, B_ROPE)
C_SIZES = (C_DIM, C_DIM, C_DIM, C_DECAY_LORA, C_DECAY_LORA, C_AAA_LORA, C_AAA_LORA, C_GATE_LORA)
IN_C = sum(C_SIZES)
D_SIZES = (3 * D_DIM, D_DIM, D_HEADS, D_HEADS, D_HEADS, D_HEADS)

F32 = jnp.float32
BF16 = jnp.bfloat16

V7X_VMEM_LIMIT_BYTES = 48 * 1024 * 1024
LANE = 128
MXU_N = 256

NN = ((1,), (0,))
NT = ((1,), (1,))
TN = ((0,), (0,))


def _mm_kernel(a_ref, b_ref, o_ref):
    o_ref[...] = jnp.dot(a_ref[...].astype(BF16), b_ref[...], preferred_element_type=F32)


def _pick_tile(n, candidates):
    for c in candidates:
        if n % c == 0:
            return c
    raise ValueError(f"no tile for {n}")


def _mm(a, w):
    lead = a.shape[:-1]
    k = a.shape[-1]
    n = w.shape[-1]
    a2 = a.reshape(-1, k)
    m = a2.shape[0]
    n_pad = -(-n // MXU_N) * MXU_N
    wb = w.astype(BF16)
    if n_pad != n:
        wb = jnp.pad(wb, ((0, 0), (0, n_pad - n)))
    tm = _pick_tile(m, (1024, 512, 256, 128, 8))
    tn = _pick_tile(n_pad, (1024, 768, 512, 256))
    out = pl.pallas_call(
        _mm_kernel,
        grid=(m // tm, n_pad // tn),
        in_specs=[pl.BlockSpec((tm, k), lambda i, j: (i, 0)),
                  pl.BlockSpec((k, tn), lambda i, j: (0, j))],
        out_specs=pl.BlockSpec((tm, tn), lambda i, j: (i, j)),
        out_shape=jax.ShapeDtypeStruct((m, n_pad), F32),
        compiler_params=pltpu.CompilerParams(
            dimension_semantics=("arbitrary", "arbitrary"),
            vmem_limit_bytes=V7X_VMEM_LIMIT_BYTES),
        name="mm",
    )(a2, wb)
    if n_pad != n:
        out = out[:, :n]
    return out.reshape(*lead, n)


def _dot(a, b, dims, passes=1):
    def dg(x, y):
        return lax.dot_general(x, y, (dims, ((), ())), preferred_element_type=F32)

    ah, bh = a.astype(BF16), b.astype(BF16)
    if passes == 1:
        return dg(ah, bh)
    al = (a - ah.astype(F32)).astype(BF16)
    bl = (b - bh.astype(F32)).astype(BF16)
    return dg(ah, bh) + (dg(ah, bl) + dg(al, bh))


def _dot_exact(a, b, dims):
    return lax.dot_general(a, b, (dims, ((), ())), precision=lax.Precision.HIGHEST, preferred_element_type=F32)


def split_cols(p, sizes):
    return jnp.split(p, [int(s) for s in np.cumsum(sizes)[:-1]], axis=-1)


def rms_norm(x, gain=None, eps=EPS):
    y = x * lax.rsqrt(jnp.mean(x * x, axis=-1, keepdims=True) + eps)
    if gain is not None:
        y = y * gain
    return y


def l2_normalize(x, eps=1e-6):
    return x * lax.rsqrt(jnp.sum(x * x, axis=-1, keepdims=True) + eps)


def rope_1d(x, pos):
    half = x.shape[-1] // 2
    inv = jnp.power(ROPE_THETA, -jnp.arange(half, dtype=F32) / half)
    ang = pos.astype(F32)[:, None] * inv[None, :]
    cos = jnp.cos(ang)[None, :, None, :]
    sin = jnp.sin(ang)[None, :, None, :]
    x1, x2 = x[..., :half], x[..., half:]
    return jnp.concatenate([x1 * cos - x2 * sin, x1 * sin + x2 * cos], axis=-1)


def rope_2d(x, row, col):
    h = x.shape[-1] // 2
    return jnp.concatenate([rope_1d(x[..., :h], row), rope_1d(x[..., h:], col)], axis=-1)


def _token_shifted(p, offset, n_ctx):
    t = p.shape[1]
    lo, hi = max(-offset, 0), max(offset, 0)
    shifted = jnp.pad(p, ((0, 0), (lo, hi), (0, 0)))[:, hi:hi + t] if offset else p
    idx = jnp.arange(t)
    src = idx + offset
    ok = (src >= 0) & (src < t) & ((src >= n_ctx) == (idx >= n_ctx))
    return jnp.where(ok[None, :, None], shifted, 0.0)


def _heads_major(z):
    return jnp.swapaxes(z, 1, 2)


def _softmax_pv(s, v, sink):
    m = jnp.max(s, axis=-1, keepdims=True)
    if sink is not None:
        m = jnp.maximum(m, sink)
    p = jnp.exp(s - m)
    den = jnp.sum(p, axis=-1, keepdims=True)
    if sink is not None:
        den = den + jnp.exp(sink - m)
    return _dot(p, v, NN) / den


def _attn_full_kernel(q_ref, k_ref, q2_ref, k2_ref, v_ref, sink_ref, o_ref, *, scale, use_sink, use_second):
    s = _dot(q_ref[0, 0], k_ref[0, 0], NT)
    if use_second:
        s = s + _dot(q2_ref[0, 0], k2_ref[0, 0], NT)
    sink = sink_ref[0] if use_sink else None
    o_ref[0, 0] = _softmax_pv(s * scale, v_ref[0, 0], sink)


def attn_full(q, k, v, scale, q_start, n_q, n_keys, sink=None, second=None):
    b, h, _, d = q.shape
    hk = k.shape[1]
    dv = v.shape[-1]
    g = h // hk
    tq = min(n_q, 256)
    q0 = q_start // tq
    use_sink = sink is not None
    use_second = second is not None
    sink_arr = (sink if use_sink else jnp.zeros((h,), F32)).astype(F32).reshape(h, 1, 1)
    q2, k2 = second if use_second else (q, k)
    d2 = q2.shape[-1]
    k2_spec = (pl.BlockSpec((1, 1, n_keys, d2), lambda i, j, t: (i, 0, 0, 0)) if use_second else
               pl.BlockSpec((1, 1, n_keys, d2), lambda i, j, t: (i, j // g, 0, 0)))
    return pl.pallas_call(
        functools.partial(_attn_full_kernel, scale=scale, use_sink=use_sink, use_second=use_second),
        grid=(b, h, n_q // tq),
        in_specs=[pl.BlockSpec((1, 1, tq, d), lambda i, j, t: (i, j, t + q0, 0)),
                  pl.BlockSpec((1, 1, n_keys, d), lambda i, j, t: (i, j // g, 0, 0)),
                  pl.BlockSpec((1, 1, tq, d2), lambda i, j, t: (i, j, t + q0, 0)),
                  k2_spec,
                  pl.BlockSpec((1, 1, n_keys, dv), lambda i, j, t: (i, j // g, 0, 0)),
                  pl.BlockSpec((1, 1, 1), lambda i, j, t: (j, 0, 0))],
        out_specs=pl.BlockSpec((1, 1, tq, dv), lambda i, j, t: (i, j, t, 0)),
        out_shape=jax.ShapeDtypeStruct((b, h, n_q, dv), F32),
        compiler_params=pltpu.CompilerParams(
            dimension_semantics=("arbitrary", "arbitrary", "arbitrary"),
            vmem_limit_bytes=V7X_VMEM_LIMIT_BYTES),
        name="attn_full",
    )(q, k, q2, k2, v, sink_arr)


def _attn_window_kernel(q_ref, kp_ref, k0_ref, kn_ref, kc_ref, vp_ref, v0_ref, vn_ref, vc_ref, sink_ref, o_ref,
                        *, scale, group, n_blocks):
    n = pl.program_id(2)
    w = WINDOW
    d = q_ref.shape[-1]
    q = q_ref[0].reshape(group * w, d)
    keys = jnp.concatenate([kp_ref[0, 0], k0_ref[0, 0], kn_ref[0, 0], kc_ref[0, 0]], axis=0)
    vals = jnp.concatenate([vp_ref[0, 0], v0_ref[0, 0], vn_ref[0, 0], vc_ref[0, 0]], axis=0)
    s = _dot(q, keys, NT) * scale
    nk = keys.shape[0]
    qi = lax.broadcasted_iota(jnp.int32, (group * w, nk), 0) % w
    kj = lax.broadcasted_iota(jnp.int32, (group * w, nk), 1)
    rel = qi + w - kj
    band_ok = (jnp.abs(rel) <= w) & ((kj >= w) | (n > 0)) & ((kj < 2 * w) | (n < n_blocks - 1))
    s = jnp.where((kj >= 3 * w) | band_ok, s, NEG_INF)
    sink = jnp.concatenate([jnp.broadcast_to(sink_ref[0, hh], (w, 1)) for hh in range(group)], axis=0)
    o = _softmax_pv(s, vals, sink)
    o_ref[0] = o.reshape(group, w, o.shape[-1])


def attn_window(q, k, v, sink, n_ctx):
    b, h, t, d = q.shape
    hk = k.shape[1]
    g = h // hk
    off = n_ctx // WINDOW
    nb = (t - n_ctx) // WINDOW
    sink_arr = sink.astype(F32).reshape(hk, g, 1, 1)

    def blk(f):
        return pl.BlockSpec((1, 1, WINDOW, d), f)

    prev = blk(lambda i, j, n: (i, j, jnp.maximum(n - 1, 0) + off, 0))
    own = blk(lambda i, j, n: (i, j, n + off, 0))
    nxt = blk(lambda i, j, n: (i, j, jnp.minimum(n + 1, nb - 1) + off, 0))
    ctx = pl.BlockSpec((1, 1, n_ctx, d), lambda i, j, n: (i, j, 0, 0))
    return pl.pallas_call(
        functools.partial(_attn_window_kernel, scale=d ** -0.5, group=g, n_blocks=nb),
        grid=(b, hk, nb),
        in_specs=[pl.BlockSpec((1, g, WINDOW, d), lambda i, j, n: (i, j, n + off, 0)),
                  prev, own, nxt, ctx, prev, own, nxt, ctx,
                  pl.BlockSpec((1, g, 1, 1), lambda i, j, n: (j, 0, 0, 0))],
        out_specs=pl.BlockSpec((1, g, WINDOW, d), lambda i, j, n: (i, j, n, 0)),
        out_shape=jax.ShapeDtypeStruct((b, h, t - n_ctx, d), F32),
        compiler_params=pltpu.CompilerParams(
            dimension_semantics=("arbitrary", "arbitrary", "arbitrary"),
            vmem_limit_bytes=V7X_VMEM_LIMIT_BYTES),
        name="attn_window",
    )(q, k, k, k, k, v, v, v, v, sink_arr)


def ab_prep(p, a_q_norm, a_k_norm, b_cq_norm, b_ckv_norm, b_w_uq, b_w_uk, b_w_uv,
            b_qn_norm, b_qr_norm, b_kn_norm, b_kr_norm):
    b, t = p.shape[:2]
    qa, ka, va, cq, ckv, kr = split_cols(p, AB_SIZES)
    qa = rms_norm(qa.reshape(b, t, A_HEADS, A_HEAD_DIM), a_q_norm)
    ka = rms_norm(ka.reshape(b, t, A_KV_HEADS, A_HEAD_DIM), a_k_norm)
    va = va.reshape(b, t, A_KV_HEADS, A_HEAD_DIM)
    qb = _mm(rms_norm(cq, b_cq_norm), b_w_uq).reshape(b, t, B_HEADS, B_NOPE + B_ROPE)
    qn = rms_norm(qb[..., :B_NOPE], b_qn_norm)
    qr = rms_norm(qb[..., B_NOPE:], b_qr_norm)
    ckv = rms_norm(ckv, b_ckv_norm)
    kn = rms_norm(_mm(ckv, b_w_uk).reshape(b, t, B_HEADS, B_NOPE), b_kn_norm)
    vb = _mm(ckv, b_w_uv).reshape(b, t, B_HEADS, B_V_DIM)
    kr = rms_norm(kr, b_kr_norm)
    return qa, ka, va, qn, qr, kn, kr, vb


def mixer_ab(p, n_ctx, row, col, a_q_norm, a_k_norm, a_sink, b_cq_norm, b_ckv_norm, b_w_uq, b_w_uk,
             b_w_uv, b_qn_norm, b_qr_norm, b_kn_norm, b_kr_norm, ctx_out):
    b, t = p.shape[:2]
    qa, ka, va, qn, qr, kn, kr, vb = ab_prep(p, a_q_norm, a_k_norm, b_cq_norm, b_ckv_norm, b_w_uq, b_w_uk, b_w_uv,
                                             b_qn_norm, b_qr_norm, b_kn_norm, b_kr_norm)
    qa = rope_2d(qa, row, col)
    ka = rope_2d(ka, row, col)
    qr = rope_2d(qr, row, col)
    kr = rope_2d(kr[:, :, None, :], row, col)
    hm = _heads_major
    qa_t, ka_t, va_t = hm(qa), hm(ka), hm(va)
    qn_t, kn_t, vb_t = hm(qn), hm(kn), hm(vb)
    rope_part = (hm(qr), hm(kr))
    b_scale = (B_NOPE + B_ROPE) ** -0.5

    o_a = attn_window(qa_t, ka_t, va_t, a_sink, n_ctx)
    o_b = attn_full(qn_t, kn_t, vb_t, b_scale, n_ctx, t - n_ctx, t, second=rope_part)
    if ctx_out:
        o_a_c = attn_full(qa_t, ka_t, va_t, A_HEAD_DIM ** -0.5, 0, n_ctx, n_ctx, sink=a_sink)
        o_b_c = attn_full(qn_t, kn_t, vb_t, b_scale, 0, n_ctx, n_ctx, second=rope_part)
    else:
        o_a_c = jnp.zeros((b, A_HEADS, n_ctx, A_HEAD_DIM), F32)
        o_b_c = jnp.zeros((b, B_HEADS, n_ctx, B_V_DIM), F32)
    o_a = hm(jnp.concatenate([o_a_c, o_a], axis=2)).reshape(b, t, -1)
    o_b = hm(jnp.concatenate([o_b_c, o_b], axis=2)).reshape(b, t, -1)
    return jnp.concatenate([o_a, o_b], axis=-1)


RW_CHUNK = 64
HEAD_PAIR = 2 * C_HEAD


def _unit_lower_inverse(xs, eye):
    ps = [eye + x for x in xs]
    xps = list(xs)
    for _ in range(int(np.log2(RW_CHUNK)) - 1):
        xps = [_dot(xp, xp, NN, passes=3) for xp in xps]
        ps = [p + _dot(p, xp, NN, passes=3) for p, xp in zip(ps, xps)]
    return ps


def _chunk_maps(nc, nctx):
    def fwd(s):
        return s

    def bwd(s):
        return jnp.where(s < nctx, nctx - 1 - s, nc + nctx - 1 - s)

    return fwd, bwd


def _rwkv_kernel(rf_ref, vf_ref, kkf_ref, lwf_ref, af_ref, kf_ref,
                 rb_ref, vb_ref, kkb_ref, lwb_ref, ab_ref, kb_ref, yf_ref, yb_ref, st_ref):
    @pl.when(pl.program_id(1) == 0)
    def _():
        st_ref[...] = jnp.zeros_like(st_ref)

    c_len = RW_CHUNK
    ii = lax.broadcasted_iota(jnp.int32, (c_len, c_len), 0)
    jj = lax.broadcasted_iota(jnp.int32, (c_len, c_len), 1)
    r2 = lax.broadcasted_iota(jnp.int32, (HEAD_PAIR, HEAD_PAIR), 0)
    c2 = lax.broadcasted_iota(jnp.int32, (HEAD_PAIR, HEAD_PAIR), 1)
    dlt = (r2 % c_len) - (c2 % c_len)
    eye = jnp.where(r2 == c2, 1.0, 0.0).astype(F32)
    lane = lax.broadcasted_iota(jnp.int32, (c_len, HEAD_PAIR), 1)
    m0 = lane < C_HEAD

    def stack_heads(x):
        return jnp.concatenate([jnp.where(m0, x, 0.0), jnp.where(m0, 0.0, x)], axis=0)

    n_pairs = C_DIM // HEAD_PAIR
    ch = []
    for d, refs in enumerate(((rf_ref, vf_ref, kkf_ref, lwf_ref, af_ref, kf_ref),
                              (rb_ref, vb_ref, kkb_ref, lwb_ref, ab_ref, kb_ref))):
        sign = 1 - 2 * d
        r_all, v_all, kk_all, lw_all, a_all, kd_all = (ref[0] for ref in refs)
        tri = jnp.where((ii - jj) * sign >= 0, 1.0, 0.0).astype(F32)
        cum_all = _dot_exact(tri, lw_all, NN)
        tot_all = jnp.sum(lw_all, axis=0, keepdims=True)
        for p in range(n_pairs):
            sl = slice(p * HEAD_PAIR, (p + 1) * HEAD_PAIR)
            ch.append(dict(strict=dlt * sign > 0, incl=dlt * sign >= 0, st=st_ref[d, p],
                           lw=lw_all[:, sl], cum=cum_all[:, sl], tot=tot_all[:, sl], kk=kk_all[:, sl],
                           kd=kd_all[:, sl], bb=kk_all[:, sl] * a_all[:, sl], r=r_all[:, sl], v=v_all[:, sl]))
    for c in ch:
        e_neg = jnp.exp(-c['cum'])
        e_end = jnp.exp(c['tot'] - c['cum'])
        c['abar'] = stack_heads(-c['kk'] * jnp.exp(c['cum'] - c['lw']))
        c['rbar'] = stack_heads(c['r'] * jnp.exp(c['cum']))
        c['ktil'] = stack_heads(c['kd'] * e_neg)
        c['btil'] = stack_heads(c['bb'] * e_neg)
        c['khat'] = stack_heads(c['kd'] * e_end)
        c['bhat'] = stack_heads(c['bb'] * e_end)
        c['vs'] = stack_heads(c['v'])
    g = [_dot(jnp.concatenate([c['abar'], c['rbar']], axis=0), jnp.concatenate([c['ktil'], c['btil']], axis=0), NT)
         for c in ch]
    a_ak = [jnp.where(c['strict'], x[:HEAD_PAIR, :HEAD_PAIR], 0.0) for c, x in zip(ch, g)]
    a_ab = [jnp.where(c['strict'], x[:HEAD_PAIR, HEAD_PAIR:], 0.0) for c, x in zip(ch, g)]
    a_rk = [jnp.where(c['incl'], x[HEAD_PAIR:, :HEAD_PAIR], 0.0) for c, x in zip(ch, g)]
    a_rb = [jnp.where(c['incl'], x[HEAD_PAIR:, HEAD_PAIR:], 0.0) for c, x in zip(ch, g)]
    tinv = _unit_lower_inverse(a_ab, eye)
    n = range(len(ch))
    wm = [_dot(tinv[i], ch[i]['abar'], NN) for i in n]
    akv = [_dot(a_ak[i], ch[i]['vs'], NN) for i in n]
    u0 = [_dot(tinv[i], akv[i], NN) for i in n]
    u = [_dot(wm[i], ch[i]['st'], NT) + u0[i] for i in n]
    ys = [_dot(ch[i]['rbar'], ch[i]['st'], NT) + _dot(a_rk[i], ch[i]['vs'], NN) + _dot(a_rb[i], u[i], NN) for i in n]
    st_new = [ch[i]['st'] * jnp.exp(ch[i]['tot']) + _dot(ch[i]['vs'], ch[i]['khat'], TN) + _dot(u[i], ch[i]['bhat'], TN)
              for i in n]
    for d, y_ref in enumerate((yf_ref, yb_ref)):
        y_ref[0] = jnp.concatenate([ys[d * n_pairs + p][:c_len] + ys[d * n_pairs + p][c_len:] for p in range(n_pairs)],
                                   axis=1)
        for p in range(n_pairs):
            st_ref[d, p] = st_new[d * n_pairs + p]


def rwkv_chunked(r, v, kk, lw_f, a_f, k_f, lw_b, a_b, k_b, n_ctx, interpret=False):
    b, t, cd = r.shape
    nc = t // RW_CHUNK
    fwd, bwd = _chunk_maps(nc, n_ctx // RW_CHUNK)
    sf = pl.BlockSpec((1, RW_CHUNK, cd), lambda i, s: (i, fwd(s), 0))
    sb = pl.BlockSpec((1, RW_CHUNK, cd), lambda i, s: (i, bwd(s), 0))
    out = jax.ShapeDtypeStruct((b, t, cd), F32)
    return pl.pallas_call(
        _rwkv_kernel,
        grid=(b, nc),
        in_specs=[sf] * 6 + [sb] * 6,
        out_specs=[sf, sb],
        out_shape=[out, out],
        scratch_shapes=[pltpu.VMEM((2, cd // HEAD_PAIR, HEAD_PAIR, HEAD_PAIR), F32)],
        compiler_params=pltpu.CompilerParams(
            dimension_semantics=("arbitrary", "arbitrary"),
            vmem_limit_bytes=V7X_VMEM_LIMIT_BYTES),
        name="rwkv7_chunked",
        interpret=interpret,
    )(r, v, kk, lw_f, a_f, k_f, r, v, kk, lw_b, a_b, k_b)


def _gdn_kernel(qf_ref, kf_ref, vf_ref, bf_ref, gf_ref, gtf_ref,
                qb_ref, kb_ref, vb_ref, bb_ref, gb_ref, gtb_ref, of_ref, ob_ref, st_ref):
    @pl.when(pl.program_id(1) == 0)
    def _():
        st_ref[...] = jnp.zeros_like(st_ref)

    c_len = D_CHUNK
    n2 = 2 * c_len
    r2 = lax.broadcasted_iota(jnp.int32, (n2, n2), 0)
    c2 = lax.broadcasted_iota(jnp.int32, (n2, n2), 1)
    same = (r2 // c_len) == (c2 // c_len)
    dlt = (r2 % c_len) - (c2 % c_len)
    ones_bd = jnp.where(same, 1.0, 0.0).astype(F32)
    eye = jnp.where(r2 == c2, 1.0, 0.0).astype(F32)

    def rows(blk, p):
        return jnp.concatenate([blk[:, h * D_HEAD_DIM:(h + 1) * D_HEAD_DIM] for h in (2 * p, 2 * p + 1)], axis=0)

    def col(blk, p):
        c = jnp.concatenate([blk[:, h:h + 1] for h in (2 * p, 2 * p + 1)], axis=0)
        return jnp.broadcast_to(c, (n2, n2))

    n_pairs = D_HEADS // 2
    ch = []
    for d, refs in enumerate(((qf_ref, kf_ref, vf_ref, bf_ref, gf_ref, gtf_ref),
                              (qb_ref, kb_ref, vb_ref, bb_ref, gb_ref, gtb_ref))):
        sign = 1 - 2 * d
        q_blk, k_blk, v_blk = refs[0][0], refs[1][0], refs[2][0]
        b_blk, g_blk, gt_blk = refs[3][0, 0], refs[4][0, 0], refs[5][0, 0]
        incl = same & (dlt * sign >= 0)
        tri = jnp.where(incl, 1.0, 0.0).astype(F32)
        for p in range(n_pairs):
            g_row = jnp.concatenate([gt_blk[h:h + 1, :] for h in (2 * p, 2 * p + 1)], axis=1)
            ch.append(dict(d=d, p=p, strict=same & (dlt * sign > 0), incl=incl, tri=tri,
                           q=rows(q_blk, p), k=rows(k_blk, p), v=rows(v_blk, p), beta=col(b_blk, p),
                           g_colb=col(g_blk, p), g_rowb=jnp.broadcast_to(g_row, (n2, n2)),
                           st=[st_ref[d, 2 * p], st_ref[d, 2 * p + 1]]))
    n = range(len(ch))
    gc_col = [_dot_exact(c['tri'], c['g_colb'], NN) for c in ch]
    gc_row = [_dot_exact(c['g_rowb'], c['tri'], NT) for c in ch]
    g_end = [_dot_exact(ones_bd, c['g_colb'], NN) for c in ch]
    decay = [jnp.where(ch[i]['incl'], jnp.exp(jnp.where(ch[i]['incl'], gc_col[i] - gc_row[i], 0.0)), 0.0) for i in n]
    kb = [c['k'] * c['beta'] for c in ch]
    vb = [c['v'] * c['beta'] for c in ch]
    g_mat = [_dot(jnp.concatenate([kb[i], ch[i]['q']], axis=0), ch[i]['k'], NT) for i in n]
    l_mat = [jnp.where(ch[i]['strict'], g_mat[i][:n2] * decay[i], 0.0) for i in n]
    a_intra = [g_mat[i][n2:] * decay[i] for i in n]
    tinv = _unit_lower_inverse([-l for l in l_mat], eye)
    u = [_dot(tinv[i], vb[i], NN) for i in n]
    wk = [_dot(tinv[i], kb[i] * jnp.exp(gc_col[i]), NN) for i in n]
    q_e = [ch[i]['q'] * jnp.exp(gc_col[i]) for i in n]
    k_e = [ch[i]['k'] * jnp.exp(g_end[i] - gc_col[i]) for i in n]
    halves = (slice(0, c_len), slice(c_len, n2))
    v_new = [[u[i][hs] - _dot(wk[i][hs], ch[i]['st'][j], NN) for j, hs in enumerate(halves)] for i in n]
    o_st = [[_dot(q_e[i][hs], ch[i]['st'][j], NN) for j, hs in enumerate(halves)] for i in n]
    v_new_s = [jnp.concatenate(v_new[i], axis=0) for i in n]
    o_s = [jnp.concatenate(o_st[i], axis=0) + _dot(a_intra[i], v_new_s[i], NN) for i in n]
    st_new = [[ch[i]['st'][j] * jnp.exp(g_end[i][hs][0:1, :]) + _dot(k_e[i][hs], v_new[i][j], TN)
               for j, hs in enumerate(halves)] for i in n]
    for d, o_ref in enumerate((of_ref, ob_ref)):
        o_ref[0] = jnp.concatenate([o_s[d * n_pairs + p][hs] for p in range(n_pairs) for hs in halves], axis=1)
        for p in range(n_pairs):
            for j in range(2):
                st_ref[d, 2 * p + j] = st_new[d * n_pairs + p][j]


def gdn_chunked(qkv, beta_f, g_f, beta_b, g_b, n_ctx, interpret=False):
    b, t, _ = qkv.shape
    cd = D_DIM
    nc = t // D_CHUNK
    fwd, bwd = _chunk_maps(nc, n_ctx // D_CHUNK)

    def chunks(z):
        return z.reshape(b, nc, D_CHUNK, D_HEADS)

    def specs(cmap):
        big = [pl.BlockSpec((1, D_CHUNK, cd), functools.partial(lambda part, i, s: (i, cmap(s), part), part))
               for part in range(3)]
        small = pl.BlockSpec((1, 1, D_CHUNK, D_HEADS), lambda i, s: (i, cmap(s), 0, 0))
        small_t = pl.BlockSpec((1, 1, D_HEADS, D_CHUNK), lambda i, s: (i, cmap(s), 0, 0))
        return big, small, small_t

    bf, sf, stf = specs(fwd)
    bb, sb, stb = specs(bwd)
    out = jax.ShapeDtypeStruct((b, t, cd), F32)
    return pl.pallas_call(
        _gdn_kernel,
        grid=(b, nc),
        in_specs=bf + [sf, sf, stf] + bb + [sb, sb, stb],
        out_specs=[bf[0], bb[0]],
        out_shape=[out, out],
        scratch_shapes=[pltpu.VMEM((2, D_HEADS, D_HEAD_DIM, D_HEAD_DIM), F32)],
        compiler_params=pltpu.CompilerParams(
            dimension_semantics=("arbitrary", "arbitrary"),
            vmem_limit_bytes=V7X_VMEM_LIMIT_BYTES),
        name="gdn_chunked",
        interpret=interpret,
    )(qkv, qkv, qkv, chunks(beta_f), chunks(g_f), jnp.swapaxes(chunks(g_f), -1, -2),
      qkv, qkv, qkv, chunks(beta_b), chunks(g_b), jnp.swapaxes(chunks(g_b), -1, -2))


def rwkv_prep(pc, n_ctx, c_mu_prev, c_mu_next, c_w0, c_w2, c_a0, c_a2, c_g2, c_k_k, c_k_a):
    b, t = pc.shape[:2]

    def heads(z):
        return z.reshape(b, t, C_HEADS, C_HEAD)

    prev = _token_shifted(pc, -1, n_ctx)
    nxt = _token_shifted(pc, 1, n_ctx)
    xs = pc + c_mu_prev * (prev - pc) + c_mu_next * (nxt - pc)
    r, k, v, wl_f, wl_b, al_f, al_b, gl = split_cols(xs, C_SIZES)

    def log_decay(wl, w0, w2):
        w = -jax.nn.softplus(-(w0 + _mm(jnp.tanh(wl), w2))) - 0.5
        return -jnp.exp(w)

    decays = (log_decay(wl_f, c_w0[0], c_w2[0]), log_decay(wl_b, c_w0[1], c_w2[1]))
    iclr = (jax.nn.sigmoid(c_a0[0] + _mm(al_f, c_a2[0])), jax.nn.sigmoid(c_a0[1] + _mm(al_b, c_a2[1])))
    g = _mm(jax.nn.sigmoid(gl), c_g2)
    kk = l2_normalize(heads(k * c_k_k)).reshape(b, t, C_DIM)
    ks = tuple(k * (1.0 + (a - 1.0) * c_k_a) for a in iclr)
    return r, v, kk, g, decays, iclr, ks


def head_group_norm(y, w, b_):
    mean = jnp.mean(y, axis=-1, keepdims=True)
    var = jnp.mean(jnp.square(y - mean), axis=-1, keepdims=True)
    yn = (y - mean) * lax.rsqrt(var + C_GN_EPS)
    return yn * w.reshape(C_HEADS, C_HEAD) + b_.reshape(C_HEADS, C_HEAD)


def rwkv_output(y, r, ks, v, g, c_r_k, c_ln_w, c_ln_b):
    b, t = y.shape[:2]

    def heads(z):
        return z.reshape(b, t, C_HEADS, C_HEAD)

    yn = head_group_norm(heads(y), c_ln_w, c_ln_b)
    bonus = sum(jnp.sum(heads(r) * heads(kd) * c_r_k, axis=-1, keepdims=True) * heads(v) for kd in ks)
    return (yn + bonus).reshape(b, t, C_DIM) * g


def _gdn_qkv_kernel(p_ref, w_ref, o_ref, *, n_ctx):
    j = pl.program_id(1)
    x = p_ref[0]
    t = x.shape[0]
    idx = lax.broadcasted_iota(jnp.int32, (t, 1), 0)
    w = w_ref[...]
    y = x * w[D_CONV // 2:D_CONV // 2 + 1]
    for tap in range(D_CONV):
        off = tap - D_CONV // 2
        if off == 0:
            continue
        src = idx + off
        same_seq = (src >= 0) & (src < t) & ((src >= n_ctx) == (idx >= n_ctx))
        y = y + jnp.where(same_seq, pltpu.roll(x, (-off) % t, 0), 0.0) * w[tap:tap + 1]
    y = y * jax.nn.sigmoid(y)
    inv = lax.rsqrt(jnp.sum(y * y, axis=-1, keepdims=True) + 1e-6)
    factor = jnp.where(j < D_HEADS, inv * D_HEAD_DIM ** -0.5, jnp.where(j < 2 * D_HEADS, inv, 1.0))
    o_ref[0] = y * factor


def gdn_qkv(p, d_conv_w, n_ctx, interpret=False):
    b, t, _ = p.shape
    col0 = IN_C // D_HEAD_DIM
    assert IN_C % D_HEAD_DIM == 0
    n_blk = 3 * D_DIM // D_HEAD_DIM
    return pl.pallas_call(
        functools.partial(_gdn_qkv_kernel, n_ctx=n_ctx),
        grid=(b, n_blk),
        in_specs=[pl.BlockSpec((1, t, D_HEAD_DIM), lambda i, j: (i, 0, col0 + j)),
                  pl.BlockSpec((D_CONV, D_HEAD_DIM), lambda i, j: (0, j))],
        out_specs=pl.BlockSpec((1, t, D_HEAD_DIM), lambda i, j: (i, 0, j)),
        out_shape=jax.ShapeDtypeStruct((b, t, 3 * D_DIM), F32),
        compiler_params=pltpu.CompilerParams(
            dimension_semantics=("arbitrary", "arbitrary"),
            vmem_limit_bytes=V7X_VMEM_LIMIT_BYTES),
        name="gdn_qkv",
        interpret=interpret,
    )(p, d_conv_w)


def gdn_prep(p, n_ctx, d_conv_w, d_A_log, d_dt_bias):
    qkv = gdn_qkv(p, d_conv_w, n_ctx)
    _, z, bf, bb, af, ab = split_cols(p[..., IN_C:], D_SIZES)
    betas = (jax.nn.sigmoid(bf), jax.nn.sigmoid(bb))
    gs = tuple(-jnp.exp(d_A_log[i]) * jax.nn.softplus(al + d_dt_bias[i]) for i, al in enumerate((af, ab)))
    return qkv, z, betas, gs


def gdn_output(o, z, d_o_norm):
    b, t = o.shape[:2]

    def heads(x):
        return x.reshape(b, t, D_HEADS, D_HEAD_DIM)

    return (rms_norm(heads(o), d_o_norm) * jax.nn.silu(heads(z))).reshape(b, t, D_DIM)


def mixer_cd(p, n_ctx, c_mu_prev, c_mu_next, c_w0, c_w2, c_a0, c_a2, c_g2, c_k_k, c_k_a, c_r_k,
             c_ln_w, c_ln_b, d_conv_w, d_A_log, d_dt_bias, d_o_norm):
    r, v, kk, g, lw, a, ks = rwkv_prep(p[..., :IN_C], n_ctx, c_mu_prev, c_mu_next, c_w0, c_w2, c_a0, c_a2, c_g2,
                                       c_k_k, c_k_a)
    y_f, y_b = rwkv_chunked(r, v, kk, lw[0], a[0], ks[0], lw[1], a[1], ks[1], n_ctx)
    qkv, z, beta, gd = gdn_prep(p, n_ctx, d_conv_w, d_A_log, d_dt_bias)
    o_f, o_b = gdn_chunked(qkv, beta[0], gd[0], beta[1], gd[1], n_ctx)
    return jnp.concatenate([rwkv_output(y_f + y_b, r, ks, v, g, c_r_k, c_ln_w, c_ln_b),
                            gdn_output(o_f + o_b, z, d_o_norm)], axis=-1)


FFN_TM = 256
FFN_HALO = 8
FFN_FC = 256
FFN_VMEM_LIMIT_BYTES = 56 * 1024 * 1024


def _modulated(x, shift, scale):
    y = x * lax.rsqrt(jnp.mean(x * x, axis=-1, keepdims=True) + EPS)
    return y * (1.0 + scale) + shift


def _ffn_kernel(xp_ref, x_ref, xn_ref, shift_ref, scale_ref, gate_ref, wup_ref, cw_ref, cb_ref, wdn_ref, o_ref, acc_ref,
                *, n_tiles, ctx_tiles):
    i = pl.program_id(1)
    tm, halo = FFN_TM, FFN_HALO
    rows = tm + 2 * halo
    x = x_ref[0]
    xe = jnp.concatenate([xp_ref[0], x, xn_ref[0]], axis=0)
    h = _modulated(xe, shift_ref[0, 0], scale_ref[0, 0]).astype(BF16)
    r = lax.broadcasted_iota(jnp.int32, (rows, 1), 0)
    seq_first = (i == 0) | (i == ctx_tiles)
    seq_last = (i == ctx_tiles - 1) | (i == n_tiles - 1)
    keep_prev = jnp.where((r == halo) & seq_first, 0.0, 1.0)
    keep_next = jnp.where((r == halo + tm - 1) & seq_last, 0.0, 1.0)
    acc_ref[...] = jnp.zeros_like(acc_ref)

    def conv(u, c0):
        w = cw_ref[:, c0:c0 + FFN_FC]
        um = pltpu.roll(u, 1, 0) * keep_prev
        up = pltpu.roll(u, rows - 1, 0) * keep_next
        y = um * w[0:1] + u * w[1:2] + up * w[2:3] + cb_ref[:, c0:c0 + FFN_FC]
        return y[halo:halo + tm]

    for c in range(D_FF // FFN_FC):
        cv, cg = c * FFN_FC, D_FF + c * FFN_FC
        val = conv(jnp.dot(h, wup_ref[:, cv:cv + FFN_FC], preferred_element_type=F32), cv)
        gat = conv(jnp.dot(h, wup_ref[:, cg:cg + FFN_FC], preferred_element_type=F32), cg)
        act = (gat * jax.nn.sigmoid(gat) * val).astype(BF16)
        acc_ref[...] += jnp.dot(act, wdn_ref[cv:cv + FFN_FC, :], preferred_element_type=F32)
    o_ref[0] = x + gate_ref[0, 0] * acc_ref[...]


def ffn_fused(x, shift, scale, gate, w_up, conv_w, conv_b, w_down, n_ctx, interpret=False):
    b, t, d = x.shape
    nt = t // FFN_TM
    ctx_tiles = n_ctx // FFN_TM
    hb = FFN_TM // FFN_HALO
    f2 = w_up.shape[1]
    mod = pl.BlockSpec((1, 1, 1, d), lambda bi, i: (bi, jnp.where(i >= ctx_tiles, 1, 0), 0, 0))

    def full(shp):
        return pl.BlockSpec(shp, lambda bi, i: (0,) * len(shp))

    def mods(m):
        return m.reshape(b, 2, 1, d)

    return pl.pallas_call(
        functools.partial(_ffn_kernel, n_tiles=nt, ctx_tiles=ctx_tiles),
        grid=(b, nt),
        in_specs=[pl.BlockSpec((1, FFN_HALO, d), lambda bi, i: (bi, jnp.maximum(i * hb - 1, 0), 0)),
                  pl.BlockSpec((1, FFN_TM, d), lambda bi, i: (bi, i, 0)),
                  pl.BlockSpec((1, FFN_HALO, d), lambda bi, i: (bi, jnp.minimum((i + 1) * hb, nt * hb - 1), 0)),
                  mod, mod, mod,
                  full((d, f2)), full((FFN_CONV, f2)), full((1, f2)), full((f2 // 2, d))],
        out_specs=pl.BlockSpec((1, FFN_TM, d), lambda bi, i: (bi, i, 0)),
        out_shape=jax.ShapeDtypeStruct((b, t, d), F32),
        scratch_shapes=[pltpu.VMEM((FFN_TM, d), F32)],
        compiler_params=pltpu.CompilerParams(
            dimension_semantics=("arbitrary", "arbitrary"),
            vmem_limit_bytes=FFN_VMEM_LIMIT_BYTES),
        name="conv_ffn",
        interpret=interpret,
    )(x, x, x, mods(shift), mods(scale), mods(gate), w_up.astype(BF16), conv_w, conv_b.reshape(1, f2),
      w_down.astype(BF16))


def kernel(x, c, ctx, c_ctx, ada_w, ada_b, ffn_w_up, ffn_conv_w, ffn_conv_b, ffn_w_down,
           ab_w_in, ab_w_out, a_q_norm, a_k_norm, a_sink, b_cq_norm, b_ckv_norm, b_w_uq, b_w_uk, b_w_uv,
           b_qn_norm, b_qr_norm, b_kn_norm, b_kr_norm, cd_w_in, cd_w_out, c_mu_prev, c_mu_next, c_w0, c_w2,
           c_a0, c_a2, c_g2, c_k_k, c_k_a, c_r_k, c_ln_w, c_ln_b, d_conv_w, d_A_log, d_dt_bias, d_o_norm):
    bsz, seq = x.shape[:2]
    n_ctx = ctx.shape[1]
    rows = seq // GRID_W
    zeros = jnp.zeros((n_ctx,), jnp.int32)
    row = jnp.concatenate([zeros, jnp.repeat(jnp.arange(rows, dtype=jnp.int32), GRID_W)])
    col = jnp.concatenate([zeros, jnp.tile(jnp.arange(GRID_W, dtype=jnp.int32), rows)])
    is_ctx = (jnp.arange(n_ctx + seq) < n_ctx)[None, :, None]
    silu_c = jax.nn.silu(c)
    silu_cc = jax.nn.silu(c_ctx)
    xa = jnp.concatenate([ctx, x], axis=1)
    for l in range(DEPTH):
        last = l == DEPTH - 1
        i = l // 2
        mod_l = jnp.split(silu_c @ ada_w[l] + ada_b[l], N_MOD, axis=-1)
        mod_c = jnp.split(silu_cc @ ada_w[l] + ada_b[l], N_MOD, axis=-1)
        mods = [jnp.stack([jnp.broadcast_to(mc, ml.shape), ml], axis=1) for mc, ml in zip(mod_c, mod_l)]

        def per_token(m):
            return jnp.where(is_ctx, m[:, 0:1, :], m[:, 1:2, :])

        h = rms_norm(xa) * (1.0 + per_token(mods[1])) + per_token(mods[0])
        if l % 2 == 0:
            y = mixer_ab(_mm(h, ab_w_in[i]), n_ctx, row, col, a_q_norm[i], a_k_norm[i],
                         a_sink[i], b_cq_norm[i], b_ckv_norm[i], b_w_uq[i], b_w_uk[i], b_w_uv[i],
                         b_qn_norm[i], b_qr_norm[i], b_kn_norm[i], b_kr_norm[i], not last)
            w_out = ab_w_out[i]
        else:
            y = mixer_cd(_mm(h, cd_w_in[i]), n_ctx, c_mu_prev[i], c_mu_next[i], c_w0[i],
                         c_w2[i], c_a0[i], c_a2[i], c_g2[i], c_k_k[i], c_k_a[i], c_r_k[i], c_ln_w[i],
                         c_ln_b[i], d_conv_w[i], d_A_log[i], d_dt_bias[i], d_o_norm[i])
            w_out = cd_w_out[i]
        xa = xa + per_token(mods[2]) * _mm(y, w_out)
        xa = ffn_fused(xa, mods[3], mods[4], mods[5], ffn_w_up[l], ffn_conv_w[l], ffn_conv_b[l], ffn_w_down[l],
                       n_ctx)
    return xa[:, n_ctx:]
```

```python
import functools

import jax
import jax.numpy as jnp
from jax import lax
import numpy as np
from jax.experimental import pallas as pl
from jax.experimental.pallas import tpu as pltpu

D_MODEL = 1024
DEPTH = 2
GRID_W = 64
N_MOD = 6
EPS = 1e-6
ROPE_THETA = 10000.0
NEG_INF = -1e30

A_HEADS = 8
A_KV_HEADS = 2
A_HEAD_DIM = 64
WINDOW = 128
B_HEADS = 8
B_Q_RANK = 256
B_KV_RANK = 256
B_NOPE = 64
B_ROPE = 32
B_V_DIM = 64
C_HEADS = 8
C_HEAD = 64
C_DIM = C_HEADS * C_HEAD
C_DECAY_LORA = 64
C_AAA_LORA = 64
C_GATE_LORA = 128
C_GN_EPS = 64e-5
D_HEADS = 4
D_HEAD_DIM = 128
D_DIM = D_HEADS * D_HEAD_DIM
D_CONV = 5
D_CHUNK = 64
D_FF = 2816
FFN_CONV = 3

AB_SIZES = (A_HEADS * A_HEAD_DIM, A_KV_HEADS * A_HEAD_DIM, A_KV_HEADS * A_HEAD_DIM, B_Q_RANK, B_KV_RANK, B_ROPE)
C_SIZES = (C_DIM, C_DIM, C_DIM, C_DECAY_LORA, C_DECAY_LORA, C_AAA_LORA, C_AAA_LORA, C_GATE_LORA)
IN_C = sum(C_SIZES)
D_SIZES = (3 * D_DIM, D_DIM, D_HEADS, D_HEADS, D_HEADS, D_HEADS)

F32 = jnp.float32
BF16 = jnp.bfloat16

V7X_VMEM_LIMIT_BYTES = 48 * 1024 * 1024
LANE = 128
MXU_N = 256

NN = ((1,), (0,))
NT = ((1,), (1,))
TN = ((0,), (0,))


def _mm_kernel(a_ref, b_ref, o_ref):
    o_ref[...] = jnp.dot(a_ref[...].astype(BF16), b_ref[...], preferred_element_type=F32)


def _pick_tile(n, candidates):
    for c in candidates:
        if n % c == 0:
            return c
    raise ValueError(f"no tile for {n}")


def _mm(a, w, keep_pad=False):
    lead = a.shape[:-1]
    k = a.shape[-1]
    n = w.shape[-1]
    a2 = a.reshape(-1, k)
    m = a2.shape[0]
    n_pad = -(-n // MXU_N) * MXU_N
    wb = w.astype(BF16)
    if n_pad != n:
        wb = jnp.pad(wb, ((0, 0), (0, n_pad - n)))
    tm = _pick_tile(m, (1024, 512, 256, 128, 8))
    tn = _pick_tile(n_pad, (1024, 768, 512, 256))
    out = pl.pallas_call(
        _mm_kernel,
        grid=(m // tm, n_pad // tn),
        in_specs=[pl.BlockSpec((tm, k), lambda i, j: (i, 0)),
                  pl.BlockSpec((k, tn), lambda i, j: (0, j))],
        out_specs=pl.BlockSpec((tm, tn), lambda i, j: (i, j)),
        out_shape=jax.ShapeDtypeStruct((m, n_pad), F32),
        compiler_params=pltpu.CompilerParams(
            dimension_semantics=("arbitrary", "arbitrary"),
            vmem_limit_bytes=V7X_VMEM_LIMIT_BYTES),
        name="mm",
    )(a2, wb)
    if n_pad != n and not keep_pad:
        out = out[:, :n]
    return out.reshape(*lead, out.shape[-1])


def _dot(a, b, dims, passes=1):
    def dg(x, y):
        return lax.dot_general(x, y, (dims, ((), ())), preferred_element_type=F32)

    ah, bh = a.astype(BF16), b.astype(BF16)
    if passes == 1:
        return dg(ah, bh)
    al = (a - ah.astype(F32)).astype(BF16)
    bl = (b - bh.astype(F32)).astype(BF16)
    return dg(ah, bh) + (dg(ah, bl) + dg(al, bh))


def _dot_exact(a, b, dims):
    return lax.dot_general(a, b, (dims, ((), ())), precision=lax.Precision.HIGHEST, preferred_element_type=F32)


def rms_norm(x, eps=EPS):
    return x * lax.rsqrt(jnp.mean(x * x, axis=-1, keepdims=True) + eps)


def _heads_major(z):
    return jnp.swapaxes(z, 1, 2)


def _softmax_pv(s, v, sink):
    m = jnp.max(s, axis=-1, keepdims=True)
    if sink is not None:
        m = jnp.maximum(m, sink)
    p = jnp.exp(s - m)
    den = jnp.sum(p, axis=-1, keepdims=True)
    if sink is not None:
        den = den + jnp.exp(sink - m)
    return _dot(p, v, NN) / den


def _attn_full_kernel(q_ref, k_ref, q2_ref, k2_ref, v_ref, sink_ref, o_ref, *, scale, use_sink, use_second):
    s = _dot(q_ref[0, 0], k_ref[0, 0], NT)
    if use_second:
        s = s + _dot(q2_ref[0, 0], k2_ref[0, 0], NT)
    sink = sink_ref[0] if use_sink else None
    o_ref[0, 0] = _softmax_pv(s * scale, v_ref[0, 0], sink)


def attn_full(q, k, v, scale, q_start, n_q, n_keys, sink=None, second=None):
    b, h, _, d = q.shape
    hk = k.shape[1]
    dv = v.shape[-1]
    g = h // hk
    tq = min(n_q, 256)
    q0 = q_start // tq
    use_sink = sink is not None
    use_second = second is not None
    sink_arr = (sink if use_sink else jnp.zeros((h,), F32)).astype(F32).reshape(h, 1, 1)
    q2, k2 = second if use_second else (q, k)
    d2 = q2.shape[-1]
    k2_spec = (pl.BlockSpec((1, 1, n_keys, d2), lambda i, j, t: (i, 0, 0, 0)) if use_second else
               pl.BlockSpec((1, 1, n_keys, d2), lambda i, j, t: (i, j // g, 0, 0)))
    return pl.pallas_call(
        functools.partial(_attn_full_kernel, scale=scale, use_sink=use_sink, use_second=use_second),
        grid=(b, h, n_q // tq),
        in_specs=[pl.BlockSpec((1, 1, tq, d), lambda i, j, t: (i, j, t + q0, 0)),
                  pl.BlockSpec((1, 1, n_keys, d), lambda i, j, t: (i, j // g, 0, 0)),
                  pl.BlockSpec((1, 1, tq, d2), lambda i, j, t: (i, j, t + q0, 0)),
                  k2_spec,
                  pl.BlockSpec((1, 1, n_keys, dv), lambda i, j, t: (i, j // g, 0, 0)),
                  pl.BlockSpec((1, 1, 1), lambda i, j, t: (j, 0, 0))],
        out_specs=pl.BlockSpec((1, 1, tq, dv), lambda i, j, t: (i, j, t, 0)),
        out_shape=jax.ShapeDtypeStruct((b, h, n_q, dv), F32),
        compiler_params=pltpu.CompilerParams(
            dimension_semantics=("arbitrary", "arbitrary", "arbitrary"),
            vmem_limit_bytes=V7X_VMEM_LIMIT_BYTES),
        name="attn_full",
    )(q, k, q2, k2, v, sink_arr)


def _attn_window_kernel(q_ref, kp_ref, k0_ref, kn_ref, kc_ref, vp_ref, v0_ref, vn_ref, vc_ref, sink_ref, o_ref,
                        *, scale, group, n_blocks):
    n = pl.program_id(2)
    w = WINDOW
    d = q_ref.shape[-1]
    q = q_ref[0].reshape(group * w, d)
    keys = jnp.concatenate([kp_ref[0, 0], k0_ref[0, 0], kn_ref[0, 0], kc_ref[0, 0]], axis=0)
    vals = jnp.concatenate([vp_ref[0, 0], v0_ref[0, 0], vn_ref[0, 0], vc_ref[0, 0]], axis=0)
    s = _dot(q, keys, NT) * scale
    nk = keys.shape[0]
    qi = lax.broadcasted_iota(jnp.int32, (group * w, nk), 0) % w
    kj = lax.broadcasted_iota(jnp.int32, (group * w, nk), 1)
    rel = qi + w - kj
    band_ok = (jnp.abs(rel) <= w) & ((kj >= w) | (n > 0)) & ((kj < 2 * w) | (n < n_blocks - 1))
    s = jnp.where((kj >= 3 * w) | band_ok, s, NEG_INF)
    sink = jnp.concatenate([jnp.broadcast_to(sink_ref[0, hh], (w, 1)) for hh in range(group)], axis=0)
    o = _softmax_pv(s, vals, sink)
    o_ref[0] = o.reshape(group, w, o.shape[-1])


def attn_window(q, k, v, sink, n_ctx):
    b, h, t, d = q.shape
    hk = k.shape[1]
    g = h // hk
    off = n_ctx // WINDOW
    nb = (t - n_ctx) // WINDOW
    sink_arr = sink.astype(F32).reshape(hk, g, 1, 1)

    def blk(f):
        return pl.BlockSpec((1, 1, WINDOW, d), f)

    prev = blk(lambda i, j, n: (i, j, jnp.maximum(n - 1, 0) + off, 0))
    own = blk(lambda i, j, n: (i, j, n + off, 0))
    nxt = blk(lambda i, j, n: (i, j, jnp.minimum(n + 1, nb - 1) + off, 0))
    ctx = pl.BlockSpec((1, 1, n_ctx, d), lambda i, j, n: (i, j, 0, 0))
    return pl.pallas_call(
        functools.partial(_attn_window_kernel, scale=d ** -0.5, group=g, n_blocks=nb),
        grid=(b, hk, nb),
        in_specs=[pl.BlockSpec((1, g, WINDOW, d), lambda i, j, n: (i, j, n + off, 0)),
                  prev, own, nxt, ctx, prev, own, nxt, ctx,
                  pl.BlockSpec((1, g, 1, 1), lambda i, j, n: (j, 0, 0, 0))],
        out_specs=pl.BlockSpec((1, g, WINDOW, d), lambda i, j, n: (i, j, n, 0)),
        out_shape=jax.ShapeDtypeStruct((b, h, t - n_ctx, d), F32),
        compiler_params=pltpu.CompilerParams(
            dimension_semantics=("arbitrary", "arbitrary", "arbitrary"),
            vmem_limit_bytes=V7X_VMEM_LIMIT_BYTES),
        name="attn_window",
    )(q, k, k, k, k, v, v, v, v, sink_arr)


def _rope_tables(row, col, head_dim, width):
    q = head_dim // 4
    inv = jnp.power(ROPE_THETA, -jnp.arange(q, dtype=F32) / q)
    ang_r = row.astype(F32)[:, None] * inv[None, :]
    ang_c = col.astype(F32)[:, None] * inv[None, :]
    cos = jnp.concatenate([jnp.cos(ang_r), jnp.cos(ang_r), jnp.cos(ang_c), jnp.cos(ang_c)], axis=-1)
    sin = jnp.concatenate([-jnp.sin(ang_r), jnp.sin(ang_r), -jnp.sin(ang_c), jnp.sin(ang_c)], axis=-1)
    reps = width // head_dim
    return jnp.tile(cos, (1, reps)), jnp.tile(sin, (1, reps))


def _rope(x, cos, sin, head_dim):
    q = head_dim // 4
    w = x.shape[-1]
    lane = lax.broadcasted_iota(jnp.int32, x.shape, 1)
    swapped = jnp.where(lane % (2 * q) < q, pltpu.roll(x, w - q, 1), pltpu.roll(x, q, 1))
    return x * cos + swapped * sin


def _head_rms(x, ones_bd, head_dim, gain):
    return x * lax.rsqrt(_seg_sum(x * x, ones_bd) * (1.0 / head_dim) + EPS) * gain


def _row_rms(x, n, gain):
    return x * lax.rsqrt(jnp.sum(x * x, axis=-1, keepdims=True) * (1.0 / n) + EPS) * gain


def _ab_prep_kernel(p_ref, c64_ref, s64_ref, c32_ref, s32_ref, gqa_ref, gka_ref, gcq_ref, gckv_ref, gqn_ref, gqr_ref,
                    gkn_ref, gkr_ref, wuq_ref, wuk_ref, wuv_ref,
                    qa_ref, ka_ref, qn_ref, qr_ref, kn_ref, vb_ref, kr_ref):
    p = p_ref[0]
    offs = np.cumsum((0,) + AB_SIZES)
    qa, ka, cq, ckv = (p[:, offs[n]:offs[n + 1]] for n in (0, 1, 3, 4))
    kr = p[:, offs[5]:offs[5] + LANE]
    c64, s64, c32, s32 = c64_ref[...], s64_ref[...], c32_ref[...], s32_ref[...]
    ones_q = _block_ones(A_HEADS * A_HEAD_DIM, A_HEAD_DIM)
    ones_k = _block_ones(A_KV_HEADS * A_HEAD_DIM, A_HEAD_DIM)
    ones_r = _block_ones(B_HEADS * B_ROPE, B_ROPE)
    ka_w = A_KV_HEADS * A_HEAD_DIM
    qa_ref[0] = _rope(_head_rms(qa, ones_q, A_HEAD_DIM, gqa_ref[...]), c64, s64, A_HEAD_DIM)
    ka_ref[0] = _rope(_head_rms(ka, ones_k, A_HEAD_DIM, gka_ref[...]), c64[:, :ka_w], s64[:, :ka_w], A_HEAD_DIM)
    qb = jnp.dot(_row_rms(cq, B_Q_RANK, gcq_ref[...]).astype(BF16), wuq_ref[...], preferred_element_type=F32)
    n_nope = B_HEADS * B_NOPE
    qn_ref[0] = _head_rms(qb[:, :n_nope], ones_q, B_NOPE, gqn_ref[...])
    qr_ref[0] = _rope(_head_rms(qb[:, n_nope:], ones_r, B_ROPE, gqr_ref[...]), c32, s32, B_ROPE)
    ckv_n = _row_rms(ckv, B_KV_RANK, gckv_ref[...]).astype(BF16)
    kn_ref[0] = _head_rms(jnp.dot(ckv_n, wuk_ref[...], preferred_element_type=F32), ones_q, B_NOPE, gkn_ref[...])
    vb_ref[0] = jnp.dot(ckv_n, wuv_ref[...], preferred_element_type=F32)
    kr_ref[0] = _rope(_row_rms(kr, B_ROPE, gkr_ref[...]), c32[:, :LANE], s32[:, :LANE], B_ROPE)


def ab_prep(p, row, col, a_q_norm, a_k_norm, b_cq_norm, b_ckv_norm, b_w_uq, b_w_uk, b_w_uv,
            b_qn_norm, b_qr_norm, b_kn_norm, b_kr_norm, interpret=False):
    b, t, pw = p.shape
    assert pw >= sum(AB_SIZES[:5]) + LANE and A_HEAD_DIM == B_NOPE
    c64, s64 = _rope_tables(row, col, A_HEAD_DIM, A_HEADS * A_HEAD_DIM)
    c32, s32 = _rope_tables(row, col, B_ROPE, B_HEADS * B_ROPE)
    wq = b_w_uq.reshape(B_Q_RANK, B_HEADS, B_NOPE + B_ROPE)
    wq = jnp.concatenate([wq[..., :B_NOPE].reshape(B_Q_RANK, -1), wq[..., B_NOPE:].reshape(B_Q_RANK, -1)], axis=1)

    def row_c(g, reps, width=None):
        x = jnp.tile(g, reps).reshape(1, -1)
        if width is not None:
            x = jnp.pad(x, ((0, 0), (0, width - x.shape[1])))
        return x

    consts = [row_c(a_q_norm, A_HEADS), row_c(a_k_norm, A_KV_HEADS), row_c(b_cq_norm, 1), row_c(b_ckv_norm, 1),
              row_c(b_qn_norm, B_HEADS), row_c(b_qr_norm, B_HEADS), row_c(b_kn_norm, B_HEADS),
              row_c(b_kr_norm, 1, LANE), wq.astype(BF16), b_w_uk.astype(BF16), b_w_uv.astype(BF16)]
    tabs = [c64, s64, c32, s32]
    widths = (A_HEADS * A_HEAD_DIM, A_KV_HEADS * A_HEAD_DIM, B_HEADS * B_NOPE, B_HEADS * B_ROPE, B_HEADS * B_NOPE,
              B_HEADS * B_V_DIM, LANE)
    return pl.pallas_call(
        _ab_prep_kernel,
        grid=(b, t // TOK_TM),
        in_specs=[pl.BlockSpec((1, TOK_TM, pw), lambda bi, i: (bi, i, 0))]
        + [pl.BlockSpec((TOK_TM, x.shape[1]), lambda bi, i: (i, 0)) for x in tabs]
        + [pl.BlockSpec(x.shape, lambda bi, i: (0, 0)) for x in consts],
        out_specs=[pl.BlockSpec((1, TOK_TM, w), lambda bi, i: (bi, i, 0)) for w in widths],
        out_shape=[jax.ShapeDtypeStruct((b, t, w), F32) for w in widths],
        compiler_params=pltpu.CompilerParams(
            dimension_semantics=("arbitrary", "arbitrary"),
            vmem_limit_bytes=V7X_VMEM_LIMIT_BYTES),
        name="ab_prep",
        interpret=interpret,
    )(p, *tabs, *consts)


def mixer_ab(p, n_ctx, row, col, a_q_norm, a_k_norm, a_sink, b_cq_norm, b_ckv_norm, b_w_uq, b_w_uk,
             b_w_uv, b_qn_norm, b_qr_norm, b_kn_norm, b_kr_norm, ctx_out):
    b, t = p.shape[:2]
    qa, ka, qn, qr, kn, vb, kr = ab_prep(p, row, col, a_q_norm, a_k_norm, b_cq_norm, b_ckv_norm, b_w_uq, b_w_uk,
                                         b_w_uv, b_qn_norm, b_qr_norm, b_kn_norm, b_kr_norm)
    va_off = AB_SIZES[0] + AB_SIZES[1]
    va = p[..., va_off:va_off + AB_SIZES[2]]

    def hm(z, heads):
        return _heads_major(z.reshape(b, t, heads, -1))

    qa_t, ka_t, va_t = hm(qa, A_HEADS), hm(ka, A_KV_HEADS), hm(va, A_KV_HEADS)
    qn_t, kn_t, vb_t = hm(qn, B_HEADS), hm(kn, B_HEADS), hm(vb, B_HEADS)
    rope_part = (hm(qr, B_HEADS), hm(kr[..., :B_ROPE], 1))
    b_scale = (B_NOPE + B_ROPE) ** -0.5

    o_a = attn_window(qa_t, ka_t, va_t, a_sink, n_ctx)
    o_b = attn_full(qn_t, kn_t, vb_t, b_scale, n_ctx, t - n_ctx, t, second=rope_part)
    if ctx_out:
        o_a_c = attn_full(qa_t, ka_t, va_t, A_HEAD_DIM ** -0.5, 0, n_ctx, n_ctx, sink=a_sink)
        o_b_c = attn_full(qn_t, kn_t, vb_t, b_scale, 0, n_ctx, n_ctx, second=rope_part)
    else:
        o_a_c = jnp.zeros((b, A_HEADS, n_ctx, A_HEAD_DIM), F32)
        o_b_c = jnp.zeros((b, B_HEADS, n_ctx, B_V_DIM), F32)
    o_a = _heads_major(jnp.concatenate([o_a_c, o_a], axis=2)).reshape(b, t, -1)
    o_b = _heads_major(jnp.concatenate([o_b_c, o_b], axis=2)).reshape(b, t, -1)
    return jnp.concatenate([o_a, o_b], axis=-1)


RW_CHUNK = 64
HEAD_PAIR = 2 * C_HEAD


def _unit_lower_inverse(xs, eye):
    ps = [eye + x for x in xs]
    xps = list(xs)
    for _ in range(int(np.log2(RW_CHUNK)) - 1):
        xps = [_dot(xp, xp, NN, passes=3) for xp in xps]
        ps = [p + _dot(p, xp, NN, passes=3) for p, xp in zip(ps, xps)]
    return ps


def _chunk_maps(nc, nctx):
    def fwd(s):
        return s

    def bwd(s):
        return jnp.where(s < nctx, nctx - 1 - s, nc + nctx - 1 - s)

    return fwd, bwd


def _rwkv_kernel(rf_ref, vf_ref, kkf_ref, lwf_ref, af_ref, kf_ref,
                 rb_ref, vb_ref, kkb_ref, lwb_ref, ab_ref, kb_ref, yf_ref, yb_ref, st_ref):
    @pl.when(pl.program_id(1) == 0)
    def _():
        st_ref[...] = jnp.zeros_like(st_ref)

    c_len = RW_CHUNK
    ii = lax.broadcasted_iota(jnp.int32, (c_len, c_len), 0)
    jj = lax.broadcasted_iota(jnp.int32, (c_len, c_len), 1)
    r2 = lax.broadcasted_iota(jnp.int32, (HEAD_PAIR, HEAD_PAIR), 0)
    c2 = lax.broadcasted_iota(jnp.int32, (HEAD_PAIR, HEAD_PAIR), 1)
    dlt = (r2 % c_len) - (c2 % c_len)
    eye = jnp.where(r2 == c2, 1.0, 0.0).astype(F32)
    lane = lax.broadcasted_iota(jnp.int32, (c_len, HEAD_PAIR), 1)
    m0 = lane < C_HEAD

    def stack_heads(x):
        return jnp.concatenate([jnp.where(m0, x, 0.0), jnp.where(m0, 0.0, x)], axis=0)

    n_pairs = C_DIM // HEAD_PAIR
    ch = []
    for d, refs in enumerate(((rf_ref, vf_ref, kkf_ref, lwf_ref, af_ref, kf_ref),
                              (rb_ref, vb_ref, kkb_ref, lwb_ref, ab_ref, kb_ref))):
        sign = 1 - 2 * d
        r_all, v_all, kk_all, lw_all, a_all, kd_all = (ref[0] for ref in refs)
        tri = jnp.where((ii - jj) * sign >= 0, 1.0, 0.0).astype(F32)
        cum_all = _dot_exact(tri, lw_all, NN)
        tot_all = jnp.sum(lw_all, axis=0, keepdims=True)
        for p in range(n_pairs):
            sl = slice(p * HEAD_PAIR, (p + 1) * HEAD_PAIR)
            ch.append(dict(strict=dlt * sign > 0, incl=dlt * sign >= 0, st=st_ref[d, p],
                           lw=lw_all[:, sl], cum=cum_all[:, sl], tot=tot_all[:, sl], kk=kk_all[:, sl],
                           kd=kd_all[:, sl], bb=kk_all[:, sl] * a_all[:, sl], r=r_all[:, sl], v=v_all[:, sl]))
    for c in ch:
        e_neg = jnp.exp(-c['cum'])
        e_end = jnp.exp(c['tot'] - c['cum'])
        c['abar'] = stack_heads(-c['kk'] * jnp.exp(c['cum'] - c['lw']))
        c['rbar'] = stack_heads(c['r'] * jnp.exp(c['cum']))
        c['ktil'] = stack_heads(c['kd'] * e_neg)
        c['btil'] = stack_heads(c['bb'] * e_neg)
        c['khat'] = stack_heads(c['kd'] * e_end)
        c['bhat'] = stack_heads(c['bb'] * e_end)
        c['vs'] = stack_heads(c['v'])
    g = [_dot(jnp.concatenate([c['abar'], c['rbar']], axis=0), jnp.concatenate([c['ktil'], c['btil']], axis=0), NT)
         for c in ch]
    a_ak = [jnp.where(c['strict'], x[:HEAD_PAIR, :HEAD_PAIR], 0.0) for c, x in zip(ch, g)]
    a_ab = [jnp.where(c['strict'], x[:HEAD_PAIR, HEAD_PAIR:], 0.0) for c, x in zip(ch, g)]
    a_rk = [jnp.where(c['incl'], x[HEAD_PAIR:, :HEAD_PAIR], 0.0) for c, x in zip(ch, g)]
    a_rb = [jnp.where(c['incl'], x[HEAD_PAIR:, HEAD_PAIR:], 0.0) for c, x in zip(ch, g)]
    tinv = _unit_lower_inverse(a_ab, eye)
    n = range(len(ch))
    wm = [_dot(tinv[i], ch[i]['abar'], NN) for i in n]
    akv = [_dot(a_ak[i], ch[i]['vs'], NN) for i in n]
    u0 = [_dot(tinv[i], akv[i], NN) for i in n]
    u = [_dot(wm[i], ch[i]['st'], NT) + u0[i] for i in n]
    ys = [_dot(ch[i]['rbar'], ch[i]['st'], NT) + _dot(a_rk[i], ch[i]['vs'], NN) + _dot(a_rb[i], u[i], NN) for i in n]
    st_new = [ch[i]['st'] * jnp.exp(ch[i]['tot']) + _dot(ch[i]['vs'], ch[i]['khat'], TN) + _dot(u[i], ch[i]['bhat'], TN)
              for i in n]
    for d, y_ref in enumerate((yf_ref, yb_ref)):
        y_ref[0] = jnp.concatenate([ys[d * n_pairs + p][:c_len] + ys[d * n_pairs + p][c_len:] for p in range(n_pairs)],
                                   axis=1)
        for p in range(n_pairs):
            st_ref[d, p] = st_new[d * n_pairs + p]


def rwkv_chunked(r, v, kk, lw_f, a_f, k_f, lw_b, a_b, k_b, n_ctx, interpret=False):
    b, t, cd = r.shape
    nc = t // RW_CHUNK
    fwd, bwd = _chunk_maps(nc, n_ctx // RW_CHUNK)
    sf = pl.BlockSpec((1, RW_CHUNK, cd), lambda i, s: (i, fwd(s), 0))
    sb = pl.BlockSpec((1, RW_CHUNK, cd), lambda i, s: (i, bwd(s), 0))
    out = jax.ShapeDtypeStruct((b, t, cd), F32)
    return pl.pallas_call(
        _rwkv_kernel,
        grid=(b, nc),
        in_specs=[sf] * 6 + [sb] * 6,
        out_specs=[sf, sb],
        out_shape=[out, out],
        scratch_shapes=[pltpu.VMEM((2, cd // HEAD_PAIR, HEAD_PAIR, HEAD_PAIR), F32)],
        compiler_params=pltpu.CompilerParams(
            dimension_semantics=("arbitrary", "arbitrary"),
            vmem_limit_bytes=V7X_VMEM_LIMIT_BYTES),
        name="rwkv7_chunked",
        interpret=interpret,
    )(r, v, kk, lw_f, a_f, k_f, r, v, kk, lw_b, a_b, k_b)


def _gdn_kernel(qf_ref, kf_ref, vf_ref, bf_ref, gf_ref, gtf_ref,
                qb_ref, kb_ref, vb_ref, bb_ref, gb_ref, gtb_ref, of_ref, ob_ref, st_ref):
    @pl.when(pl.program_id(1) == 0)
    def _():
        st_ref[...] = jnp.zeros_like(st_ref)

    c_len = D_CHUNK
    n2 = 2 * c_len
    r2 = lax.broadcasted_iota(jnp.int32, (n2, n2), 0)
    c2 = lax.broadcasted_iota(jnp.int32, (n2, n2), 1)
    same = (r2 // c_len) == (c2 // c_len)
    dlt = (r2 % c_len) - (c2 % c_len)
    ones_bd = jnp.where(same, 1.0, 0.0).astype(F32)
    eye = jnp.where(r2 == c2, 1.0, 0.0).astype(F32)

    def rows(blk, p):
        return jnp.concatenate([blk[:, h * D_HEAD_DIM:(h + 1) * D_HEAD_DIM] for h in (2 * p, 2 * p + 1)], axis=0)

    def col(blk, p):
        c = jnp.concatenate([blk[:, h:h + 1] for h in (2 * p, 2 * p + 1)], axis=0)
        return jnp.broadcast_to(c, (n2, n2))

    n_pairs = D_HEADS // 2
    ch = []
    for d, refs in enumerate(((qf_ref, kf_ref, vf_ref, bf_ref, gf_ref, gtf_ref),
                              (qb_ref, kb_ref, vb_ref, bb_ref, gb_ref, gtb_ref))):
        sign = 1 - 2 * d
        q_blk, k_blk, v_blk = refs[0][0], refs[1][0], refs[2][0]
        b_blk, g_blk, gt_blk = refs[3][0, 0], refs[4][0, 0], refs[5][0, 0]
        incl = same & (dlt * sign >= 0)
        tri = jnp.where(incl, 1.0, 0.0).astype(F32)
        for p in range(n_pairs):
            g_row = jnp.concatenate([gt_blk[h:h + 1, :] for h in (2 * p, 2 * p + 1)], axis=1)
            ch.append(dict(d=d, p=p, strict=same & (dlt * sign > 0), incl=incl, tri=tri,
                           q=rows(q_blk, p), k=rows(k_blk, p), v=rows(v_blk, p), beta=col(b_blk, p),
                           g_colb=col(g_blk, p), g_rowb=jnp.broadcast_to(g_row, (n2, n2)),
                           st=[st_ref[d, 2 * p], st_ref[d, 2 * p + 1]]))
    n = range(len(ch))
    gc_col = [_dot_exact(c['tri'], c['g_colb'], NN) for c in ch]
    gc_row = [_dot_exact(c['g_rowb'], c['tri'], NT) for c in ch]
    g_end = [_dot_exact(ones_bd, c['g_colb'], NN) for c in ch]
    decay = [jnp.where(ch[i]['incl'], jnp.exp(jnp.where(ch[i]['incl'], gc_col[i] - gc_row[i], 0.0)), 0.0) for i in n]
    kb = [c['k'] * c['beta'] for c in ch]
    vb = [c['v'] * c['beta'] for c in ch]
    g_mat = [_dot(jnp.concatenate([kb[i], ch[i]['q']], axis=0), ch[i]['k'], NT) for i in n]
    l_mat = [jnp.where(ch[i]['strict'], g_mat[i][:n2] * decay[i], 0.0) for i in n]
    a_intra = [g_mat[i][n2:] * decay[i] for i in n]
    tinv = _unit_lower_inverse([-l for l in l_mat], eye)
    u = [_dot(tinv[i], vb[i], NN) for i in n]
    wk = [_dot(tinv[i], kb[i] * jnp.exp(gc_col[i]), NN) for i in n]
    q_e = [ch[i]['q'] * jnp.exp(gc_col[i]) for i in n]
    k_e = [ch[i]['k'] * jnp.exp(g_end[i] - gc_col[i]) for i in n]
    halves = (slice(0, c_len), slice(c_len, n2))
    v_new = [[u[i][hs] - _dot(wk[i][hs], ch[i]['st'][j], NN) for j, hs in enumerate(halves)] for i in n]
    o_st = [[_dot(q_e[i][hs], ch[i]['st'][j], NN) for j, hs in enumerate(halves)] for i in n]
    v_new_s = [jnp.concatenate(v_new[i], axis=0) for i in n]
    o_s = [jnp.concatenate(o_st[i], axis=0) + _dot(a_intra[i], v_new_s[i], NN) for i in n]
    st_new = [[ch[i]['st'][j] * jnp.exp(g_end[i][hs][0:1, :]) + _dot(k_e[i][hs], v_new[i][j], TN)
               for j, hs in enumerate(halves)] for i in n]
    for d, o_ref in enumerate((of_ref, ob_ref)):
        o_ref[0] = jnp.concatenate([o_s[d * n_pairs + p][hs] for p in range(n_pairs) for hs in halves], axis=1)
        for p in range(n_pairs):
            for j in range(2):
                st_ref[d, 2 * p + j] = st_new[d * n_pairs + p][j]


def gdn_chunked(qkv, beta_f, g_f, beta_b, g_b, n_ctx, interpret=False):
    b, t, _ = qkv.shape
    cd = D_DIM
    nc = t // D_CHUNK
    fwd, bwd = _chunk_maps(nc, n_ctx // D_CHUNK)

    def chunks(z):
        return z.reshape(b, nc, D_CHUNK, D_HEADS)

    def specs(cmap):
        big = [pl.BlockSpec((1, D_CHUNK, cd), functools.partial(lambda part, i, s: (i, cmap(s), part), part))
               for part in range(3)]
        small = pl.BlockSpec((1, 1, D_CHUNK, D_HEADS), lambda i, s: (i, cmap(s), 0, 0))
        small_t = pl.BlockSpec((1, 1, D_HEADS, D_CHUNK), lambda i, s: (i, cmap(s), 0, 0))
        return big, small, small_t

    bf, sf, stf = specs(fwd)
    bb, sb, stb = specs(bwd)
    out = jax.ShapeDtypeStruct((b, t, cd), F32)
    return pl.pallas_call(
        _gdn_kernel,
        grid=(b, nc),
        in_specs=bf + [sf, sf, stf] + bb + [sb, sb, stb],
        out_specs=[bf[0], bb[0]],
        out_shape=[out, out],
        scratch_shapes=[pltpu.VMEM((2, D_HEADS, D_HEAD_DIM, D_HEAD_DIM), F32)],
        compiler_params=pltpu.CompilerParams(
            dimension_semantics=("arbitrary", "arbitrary"),
            vmem_limit_bytes=V7X_VMEM_LIMIT_BYTES),
        name="gdn_chunked",
        interpret=interpret,
    )(qkv, qkv, qkv, chunks(beta_f), chunks(g_f), jnp.swapaxes(chunks(g_f), -1, -2),
      qkv, qkv, qkv, chunks(beta_b), chunks(g_b), jnp.swapaxes(chunks(g_b), -1, -2))


TOK_TM = 256
TOK_HALO = 8


def _block_ones(width, seg):
    r = lax.broadcasted_iota(jnp.int32, (width, width), 0)
    c = lax.broadcasted_iota(jnp.int32, (width, width), 1)
    return jnp.where((r // seg) == (c // seg), 1.0, 0.0).astype(BF16)


def _seg_sum(x, ones_bd):
    hi = x.astype(BF16)
    lo = (x - hi.astype(F32)).astype(BF16)
    return jnp.dot(hi, ones_bd, preferred_element_type=F32) + jnp.dot(lo, ones_bd, preferred_element_type=F32)


def _softplus(x):
    return jnp.maximum(x, 0.0) + jnp.log(1.0 + jnp.exp(-jnp.abs(x)))


def _seq_edges(i, n_tiles, ctx_tiles):
    return (i == 0) | (i == ctx_tiles), (i == ctx_tiles - 1) | (i == n_tiles - 1)


def _rwkv_prep_kernel(pp_ref, p_ref, pn_ref, mup_ref, mun_ref, w0_ref, a0_ref, kkw_ref, kaw_ref,
                      w2f_ref, w2b_ref, a2f_ref, a2b_ref, g2_ref,
                      r_ref, v_ref, kk_ref, g_ref, lwf_ref, af_ref, kf_ref, lwb_ref, ab_ref, kb_ref,
                      *, n_tiles, ctx_tiles):
    i = pl.program_id(1)
    tm, halo = TOK_TM, TOK_HALO
    rows = tm + 2 * halo
    pe = jnp.concatenate([pp_ref[0], p_ref[0], pn_ref[0]], axis=0)
    ridx = lax.broadcasted_iota(jnp.int32, (rows, 1), 0)
    seq_first, seq_last = _seq_edges(i, n_tiles, ctx_tiles)
    prev = pltpu.roll(pe, 1, 0) * jnp.where((ridx == halo) & seq_first, 0.0, 1.0)
    nxt = pltpu.roll(pe, rows - 1, 0) * jnp.where((ridx == halo + tm - 1) & seq_last, 0.0, 1.0)
    xs = (pe + mup_ref[...] * (prev - pe) + mun_ref[...] * (nxt - pe))[halo:halo + tm]
    offs = np.cumsum((0,) + C_SIZES)
    r, k, v, wl_f, wl_b, al_f, al_b, gl = (xs[:, offs[n]:offs[n + 1]] for n in range(len(C_SIZES)))

    def lora(x, w_ref):
        return jnp.dot(x.astype(BF16), w_ref[...], preferred_element_type=F32)

    ones = _block_ones(C_DIM, C_HEAD)
    kq = k * kkw_ref[...]
    r_ref[0] = r
    v_ref[0] = v
    kk_ref[0] = kq * lax.rsqrt(_seg_sum(kq * kq, ones) + 1e-6)
    g_ref[0] = lora(jax.nn.sigmoid(gl), g2_ref)
    for d, (wl, al, w2_ref, a2_ref, lw_ref, a_ref, kd_ref) in enumerate((
            (wl_f, al_f, w2f_ref, a2f_ref, lwf_ref, af_ref, kf_ref),
            (wl_b, al_b, w2b_ref, a2b_ref, lwb_ref, ab_ref, kb_ref))):
        w = -_softplus(-(w0_ref[d:d + 1] + lora(jnp.tanh(wl), w2_ref))) - 0.5
        lw_ref[0] = -jnp.exp(w)
        a = jax.nn.sigmoid(a0_ref[d:d + 1] + lora(al, a2_ref))
        a_ref[0] = a
        kd_ref[0] = k * (1.0 + (a - 1.0) * kaw_ref[...])


def rwkv_prep(p, n_ctx, c_mu_prev, c_mu_next, c_w0, c_w2, c_a0, c_a2, c_g2, c_k_k, c_k_a, interpret=False):
    b, t, _ = p.shape
    nt = t // TOK_TM
    hb = TOK_TM // TOK_HALO
    assert IN_C % LANE == 0

    def full(x):
        x2 = x.reshape(-1, x.shape[-1])
        return x2, pl.BlockSpec(x2.shape, lambda bi, i: (0, 0))

    consts = [full(x) for x in (c_mu_prev, c_mu_next, c_w0, c_a0, c_k_k, c_k_a,
                                c_w2[0].astype(BF16), c_w2[1].astype(BF16), c_a2[0].astype(BF16),
                                c_a2[1].astype(BF16), c_g2.astype(BF16))]
    tile = pl.BlockSpec((1, TOK_TM, C_DIM), lambda bi, i: (bi, i, 0))
    out = jax.ShapeDtypeStruct((b, t, C_DIM), F32)
    r, v, kk, g, lw_f, a_f, k_f, lw_b, a_b, k_b = pl.pallas_call(
        functools.partial(_rwkv_prep_kernel, n_tiles=nt, ctx_tiles=n_ctx // TOK_TM),
        grid=(b, nt),
        in_specs=[pl.BlockSpec((1, TOK_HALO, IN_C), lambda bi, i: (bi, jnp.maximum(i * hb - 1, 0), 0)),
                  pl.BlockSpec((1, TOK_TM, IN_C), lambda bi, i: (bi, i, 0)),
                  pl.BlockSpec((1, TOK_HALO, IN_C), lambda bi, i: (bi, jnp.minimum((i + 1) * hb, nt * hb - 1), 0))]
        + [s for _, s in consts],
        out_specs=[tile] * 10,
        out_shape=[out] * 10,
        compiler_params=pltpu.CompilerParams(
            dimension_semantics=("arbitrary", "arbitrary"),
            vmem_limit_bytes=V7X_VMEM_LIMIT_BYTES),
        name="rwkv_prep",
        interpret=interpret,
    )(p, p, p, *[x for x, _ in consts])
    return r, v, kk, g, (lw_f, lw_b), (a_f, a_b), (k_f, k_b)


def _cd_out_kernel(yf_ref, yb_ref, r_ref, kf_ref, kb_ref, v_ref, g_ref, of_ref, ob_ref, z_ref, x_ref, gate_ref,
                   crk_ref, lnw_ref, lnb_ref, onorm_ref, wout_ref, o_ref):
    ones_c = _block_ones(C_DIM, C_HEAD)
    ones_d = _block_ones(D_DIM, D_HEAD_DIM)
    y = yf_ref[0] + yb_ref[0]
    mean = _seg_sum(y, ones_c) * (1.0 / C_HEAD)
    dev = y - mean
    var = _seg_sum(dev * dev, ones_c) * (1.0 / C_HEAD)
    yn = dev * lax.rsqrt(var + C_GN_EPS) * lnw_ref[...] + lnb_ref[...]
    bonus = _seg_sum(r_ref[0] * (kf_ref[0] + kb_ref[0]) * crk_ref[...], ones_c) * v_ref[0]
    out_c = (yn + bonus) * g_ref[0]
    o = of_ref[0] + ob_ref[0]
    ms = _seg_sum(o * o, ones_d) * (1.0 / D_HEAD_DIM)
    z = z_ref[0]
    out_d = o * lax.rsqrt(ms + EPS) * onorm_ref[...] * (z * jax.nn.sigmoid(z))
    y_cat = jnp.concatenate([out_c, out_d], axis=1).astype(BF16)
    o_ref[0] = x_ref[0] + gate_ref[0, 0] * jnp.dot(y_cat, wout_ref[...], preferred_element_type=F32)


def cd_out(xa, gate, y_f, y_b, r, k_f, k_b, v, g, o_f, o_b, z, c_r_k, c_ln_w, c_ln_b, d_o_norm, w_out, n_ctx,
           interpret=False):
    b, t, d = xa.shape
    nt = t // TOK_TM
    ctx_tiles = n_ctx // TOK_TM
    tile = pl.BlockSpec((1, TOK_TM, C_DIM), lambda bi, i: (bi, i, 0))
    xtile = pl.BlockSpec((1, TOK_TM, d), lambda bi, i: (bi, i, 0))

    def row(x):
        x2 = x.reshape(1, -1)
        return x2, pl.BlockSpec(x2.shape, lambda bi, i: (0, 0))

    consts = [row(c_r_k), row(c_ln_w), row(c_ln_b), row(jnp.tile(d_o_norm, D_HEADS))]
    return pl.pallas_call(
        _cd_out_kernel,
        grid=(b, nt),
        in_specs=[tile] * 10 + [xtile,
                                pl.BlockSpec((1, 1, 1, d), lambda bi, i: (bi, jnp.where(i >= ctx_tiles, 1, 0), 0, 0))]
        + [s for _, s in consts] + [pl.BlockSpec(w_out.shape, lambda bi, i: (0, 0))],
        out_specs=xtile,
        out_shape=jax.ShapeDtypeStruct((b, t, d), F32),
        compiler_params=pltpu.CompilerParams(
            dimension_semantics=("arbitrary", "arbitrary"),
            vmem_limit_bytes=V7X_VMEM_LIMIT_BYTES),
        name="cd_out",
        interpret=interpret,
    )(y_f, y_b, r, k_f, k_b, v, g, o_f, o_b, z, xa, gate.reshape(b, 2, 1, d), *[x for x, _ in consts],
      w_out.astype(BF16))


def _gdn_qkv_kernel(p_ref, w_ref, o_ref, *, n_ctx):
    j = pl.program_id(1)
    x = p_ref[0]
    t = x.shape[0]
    idx = lax.broadcasted_iota(jnp.int32, (t, 1), 0)
    w = w_ref[...]
    y = x * w[D_CONV // 2:D_CONV // 2 + 1]
    for tap in range(D_CONV):
        off = tap - D_CONV // 2
        if off == 0:
            continue
        src = idx + off
        same_seq = (src >= 0) & (src < t) & ((src >= n_ctx) == (idx >= n_ctx))
        y = y + jnp.where(same_seq, pltpu.roll(x, (-off) % t, 0), 0.0) * w[tap:tap + 1]
    y = y * jax.nn.sigmoid(y)
    inv = lax.rsqrt(jnp.sum(y * y, axis=-1, keepdims=True) + 1e-6)
    factor = jnp.where(j < D_HEADS, inv * D_HEAD_DIM ** -0.5, jnp.where(j < 2 * D_HEADS, inv, 1.0))
    o_ref[0] = y * factor


def gdn_qkv(p, d_conv_w, n_ctx, interpret=False):
    b, t, _ = p.shape
    col0 = IN_C // D_HEAD_DIM
    assert IN_C % D_HEAD_DIM == 0
    n_blk = 3 * D_DIM // D_HEAD_DIM
    return pl.pallas_call(
        functools.partial(_gdn_qkv_kernel, n_ctx=n_ctx),
        grid=(b, n_blk),
        in_specs=[pl.BlockSpec((1, t, D_HEAD_DIM), lambda i, j: (i, 0, col0 + j)),
                  pl.BlockSpec((D_CONV, D_HEAD_DIM), lambda i, j: (0, j))],
        out_specs=pl.BlockSpec((1, t, D_HEAD_DIM), lambda i, j: (i, 0, j)),
        out_shape=jax.ShapeDtypeStruct((b, t, 3 * D_DIM), F32),
        compiler_params=pltpu.CompilerParams(
            dimension_semantics=("arbitrary", "arbitrary"),
            vmem_limit_bytes=V7X_VMEM_LIMIT_BYTES),
        name="gdn_qkv",
        interpret=interpret,
    )(p, d_conv_w)


def gdn_prep(p, n_ctx, d_conv_w, d_A_log, d_dt_bias):
    qkv = gdn_qkv(p, d_conv_w, n_ctx)
    offs = IN_C + np.cumsum((0,) + D_SIZES)
    z, bf, bb, af, ab = (p[..., offs[n]:offs[n + 1]] for n in range(1, len(D_SIZES)))
    betas = (jax.nn.sigmoid(bf), jax.nn.sigmoid(bb))
    gs = tuple(-jnp.exp(d_A_log[i]) * jax.nn.softplus(al + d_dt_bias[i]) for i, al in enumerate((af, ab)))
    return qkv, z, betas, gs


def mixer_cd(xa, gate, p, n_ctx, c_mu_prev, c_mu_next, c_w0, c_w2, c_a0, c_a2, c_g2, c_k_k, c_k_a, c_r_k,
             c_ln_w, c_ln_b, d_conv_w, d_A_log, d_dt_bias, d_o_norm, w_out):
    r, v, kk, g, lw, a, ks = rwkv_prep(p, n_ctx, c_mu_prev, c_mu_next, c_w0, c_w2, c_a0, c_a2, c_g2, c_k_k, c_k_a)
    y_f, y_b = rwkv_chunked(r, v, kk, lw[0], a[0], ks[0], lw[1], a[1], ks[1], n_ctx)
    qkv, z, beta, gd = gdn_prep(p, n_ctx, d_conv_w, d_A_log, d_dt_bias)
    o_f, o_b = gdn_chunked(qkv, beta[0], gd[0], beta[1], gd[1], n_ctx)
    return cd_out(xa, gate, y_f, y_b, r, ks[0], ks[1], v, g, o_f, o_b, z, c_r_k, c_ln_w, c_ln_b, d_o_norm, w_out,
                  n_ctx)


FFN_TM = 256
FFN_HALO = 8
FFN_FC = 256
FFN_VMEM_LIMIT_BYTES = 56 * 1024 * 1024


def _modulated(x, shift, scale):
    y = x * lax.rsqrt(jnp.mean(x * x, axis=-1, keepdims=True) + EPS)
    return y * (1.0 + scale) + shift


def _ffn_kernel(xp_ref, x_ref, xn_ref, shift_ref, scale_ref, gate_ref, wup_ref, cw_ref, cb_ref, wdn_ref, o_ref, acc_ref,
                *, n_tiles, ctx_tiles):
    i = pl.program_id(1)
    tm, halo = FFN_TM, FFN_HALO
    rows = tm + 2 * halo
    x = x_ref[0]
    xe = jnp.concatenate([xp_ref[0], x, xn_ref[0]], axis=0)
    h = _modulated(xe, shift_ref[0, 0], scale_ref[0, 0]).astype(BF16)
    r = lax.broadcasted_iota(jnp.int32, (rows, 1), 0)
    seq_first = (i == 0) | (i == ctx_tiles)
    seq_last = (i == ctx_tiles - 1) | (i == n_tiles - 1)
    keep_prev = jnp.where((r == halo) & seq_first, 0.0, 1.0)
    keep_next = jnp.where((r == halo + tm - 1) & seq_last, 0.0, 1.0)
    acc_ref[...] = jnp.zeros_like(acc_ref)

    def conv(u, c0):
        w = cw_ref[:, c0:c0 + FFN_FC]
        um = pltpu.roll(u, 1, 0) * keep_prev
        up = pltpu.roll(u, rows - 1, 0) * keep_next
        y = um * w[0:1] + u * w[1:2] + up * w[2:3] + cb_ref[:, c0:c0 + FFN_FC]
        return y[halo:halo + tm]

    for c in range(D_FF // FFN_FC):
        cv, cg = c * FFN_FC, D_FF + c * FFN_FC
        val = conv(jnp.dot(h, wup_ref[:, cv:cv + FFN_FC], preferred_element_type=F32), cv)
        gat = conv(jnp.dot(h, wup_ref[:, cg:cg + FFN_FC], preferred_element_type=F32), cg)
        act = (gat * jax.nn.sigmoid(gat) * val).astype(BF16)
        acc_ref[...] += jnp.dot(act, wdn_ref[cv:cv + FFN_FC, :], preferred_element_type=F32)
    o_ref[0] = x + gate_ref[0, 0] * acc_ref[...]


def ffn_fused(x, shift, scale, gate, w_up, conv_w, conv_b, w_down, n_ctx, interpret=False):
    b, t, d = x.shape
    nt = t // FFN_TM
    ctx_tiles = n_ctx // FFN_TM
    hb = FFN_TM // FFN_HALO
    f2 = w_up.shape[1]
    mod = pl.BlockSpec((1, 1, 1, d), lambda bi, i: (bi, jnp.where(i >= ctx_tiles, 1, 0), 0, 0))

    def full(shp):
        return pl.BlockSpec(shp, lambda bi, i: (0,) * len(shp))

    def mods(m):
        return m.reshape(b, 2, 1, d)

    return pl.pallas_call(
        functools.partial(_ffn_kernel, n_tiles=nt, ctx_tiles=ctx_tiles),
        grid=(b, nt),
        in_specs=[pl.BlockSpec((1, FFN_HALO, d), lambda bi, i: (bi, jnp.maximum(i * hb - 1, 0), 0)),
                  pl.BlockSpec((1, FFN_TM, d), lambda bi, i: (bi, i, 0)),
                  pl.BlockSpec((1, FFN_HALO, d), lambda bi, i: (bi, jnp.minimum((i + 1) * hb, nt * hb - 1), 0)),
                  mod, mod, mod,
                  full((d, f2)), full((FFN_CONV, f2)), full((1, f2)), full((f2 // 2, d))],
        out_specs=pl.BlockSpec((1, FFN_TM, d), lambda bi, i: (bi, i, 0)),
        out_shape=jax.ShapeDtypeStruct((b, t, d), F32),
        scratch_shapes=[pltpu.VMEM((FFN_TM, d), F32)],
        compiler_params=pltpu.CompilerParams(
            dimension_semantics=("arbitrary", "arbitrary"),
            vmem_limit_bytes=FFN_VMEM_LIMIT_BYTES),
        name="conv_ffn",
        interpret=interpret,
    )(x, x, x, mods(shift), mods(scale), mods(gate), w_up.astype(BF16), conv_w, conv_b.reshape(1, f2),
      w_down.astype(BF16))


def kernel(x, c, ctx, c_ctx, ada_w, ada_b, ffn_w_up, ffn_conv_w, ffn_conv_b, ffn_w_down,
           ab_w_in, ab_w_out, a_q_norm, a_k_norm, a_sink, b_cq_norm, b_ckv_norm, b_w_uq, b_w_uk, b_w_uv,
           b_qn_norm, b_qr_norm, b_kn_norm, b_kr_norm, cd_w_in, cd_w_out, c_mu_prev, c_mu_next, c_w0, c_w2,
           c_a0, c_a2, c_g2, c_k_k, c_k_a, c_r_k, c_ln_w, c_ln_b, d_conv_w, d_A_log, d_dt_bias, d_o_norm):
    bsz, seq = x.shape[:2]
    n_ctx = ctx.shape[1]
    rows = seq // GRID_W
    zeros = jnp.zeros((n_ctx,), jnp.int32)
    row = jnp.concatenate([zeros, jnp.repeat(jnp.arange(rows, dtype=jnp.int32), GRID_W)])
    col = jnp.concatenate([zeros, jnp.tile(jnp.arange(GRID_W, dtype=jnp.int32), rows)])
    is_ctx = (jnp.arange(n_ctx + seq) < n_ctx)[None, :, None]
    silu_c = jax.nn.silu(c)
    silu_cc = jax.nn.silu(c_ctx)
    xa = jnp.concatenate([ctx, x], axis=1)
    for l in range(DEPTH):
        last = l == DEPTH - 1
        i = l // 2
        mod_l = jnp.split(silu_c @ ada_w[l] + ada_b[l], N_MOD, axis=-1)
        mod_c = jnp.split(silu_cc @ ada_w[l] + ada_b[l], N_MOD, axis=-1)
        mods = [jnp.stack([jnp.broadcast_to(mc, ml.shape), ml], axis=1) for mc, ml in zip(mod_c, mod_l)]

        def per_token(m):
            return jnp.where(is_ctx, m[:, 0:1, :], m[:, 1:2, :])

        h = rms_norm(xa) * (1.0 + per_token(mods[1])) + per_token(mods[0])
        if l % 2 == 0:
            y = mixer_ab(_mm(h, ab_w_in[i], keep_pad=True), n_ctx, row, col, a_q_norm[i], a_k_norm[i],
                         a_sink[i], b_cq_norm[i], b_ckv_norm[i], b_w_uq[i], b_w_uk[i], b_w_uv[i],
                         b_qn_norm[i], b_qr_norm[i], b_kn_norm[i], b_kr_norm[i], not last)
            xa = xa + per_token(mods[2]) * _mm(y, ab_w_out[i])
        else:
            xa = mixer_cd(xa, mods[2], _mm(h, cd_w_in[i], keep_pad=True), n_ctx, c_mu_prev[i], c_mu_next[i], c_w0[i],
                          c_w2[i], c_a0[i], c_a2[i], c_g2[i], c_k_k[i], c_k_a[i], c_r_k[i], c_ln_w[i],
                          c_ln_b[i], d_conv_w[i], d_A_log[i], d_dt_bias[i], d_o_norm[i], cd_w_out[i])
        xa = ffn_fused(xa, mods[3], mods[4], mods[5], ffn_w_up[l], ffn_conv_w[l], ffn_conv_b[l], ffn_w_down[l],
                       n_ctx)
    return xa[:, n_ctx:]
```

```python
import functools

import jax
import jax.numpy as jnp
from jax import lax
import numpy as np
from jax.experimental import pallas as pl
from jax.experimental.pallas import tpu as pltpu

D_MODEL = 1024
DEPTH = 2
GRID_W = 64
N_MOD = 6
EPS = 1e-6
ROPE_THETA = 10000.0
NEG_INF = -1e30

A_HEADS = 8
A_KV_HEADS = 2
A_HEAD_DIM = 64
WINDOW = 128
B_HEADS = 8
B_Q_RANK = 256
B_KV_RANK = 256
B_NOPE = 64
B_ROPE = 32
B_V_DIM = 64
C_HEADS = 8
C_HEAD = 64
C_DIM = C_HEADS * C_HEAD
C_DECAY_LORA = 64
C_AAA_LORA = 64
C_GATE_LORA = 128
C_GN_EPS = 64e-5
D_HEADS = 4
D_HEAD_DIM = 128
D_DIM = D_HEADS * D_HEAD_DIM
D_CONV = 5
D_CHUNK = 64
D_FF = 2816
FFN_CONV = 3

AB_SIZES = (A_HEADS * A_HEAD_DIM, A_KV_HEADS * A_HEAD_DIM, A_KV_HEADS * A_HEAD_DIM, B_Q_RANK, B_KV_RANK, B_ROPE)
C_SIZES = (C_DIM, C_DIM, C_DIM, C_DECAY_LORA, C_DECAY_LORA, C_AAA_LORA, C_AAA_LORA, C_GATE_LORA)
IN_C = sum(C_SIZES)
D_SIZES = (3 * D_DIM, D_DIM, D_HEADS, D_HEADS, D_HEADS, D_HEADS)

F32 = jnp.float32
BF16 = jnp.bfloat16

V7X_VMEM_LIMIT_BYTES = 48 * 1024 * 1024
LANE = 128
MXU_N = 256

NN = ((1,), (0,))
NT = ((1,), (1,))
TN = ((0,), (0,))


def _mm_kernel(a_ref, b_ref, o_ref):
    o_ref[...] = jnp.dot(a_ref[...].astype(BF16), b_ref[...], preferred_element_type=F32)


def _pick_tile(n, candidates):
    for c in candidates:
        if n % c == 0:
            return c
    raise ValueError(f"no tile for {n}")


def _mm(a, w, keep_pad=False):
    lead = a.shape[:-1]
    k = a.shape[-1]
    n = w.shape[-1]
    a2 = a.reshape(-1, k)
    m = a2.shape[0]
    n_pad = -(-n // MXU_N) * MXU_N
    wb = w.astype(BF16)
    if n_pad != n:
        wb = jnp.pad(wb, ((0, 0), (0, n_pad - n)))
    tm = _pick_tile(m, (1024, 512, 256, 128, 8))
    tn = _pick_tile(n_pad, (1024, 768, 512, 256))
    out = pl.pallas_call(
        _mm_kernel,
        grid=(m // tm, n_pad // tn),
        in_specs=[pl.BlockSpec((tm, k), lambda i, j: (i, 0)),
                  pl.BlockSpec((k, tn), lambda i, j: (0, j))],
        out_specs=pl.BlockSpec((tm, tn), lambda i, j: (i, j)),
        out_shape=jax.ShapeDtypeStruct((m, n_pad), F32),
        compiler_params=pltpu.CompilerParams(
            dimension_semantics=("arbitrary", "arbitrary"),
            vmem_limit_bytes=V7X_VMEM_LIMIT_BYTES),
        name="mm",
    )(a2, wb)
    if n_pad != n and not keep_pad:
        out = out[:, :n]
    return out.reshape(*lead, out.shape[-1])


def _dot(a, b, dims, passes=1):
    def dg(x, y):
        return lax.dot_general(x, y, (dims, ((), ())), preferred_element_type=F32)

    ah, bh = a.astype(BF16), b.astype(BF16)
    if passes == 1:
        return dg(ah, bh)
    al = (a - ah.astype(F32)).astype(BF16)
    bl = (b - bh.astype(F32)).astype(BF16)
    return dg(ah, bh) + (dg(ah, bl) + dg(al, bh))


def _dot_exact(a, b, dims):
    return lax.dot_general(a, b, (dims, ((), ())), precision=lax.Precision.HIGHEST, preferred_element_type=F32)


def rms_norm(x, eps=EPS):
    return x * lax.rsqrt(jnp.mean(x * x, axis=-1, keepdims=True) + eps)


def _heads_major(z):
    return jnp.swapaxes(z, 1, 2)


def _softmax_pv(s, v, sink):
    m = jnp.max(s, axis=-1, keepdims=True)
    if sink is not None:
        m = jnp.maximum(m, sink)
    p = jnp.exp(s - m)
    den = jnp.sum(p, axis=-1, keepdims=True)
    if sink is not None:
        den = den + jnp.exp(sink - m)
    return _dot(p, v, NN) / den


def _attn_full_kernel(q_ref, k_ref, q2_ref, k2_ref, v_ref, sink_ref, o_ref, *, scale, use_sink, use_second):
    s = _dot(q_ref[0, 0] * scale, k_ref[0, 0], NT)
    if use_second:
        s = s + _dot(q2_ref[0, 0] * scale, k2_ref[0, 0], NT)
    sink = sink_ref[0] if use_sink else None
    o_ref[0, 0] = _softmax_pv(s, v_ref[0, 0], sink)


def attn_full(q, k, v, scale, q_start, n_q, n_keys, sink=None, second=None):
    b, h, _, d = q.shape
    hk = k.shape[1]
    dv = v.shape[-1]
    g = h // hk
    tq = min(n_q, 256)
    q0 = q_start // tq
    use_sink = sink is not None
    use_second = second is not None
    sink_arr = (sink if use_sink else jnp.zeros((h,), F32)).astype(F32).reshape(h, 1, 1)
    q2, k2 = second if use_second else (q, k)
    d2 = q2.shape[-1]
    k2_spec = (pl.BlockSpec((1, 1, n_keys, d2), lambda i, j, t: (i, 0, 0, 0)) if use_second else
               pl.BlockSpec((1, 1, n_keys, d2), lambda i, j, t: (i, j // g, 0, 0)))
    return pl.pallas_call(
        functools.partial(_attn_full_kernel, scale=scale, use_sink=use_sink, use_second=use_second),
        grid=(b, h, n_q // tq),
        in_specs=[pl.BlockSpec((1, 1, tq, d), lambda i, j, t: (i, j, t + q0, 0)),
                  pl.BlockSpec((1, 1, n_keys, d), lambda i, j, t: (i, j // g, 0, 0)),
                  pl.BlockSpec((1, 1, tq, d2), lambda i, j, t: (i, j, t + q0, 0)),
                  k2_spec,
                  pl.BlockSpec((1, 1, n_keys, dv), lambda i, j, t: (i, j // g, 0, 0)),
                  pl.BlockSpec((1, 1, 1), lambda i, j, t: (j, 0, 0))],
        out_specs=pl.BlockSpec((1, 1, tq, dv), lambda i, j, t: (i, j, t, 0)),
        out_shape=jax.ShapeDtypeStruct((b, h, n_q, dv), F32),
        compiler_params=pltpu.CompilerParams(
            dimension_semantics=("arbitrary", "arbitrary", "arbitrary"),
            vmem_limit_bytes=V7X_VMEM_LIMIT_BYTES),
        name="attn_full",
    )(q, k, q2, k2, v, sink_arr)


def _attn_window_kernel(q_ref, kp_ref, k0_ref, kn_ref, kc_ref, vp_ref, v0_ref, vn_ref, vc_ref, sink_ref, o_ref,
                        *, scale, group, n_blocks):
    n = pl.program_id(2)
    w = WINDOW
    d = q_ref.shape[-1]
    q = q_ref[0].reshape(group * w, d) * scale
    keys = jnp.concatenate([kp_ref[0, 0], k0_ref[0, 0], kn_ref[0, 0], kc_ref[0, 0]], axis=0)
    vals = jnp.concatenate([vp_ref[0, 0], v0_ref[0, 0], vn_ref[0, 0], vc_ref[0, 0]], axis=0)
    s = _dot(q, keys, NT)
    nk = keys.shape[0]
    qi = lax.broadcasted_iota(jnp.int32, (group * w, nk), 0) % w
    kj = lax.broadcasted_iota(jnp.int32, (group * w, nk), 1)
    rel = qi + w - kj
    band_ok = (jnp.abs(rel) <= w) & ((kj >= w) | (n > 0)) & ((kj < 2 * w) | (n < n_blocks - 1))
    s = jnp.where((kj >= 3 * w) | band_ok, s, NEG_INF)
    sink = jnp.concatenate([jnp.broadcast_to(sink_ref[0, hh], (w, 1)) for hh in range(group)], axis=0)
    o = _softmax_pv(s, vals, sink)
    o_ref[0] = o.reshape(group, w, o.shape[-1])


def attn_window(q, k, v, sink, n_ctx):
    b, h, t, d = q.shape
    hk = k.shape[1]
    g = h // hk
    off = n_ctx // WINDOW
    nb = (t - n_ctx) // WINDOW
    sink_arr = sink.astype(F32).reshape(hk, g, 1, 1)

    def blk(f):
        return pl.BlockSpec((1, 1, WINDOW, d), f)

    prev = blk(lambda i, j, n: (i, j, jnp.maximum(n - 1, 0) + off, 0))
    own = blk(lambda i, j, n: (i, j, n + off, 0))
    nxt = blk(lambda i, j, n: (i, j, jnp.minimum(n + 1, nb - 1) + off, 0))
    ctx = pl.BlockSpec((1, 1, n_ctx, d), lambda i, j, n: (i, j, 0, 0))
    return pl.pallas_call(
        functools.partial(_attn_window_kernel, scale=d ** -0.5, group=g, n_blocks=nb),
        grid=(b, hk, nb),
        in_specs=[pl.BlockSpec((1, g, WINDOW, d), lambda i, j, n: (i, j, n + off, 0)),
                  prev, own, nxt, ctx, prev, own, nxt, ctx,
                  pl.BlockSpec((1, g, 1, 1), lambda i, j, n: (j, 0, 0, 0))],
        out_specs=pl.BlockSpec((1, g, WINDOW, d), lambda i, j, n: (i, j, n, 0)),
        out_shape=jax.ShapeDtypeStruct((b, h, t - n_ctx, d), F32),
        compiler_params=pltpu.CompilerParams(
            dimension_semantics=("arbitrary", "arbitrary", "arbitrary"),
            vmem_limit_bytes=V7X_VMEM_LIMIT_BYTES),
        name="attn_window",
    )(q, k, k, k, k, v, v, v, v, sink_arr)


def _rope_tables(row, col, head_dim, width):
    q = head_dim // 4
    inv = jnp.power(ROPE_THETA, -jnp.arange(q, dtype=F32) / q)
    ang_r = row.astype(F32)[:, None] * inv[None, :]
    ang_c = col.astype(F32)[:, None] * inv[None, :]
    cos = jnp.concatenate([jnp.cos(ang_r), jnp.cos(ang_r), jnp.cos(ang_c), jnp.cos(ang_c)], axis=-1)
    sin = jnp.concatenate([-jnp.sin(ang_r), jnp.sin(ang_r), -jnp.sin(ang_c), jnp.sin(ang_c)], axis=-1)
    reps = width // head_dim
    return jnp.tile(cos, (1, reps)), jnp.tile(sin, (1, reps))


def _rope(x, cos, sin, head_dim):
    q = head_dim // 4
    w = x.shape[-1]
    lane = lax.broadcasted_iota(jnp.int32, x.shape, 1)
    swapped = jnp.where(lane % (2 * q) < q, pltpu.roll(x, w - q, 1), pltpu.roll(x, q, 1))
    return x * cos + swapped * sin


def _head_rms(x, ones_bd, head_dim, gain):
    return x * lax.rsqrt(_seg_sum(x * x, ones_bd) * (1.0 / head_dim) + EPS) * gain


def _row_rms(x, n, gain):
    return x * lax.rsqrt(jnp.sum(x * x, axis=-1, keepdims=True) * (1.0 / n) + EPS) * gain


def _ab_prep_kernel(p_ref, c64_ref, s64_ref, c32_ref, s32_ref, gqa_ref, gka_ref, gcq_ref, gckv_ref, gqn_ref, gqr_ref,
                    gkn_ref, gkr_ref, wuq_ref, wuk_ref, wuv_ref,
                    qa_ref, ka_ref, qn_ref, qr_ref, kn_ref, vb_ref, kr_ref):
    p = p_ref[0]
    offs = np.cumsum((0,) + AB_SIZES)
    qa, ka, cq, ckv = (p[:, offs[n]:offs[n + 1]] for n in (0, 1, 3, 4))
    kr = p[:, offs[5]:offs[5] + LANE]
    c64, s64, c32, s32 = c64_ref[...], s64_ref[...], c32_ref[...], s32_ref[...]
    ones_q = _block_ones(A_HEADS * A_HEAD_DIM, A_HEAD_DIM)
    ones_k = _block_ones(A_KV_HEADS * A_HEAD_DIM, A_HEAD_DIM)
    ones_r = _block_ones(B_HEADS * B_ROPE, B_ROPE)
    ka_w = A_KV_HEADS * A_HEAD_DIM
    qa_ref[0] = _rope(_head_rms(qa, ones_q, A_HEAD_DIM, gqa_ref[...]), c64, s64, A_HEAD_DIM)
    ka_ref[0] = _rope(_head_rms(ka, ones_k, A_HEAD_DIM, gka_ref[...]), c64[:, :ka_w], s64[:, :ka_w], A_HEAD_DIM)
    qb = jnp.dot(_row_rms(cq, B_Q_RANK, gcq_ref[...]).astype(BF16), wuq_ref[...], preferred_element_type=F32)
    n_nope = B_HEADS * B_NOPE
    qn_ref[0] = _head_rms(qb[:, :n_nope], ones_q, B_NOPE, gqn_ref[...])
    qr_ref[0] = _rope(_head_rms(qb[:, n_nope:], ones_r, B_ROPE, gqr_ref[...]), c32, s32, B_ROPE)
    ckv_n = _row_rms(ckv, B_KV_RANK, gckv_ref[...]).astype(BF16)
    kn_ref[0] = _head_rms(jnp.dot(ckv_n, wuk_ref[...], preferred_element_type=F32), ones_q, B_NOPE, gkn_ref[...])
    vb_ref[0] = jnp.dot(ckv_n, wuv_ref[...], preferred_element_type=F32)
    kr_ref[0] = _rope(_row_rms(kr, B_ROPE, gkr_ref[...]), c32[:, :LANE], s32[:, :LANE], B_ROPE)


def ab_prep(p, row, col, a_q_norm, a_k_norm, b_cq_norm, b_ckv_norm, b_w_uq, b_w_uk, b_w_uv,
            b_qn_norm, b_qr_norm, b_kn_norm, b_kr_norm, interpret=False):
    b, t, pw = p.shape
    assert pw >= sum(AB_SIZES[:5]) + LANE and A_HEAD_DIM == B_NOPE
    c64, s64 = _rope_tables(row, col, A_HEAD_DIM, A_HEADS * A_HEAD_DIM)
    c32, s32 = _rope_tables(row, col, B_ROPE, B_HEADS * B_ROPE)
    wq = b_w_uq.reshape(B_Q_RANK, B_HEADS, B_NOPE + B_ROPE)
    wq = jnp.concatenate([wq[..., :B_NOPE].reshape(B_Q_RANK, -1), wq[..., B_NOPE:].reshape(B_Q_RANK, -1)], axis=1)

    def row_c(g, reps, width=None):
        x = jnp.tile(g, reps).reshape(1, -1)
        if width is not None:
            x = jnp.pad(x, ((0, 0), (0, width - x.shape[1])))
        return x

    consts = [row_c(a_q_norm, A_HEADS), row_c(a_k_norm, A_KV_HEADS), row_c(b_cq_norm, 1), row_c(b_ckv_norm, 1),
              row_c(b_qn_norm, B_HEADS), row_c(b_qr_norm, B_HEADS), row_c(b_kn_norm, B_HEADS),
              row_c(b_kr_norm, 1, LANE), wq.astype(BF16), b_w_uk.astype(BF16), b_w_uv.astype(BF16)]
    tabs = [c64, s64, c32, s32]
    widths = (A_HEADS * A_HEAD_DIM, A_KV_HEADS * A_HEAD_DIM, B_HEADS * B_NOPE, B_HEADS * B_ROPE, B_HEADS * B_NOPE,
              B_HEADS * B_V_DIM, LANE)
    return pl.pallas_call(
        _ab_prep_kernel,
        grid=(b, t // TOK_TM),
        in_specs=[pl.BlockSpec((1, TOK_TM, pw), lambda bi, i: (bi, i, 0))]
        + [pl.BlockSpec((TOK_TM, x.shape[1]), lambda bi, i: (i, 0)) for x in tabs]
        + [pl.BlockSpec(x.shape, lambda bi, i: (0, 0)) for x in consts],
        out_specs=[pl.BlockSpec((1, TOK_TM, w), lambda bi, i: (bi, i, 0)) for w in widths],
        out_shape=[jax.ShapeDtypeStruct((b, t, w), F32) for w in widths],
        compiler_params=pltpu.CompilerParams(
            dimension_semantics=("arbitrary", "arbitrary"),
            vmem_limit_bytes=V7X_VMEM_LIMIT_BYTES),
        name="ab_prep",
        interpret=interpret,
    )(p, *tabs, *consts)


def mixer_ab(p, n_ctx, row, col, a_q_norm, a_k_norm, a_sink, b_cq_norm, b_ckv_norm, b_w_uq, b_w_uk,
             b_w_uv, b_qn_norm, b_qr_norm, b_kn_norm, b_kr_norm, ctx_out):
    b, t = p.shape[:2]
    qa, ka, qn, qr, kn, vb, kr = ab_prep(p, row, col, a_q_norm, a_k_norm, b_cq_norm, b_ckv_norm, b_w_uq, b_w_uk,
                                         b_w_uv, b_qn_norm, b_qr_norm, b_kn_norm, b_kr_norm)
    va_off = AB_SIZES[0] + AB_SIZES[1]
    va = p[..., va_off:va_off + AB_SIZES[2]]

    def hm(z, heads):
        return _heads_major(z.reshape(b, t, heads, -1))

    qa_t, ka_t, va_t = hm(qa, A_HEADS), hm(ka, A_KV_HEADS), hm(va, A_KV_HEADS)
    qn_t, kn_t, vb_t = hm(qn, B_HEADS), hm(kn, B_HEADS), hm(vb, B_HEADS)
    rope_part = (hm(qr, B_HEADS), hm(kr[..., :B_ROPE], 1))
    b_scale = (B_NOPE + B_ROPE) ** -0.5

    o_a = attn_window(qa_t, ka_t, va_t, a_sink, n_ctx)
    o_b = attn_full(qn_t, kn_t, vb_t, b_scale, n_ctx, t - n_ctx, t, second=rope_part)
    if ctx_out:
        o_a_c = attn_full(qa_t, ka_t, va_t, A_HEAD_DIM ** -0.5, 0, n_ctx, n_ctx, sink=a_sink)
        o_b_c = attn_full(qn_t, kn_t, vb_t, b_scale, 0, n_ctx, n_ctx, second=rope_part)
    else:
        o_a_c = jnp.zeros((b, A_HEADS, n_ctx, A_HEAD_DIM), F32)
        o_b_c = jnp.zeros((b, B_HEADS, n_ctx, B_V_DIM), F32)
    o_a = _heads_major(jnp.concatenate([o_a_c, o_a], axis=2)).reshape(b, t, -1)
    o_b = _heads_major(jnp.concatenate([o_b_c, o_b], axis=2)).reshape(b, t, -1)
    return jnp.concatenate([o_a, o_b], axis=-1)


RW_CHUNK = 64
HEAD_PAIR = 2 * C_HEAD


def _unit_lower_inverse(xs, eye):
    ps = [eye + x for x in xs]
    xps = list(xs)
    for _ in range(int(np.log2(RW_CHUNK)) - 1):
        xps = [_dot(xp, xp, NN, passes=3) for xp in xps]
        ps = [p + _dot(p, xp, NN, passes=3) for p, xp in zip(ps, xps)]
    return ps


def _chunk_maps(nc, nctx):
    def fwd(s):
        return s

    def bwd(s):
        return jnp.where(s < nctx, nctx - 1 - s, nc + nctx - 1 - s)

    return fwd, bwd


def _rwkv_kernel(rf_ref, vf_ref, kkf_ref, lwf_ref, af_ref, kf_ref,
                 rb_ref, vb_ref, kkb_ref, lwb_ref, ab_ref, kb_ref, yf_ref, yb_ref, st_ref):
    @pl.when(pl.program_id(1) == 0)
    def _():
        st_ref[...] = jnp.zeros_like(st_ref)

    c_len = RW_CHUNK
    ii = lax.broadcasted_iota(jnp.int32, (c_len, c_len), 0)
    jj = lax.broadcasted_iota(jnp.int32, (c_len, c_len), 1)
    r2 = lax.broadcasted_iota(jnp.int32, (HEAD_PAIR, HEAD_PAIR), 0)
    c2 = lax.broadcasted_iota(jnp.int32, (HEAD_PAIR, HEAD_PAIR), 1)
    dlt = (r2 % c_len) - (c2 % c_len)
    eye = jnp.where(r2 == c2, 1.0, 0.0).astype(F32)
    lane = lax.broadcasted_iota(jnp.int32, (c_len, HEAD_PAIR), 1)
    m0 = lane < C_HEAD

    def stack_heads(x):
        return jnp.concatenate([jnp.where(m0, x, 0.0), jnp.where(m0, 0.0, x)], axis=0)

    n_pairs = C_DIM // HEAD_PAIR
    ch = []
    for d, refs in enumerate(((rf_ref, vf_ref, kkf_ref, lwf_ref, af_ref, kf_ref),
                              (rb_ref, vb_ref, kkb_ref, lwb_ref, ab_ref, kb_ref))):
        sign = 1 - 2 * d
        r_all, v_all, kk_all, lw_all, a_all, kd_all = (ref[0] for ref in refs)
        tri = jnp.where((ii - jj) * sign >= 0, 1.0, 0.0).astype(F32)
        cum_all = _dot_exact(tri, lw_all, NN)
        tot_all = jnp.sum(lw_all, axis=0, keepdims=True)
        for p in range(n_pairs):
            sl = slice(p * HEAD_PAIR, (p + 1) * HEAD_PAIR)
            ch.append(dict(strict=dlt * sign > 0, incl=dlt * sign >= 0, st=st_ref[d, p],
                           lw=lw_all[:, sl], cum=cum_all[:, sl], tot=tot_all[:, sl], kk=kk_all[:, sl],
                           kd=kd_all[:, sl], bb=kk_all[:, sl] * a_all[:, sl], r=r_all[:, sl], v=v_all[:, sl]))
    for c in ch:
        e_neg = jnp.exp(-c['cum'])
        e_end = jnp.exp(c['tot'] - c['cum'])
        c['abar'] = stack_heads(-c['kk'] * jnp.exp(c['cum'] - c['lw']))
        c['rbar'] = stack_heads(c['r'] * jnp.exp(c['cum']))
        c['ktil'] = stack_heads(c['kd'] * e_neg)
        c['btil'] = stack_heads(c['bb'] * e_neg)
        c['khat'] = stack_heads(c['kd'] * e_end)
        c['bhat'] = stack_heads(c['bb'] * e_end)
        c['vs'] = stack_heads(c['v'])
    g = [_dot(jnp.concatenate([c['abar'], c['rbar']], axis=0), jnp.concatenate([c['ktil'], c['btil']], axis=0), NT)
         for c in ch]
    a_ak = [jnp.where(c['strict'], x[:HEAD_PAIR, :HEAD_PAIR], 0.0) for c, x in zip(ch, g)]
    a_ab = [jnp.where(c['strict'], x[:HEAD_PAIR, HEAD_PAIR:], 0.0) for c, x in zip(ch, g)]
    a_rk = [jnp.where(c['incl'], x[HEAD_PAIR:, :HEAD_PAIR], 0.0) for c, x in zip(ch, g)]
    a_rb = [jnp.where(c['incl'], x[HEAD_PAIR:, HEAD_PAIR:], 0.0) for c, x in zip(ch, g)]
    tinv = _unit_lower_inverse(a_ab, eye)
    n = range(len(ch))
    wm = [_dot(tinv[i], ch[i]['abar'], NN) for i in n]
    akv = [_dot(a_ak[i], ch[i]['vs'], NN) for i in n]
    u0 = [_dot(tinv[i], akv[i], NN) for i in n]
    u = [_dot(wm[i], ch[i]['st'], NT) + u0[i] for i in n]
    ys = [_dot(ch[i]['rbar'], ch[i]['st'], NT) + _dot(a_rk[i], ch[i]['vs'], NN) + _dot(a_rb[i], u[i], NN) for i in n]
    st_new = [ch[i]['st'] * jnp.exp(ch[i]['tot']) + _dot(ch[i]['vs'], ch[i]['khat'], TN) + _dot(u[i], ch[i]['bhat'], TN)
              for i in n]
    for d, y_ref in enumerate((yf_ref, yb_ref)):
        y_ref[0] = jnp.concatenate([ys[d * n_pairs + p][:c_len] + ys[d * n_pairs + p][c_len:] for p in range(n_pairs)],
                                   axis=1)
        for p in range(n_pairs):
            st_ref[d, p] = st_new[d * n_pairs + p]


def rwkv_chunked(r, v, kk, lw_f, a_f, k_f, lw_b, a_b, k_b, n_ctx, interpret=False):
    b, t, cd = r.shape
    nc = t // RW_CHUNK
    fwd, bwd = _chunk_maps(nc, n_ctx // RW_CHUNK)
    sf = pl.BlockSpec((1, RW_CHUNK, cd), lambda i, s: (i, fwd(s), 0))
    sb = pl.BlockSpec((1, RW_CHUNK, cd), lambda i, s: (i, bwd(s), 0))
    out = jax.ShapeDtypeStruct((b, t, cd), F32)
    return pl.pallas_call(
        _rwkv_kernel,
        grid=(b, nc),
        in_specs=[sf] * 6 + [sb] * 6,
        out_specs=[sf, sb],
        out_shape=[out, out],
        scratch_shapes=[pltpu.VMEM((2, cd // HEAD_PAIR, HEAD_PAIR, HEAD_PAIR), F32)],
        compiler_params=pltpu.CompilerParams(
            dimension_semantics=("arbitrary", "arbitrary"),
            vmem_limit_bytes=V7X_VMEM_LIMIT_BYTES),
        name="rwkv7_chunked",
        interpret=interpret,
    )(r, v, kk, lw_f, a_f, k_f, r, v, kk, lw_b, a_b, k_b)


def _gdn_kernel(qf_ref, kf_ref, vf_ref, bf_ref, gf_ref, gtf_ref,
                qb_ref, kb_ref, vb_ref, bb_ref, gb_ref, gtb_ref, of_ref, ob_ref, st_ref):
    @pl.when(pl.program_id(1) == 0)
    def _():
        st_ref[...] = jnp.zeros_like(st_ref)

    c_len = D_CHUNK
    n2 = 2 * c_len
    r2 = lax.broadcasted_iota(jnp.int32, (n2, n2), 0)
    c2 = lax.broadcasted_iota(jnp.int32, (n2, n2), 1)
    same = (r2 // c_len) == (c2 // c_len)
    dlt = (r2 % c_len) - (c2 % c_len)
    ones_bd = jnp.where(same, 1.0, 0.0).astype(F32)
    eye = jnp.where(r2 == c2, 1.0, 0.0).astype(F32)

    def rows(blk, p):
        return jnp.concatenate([blk[:, h * D_HEAD_DIM:(h + 1) * D_HEAD_DIM] for h in (2 * p, 2 * p + 1)], axis=0)

    def col(blk, p):
        c = jnp.concatenate([blk[:, h:h + 1] for h in (2 * p, 2 * p + 1)], axis=0)
        return jnp.broadcast_to(c, (n2, n2))

    n_pairs = D_HEADS // 2
    ch = []
    for d, refs in enumerate(((qf_ref, kf_ref, vf_ref, bf_ref, gf_ref, gtf_ref),
                              (qb_ref, kb_ref, vb_ref, bb_ref, gb_ref, gtb_ref))):
        sign = 1 - 2 * d
        q_blk, k_blk, v_blk = refs[0][0], refs[1][0], refs[2][0]
        b_blk, g_blk, gt_blk = refs[3][0, 0], refs[4][0, 0], refs[5][0, 0]
        incl = same & (dlt * sign >= 0)
        tri = jnp.where(incl, 1.0, 0.0).astype(F32)
        for p in range(n_pairs):
            g_row = jnp.concatenate([gt_blk[h:h + 1, :] for h in (2 * p, 2 * p + 1)], axis=1)
            ch.append(dict(d=d, p=p, strict=same & (dlt * sign > 0), incl=incl, tri=tri,
                           q=rows(q_blk, p), k=rows(k_blk, p), v=rows(v_blk, p), beta=col(b_blk, p),
                           g_colb=col(g_blk, p), g_rowb=jnp.broadcast_to(g_row, (n2, n2)),
                           st=[st_ref[d, 2 * p], st_ref[d, 2 * p + 1]]))
    n = range(len(ch))
    gc_col = [_dot_exact(c['tri'], c['g_colb'], NN) for c in ch]
    gc_row = [_dot_exact(c['g_rowb'], c['tri'], NT) for c in ch]
    g_end = [_dot_exact(ones_bd, c['g_colb'], NN) for c in ch]
    decay = [jnp.where(ch[i]['incl'], jnp.exp(jnp.where(ch[i]['incl'], gc_col[i] - gc_row[i], 0.0)), 0.0) for i in n]
    kb = [c['k'] * c['beta'] for c in ch]
    vb = [c['v'] * c['beta'] for c in ch]
    g_mat = [_dot(jnp.concatenate([kb[i], ch[i]['q']], axis=0), ch[i]['k'], NT) for i in n]
    l_mat = [jnp.where(ch[i]['strict'], g_mat[i][:n2] * decay[i], 0.0) for i in n]
    a_intra = [g_mat[i][n2:] * decay[i] for i in n]
    tinv = _unit_lower_inverse([-l for l in l_mat], eye)
    u = [_dot(tinv[i], vb[i], NN) for i in n]
    wk = [_dot(tinv[i], kb[i] * jnp.exp(gc_col[i]), NN) for i in n]
    q_e = [ch[i]['q'] * jnp.exp(gc_col[i]) for i in n]
    k_e = [ch[i]['k'] * jnp.exp(g_end[i] - gc_col[i]) for i in n]
    halves = (slice(0, c_len), slice(c_len, n2))
    v_new = [[u[i][hs] - _dot(wk[i][hs], ch[i]['st'][j], NN) for j, hs in enumerate(halves)] for i in n]
    o_st = [[_dot(q_e[i][hs], ch[i]['st'][j], NN) for j, hs in enumerate(halves)] for i in n]
    v_new_s = [jnp.concatenate(v_new[i], axis=0) for i in n]
    o_s = [jnp.concatenate(o_st[i], axis=0) + _dot(a_intra[i], v_new_s[i], NN) for i in n]
    st_new = [[ch[i]['st'][j] * jnp.exp(g_end[i][hs][0:1, :]) + _dot(k_e[i][hs], v_new[i][j], TN)
               for j, hs in enumerate(halves)] for i in n]
    for d, o_ref in enumerate((of_ref, ob_ref)):
        o_ref[0] = jnp.concatenate([o_s[d * n_pairs + p][hs] for p in range(n_pairs) for hs in halves], axis=1)
        for p in range(n_pairs):
            for j in range(2):
                st_ref[d, 2 * p + j] = st_new[d * n_pairs + p][j]


def gdn_chunked(qkv, beta_f, g_f, beta_b, g_b, n_ctx, interpret=False):
    b, t, _ = qkv.shape
    cd = D_DIM
    nc = t // D_CHUNK
    fwd, bwd = _chunk_maps(nc, n_ctx // D_CHUNK)

    def chunks(z):
        return z.reshape(b, nc, D_CHUNK, D_HEADS)

    def specs(cmap):
        big = [pl.BlockSpec((1, D_CHUNK, cd), functools.partial(lambda part, i, s: (i, cmap(s), part), part))
               for part in range(3)]
        small = pl.BlockSpec((1, 1, D_CHUNK, D_HEADS), lambda i, s: (i, cmap(s), 0, 0))
        small_t = pl.BlockSpec((1, 1, D_HEADS, D_CHUNK), lambda i, s: (i, cmap(s), 0, 0))
        return big, small, small_t

    bf, sf, stf = specs(fwd)
    bb, sb, stb = specs(bwd)
    out = jax.ShapeDtypeStruct((b, t, cd), F32)
    return pl.pallas_call(
        _gdn_kernel,
        grid=(b, nc),
        in_specs=bf + [sf, sf, stf] + bb + [sb, sb, stb],
        out_specs=[bf[0], bb[0]],
        out_shape=[out, out],
        scratch_shapes=[pltpu.VMEM((2, D_HEADS, D_HEAD_DIM, D_HEAD_DIM), F32)],
        compiler_params=pltpu.CompilerParams(
            dimension_semantics=("arbitrary", "arbitrary"),
            vmem_limit_bytes=V7X_VMEM_LIMIT_BYTES),
        name="gdn_chunked",
        interpret=interpret,
    )(qkv, qkv, qkv, chunks(beta_f), chunks(g_f), jnp.swapaxes(chunks(g_f), -1, -2),
      qkv, qkv, qkv, chunks(beta_b), chunks(g_b), jnp.swapaxes(chunks(g_b), -1, -2))


TOK_TM = 256
TOK_HALO = 8


def _block_ones(width, seg):
    r = lax.broadcasted_iota(jnp.int32, (width, width), 0)
    c = lax.broadcasted_iota(jnp.int32, (width, width), 1)
    return jnp.where((r // seg) == (c // seg), 1.0, 0.0).astype(BF16)


def _seg_sum(x, ones_bd):
    hi = x.astype(BF16)
    lo = (x - hi.astype(F32)).astype(BF16)
    return jnp.dot(hi, ones_bd, preferred_element_type=F32) + jnp.dot(lo, ones_bd, preferred_element_type=F32)


def _softplus(x):
    return jnp.maximum(x, 0.0) + jnp.log(1.0 + jnp.exp(-jnp.abs(x)))


def _seq_edges(i, n_tiles, ctx_tiles):
    return (i == 0) | (i == ctx_tiles), (i == ctx_tiles - 1) | (i == n_tiles - 1)


def _rwkv_prep_kernel(pp_ref, p_ref, pn_ref, mup_ref, mun_ref, w0_ref, a0_ref, kkw_ref, kaw_ref,
                      w2f_ref, w2b_ref, a2f_ref, a2b_ref, g2_ref,
                      r_ref, v_ref, kk_ref, g_ref, lwf_ref, af_ref, kf_ref, lwb_ref, ab_ref, kb_ref,
                      *, n_tiles, ctx_tiles):
    i = pl.program_id(1)
    tm, halo = TOK_TM, TOK_HALO
    rows = tm + 2 * halo
    pe = jnp.concatenate([pp_ref[0], p_ref[0], pn_ref[0]], axis=0)
    ridx = lax.broadcasted_iota(jnp.int32, (rows, 1), 0)
    seq_first, seq_last = _seq_edges(i, n_tiles, ctx_tiles)
    outside = ((ridx == halo - 1) & seq_first) | ((ridx == halo + tm) & seq_last)
    pe = jnp.where(outside, 0.0, pe)
    prev = pltpu.roll(pe, 1, 0)
    nxt = pltpu.roll(pe, rows - 1, 0)
    xs = (pe + mup_ref[...] * (prev - pe) + mun_ref[...] * (nxt - pe))[halo:halo + tm]
    offs = np.cumsum((0,) + C_SIZES)
    r, k, v, wl_f, wl_b, al_f, al_b, gl = (xs[:, offs[n]:offs[n + 1]] for n in range(len(C_SIZES)))

    def lora(x, w_ref):
        return jnp.dot(x.astype(BF16), w_ref[...], preferred_element_type=F32)

    ones = _block_ones(C_DIM, C_HEAD)
    kq = k * kkw_ref[...]
    r_ref[0] = r
    v_ref[0] = v
    kk_ref[0] = kq * lax.rsqrt(_seg_sum(kq * kq, ones) + 1e-6)
    g_ref[0] = lora(jax.nn.sigmoid(gl), g2_ref)
    for d, (wl, al, w2_ref, a2_ref, lw_ref, a_ref, kd_ref) in enumerate((
            (wl_f, al_f, w2f_ref, a2f_ref, lwf_ref, af_ref, kf_ref),
            (wl_b, al_b, w2b_ref, a2b_ref, lwb_ref, ab_ref, kb_ref))):
        w = -_softplus(-(w0_ref[d:d + 1] + lora(jnp.tanh(wl), w2_ref))) - 0.5
        lw_ref[0] = -jnp.exp(w)
        a = jax.nn.sigmoid(a0_ref[d:d + 1] + lora(al, a2_ref))
        a_ref[0] = a
        kd_ref[0] = k * (1.0 + (a - 1.0) * kaw_ref[...])


def rwkv_prep(p, n_ctx, c_mu_prev, c_mu_next, c_w0, c_w2, c_a0, c_a2, c_g2, c_k_k, c_k_a, interpret=False):
    b, t, _ = p.shape
    nt = t // TOK_TM
    hb = TOK_TM // TOK_HALO
    assert IN_C % LANE == 0

    def full(x):
        x2 = x.reshape(-1, x.shape[-1])
        return x2, pl.BlockSpec(x2.shape, lambda bi, i: (0, 0))

    consts = [full(x) for x in (c_mu_prev, c_mu_next, c_w0, c_a0, c_k_k, c_k_a,
                                c_w2[0].astype(BF16), c_w2[1].astype(BF16), c_a2[0].astype(BF16),
                                c_a2[1].astype(BF16), c_g2.astype(BF16))]
    tile = pl.BlockSpec((1, TOK_TM, C_DIM), lambda bi, i: (bi, i, 0))
    out = jax.ShapeDtypeStruct((b, t, C_DIM), F32)
    r, v, kk, g, lw_f, a_f, k_f, lw_b, a_b, k_b = pl.pallas_call(
        functools.partial(_rwkv_prep_kernel, n_tiles=nt, ctx_tiles=n_ctx // TOK_TM),
        grid=(b, nt),
        in_specs=[pl.BlockSpec((1, TOK_HALO, IN_C), lambda bi, i: (bi, jnp.maximum(i * hb - 1, 0), 0)),
                  pl.BlockSpec((1, TOK_TM, IN_C), lambda bi, i: (bi, i, 0)),
                  pl.BlockSpec((1, TOK_HALO, IN_C), lambda bi, i: (bi, jnp.minimum((i + 1) * hb, nt * hb - 1), 0))]
        + [s for _, s in consts],
        out_specs=[tile] * 10,
        out_shape=[out] * 10,
        compiler_params=pltpu.CompilerParams(
            dimension_semantics=("arbitrary", "arbitrary"),
            vmem_limit_bytes=V7X_VMEM_LIMIT_BYTES),
        name="rwkv_prep",
        interpret=interpret,
    )(p, p, p, *[x for x, _ in consts])
    return r, v, kk, g, (lw_f, lw_b), (a_f, a_b), (k_f, k_b)


def _cd_out_kernel(yf_ref, yb_ref, r_ref, kf_ref, kb_ref, v_ref, g_ref, of_ref, ob_ref, z_ref, x_ref, gate_ref,
                   crk_ref, lnw_ref, lnb_ref, onorm_ref, wout_ref, o_ref):
    ones_c = _block_ones(C_DIM, C_HEAD)
    ones_d = _block_ones(D_DIM, D_HEAD_DIM)
    y = yf_ref[0] + yb_ref[0]
    mean = _seg_sum(y, ones_c) * (1.0 / C_HEAD)
    dev = y - mean
    var = _seg_sum(dev * dev, ones_c) * (1.0 / C_HEAD)
    yn = dev * lax.rsqrt(var + C_GN_EPS) * lnw_ref[...] + lnb_ref[...]
    bonus = _seg_sum(r_ref[0] * (kf_ref[0] + kb_ref[0]) * crk_ref[...], ones_c) * v_ref[0]
    out_c = (yn + bonus) * g_ref[0]
    o = of_ref[0] + ob_ref[0]
    ms = _seg_sum(o * o, ones_d) * (1.0 / D_HEAD_DIM)
    z = z_ref[0]
    out_d = o * lax.rsqrt(ms + EPS) * onorm_ref[...] * (z * jax.nn.sigmoid(z))
    y_cat = jnp.concatenate([out_c, out_d], axis=1).astype(BF16)
    o_ref[0] = x_ref[0] + gate_ref[0, 0] * jnp.dot(y_cat, wout_ref[...], preferred_element_type=F32)


def cd_out(xa, gate, y_f, y_b, r, k_f, k_b, v, g, o_f, o_b, z, c_r_k, c_ln_w, c_ln_b, d_o_norm, w_out, n_ctx,
           interpret=False):
    b, t, d = xa.shape
    nt = t // TOK_TM
    ctx_tiles = n_ctx // TOK_TM
    tile = pl.BlockSpec((1, TOK_TM, C_DIM), lambda bi, i: (bi, i, 0))
    xtile = pl.BlockSpec((1, TOK_TM, d), lambda bi, i: (bi, i, 0))

    def row(x):
        x2 = x.reshape(1, -1)
        return x2, pl.BlockSpec(x2.shape, lambda bi, i: (0, 0))

    consts = [row(c_r_k), row(c_ln_w), row(c_ln_b), row(jnp.tile(d_o_norm, D_HEADS))]
    return pl.pallas_call(
        _cd_out_kernel,
        grid=(b, nt),
        in_specs=[tile] * 10 + [xtile,
                                pl.BlockSpec((1, 1, 1, d), lambda bi, i: (bi, jnp.where(i >= ctx_tiles, 1, 0), 0, 0))]
        + [s for _, s in consts] + [pl.BlockSpec(w_out.shape, lambda bi, i: (0, 0))],
        out_specs=xtile,
        out_shape=jax.ShapeDtypeStruct((b, t, d), F32),
        compiler_params=pltpu.CompilerParams(
            dimension_semantics=("arbitrary", "arbitrary"),
            vmem_limit_bytes=V7X_VMEM_LIMIT_BYTES),
        name="cd_out",
        interpret=interpret,
    )(y_f, y_b, r, k_f, k_b, v, g, o_f, o_b, z, xa, gate.reshape(b, 2, 1, d), *[x for x, _ in consts],
      w_out.astype(BF16))


def _gdn_qkv_kernel(p_ref, w_ref, o_ref, *, n_ctx):
    j = pl.program_id(1)
    x = p_ref[0]
    t = x.shape[0]
    idx = lax.broadcasted_iota(jnp.int32, (t, 1), 0)
    w = w_ref[...]
    y = x * w[D_CONV // 2:D_CONV // 2 + 1]
    for tap in range(D_CONV):
        off = tap - D_CONV // 2
        if off == 0:
            continue
        src = idx + off
        same_seq = (src >= 0) & (src < t) & ((src >= n_ctx) == (idx >= n_ctx))
        y = y + jnp.where(same_seq, pltpu.roll(x, (-off) % t, 0), 0.0) * w[tap:tap + 1]
    y = y * jax.nn.sigmoid(y)
    inv = lax.rsqrt(jnp.sum(y * y, axis=-1, keepdims=True) + 1e-6)
    factor = jnp.where(j < D_HEADS, inv * D_HEAD_DIM ** -0.5, jnp.where(j < 2 * D_HEADS, inv, 1.0))
    o_ref[0] = y * factor


def gdn_qkv(p, d_conv_w, n_ctx, interpret=False):
    b, t, _ = p.shape
    col0 = IN_C // D_HEAD_DIM
    assert IN_C % D_HEAD_DIM == 0
    n_blk = 3 * D_DIM // D_HEAD_DIM
    return pl.pallas_call(
        functools.partial(_gdn_qkv_kernel, n_ctx=n_ctx),
        grid=(b, n_blk),
        in_specs=[pl.BlockSpec((1, t, D_HEAD_DIM), lambda i, j: (i, 0, col0 + j)),
                  pl.BlockSpec((D_CONV, D_HEAD_DIM), lambda i, j: (0, j))],
        out_specs=pl.BlockSpec((1, t, D_HEAD_DIM), lambda i, j: (i, 0, j)),
        out_shape=jax.ShapeDtypeStruct((b, t, 3 * D_DIM), F32),
        compiler_params=pltpu.CompilerParams(
            dimension_semantics=("arbitrary", "arbitrary"),
            vmem_limit_bytes=V7X_VMEM_LIMIT_BYTES),
        name="gdn_qkv",
        interpret=interpret,
    )(p, d_conv_w)


def gdn_prep(p, n_ctx, d_conv_w, d_A_log, d_dt_bias):
    qkv = gdn_qkv(p, d_conv_w, n_ctx)
    offs = IN_C + np.cumsum((0,) + D_SIZES)
    z, bf, bb, af, ab = (p[..., offs[n]:offs[n + 1]] for n in range(1, len(D_SIZES)))
    betas = (jax.nn.sigmoid(bf), jax.nn.sigmoid(bb))
    gs = tuple(-jnp.exp(d_A_log[i]) * jax.nn.softplus(al + d_dt_bias[i]) for i, al in enumerate((af, ab)))
    return qkv, z, betas, gs


def mixer_cd(xa, gate, p, n_ctx, c_mu_prev, c_mu_next, c_w0, c_w2, c_a0, c_a2, c_g2, c_k_k, c_k_a, c_r_k,
             c_ln_w, c_ln_b, d_conv_w, d_A_log, d_dt_bias, d_o_norm, w_out):
    r, v, kk, g, lw, a, ks = rwkv_prep(p, n_ctx, c_mu_prev, c_mu_next, c_w0, c_w2, c_a0, c_a2, c_g2, c_k_k, c_k_a)
    y_f, y_b = rwkv_chunked(r, v, kk, lw[0], a[0], ks[0], lw[1], a[1], ks[1], n_ctx)
    qkv, z, beta, gd = gdn_prep(p, n_ctx, d_conv_w, d_A_log, d_dt_bias)
    o_f, o_b = gdn_chunked(qkv, beta[0], gd[0], beta[1], gd[1], n_ctx)
    return cd_out(xa, gate, y_f, y_b, r, ks[0], ks[1], v, g, o_f, o_b, z, c_r_k, c_ln_w, c_ln_b, d_o_norm, w_out,
                  n_ctx)


FFN_TM = 256
FFN_HALO = 8
FFN_FC = 256
FFN_VMEM_LIMIT_BYTES = 56 * 1024 * 1024


def _modulated(x, shift, scale):
    y = x * lax.rsqrt(jnp.mean(x * x, axis=-1, keepdims=True) + EPS)
    return y * (1.0 + scale) + shift


def _ffn_kernel(xp_ref, x_ref, xn_ref, shift_ref, scale_ref, gate_ref, wup_ref, cw_ref, cb_ref, wdn_ref, o_ref, acc_ref,
                *, n_tiles, ctx_tiles):
    i = pl.program_id(1)
    tm, halo = FFN_TM, FFN_HALO
    rows = tm + 2 * halo
    x = x_ref[0]
    xe = jnp.concatenate([xp_ref[0], x, xn_ref[0]], axis=0)
    h = _modulated(xe, shift_ref[0, 0], scale_ref[0, 0])
    r = lax.broadcasted_iota(jnp.int32, (rows, 1), 0)
    seq_first, seq_last = _seq_edges(i, n_tiles, ctx_tiles)
    outside = ((r == halo - 1) & seq_first) | ((r == halo + tm) & seq_last)
    h = jnp.where(outside, 0.0, h).astype(BF16)
    acc_ref[...] = jnp.zeros_like(acc_ref)

    def conv(u, c0):
        w = cw_ref[:, c0:c0 + FFN_FC]
        y = (pltpu.roll(u, 1, 0) * w[0:1] + u * w[1:2] + pltpu.roll(u, rows - 1, 0) * w[2:3]
             + cb_ref[:, c0:c0 + FFN_FC])
        return y[halo:halo + tm]

    def up(c):
        cv, cg = c * FFN_FC, D_FF + c * FFN_FC
        return (jnp.dot(h, wup_ref[:, cv:cv + FFN_FC], preferred_element_type=F32),
                jnp.dot(h, wup_ref[:, cg:cg + FFN_FC], preferred_element_type=F32))

    def down(act, c):
        acc_ref[...] += jnp.dot(act, wdn_ref[c * FFN_FC:(c + 1) * FFN_FC, :], preferred_element_type=F32)

    n_chunks = D_FF // FFN_FC
    u_next = up(0)
    act_prev = None
    for c in range(n_chunks):
        u_val, u_gat = u_next
        if c + 1 < n_chunks:
            u_next = up(c + 1)
        if act_prev is not None:
            down(act_prev, c - 1)
        val = conv(u_val, c * FFN_FC)
        gat = conv(u_gat, D_FF + c * FFN_FC)
        act_prev = (gat * jax.nn.sigmoid(gat) * val).astype(BF16)
    down(act_prev, n_chunks - 1)
    o_ref[0] = x + gate_ref[0, 0] * acc_ref[...]


def ffn_fused(x, shift, scale, gate, w_up, conv_w, conv_b, w_down, n_ctx, interpret=False):
    b, t, d = x.shape
    nt = t // FFN_TM
    ctx_tiles = n_ctx // FFN_TM
    hb = FFN_TM // FFN_HALO
    f2 = w_up.shape[1]
    mod = pl.BlockSpec((1, 1, 1, d), lambda bi, i: (bi, jnp.where(i >= ctx_tiles, 1, 0), 0, 0))

    def full(shp):
        return pl.BlockSpec(shp, lambda bi, i: (0,) * len(shp))

    def mods(m):
        return m.reshape(b, 2, 1, d)

    return pl.pallas_call(
        functools.partial(_ffn_kernel, n_tiles=nt, ctx_tiles=ctx_tiles),
        grid=(b, nt),
        in_specs=[pl.BlockSpec((1, FFN_HALO, d), lambda bi, i: (bi, jnp.maximum(i * hb - 1, 0), 0)),
                  pl.BlockSpec((1, FFN_TM, d), lambda bi, i: (bi, i, 0)),
                  pl.BlockSpec((1, FFN_HALO, d), lambda bi, i: (bi, jnp.minimum((i + 1) * hb, nt * hb - 1), 0)),
                  mod, mod, mod,
                  full((d, f2)), full((FFN_CONV, f2)), full((1, f2)), full((f2 // 2, d))],
        out_specs=pl.BlockSpec((1, FFN_TM, d), lambda bi, i: (bi, i, 0)),
        out_shape=jax.ShapeDtypeStruct((b, t, d), F32),
        scratch_shapes=[pltpu.VMEM((FFN_TM, d), F32)],
        compiler_params=pltpu.CompilerParams(
            dimension_semantics=("arbitrary", "arbitrary"),
            vmem_limit_bytes=FFN_VMEM_LIMIT_BYTES),
        name="conv_ffn",
        interpret=interpret,
    )(x, x, x, mods(shift), mods(scale), mods(gate), w_up.astype(BF16), conv_w, conv_b.reshape(1, f2),
      w_down.astype(BF16))


def kernel(x, c, ctx, c_ctx, ada_w, ada_b, ffn_w_up, ffn_conv_w, ffn_conv_b, ffn_w_down,
           ab_w_in, ab_w_out, a_q_norm, a_k_norm, a_sink, b_cq_norm, b_ckv_norm, b_w_uq, b_w_uk, b_w_uv,
           b_qn_norm, b_qr_norm, b_kn_norm, b_kr_norm, cd_w_in, cd_w_out, c_mu_prev, c_mu_next, c_w0, c_w2,
           c_a0, c_a2, c_g2, c_k_k, c_k_a, c_r_k, c_ln_w, c_ln_b, d_conv_w, d_A_log, d_dt_bias, d_o_norm):
    bsz, seq = x.shape[:2]
    n_ctx = ctx.shape[1]
    rows = seq // GRID_W
    zeros = jnp.zeros((n_ctx,), jnp.int32)
    row = jnp.concatenate([zeros, jnp.repeat(jnp.arange(rows, dtype=jnp.int32), GRID_W)])
    col = jnp.concatenate([zeros, jnp.tile(jnp.arange(GRID_W, dtype=jnp.int32), rows)])
    is_ctx = (jnp.arange(n_ctx + seq) < n_ctx)[None, :, None]
    silu_c = jax.nn.silu(c)
    silu_cc = jax.nn.silu(c_ctx)
    xa = jnp.concatenate([ctx, x], axis=1)
    for l in range(DEPTH):
        last = l == DEPTH - 1
        i = l // 2
        mod_l = jnp.split(silu_c @ ada_w[l] + ada_b[l], N_MOD, axis=-1)
        mod_c = jnp.split(silu_cc @ ada_w[l] + ada_b[l], N_MOD, axis=-1)
        mods = [jnp.stack([jnp.broadcast_to(mc, ml.shape), ml], axis=1) for mc, ml in zip(mod_c, mod_l)]

        def per_token(m):
            return jnp.where(is_ctx, m[:, 0:1, :], m[:, 1:2, :])

        h = rms_norm(xa) * (1.0 + per_token(mods[1])) + per_token(mods[0])
        if l % 2 == 0:
            y = mixer_ab(_mm(h, ab_w_in[i], keep_pad=True), n_ctx, row, col, a_q_norm[i], a_k_norm[i],
                         a_sink[i], b_cq_norm[i], b_ckv_norm[i], b_w_uq[i], b_w_uk[i], b_w_uv[i],
                         b_qn_norm[i], b_qr_norm[i], b_kn_norm[i], b_kr_norm[i], not last)
            xa = xa + per_token(mods[2]) * _mm(y, ab_w_out[i])
        else:
            xa = mixer_cd(xa, mods[2], _mm(h, cd_w_in[i], keep_pad=True), n_ctx, c_mu_prev[i], c_mu_next[i], c_w0[i],
                          c_w2[i], c_a0[i], c_a2[i], c_g2[i], c_k_k[i], c_k_a[i], c_r_k[i], c_ln_w[i],
                          c_ln_b[i], d_conv_w[i], d_A_log[i], d_dt_bias[i], d_o_norm[i], cd_w_out[i])
        xa = ffn_fused(xa, mods[3], mods[4], mods[5], ffn_w_up[l], ffn_conv_w[l], ffn_conv_b[l], ffn_w_down[l],
                       n_ctx)
    return xa[:, n_ctx:]
```

```python
import functools

import jax
import jax.numpy as jnp
from jax import lax
import numpy as np
from jax.experimental import pallas as pl
from jax.experimental.pallas import tpu as pltpu

D_MODEL = 1024
DEPTH = 2
GRID_W = 64
N_MOD = 6
EPS = 1e-6
ROPE_THETA = 10000.0
NEG_INF = -1e30

A_HEADS = 8
A_KV_HEADS = 2
A_HEAD_DIM = 64
WINDOW = 128
B_HEADS = 8
B_Q_RANK = 256
B_KV_RANK = 256
B_NOPE = 64
B_ROPE = 32
B_V_DIM = 64
C_HEADS = 8
C_HEAD = 64
C_DIM = C_HEADS * C_HEAD
C_DECAY_LORA = 64
C_AAA_LORA = 64
C_GATE_LORA = 128
C_GN_EPS = 64e-5
D_HEADS = 4
D_HEAD_DIM = 128
D_DIM = D_HEADS * D_HEAD_DIM
D_CONV = 5
D_CHUNK = 64
D_FF = 2816
FFN_CONV = 3

AB_SIZES = (A_HEADS * A_HEAD_DIM, A_KV_HEADS * A_HEAD_DIM, A_KV_HEADS * A_HEAD_DIM, B_Q_RANK, B_KV_RANK, B_ROPE)
C_SIZES = (C_DIM, C_DIM, C_DIM, C_DECAY_LORA, C_DECAY_LORA, C_AAA_LORA, C_AAA_LORA, C_GATE_LORA)
IN_C = sum(C_SIZES)
D_SIZES = (3 * D_DIM, D_DIM, D_HEADS, D_HEADS, D_HEADS, D_HEADS)

F32 = jnp.float32
BF16 = jnp.bfloat16

V7X_VMEM_LIMIT_BYTES = 48 * 1024 * 1024
LANE = 128
MXU_N = 256

NN = ((1,), (0,))
NT = ((1,), (1,))
TN = ((0,), (0,))


def _mm_kernel(a_ref, b_ref, o_ref):
    o_ref[...] = jnp.dot(a_ref[...].astype(BF16), b_ref[...], preferred_element_type=F32)


def _pick_tile(n, candidates):
    for c in candidates:
        if n % c == 0:
            return c
    raise ValueError(f"no tile for {n}")


def _mm(a, w):
    lead = a.shape[:-1]
    k = a.shape[-1]
    n = w.shape[-1]
    a2 = a.reshape(-1, k)
    m = a2.shape[0]
    n_pad = -(-n // MXU_N) * MXU_N
    wb = w.astype(BF16)
    if n_pad != n:
        wb = jnp.pad(wb, ((0, 0), (0, n_pad - n)))
    tm = _pick_tile(m, (1024, 512, 256, 128, 8))
    tn = _pick_tile(n_pad, (1024, 768, 512, 256))
    out = pl.pallas_call(
        _mm_kernel,
        grid=(m // tm, n_pad // tn),
        in_specs=[pl.BlockSpec((tm, k), lambda i, j: (i, 0)),
                  pl.BlockSpec((k, tn), lambda i, j: (0, j))],
        out_specs=pl.BlockSpec((tm, tn), lambda i, j: (i, j)),
        out_shape=jax.ShapeDtypeStruct((m, n_pad), F32),
        compiler_params=pltpu.CompilerParams(
            dimension_semantics=("arbitrary", "arbitrary"),
            vmem_limit_bytes=V7X_VMEM_LIMIT_BYTES),
        name="mm",
    )(a2, wb)
    if n_pad != n:
        out = out[:, :n]
    return out.reshape(*lead, out.shape[-1])


MOD_TM = 768


def _mod_mm_kernel(x_ref, shift_ref, scale_ref, w_ref, o_ref, h_ref, *, n_ctx):
    @pl.when(pl.program_id(2) == 0)
    def _():
        x = x_ref[0]
        tok = pl.program_id(1) * MOD_TM + lax.broadcasted_iota(jnp.int32, (MOD_TM, 1), 0)
        is_ctx = tok < n_ctx
        shift = jnp.where(is_ctx, shift_ref[0, 0:1], shift_ref[0, 1:2])
        scale = jnp.where(is_ctx, scale_ref[0, 0:1], scale_ref[0, 1:2])
        y = x * lax.rsqrt(jnp.mean(x * x, axis=-1, keepdims=True) + EPS)
        h_ref[...] = (y * (1.0 + scale) + shift).astype(BF16)

    o_ref[0] = jnp.dot(h_ref[...], w_ref[...], preferred_element_type=F32)


def mod_mm(x, shift, scale, w, n_ctx):
    b, t, d = x.shape
    n = w.shape[-1]
    n_pad = -(-n // MXU_N) * MXU_N
    wb = jnp.pad(w.astype(BF16), ((0, 0), (0, n_pad - n)))
    tn = _pick_tile(n_pad, (1024, 768, 512, 256))
    mod = pl.BlockSpec((1, 2, d), lambda bi, i, j: (bi, 0, 0))
    return pl.pallas_call(
        functools.partial(_mod_mm_kernel, n_ctx=n_ctx),
        grid=(b, t // MOD_TM, n_pad // tn),
        in_specs=[pl.BlockSpec((1, MOD_TM, d), lambda bi, i, j: (bi, i, 0)), mod, mod,
                  pl.BlockSpec((d, tn), lambda bi, i, j: (0, j))],
        out_specs=pl.BlockSpec((1, MOD_TM, tn), lambda bi, i, j: (bi, i, j)),
        out_shape=jax.ShapeDtypeStruct((b, t, n_pad), F32),
        scratch_shapes=[pltpu.VMEM((MOD_TM, d), BF16)],
        compiler_params=pltpu.CompilerParams(
            dimension_semantics=("arbitrary", "arbitrary", "arbitrary"),
            vmem_limit_bytes=V7X_VMEM_LIMIT_BYTES),
        name="mod_mm",
    )(x, shift, scale, wb)


def _dot(a, b, dims, passes=1):
    def dg(x, y):
        return lax.dot_general(x, y, (dims, ((), ())), preferred_element_type=F32)

    ah, bh = a.astype(BF16), b.astype(BF16)
    if passes == 1:
        return dg(ah, bh)
    al = (a - ah.astype(F32)).astype(BF16)
    bl = (b - bh.astype(F32)).astype(BF16)
    return dg(ah, bh) + (dg(ah, bl) + dg(al, bh))


def _split3(x):
    hi = x.astype(BF16)
    r1 = x - hi.astype(F32)
    mid = r1.astype(BF16)
    lo = (r1 - mid.astype(F32)).astype(BF16)
    return hi, mid, lo


def _dot_ones(ones, x, dims, ones_first=True):
    def dg(piece):
        a, b = (ones, piece) if ones_first else (piece, ones)
        return lax.dot_general(a, b, (dims, ((), ())), preferred_element_type=F32)

    hi, mid, lo = _split3(x)
    return dg(hi) + (dg(mid) + dg(lo))


def _heads_major(z):
    return jnp.swapaxes(z, 1, 2)


def _softmax_pv(s, v, sink):
    m = jnp.max(s, axis=-1, keepdims=True)
    if sink is not None:
        m = jnp.maximum(m, sink)
    p = jnp.exp(s - m)
    den = jnp.sum(p, axis=-1, keepdims=True)
    if sink is not None:
        den = den + jnp.exp(sink - m)
    return _dot(p, v, NN) / den


def _attn_full_kernel(q_ref, k_ref, q2_ref, k2_ref, v_ref, sink_ref, o_ref, *, scale, use_sink, use_second):
    s = _dot(q_ref[0, 0] * scale, k_ref[0, 0], NT)
    if use_second:
        s = s + _dot(q2_ref[0, 0] * scale, k2_ref[0, 0], NT)
    sink = sink_ref[0] if use_sink else None
    o_ref[0, 0] = _softmax_pv(s, v_ref[0, 0], sink)


def attn_full(q, k, v, scale, q_start, n_q, n_keys, sink=None, second=None):
    b, h, _, d = q.shape
    hk = k.shape[1]
    dv = v.shape[-1]
    g = h // hk
    tq = min(n_q, 256)
    q0 = q_start // tq
    use_sink = sink is not None
    use_second = second is not None
    sink_arr = (sink if use_sink else jnp.zeros((h,), F32)).astype(F32).reshape(h, 1, 1)
    q2, k2 = second if use_second else (q, k)
    d2 = q2.shape[-1]
    k2_spec = (pl.BlockSpec((1, 1, n_keys, d2), lambda i, j, t: (i, 0, 0, 0)) if use_second else
               pl.BlockSpec((1, 1, n_keys, d2), lambda i, j, t: (i, j // g, 0, 0)))
    return pl.pallas_call(
        functools.partial(_attn_full_kernel, scale=scale, use_sink=use_sink, use_second=use_second),
        grid=(b, h, n_q // tq),
        in_specs=[pl.BlockSpec((1, 1, tq, d), lambda i, j, t: (i, j, t + q0, 0)),
                  pl.BlockSpec((1, 1, n_keys, d), lambda i, j, t: (i, j // g, 0, 0)),
                  pl.BlockSpec((1, 1, tq, d2), lambda i, j, t: (i, j, t + q0, 0)),
                  k2_spec,
                  pl.BlockSpec((1, 1, n_keys, dv), lambda i, j, t: (i, j // g, 0, 0)),
                  pl.BlockSpec((1, 1, 1), lambda i, j, t: (j, 0, 0))],
        out_specs=pl.BlockSpec((1, 1, tq, dv), lambda i, j, t: (i, j, t, 0)),
        out_shape=jax.ShapeDtypeStruct((b, h, n_q, dv), F32),
        compiler_params=pltpu.CompilerParams(
            dimension_semantics=("arbitrary", "arbitrary", "arbitrary"),
            vmem_limit_bytes=V7X_VMEM_LIMIT_BYTES),
        name="attn_full",
    )(q, k, q2, k2, v, sink_arr)


def _attn_window_kernel(q_ref, kp_ref, k0_ref, kn_ref, kc_ref, vp_ref, v0_ref, vn_ref, vc_ref, sink_ref, o_ref,
                        *, scale, group, n_blocks):
    n = pl.program_id(2)
    w = WINDOW
    d = q_ref.shape[-1]
    q = q_ref[0].reshape(group * w, d) * scale
    keys = jnp.concatenate([kp_ref[0, 0], k0_ref[0, 0], kn_ref[0, 0], kc_ref[0, 0]], axis=0)
    vals = jnp.concatenate([vp_ref[0, 0], v0_ref[0, 0], vn_ref[0, 0], vc_ref[0, 0]], axis=0)
    s = _dot(q, keys, NT)
    nk = keys.shape[0]
    qi = lax.broadcasted_iota(jnp.int32, (group * w, nk), 0) % w
    kj = lax.broadcasted_iota(jnp.int32, (group * w, nk), 1)
    rel = qi + w - kj
    band_ok = (jnp.abs(rel) <= w) & ((kj >= w) | (n > 0)) & ((kj < 2 * w) | (n < n_blocks - 1))
    s = jnp.where((kj >= 3 * w) | band_ok, s, NEG_INF)
    sink = jnp.concatenate([jnp.broadcast_to(sink_ref[0, hh], (w, 1)) for hh in range(group)], axis=0)
    o = _softmax_pv(s, vals, sink)
    o_ref[0] = o.reshape(group, w, o.shape[-1])


def attn_window(q, k, v, sink, n_ctx):
    b, h, t, d = q.shape
    hk = k.shape[1]
    g = h // hk
    off = n_ctx // WINDOW
    nb = (t - n_ctx) // WINDOW
    sink_arr = sink.astype(F32).reshape(hk, g, 1, 1)

    def blk(f):
        return pl.BlockSpec((1, 1, WINDOW, d), f)

    prev = blk(lambda i, j, n: (i, j, jnp.maximum(n - 1, 0) + off, 0))
    own = blk(lambda i, j, n: (i, j, n + off, 0))
    nxt = blk(lambda i, j, n: (i, j, jnp.minimum(n + 1, nb - 1) + off, 0))
    ctx = pl.BlockSpec((1, 1, n_ctx, d), lambda i, j, n: (i, j, 0, 0))
    return pl.pallas_call(
        functools.partial(_attn_window_kernel, scale=d ** -0.5, group=g, n_blocks=nb),
        grid=(b, hk, nb),
        in_specs=[pl.BlockSpec((1, g, WINDOW, d), lambda i, j, n: (i, j, n + off, 0)),
                  prev, own, nxt, ctx, prev, own, nxt, ctx,
                  pl.BlockSpec((1, g, 1, 1), lambda i, j, n: (j, 0, 0, 0))],
        out_specs=pl.BlockSpec((1, g, WINDOW, d), lambda i, j, n: (i, j, n, 0)),
        out_shape=jax.ShapeDtypeStruct((b, h, t - n_ctx, d), F32),
        compiler_params=pltpu.CompilerParams(
            dimension_semantics=("arbitrary", "arbitrary", "arbitrary"),
            vmem_limit_bytes=V7X_VMEM_LIMIT_BYTES),
        name="attn_window",
    )(q, k, k, k, k, v, v, v, v, sink_arr)


def _rope_tables(row, col, head_dim, width):
    q = head_dim // 4
    inv = jnp.power(ROPE_THETA, -jnp.arange(q, dtype=F32) / q)
    ang_r = row.astype(F32)[:, None] * inv[None, :]
    ang_c = col.astype(F32)[:, None] * inv[None, :]
    cos = jnp.concatenate([jnp.cos(ang_r), jnp.cos(ang_r), jnp.cos(ang_c), jnp.cos(ang_c)], axis=-1)
    sin = jnp.concatenate([-jnp.sin(ang_r), jnp.sin(ang_r), -jnp.sin(ang_c), jnp.sin(ang_c)], axis=-1)
    reps = width // head_dim
    return jnp.tile(cos, (1, reps)), jnp.tile(sin, (1, reps))


def _rope(x, cos, sin, head_dim):
    q = head_dim // 4
    w = x.shape[-1]
    lane = lax.broadcasted_iota(jnp.int32, x.shape, 1)
    swapped = jnp.where(lane % (2 * q) < q, pltpu.roll(x, w - q, 1), pltpu.roll(x, q, 1))
    return x * cos + swapped * sin


def _head_rms(x, ones_bd, head_dim, gain):
    return x * lax.rsqrt(_seg_sum(x * x, ones_bd) * (1.0 / head_dim) + EPS) * gain


def _row_rms(x, n, gain):
    return x * lax.rsqrt(jnp.sum(x * x, axis=-1, keepdims=True) * (1.0 / n) + EPS) * gain


def _ab_prep_kernel(p_ref, c64_ref, s64_ref, c32_ref, s32_ref, gqa_ref, gka_ref, gcq_ref, gckv_ref, gqn_ref, gqr_ref,
                    gkn_ref, gkr_ref, wuq_ref, wuk_ref, wuv_ref,
                    qa_ref, ka_ref, qn_ref, qr_ref, kn_ref, vb_ref, kr_ref):
    p = p_ref[0]
    offs = np.cumsum((0,) + AB_SIZES)
    qa, ka, cq, ckv = (p[:, offs[n]:offs[n + 1]] for n in (0, 1, 3, 4))
    kr = p[:, offs[5]:offs[5] + LANE]
    c64, s64, c32, s32 = c64_ref[...], s64_ref[...], c32_ref[...], s32_ref[...]
    ones_q = _block_ones(A_HEADS * A_HEAD_DIM, A_HEAD_DIM)
    ones_k = _block_ones(A_KV_HEADS * A_HEAD_DIM, A_HEAD_DIM)
    ones_r = _block_ones(B_HEADS * B_ROPE, B_ROPE)
    ka_w = A_KV_HEADS * A_HEAD_DIM
    qa_ref[0] = _rope(_head_rms(qa, ones_q, A_HEAD_DIM, gqa_ref[...]), c64, s64, A_HEAD_DIM)
    ka_ref[0] = _rope(_head_rms(ka, ones_k, A_HEAD_DIM, gka_ref[...]), c64[:, :ka_w], s64[:, :ka_w], A_HEAD_DIM)
    qb = jnp.dot(_row_rms(cq, B_Q_RANK, gcq_ref[...]).astype(BF16), wuq_ref[...], preferred_element_type=F32)
    n_nope = B_HEADS * B_NOPE
    qn_ref[0] = _head_rms(qb[:, :n_nope], ones_q, B_NOPE, gqn_ref[...])
    qr_ref[0] = _rope(_head_rms(qb[:, n_nope:], ones_r, B_ROPE, gqr_ref[...]), c32, s32, B_ROPE)
    ckv_n = _row_rms(ckv, B_KV_RANK, gckv_ref[...]).astype(BF16)
    kn_ref[0] = _head_rms(jnp.dot(ckv_n, wuk_ref[...], preferred_element_type=F32), ones_q, B_NOPE, gkn_ref[...])
    vb_ref[0] = jnp.dot(ckv_n, wuv_ref[...], preferred_element_type=F32)
    kr_ref[0] = _rope(_row_rms(kr, B_ROPE, gkr_ref[...]), c32[:, :LANE], s32[:, :LANE], B_ROPE)


def ab_prep(p, row, col, a_q_norm, a_k_norm, b_cq_norm, b_ckv_norm, b_w_uq, b_w_uk, b_w_uv,
            b_qn_norm, b_qr_norm, b_kn_norm, b_kr_norm, interpret=False):
    b, t, pw = p.shape
    assert pw >= sum(AB_SIZES[:5]) + LANE and A_HEAD_DIM == B_NOPE
    c64, s64 = _rope_tables(row, col, A_HEAD_DIM, A_HEADS * A_HEAD_DIM)
    c32, s32 = _rope_tables(row, col, B_ROPE, B_HEADS * B_ROPE)
    wq = b_w_uq.reshape(B_Q_RANK, B_HEADS, B_NOPE + B_ROPE)
    wq = jnp.concatenate([wq[..., :B_NOPE].reshape(B_Q_RANK, -1), wq[..., B_NOPE:].reshape(B_Q_RANK, -1)], axis=1)

    def row_c(g, reps, width=None):
        x = jnp.tile(g, reps).reshape(1, -1)
        if width is not None:
            x = jnp.pad(x, ((0, 0), (0, width - x.shape[1])))
        return x

    consts = [row_c(a_q_norm, A_HEADS), row_c(a_k_norm, A_KV_HEADS), row_c(b_cq_norm, 1), row_c(b_ckv_norm, 1),
              row_c(b_qn_norm, B_HEADS), row_c(b_qr_norm, B_HEADS), row_c(b_kn_norm, B_HEADS),
              row_c(b_kr_norm, 1, LANE), wq.astype(BF16), b_w_uk.astype(BF16), b_w_uv.astype(BF16)]
    tabs = [c64, s64, c32, s32]
    widths = (A_HEADS * A_HEAD_DIM, A_KV_HEADS * A_HEAD_DIM, B_HEADS * B_NOPE, B_HEADS * B_ROPE, B_HEADS * B_NOPE,
              B_HEADS * B_V_DIM, LANE)
    return pl.pallas_call(
        _ab_prep_kernel,
        grid=(b, t // TOK_TM),
        in_specs=[pl.BlockSpec((1, TOK_TM, pw), lambda bi, i: (bi, i, 0))]
        + [pl.BlockSpec((TOK_TM, x.shape[1]), lambda bi, i: (i, 0)) for x in tabs]
        + [pl.BlockSpec(x.shape, lambda bi, i: (0, 0)) for x in consts],
        out_specs=[pl.BlockSpec((1, TOK_TM, w), lambda bi, i: (bi, i, 0)) for w in widths],
        out_shape=[jax.ShapeDtypeStruct((b, t, w), F32) for w in widths],
        compiler_params=pltpu.CompilerParams(
            dimension_semantics=("arbitrary", "arbitrary"),
            vmem_limit_bytes=V7X_VMEM_LIMIT_BYTES),
        name="ab_prep",
        interpret=interpret,
    )(p, *tabs, *consts)


def mixer_ab(p, n_ctx, row, col, a_q_norm, a_k_norm, a_sink, b_cq_norm, b_ckv_norm, b_w_uq, b_w_uk,
             b_w_uv, b_qn_norm, b_qr_norm, b_kn_norm, b_kr_norm, ctx_out):
    b, t = p.shape[:2]
    qa, ka, qn, qr, kn, vb, kr = ab_prep(p, row, col, a_q_norm, a_k_norm, b_cq_norm, b_ckv_norm, b_w_uq, b_w_uk,
                                         b_w_uv, b_qn_norm, b_qr_norm, b_kn_norm, b_kr_norm)
    va_off = AB_SIZES[0] + AB_SIZES[1]
    va = p[..., va_off:va_off + AB_SIZES[2]]

    def hm(z, heads):
        return _heads_major(z.reshape(b, t, heads, -1))

    qa_t, ka_t, va_t = hm(qa, A_HEADS), hm(ka, A_KV_HEADS), hm(va, A_KV_HEADS)
    qn_t, kn_t, vb_t = hm(qn, B_HEADS), hm(kn, B_HEADS), hm(vb, B_HEADS)
    rope_part = (hm(qr, B_HEADS), hm(kr[..., :B_ROPE], 1))
    b_scale = (B_NOPE + B_ROPE) ** -0.5

    o_a = attn_window(qa_t, ka_t, va_t, a_sink, n_ctx)
    o_b = attn_full(qn_t, kn_t, vb_t, b_scale, n_ctx, t - n_ctx, t, second=rope_part)
    if ctx_out:
        o_a_c = attn_full(qa_t, ka_t, va_t, A_HEAD_DIM ** -0.5, 0, n_ctx, n_ctx, sink=a_sink)
        o_b_c = attn_full(qn_t, kn_t, vb_t, b_scale, 0, n_ctx, n_ctx, second=rope_part)
    else:
        o_a_c = jnp.zeros((b, A_HEADS, n_ctx, A_HEAD_DIM), F32)
        o_b_c = jnp.zeros((b, B_HEADS, n_ctx, B_V_DIM), F32)
    o_a = _heads_major(jnp.concatenate([o_a_c, o_a], axis=2)).reshape(b, t, -1)
    o_b = _heads_major(jnp.concatenate([o_b_c, o_b], axis=2)).reshape(b, t, -1)
    return jnp.concatenate([o_a, o_b], axis=-1)


RW_CHUNK = 64
HEAD_PAIR = 2 * C_HEAD


def _unit_lower_inverse(xs, eye):
    ps = [eye + x for x in xs]
    xps = list(xs)
    for _ in range(int(np.log2(RW_CHUNK)) - 1):
        xps = [_dot(xp, xp, NN, passes=3) for xp in xps]
        ps = [p + _dot(p, xp, NN, passes=3) for p, xp in zip(ps, xps)]
    return ps


def _chunk_maps(nc, nctx):
    def fwd(s):
        return s

    def bwd(s):
        return jnp.where(s < nctx, nctx - 1 - s, nc + nctx - 1 - s)

    return fwd, bwd


def _rwkv_kernel(rf_ref, vf_ref, kkf_ref, lwf_ref, af_ref, kf_ref,
                 rb_ref, vb_ref, kkb_ref, lwb_ref, ab_ref, kb_ref, yf_ref, yb_ref, st_ref):
    @pl.when(pl.program_id(1) == 0)
    def _():
        st_ref[...] = jnp.zeros_like(st_ref)

    c_len = RW_CHUNK
    ii = lax.broadcasted_iota(jnp.int32, (c_len, c_len), 0)
    jj = lax.broadcasted_iota(jnp.int32, (c_len, c_len), 1)
    r2 = lax.broadcasted_iota(jnp.int32, (HEAD_PAIR, HEAD_PAIR), 0)
    c2 = lax.broadcasted_iota(jnp.int32, (HEAD_PAIR, HEAD_PAIR), 1)
    dlt = (r2 % c_len) - (c2 % c_len)
    eye = jnp.where(r2 == c2, 1.0, 0.0).astype(F32)
    lane = lax.broadcasted_iota(jnp.int32, (c_len, HEAD_PAIR), 1)
    m0 = lane < C_HEAD

    def stack_heads(x):
        return jnp.concatenate([jnp.where(m0, x, 0.0), jnp.where(m0, 0.0, x)], axis=0)

    n_pairs = C_DIM // HEAD_PAIR
    ch = []
    for d, refs in enumerate(((rf_ref, vf_ref, kkf_ref, lwf_ref, af_ref, kf_ref),
                              (rb_ref, vb_ref, kkb_ref, lwb_ref, ab_ref, kb_ref))):
        sign = 1 - 2 * d
        r_all, v_all, kk_all, lw_all, a_all, kd_all = (ref[0] for ref in refs)
        tri = jnp.where((ii - jj) * sign >= 0, 1.0, 0.0).astype(BF16)
        cum_all = _dot_ones(tri, lw_all, NN)
        tot_all = jnp.sum(lw_all, axis=0, keepdims=True)
        for p in range(n_pairs):
            sl = slice(p * HEAD_PAIR, (p + 1) * HEAD_PAIR)
            ch.append(dict(strict=dlt * sign > 0, incl=dlt * sign >= 0, st=st_ref[d, p],
                           lw=lw_all[:, sl], cum=cum_all[:, sl], tot=tot_all[:, sl], kk=kk_all[:, sl],
                           kd=kd_all[:, sl], bb=kk_all[:, sl] * a_all[:, sl], r=r_all[:, sl], v=v_all[:, sl]))
    for c in ch:
        e_neg = jnp.exp(-c['cum'])
        e_end = jnp.exp(c['tot'] - c['cum'])
        c['abar'] = stack_heads(-c['kk'] * jnp.exp(c['cum'] - c['lw']))
        c['rbar'] = stack_heads(c['r'] * jnp.exp(c['cum']))
        c['ktil'] = stack_heads(c['kd'] * e_neg)
        c['btil'] = stack_heads(c['bb'] * e_neg)
        c['khat'] = stack_heads(c['kd'] * e_end)
        c['bhat'] = stack_heads(c['bb'] * e_end)
        c['vs'] = stack_heads(c['v'])
    g = [_dot(jnp.concatenate([c['abar'], c['rbar']], axis=0), jnp.concatenate([c['ktil'], c['btil']], axis=0), NT)
         for c in ch]
    a_ak = [jnp.where(c['strict'], x[:HEAD_PAIR, :HEAD_PAIR], 0.0) for c, x in zip(ch, g)]
    a_ab = [jnp.where(c['strict'], x[:HEAD_PAIR, HEAD_PAIR:], 0.0) for c, x in zip(ch, g)]
    a_rk = [jnp.where(c['incl'], x[HEAD_PAIR:, :HEAD_PAIR], 0.0) for c, x in zip(ch, g)]
    a_rb = [jnp.where(c['incl'], x[HEAD_PAIR:, HEAD_PAIR:], 0.0) for c, x in zip(ch, g)]
    tinv = _unit_lower_inverse(a_ab, eye)
    n = range(len(ch))
    wm = [_dot(tinv[i], ch[i]['abar'], NN) for i in n]
    akv = [_dot(a_ak[i], ch[i]['vs'], NN) for i in n]
    u0 = [_dot(tinv[i], akv[i], NN) for i in n]
    u = [_dot(wm[i], ch[i]['st'], NT) + u0[i] for i in n]
    ys = [_dot(ch[i]['rbar'], ch[i]['st'], NT) + _dot(a_rk[i], ch[i]['vs'], NN) + _dot(a_rb[i], u[i], NN) for i in n]
    st_new = [ch[i]['st'] * jnp.exp(ch[i]['tot']) + _dot(ch[i]['vs'], ch[i]['khat'], TN) + _dot(u[i], ch[i]['bhat'], TN)
              for i in n]
    for d, y_ref in enumerate((yf_ref, yb_ref)):
        y_ref[0] = jnp.concatenate([ys[d * n_pairs + p][:c_len] + ys[d * n_pairs + p][c_len:] for p in range(n_pairs)],
                                   axis=1)
        for p in range(n_pairs):
            st_ref[d, p] = st_new[d * n_pairs + p]


def rwkv_chunked(r, v, kk, lw_f, a_f, k_f, lw_b, a_b, k_b, n_ctx, interpret=False):
    b, t, cd = r.shape
    nc = t // RW_CHUNK
    fwd, bwd = _chunk_maps(nc, n_ctx // RW_CHUNK)
    sf = pl.BlockSpec((1, RW_CHUNK, cd), lambda i, s: (i, fwd(s), 0))
    sb = pl.BlockSpec((1, RW_CHUNK, cd), lambda i, s: (i, bwd(s), 0))
    out = jax.ShapeDtypeStruct((b, t, cd), F32)
    return pl.pallas_call(
        _rwkv_kernel,
        grid=(b, nc),
        in_specs=[sf] * 6 + [sb] * 6,
        out_specs=[sf, sb],
        out_shape=[out, out],
        scratch_shapes=[pltpu.VMEM((2, cd // HEAD_PAIR, HEAD_PAIR, HEAD_PAIR), F32)],
        compiler_params=pltpu.CompilerParams(
            dimension_semantics=("arbitrary", "arbitrary"),
            vmem_limit_bytes=V7X_VMEM_LIMIT_BYTES),
        name="rwkv7_chunked",
        interpret=interpret,
    )(r, v, kk, lw_f, a_f, k_f, r, v, kk, lw_b, a_b, k_b)


def _gdn_kernel(qf_ref, kf_ref, vf_ref, bf_ref, gf_ref, gtf_ref,
                qb_ref, kb_ref, vb_ref, bb_ref, gb_ref, gtb_ref, of_ref, ob_ref, st_ref):
    @pl.when(pl.program_id(1) == 0)
    def _():
        st_ref[...] = jnp.zeros_like(st_ref)

    c_len = D_CHUNK
    n2 = 2 * c_len
    r2 = lax.broadcasted_iota(jnp.int32, (n2, n2), 0)
    c2 = lax.broadcasted_iota(jnp.int32, (n2, n2), 1)
    same = (r2 // c_len) == (c2 // c_len)
    dlt = (r2 % c_len) - (c2 % c_len)
    ones_bd = jnp.where(same, 1.0, 0.0).astype(BF16)
    eye = jnp.where(r2 == c2, 1.0, 0.0).astype(F32)

    def rows(blk, p):
        return jnp.concatenate([blk[:, h * D_HEAD_DIM:(h + 1) * D_HEAD_DIM] for h in (2 * p, 2 * p + 1)], axis=0)

    def col(blk, p):
        c = jnp.concatenate([blk[:, h:h + 1] for h in (2 * p, 2 * p + 1)], axis=0)
        return jnp.broadcast_to(c, (n2, n2))

    n_pairs = D_HEADS // 2
    ch = []
    for d, refs in enumerate(((qf_ref, kf_ref, vf_ref, bf_ref, gf_ref, gtf_ref),
                              (qb_ref, kb_ref, vb_ref, bb_ref, gb_ref, gtb_ref))):
        sign = 1 - 2 * d
        q_blk, k_blk, v_blk = refs[0][0], refs[1][0], refs[2][0]
        b_blk, g_blk, gt_blk = refs[3][0, 0], refs[4][0, 0], refs[5][0, 0]
        incl = same & (dlt * sign >= 0)
        tri = jnp.where(incl, 1.0, 0.0).astype(BF16)
        for p in range(n_pairs):
            g_row = jnp.concatenate([gt_blk[h:h + 1, :] for h in (2 * p, 2 * p + 1)], axis=1)
            ch.append(dict(d=d, p=p, strict=same & (dlt * sign > 0), incl=incl, tri=tri,
                           q=rows(q_blk, p), k=rows(k_blk, p), v=rows(v_blk, p), beta=col(b_blk, p),
                           g_colb=col(g_blk, p), g_rowb=jnp.broadcast_to(g_row, (n2, n2)),
                           st=[st_ref[d, 2 * p], st_ref[d, 2 * p + 1]]))
    n = range(len(ch))
    gc_col = [_dot_ones(c['tri'], c['g_colb'], NN) for c in ch]
    gc_row = [_dot_ones(c['tri'], c['g_rowb'], NT, ones_first=False) for c in ch]
    g_end = [_dot_ones(ones_bd, c['g_colb'], NN) for c in ch]
    decay = [jnp.where(ch[i]['incl'], jnp.exp(jnp.where(ch[i]['incl'], gc_col[i] - gc_row[i], 0.0)), 0.0) for i in n]
    kb = [c['k'] * c['beta'] for c in ch]
    vb = [c['v'] * c['beta'] for c in ch]
    g_mat = [_dot(jnp.concatenate([kb[i], ch[i]['q']], axis=0), ch[i]['k'], NT) for i in n]
    l_mat = [jnp.where(ch[i]['strict'], g_mat[i][:n2] * decay[i], 0.0) for i in n]
    a_intra = [g_mat[i][n2:] * decay[i] for i in n]
    tinv = _unit_lower_inverse([-l for l in l_mat], eye)
    u = [_dot(tinv[i], vb[i], NN) for i in n]
    wk = [_dot(tinv[i], kb[i] * jnp.exp(gc_col[i]), NN) for i in n]
    q_e = [ch[i]['q'] * jnp.exp(gc_col[i]) for i in n]
    k_e = [ch[i]['k'] * jnp.exp(g_end[i] - gc_col[i]) for i in n]
    halves = (slice(0, c_len), slice(c_len, n2))
    v_new = [[u[i][hs] - _dot(wk[i][hs], ch[i]['st'][j], NN) for j, hs in enumerate(halves)] for i in n]
    o_st = [[_dot(q_e[i][hs], ch[i]['st'][j], NN) for j, hs in enumerate(halves)] for i in n]
    v_new_s = [jnp.concatenate(v_new[i], axis=0) for i in n]
    o_s = [jnp.concatenate(o_st[i], axis=0) + _dot(a_intra[i], v_new_s[i], NN) for i in n]
    st_new = [[ch[i]['st'][j] * jnp.exp(g_end[i][hs][0:1, :]) + _dot(k_e[i][hs], v_new[i][j], TN)
               for j, hs in enumerate(halves)] for i in n]
    for d, o_ref in enumerate((of_ref, ob_ref)):
        o_ref[0] = jnp.concatenate([o_s[d * n_pairs + p][hs] for p in range(n_pairs) for hs in halves], axis=1)
        for p in range(n_pairs):
            for j in range(2):
                st_ref[d, 2 * p + j] = st_new[d * n_pairs + p][j]


def gdn_chunked(qkv, beta_f, g_f, beta_b, g_b, n_ctx, interpret=False):
    b, t, _ = qkv.shape
    cd = D_DIM
    nc = t // D_CHUNK
    fwd, bwd = _chunk_maps(nc, n_ctx // D_CHUNK)

    def chunks(z):
        return z.reshape(b, nc, D_CHUNK, D_HEADS)

    def specs(cmap):
        big = [pl.BlockSpec((1, D_CHUNK, cd), functools.partial(lambda part, i, s: (i, cmap(s), part), part))
               for part in range(3)]
        small = pl.BlockSpec((1, 1, D_CHUNK, D_HEADS), lambda i, s: (i, cmap(s), 0, 0))
        small_t = pl.BlockSpec((1, 1, D_HEADS, D_CHUNK), lambda i, s: (i, cmap(s), 0, 0))
        return big, small, small_t

    bf, sf, stf = specs(fwd)
    bb, sb, stb = specs(bwd)
    out = jax.ShapeDtypeStruct((b, t, cd), F32)
    return pl.pallas_call(
        _gdn_kernel,
        grid=(b, nc),
        in_specs=bf + [sf, sf, stf] + bb + [sb, sb, stb],
        out_specs=[bf[0], bb[0]],
        out_shape=[out, out],
        scratch_shapes=[pltpu.VMEM((2, D_HEADS, D_HEAD_DIM, D_HEAD_DIM), F32)],
        compiler_params=pltpu.CompilerParams(
            dimension_semantics=("arbitrary", "arbitrary"),
            vmem_limit_bytes=V7X_VMEM_LIMIT_BYTES),
        name="gdn_chunked",
        interpret=interpret,
    )(qkv, qkv, qkv, chunks(beta_f), chunks(g_f), jnp.swapaxes(chunks(g_f), -1, -2),
      qkv, qkv, qkv, chunks(beta_b), chunks(g_b), jnp.swapaxes(chunks(g_b), -1, -2))


TOK_TM = 256
TOK_HALO = 8


def _block_ones(width, seg):
    r = lax.broadcasted_iota(jnp.int32, (width, width), 0)
    c = lax.broadcasted_iota(jnp.int32, (width, width), 1)
    return jnp.where((r // seg) == (c // seg), 1.0, 0.0).astype(BF16)


def _seg_sum(x, ones_bd):
    hi = x.astype(BF16)
    lo = (x - hi.astype(F32)).astype(BF16)
    return jnp.dot(hi, ones_bd, preferred_element_type=F32) + jnp.dot(lo, ones_bd, preferred_element_type=F32)


def _softplus(x):
    return jnp.maximum(x, 0.0) + jnp.log(1.0 + jnp.exp(-jnp.abs(x)))


def _seq_edges(i, n_tiles, ctx_tiles):
    return (i == 0) | (i == ctx_tiles), (i == ctx_tiles - 1) | (i == n_tiles - 1)


def _rwkv_prep_kernel(pp_ref, p_ref, pn_ref, mup_ref, mun_ref, w0_ref, a0_ref, kkw_ref, kaw_ref,
                      w2f_ref, w2b_ref, a2f_ref, a2b_ref, g2_ref,
                      r_ref, v_ref, kk_ref, g_ref, lwf_ref, af_ref, kf_ref, lwb_ref, ab_ref, kb_ref,
                      *, n_tiles, ctx_tiles):
    i = pl.program_id(1)
    tm, halo = TOK_TM, TOK_HALO
    rows = tm + 2 * halo
    pe = jnp.concatenate([pp_ref[0], p_ref[0], pn_ref[0]], axis=0)
    ridx = lax.broadcasted_iota(jnp.int32, (rows, 1), 0)
    seq_first, seq_last = _seq_edges(i, n_tiles, ctx_tiles)
    outside = ((ridx == halo - 1) & seq_first) | ((ridx == halo + tm) & seq_last)
    pe = jnp.where(outside, 0.0, pe)
    prev = pltpu.roll(pe, 1, 0)
    nxt = pltpu.roll(pe, rows - 1, 0)
    xs = (pe + mup_ref[...] * (prev - pe) + mun_ref[...] * (nxt - pe))[halo:halo + tm]
    offs = np.cumsum((0,) + C_SIZES)
    r, k, v, wl_f, wl_b, al_f, al_b, gl = (xs[:, offs[n]:offs[n + 1]] for n in range(len(C_SIZES)))

    def lora(x, w_ref):
        return jnp.dot(x.astype(BF16), w_ref[...], preferred_element_type=F32)

    ones = _block_ones(C_DIM, C_HEAD)
    kq = k * kkw_ref[...]
    r_ref[0] = r
    v_ref[0] = v
    kk_ref[0] = kq * lax.rsqrt(_seg_sum(kq * kq, ones) + 1e-6)
    g_ref[0] = lora(jax.nn.sigmoid(gl), g2_ref)
    for d, (wl, al, w2_ref, a2_ref, lw_ref, a_ref, kd_ref) in enumerate((
            (wl_f, al_f, w2f_ref, a2f_ref, lwf_ref, af_ref, kf_ref),
            (wl_b, al_b, w2b_ref, a2b_ref, lwb_ref, ab_ref, kb_ref))):
        w = -_softplus(-(w0_ref[d:d + 1] + lora(jnp.tanh(wl), w2_ref))) - 0.5
        lw_ref[0] = -jnp.exp(w)
        a = jax.nn.sigmoid(a0_ref[d:d + 1] + lora(al, a2_ref))
        a_ref[0] = a
        kd_ref[0] = k * (1.0 + (a - 1.0) * kaw_ref[...])


def rwkv_prep(p, n_ctx, c_mu_prev, c_mu_next, c_w0, c_w2, c_a0, c_a2, c_g2, c_k_k, c_k_a, interpret=False):
    b, t, _ = p.shape
    nt = t // TOK_TM
    hb = TOK_TM // TOK_HALO
    assert IN_C % LANE == 0

    def full(x):
        x2 = x.reshape(-1, x.shape[-1])
        return x2, pl.BlockSpec(x2.shape, lambda bi, i: (0, 0))

    consts = [full(x) for x in (c_mu_prev, c_mu_next, c_w0, c_a0, c_k_k, c_k_a,
                                c_w2[0].astype(BF16), c_w2[1].astype(BF16), c_a2[0].astype(BF16),
                                c_a2[1].astype(BF16), c_g2.astype(BF16))]
    tile = pl.BlockSpec((1, TOK_TM, C_DIM), lambda bi, i: (bi, i, 0))
    out = jax.ShapeDtypeStruct((b, t, C_DIM), F32)
    r, v, kk, g, lw_f, a_f, k_f, lw_b, a_b, k_b = pl.pallas_call(
        functools.partial(_rwkv_prep_kernel, n_tiles=nt, ctx_tiles=n_ctx // TOK_TM),
        grid=(b, nt),
        in_specs=[pl.BlockSpec((1, TOK_HALO, IN_C), lambda bi, i: (bi, jnp.maximum(i * hb - 1, 0), 0)),
                  pl.BlockSpec((1, TOK_TM, IN_C), lambda bi, i: (bi, i, 0)),
                  pl.BlockSpec((1, TOK_HALO, IN_C), lambda bi, i: (bi, jnp.minimum((i + 1) * hb, nt * hb - 1), 0))]
        + [s for _, s in consts],
        out_specs=[tile] * 10,
        out_shape=[out] * 10,
        compiler_params=pltpu.CompilerParams(
            dimension_semantics=("arbitrary", "arbitrary"),
            vmem_limit_bytes=V7X_VMEM_LIMIT_BYTES),
        name="rwkv_prep",
        interpret=interpret,
    )(p, p, p, *[x for x, _ in consts])
    return r, v, kk, g, (lw_f, lw_b), (a_f, a_b), (k_f, k_b)


def _cd_out_kernel(yf_ref, yb_ref, r_ref, kf_ref, kb_ref, v_ref, g_ref, of_ref, ob_ref, z_ref, x_ref, gate_ref,
                   crk_ref, lnw_ref, lnb_ref, onorm_ref, wout_ref, o_ref):
    ones_c = _block_ones(C_DIM, C_HEAD)
    ones_d = _block_ones(D_DIM, D_HEAD_DIM)
    y = yf_ref[0] + yb_ref[0]
    mean = _seg_sum(y, ones_c) * (1.0 / C_HEAD)
    dev = y - mean
    var = _seg_sum(dev * dev, ones_c) * (1.0 / C_HEAD)
    yn = dev * lax.rsqrt(var + C_GN_EPS) * lnw_ref[...] + lnb_ref[...]
    bonus = _seg_sum(r_ref[0] * (kf_ref[0] + kb_ref[0]) * crk_ref[...], ones_c) * v_ref[0]
    out_c = (yn + bonus) * g_ref[0]
    o = of_ref[0] + ob_ref[0]
    ms = _seg_sum(o * o, ones_d) * (1.0 / D_HEAD_DIM)
    z = z_ref[0]
    out_d = o * lax.rsqrt(ms + EPS) * onorm_ref[...] * (z * jax.nn.sigmoid(z))
    y_cat = jnp.concatenate([out_c, out_d], axis=1).astype(BF16)
    o_ref[0] = x_ref[0] + gate_ref[0, 0] * jnp.dot(y_cat, wout_ref[...], preferred_element_type=F32)


def cd_out(xa, gate, y_f, y_b, r, k_f, k_b, v, g, o_f, o_b, z, c_r_k, c_ln_w, c_ln_b, d_o_norm, w_out, n_ctx,
           latent_only, interpret=False):
    b, t, d = xa.shape
    nt = t // TOK_TM
    ctx_tiles = n_ctx // TOK_TM
    t0 = ctx_tiles if latent_only else 0
    tile = pl.BlockSpec((1, TOK_TM, C_DIM), lambda bi, i: (bi, i + t0, 0))
    xtile = pl.BlockSpec((1, TOK_TM, d), lambda bi, i: (bi, i + t0, 0))

    def row(x):
        x2 = x.reshape(1, -1)
        return x2, pl.BlockSpec(x2.shape, lambda bi, i: (0, 0))

    consts = [row(c_r_k), row(c_ln_w), row(c_ln_b), row(jnp.tile(d_o_norm, D_HEADS))]
    return pl.pallas_call(
        _cd_out_kernel,
        grid=(b, nt - t0),
        in_specs=[tile] * 10 + [xtile,
                                pl.BlockSpec((1, 1, 1, d),
                                             lambda bi, i: (bi, jnp.where(i + t0 >= ctx_tiles, 1, 0), 0, 0))]
        + [s for _, s in consts] + [pl.BlockSpec(w_out.shape, lambda bi, i: (0, 0))],
        out_specs=pl.BlockSpec((1, TOK_TM, d), lambda bi, i: (bi, i, 0)),
        out_shape=jax.ShapeDtypeStruct((b, t - t0 * TOK_TM, d), F32),
        compiler_params=pltpu.CompilerParams(
            dimension_semantics=("arbitrary", "arbitrary"),
            vmem_limit_bytes=V7X_VMEM_LIMIT_BYTES),
        name="cd_out",
        interpret=interpret,
    )(y_f, y_b, r, k_f, k_b, v, g, o_f, o_b, z, xa, gate.reshape(b, 2, 1, d), *[x for x, _ in consts],
      w_out.astype(BF16))


def _gdn_qkv_kernel(p_ref, w_ref, o_ref, *, n_ctx):
    j = pl.program_id(1)
    x = p_ref[0]
    t = x.shape[0]
    idx = lax.broadcasted_iota(jnp.int32, (t, 1), 0)
    w = w_ref[...]
    y = x * w[D_CONV // 2:D_CONV // 2 + 1]
    for tap in range(D_CONV):
        off = tap - D_CONV // 2
        if off == 0:
            continue
        src = idx + off
        same_seq = (src >= 0) & (src < t) & ((src >= n_ctx) == (idx >= n_ctx))
        y = y + jnp.where(same_seq, pltpu.roll(x, (-off) % t, 0), 0.0) * w[tap:tap + 1]
    y = y * jax.nn.sigmoid(y)
    inv = lax.rsqrt(jnp.sum(y * y, axis=-1, keepdims=True) + 1e-6)
    factor = jnp.where(j < D_HEADS, inv * D_HEAD_DIM ** -0.5, jnp.where(j < 2 * D_HEADS, inv, 1.0))
    o_ref[0] = y * factor


def gdn_qkv(p, d_conv_w, n_ctx, interpret=False):
    b, t, _ = p.shape
    col0 = IN_C // D_HEAD_DIM
    assert IN_C % D_HEAD_DIM == 0
    n_blk = 3 * D_DIM // D_HEAD_DIM
    return pl.pallas_call(
        functools.partial(_gdn_qkv_kernel, n_ctx=n_ctx),
        grid=(b, n_blk),
        in_specs=[pl.BlockSpec((1, t, D_HEAD_DIM), lambda i, j: (i, 0, col0 + j)),
                  pl.BlockSpec((D_CONV, D_HEAD_DIM), lambda i, j: (0, j))],
        out_specs=pl.BlockSpec((1, t, D_HEAD_DIM), lambda i, j: (i, 0, j)),
        out_shape=jax.ShapeDtypeStruct((b, t, 3 * D_DIM), F32),
        compiler_params=pltpu.CompilerParams(
            dimension_semantics=("arbitrary", "arbitrary"),
            vmem_limit_bytes=V7X_VMEM_LIMIT_BYTES),
        name="gdn_qkv",
        interpret=interpret,
    )(p, d_conv_w)


def gdn_prep(p, n_ctx, d_conv_w, d_A_log, d_dt_bias):
    qkv = gdn_qkv(p, d_conv_w, n_ctx)
    offs = IN_C + np.cumsum((0,) + D_SIZES)
    z, bf, bb, af, ab = (p[..., offs[n]:offs[n + 1]] for n in range(1, len(D_SIZES)))
    betas = (jax.nn.sigmoid(bf), jax.nn.sigmoid(bb))
    gs = tuple(-jnp.exp(d_A_log[i]) * jax.nn.softplus(al + d_dt_bias[i]) for i, al in enumerate((af, ab)))
    return qkv, z, betas, gs


def mixer_cd(xa, gate, p, n_ctx, c_mu_prev, c_mu_next, c_w0, c_w2, c_a0, c_a2, c_g2, c_k_k, c_k_a, c_r_k,
             c_ln_w, c_ln_b, d_conv_w, d_A_log, d_dt_bias, d_o_norm, w_out, latent_only):
    r, v, kk, g, lw, a, ks = rwkv_prep(p, n_ctx, c_mu_prev, c_mu_next, c_w0, c_w2, c_a0, c_a2, c_g2, c_k_k, c_k_a)
    y_f, y_b = rwkv_chunked(r, v, kk, lw[0], a[0], ks[0], lw[1], a[1], ks[1], n_ctx)
    qkv, z, beta, gd = gdn_prep(p, n_ctx, d_conv_w, d_A_log, d_dt_bias)
    o_f, o_b = gdn_chunked(qkv, beta[0], gd[0], beta[1], gd[1], n_ctx)
    return cd_out(xa, gate, y_f, y_b, r, ks[0], ks[1], v, g, o_f, o_b, z, c_r_k, c_ln_w, c_ln_b, d_o_norm, w_out,
                  n_ctx, latent_only)


FFN_TM = 256
FFN_HALO = 8
FFN_FC = 256
FFN_VMEM_LIMIT_BYTES = 56 * 1024 * 1024


def _modulated(x, shift, scale):
    y = x * lax.rsqrt(jnp.mean(x * x, axis=-1, keepdims=True) + EPS)
    return y * (1.0 + scale) + shift


def _ffn_kernel(xp_ref, x_ref, xn_ref, shift_ref, scale_ref, gate_ref, wup_ref, cw_ref, cb_ref, wdn_ref, o_ref, acc_ref,
                *, n_tiles, ctx_tiles):
    i = pl.program_id(1)
    tm, halo = FFN_TM, FFN_HALO
    rows = tm + 2 * halo
    x = x_ref[0]
    xe = jnp.concatenate([xp_ref[0], x, xn_ref[0]], axis=0)
    h = _modulated(xe, shift_ref[0, 0], scale_ref[0, 0])
    r = lax.broadcasted_iota(jnp.int32, (rows, 1), 0)
    seq_first, seq_last = _seq_edges(i, n_tiles, ctx_tiles)
    outside = ((r == halo - 1) & seq_first) | ((r == halo + tm) & seq_last)
    h = jnp.where(outside, 0.0, h).astype(BF16)
    acc_ref[...] = jnp.zeros_like(acc_ref)

    def conv(u, c0):
        w = cw_ref[:, c0:c0 + FFN_FC]
        y = (pltpu.roll(u, 1, 0) * w[0:1] + u * w[1:2] + pltpu.roll(u, rows - 1, 0) * w[2:3]
             + cb_ref[:, c0:c0 + FFN_FC])
        return y[halo:halo + tm]

    def up(c):
        cv, cg = c * FFN_FC, D_FF + c * FFN_FC
        return (jnp.dot(h, wup_ref[:, cv:cv + FFN_FC], preferred_element_type=F32),
                jnp.dot(h, wup_ref[:, cg:cg + FFN_FC], preferred_element_type=F32))

    def down(act, c):
        acc_ref[...] += jnp.dot(act, wdn_ref[c * FFN_FC:(c + 1) * FFN_FC, :], preferred_element_type=F32)

    n_chunks = D_FF // FFN_FC
    u_next = up(0)
    act_prev = None
    for c in range(n_chunks):
        u_val, u_gat = u_next
        if c + 1 < n_chunks:
            u_next = up(c + 1)
        if act_prev is not None:
            down(act_prev, c - 1)
        val = conv(u_val, c * FFN_FC)
        gat = conv(u_gat, D_FF + c * FFN_FC)
        act_prev = (gat * jax.nn.sigmoid(gat) * val).astype(BF16)
    down(act_prev, n_chunks - 1)
    o_ref[0] = x + gate_ref[0, 0] * acc_ref[...]


def ffn_fused(x, shift, scale, gate, w_up, conv_w, conv_b, w_down, n_ctx, interpret=False):
    b, t, d = x.shape
    nt = t // FFN_TM
    ctx_tiles = n_ctx // FFN_TM
    hb = FFN_TM // FFN_HALO
    f2 = w_up.shape[1]
    mod = pl.BlockSpec((1, 1, 1, d), lambda bi, i: (bi, jnp.where(i >= ctx_tiles, 1, 0), 0, 0))

    def full(shp):
        return pl.BlockSpec(shp, lambda bi, i: (0,) * len(shp))

    def mods(m):
        return m.reshape(b, 2, 1, d)

    return pl.pallas_call(
        functools.partial(_ffn_kernel, n_tiles=nt, ctx_tiles=ctx_tiles),
        grid=(b, nt),
        in_specs=[pl.BlockSpec((1, FFN_HALO, d), lambda bi, i: (bi, jnp.maximum(i * hb - 1, 0), 0)),
                  pl.BlockSpec((1, FFN_TM, d), lambda bi, i: (bi, i, 0)),
                  pl.BlockSpec((1, FFN_HALO, d), lambda bi, i: (bi, jnp.minimum((i + 1) * hb, nt * hb - 1), 0)),
                  mod, mod, mod,
                  full((d, f2)), full((FFN_CONV, f2)), full((1, f2)), full((f2 // 2, d))],
        out_specs=pl.BlockSpec((1, FFN_TM, d), lambda bi, i: (bi, i, 0)),
        out_shape=jax.ShapeDtypeStruct((b, t, d), F32),
        scratch_shapes=[pltpu.VMEM((FFN_TM, d), F32)],
        compiler_params=pltpu.CompilerParams(
            dimension_semantics=("arbitrary", "arbitrary"),
            vmem_limit_bytes=FFN_VMEM_LIMIT_BYTES),
        name="conv_ffn",
        interpret=interpret,
    )(x, x, x, mods(shift), mods(scale), mods(gate), w_up.astype(BF16), conv_w, conv_b.reshape(1, f2),
      w_down.astype(BF16))


def kernel(x, c, ctx, c_ctx, ada_w, ada_b, ffn_w_up, ffn_conv_w, ffn_conv_b, ffn_w_down,
           ab_w_in, ab_w_out, a_q_norm, a_k_norm, a_sink, b_cq_norm, b_ckv_norm, b_w_uq, b_w_uk, b_w_uv,
           b_qn_norm, b_qr_norm, b_kn_norm, b_kr_norm, cd_w_in, cd_w_out, c_mu_prev, c_mu_next, c_w0, c_w2,
           c_a0, c_a2, c_g2, c_k_k, c_k_a, c_r_k, c_ln_w, c_ln_b, d_conv_w, d_A_log, d_dt_bias, d_o_norm):
    bsz, seq = x.shape[:2]
    n_ctx = ctx.shape[1]
    rows = seq // GRID_W
    zeros = jnp.zeros((n_ctx,), jnp.int32)
    row = jnp.concatenate([zeros, jnp.repeat(jnp.arange(rows, dtype=jnp.int32), GRID_W)])
    col = jnp.concatenate([zeros, jnp.tile(jnp.arange(GRID_W, dtype=jnp.int32), rows)])
    is_ctx = (jnp.arange(n_ctx + seq) < n_ctx)[None, :, None]
    silu_c = jax.nn.silu(c)
    silu_cc = jax.nn.silu(c_ctx)
    xa = jnp.concatenate([ctx, x], axis=1)
    for l in range(DEPTH):
        last = l == DEPTH - 1
        i = l // 2
        mod_l = jnp.split(silu_c @ ada_w[l] + ada_b[l], N_MOD, axis=-1)
        mod_c = jnp.split(silu_cc @ ada_w[l] + ada_b[l], N_MOD, axis=-1)
        mods = [jnp.stack([jnp.broadcast_to(mc, ml.shape), ml], axis=1) for mc, ml in zip(mod_c, mod_l)]

        def per_token(m):
            return jnp.where(is_ctx, m[:, 0:1, :], m[:, 1:2, :])

        w_in = ab_w_in[i] if l % 2 == 0 else cd_w_in[i]
        p = mod_mm(xa, mods[0], mods[1], w_in, n_ctx)
        if l % 2 == 0:
            y = mixer_ab(p, n_ctx, row, col, a_q_norm[i], a_k_norm[i],
                         a_sink[i], b_cq_norm[i], b_ckv_norm[i], b_w_uq[i], b_w_uk[i], b_w_uv[i],
                         b_qn_norm[i], b_qr_norm[i], b_kn_norm[i], b_kr_norm[i], not last)
            xa = xa + per_token(mods[2]) * _mm(y, ab_w_out[i])
            if last:
                xa = xa[:, n_ctx:]
        else:
            xa = mixer_cd(xa, mods[2], p, n_ctx, c_mu_prev[i], c_mu_next[i], c_w0[i],
                          c_w2[i], c_a0[i], c_a2[i], c_g2[i], c_k_k[i], c_k_a[i], c_r_k[i], c_ln_w[i],
                          c_ln_b[i], d_conv_w[i], d_A_log[i], d_dt_bias[i], d_o_norm[i], cd_w_out[i], last)
        xa = ffn_fused(xa, mods[3], mods[4], mods[5], ffn_w_up[l], ffn_conv_w[l], ffn_conv_b[l], ffn_w_down[l],
                       0 if last else n_ctx)
    return xa
```

```python
import functools

import jax
import jax.numpy as jnp
from jax import lax
import numpy as np
from jax.experimental import pallas as pl
from jax.experimental.pallas import tpu as pltpu

D_MODEL = 1024
DEPTH = 2
GRID_W = 64
N_MOD = 6
EPS = 1e-6
ROPE_THETA = 10000.0
NEG_INF = -1e30

A_HEADS = 8
A_KV_HEADS = 2
A_HEAD_DIM = 64
WINDOW = 128
B_HEADS = 8
B_Q_RANK = 256
B_KV_RANK = 256
B_NOPE = 64
B_ROPE = 32
B_V_DIM = 64
C_HEADS = 8
C_HEAD = 64
C_DIM = C_HEADS * C_HEAD
C_DECAY_LORA = 64
C_AAA_LORA = 64
C_GATE_LORA = 128
C_GN_EPS = 64e-5
D_HEADS = 4
D_HEAD_DIM = 128
D_DIM = D_HEADS * D_HEAD_DIM
D_CONV = 5
D_CHUNK = 64
D_FF = 2816
FFN_CONV = 3

AB_SIZES = (A_HEADS * A_HEAD_DIM, A_KV_HEADS * A_HEAD_DIM, A_KV_HEADS * A_HEAD_DIM, B_Q_RANK, B_KV_RANK, B_ROPE)
C_SIZES = (C_DIM, C_DIM, C_DIM, C_DECAY_LORA, C_DECAY_LORA, C_AAA_LORA, C_AAA_LORA, C_GATE_LORA)
IN_C = sum(C_SIZES)
D_SIZES = (3 * D_DIM, D_DIM, D_HEADS, D_HEADS, D_HEADS, D_HEADS)

F32 = jnp.float32
BF16 = jnp.bfloat16

V7X_VMEM_LIMIT_BYTES = 48 * 1024 * 1024
LANE = 128
MXU_N = 256

NN = ((1,), (0,))
NT = ((1,), (1,))
TN = ((0,), (0,))


def _mm_kernel(a_ref, b_ref, o_ref):
    o_ref[...] = jnp.dot(a_ref[...].astype(BF16), b_ref[...], preferred_element_type=F32)


def _pick_tile(n, candidates):
    for c in candidates:
        if n % c == 0:
            return c
    raise ValueError(f"no tile for {n}")


def _mm(a, w):
    lead = a.shape[:-1]
    k = a.shape[-1]
    n = w.shape[-1]
    a2 = a.reshape(-1, k)
    m = a2.shape[0]
    n_pad = -(-n // MXU_N) * MXU_N
    wb = w.astype(BF16)
    if n_pad != n:
        wb = jnp.pad(wb, ((0, 0), (0, n_pad - n)))
    tm = _pick_tile(m, (1024, 512, 256, 128, 8))
    tn = _pick_tile(n_pad, (1024, 768, 512, 256))
    out = pl.pallas_call(
        _mm_kernel,
        grid=(m // tm, n_pad // tn),
        in_specs=[pl.BlockSpec((tm, k), lambda i, j: (i, 0)),
                  pl.BlockSpec((k, tn), lambda i, j: (0, j))],
        out_specs=pl.BlockSpec((tm, tn), lambda i, j: (i, j)),
        out_shape=jax.ShapeDtypeStruct((m, n_pad), F32),
        compiler_params=pltpu.CompilerParams(
            dimension_semantics=("arbitrary", "arbitrary"),
            vmem_limit_bytes=V7X_VMEM_LIMIT_BYTES),
        name="mm",
    )(a2, wb)
    if n_pad != n:
        out = out[:, :n]
    return out.reshape(*lead, out.shape[-1])


MOD_TM = 768


def _mod_mm_kernel(x_ref, shift_ref, scale_ref, w_ref, o_ref, h_ref, *, n_ctx):
    @pl.when(pl.program_id(2) == 0)
    def _():
        x = x_ref[0]
        tok = pl.program_id(1) * MOD_TM + lax.broadcasted_iota(jnp.int32, (MOD_TM, 1), 0)
        is_ctx = tok < n_ctx
        shift = jnp.where(is_ctx, shift_ref[0, 0:1], shift_ref[0, 1:2])
        scale = jnp.where(is_ctx, scale_ref[0, 0:1], scale_ref[0, 1:2])
        y = x * lax.rsqrt(jnp.mean(x * x, axis=-1, keepdims=True) + EPS)
        h_ref[...] = (y * (1.0 + scale) + shift).astype(BF16)

    o_ref[0] = jnp.dot(h_ref[...], w_ref[...], preferred_element_type=F32)


def mod_mm(x, shift, scale, w, n_ctx):
    b, t, d = x.shape
    n = w.shape[-1]
    n_pad = -(-n // MXU_N) * MXU_N
    wb = jnp.pad(w.astype(BF16), ((0, 0), (0, n_pad - n)))
    tn = _pick_tile(n_pad, (1024, 768, 512, 256))
    mod = pl.BlockSpec((1, 2, d), lambda bi, i, j: (bi, 0, 0))
    return pl.pallas_call(
        functools.partial(_mod_mm_kernel, n_ctx=n_ctx),
        grid=(b, t // MOD_TM, n_pad // tn),
        in_specs=[pl.BlockSpec((1, MOD_TM, d), lambda bi, i, j: (bi, i, 0)), mod, mod,
                  pl.BlockSpec((d, tn), lambda bi, i, j: (0, j))],
        out_specs=pl.BlockSpec((1, MOD_TM, tn), lambda bi, i, j: (bi, i, j)),
        out_shape=jax.ShapeDtypeStruct((b, t, n_pad), F32),
        scratch_shapes=[pltpu.VMEM((MOD_TM, d), BF16)],
        compiler_params=pltpu.CompilerParams(
            dimension_semantics=("arbitrary", "arbitrary", "arbitrary"),
            vmem_limit_bytes=V7X_VMEM_LIMIT_BYTES),
        name="mod_mm",
    )(x, shift, scale, wb)


def _dot(a, b, dims, passes=1):
    def dg(x, y):
        return lax.dot_general(x, y, (dims, ((), ())), preferred_element_type=F32)

    ah, bh = a.astype(BF16), b.astype(BF16)
    if passes == 1:
        return dg(ah, bh)
    al = (a - ah.astype(F32)).astype(BF16)
    bl = (b - bh.astype(F32)).astype(BF16)
    return dg(ah, bh) + (dg(ah, bl) + dg(al, bh))


def _split3(x):
    hi = x.astype(BF16)
    r1 = x - hi.astype(F32)
    mid = r1.astype(BF16)
    lo = (r1 - mid.astype(F32)).astype(BF16)
    return hi, mid, lo


def _dot_ones(ones, x, dims, ones_first=True):
    def dg(piece):
        a, b = (ones, piece) if ones_first else (piece, ones)
        return lax.dot_general(a, b, (dims, ((), ())), preferred_element_type=F32)

    hi, mid, lo = _split3(x)
    return dg(hi) + (dg(mid) + dg(lo))


def _heads_major(z):
    return jnp.swapaxes(z, 1, 2)


def _softmax_pv(s, v, sink):
    m = jnp.max(s, axis=-1, keepdims=True)
    if sink is not None:
        m = jnp.maximum(m, sink)
    p = jnp.exp(s - m)
    den = jnp.sum(p, axis=-1, keepdims=True)
    if sink is not None:
        den = den + jnp.exp(sink - m)
    return _dot(p, v, NN) / den


ATTN_HEADS_PER_STEP = 2


def _attn_full_kernel(q_ref, k_ref, q2_ref, k2_ref, v_ref, sink_ref, o_ref, *, scale, use_sink, use_second,
                      kv_heads):
    outs = []
    for hh in range(ATTN_HEADS_PER_STEP):
        kh = hh if kv_heads == ATTN_HEADS_PER_STEP else 0
        s = _dot(q_ref[0, hh] * scale, k_ref[0, kh], NT)
        if use_second:
            s = s + _dot(q2_ref[0, hh] * scale, k2_ref[0, 0], NT)
        sink = sink_ref[hh] if use_sink else None
        outs.append(_softmax_pv(s, v_ref[0, kh], sink))
    o_ref[0] = jnp.concatenate(outs, axis=1)


def attn_full(q, k, v, scale, q_start, n_q, n_keys, sink=None, second=None):
    b, h, _, d = q.shape
    hk = k.shape[1]
    dv = v.shape[-1]
    g = h // hk
    hp = ATTN_HEADS_PER_STEP
    assert hp * dv == LANE and h % hp == 0 and (g == 1 or g % hp == 0)
    kv_heads = hp if g == 1 else 1
    tq = min(n_q, 256)
    q0 = q_start // tq
    use_sink = sink is not None
    use_second = second is not None
    sink_arr = (sink if use_sink else jnp.zeros((h,), F32)).astype(F32).reshape(h, 1, 1)
    q2, k2 = second if use_second else (q, k)
    d2 = q2.shape[-1]

    def kv_spec(width):
        return pl.BlockSpec((1, kv_heads, n_keys, width), lambda i, j, t: (i, (j * hp) // (g * kv_heads), 0, 0))

    k2_spec = pl.BlockSpec((1, 1, n_keys, d2), lambda i, j, t: (i, 0, 0, 0)) if use_second else kv_spec(d2)
    return pl.pallas_call(
        functools.partial(_attn_full_kernel, scale=scale, use_sink=use_sink, use_second=use_second,
                          kv_heads=kv_heads),
        grid=(b, h // hp, n_q // tq),
        in_specs=[pl.BlockSpec((1, hp, tq, d), lambda i, j, t: (i, j, t + q0, 0)),
                  kv_spec(d),
                  pl.BlockSpec((1, hp, tq, d2), lambda i, j, t: (i, j, t + q0, 0)),
                  k2_spec,
                  kv_spec(dv),
                  pl.BlockSpec((hp, 1, 1), lambda i, j, t: (j, 0, 0))],
        out_specs=pl.BlockSpec((1, tq, hp * dv), lambda i, j, t: (i, t, j)),
        out_shape=jax.ShapeDtypeStruct((b, n_q, h * dv), F32),
        compiler_params=pltpu.CompilerParams(
            dimension_semantics=("arbitrary", "arbitrary", "arbitrary"),
            vmem_limit_bytes=V7X_VMEM_LIMIT_BYTES),
        name="attn_full",
    )(q, k, q2, k2, v, sink_arr)


def _attn_window_kernel(q_ref, kp_ref, k0_ref, kn_ref, kc_ref, vp_ref, v0_ref, vn_ref, vc_ref, sink_ref, o_ref,
                        *, scale, group, n_blocks):
    n = pl.program_id(2)
    w = WINDOW
    d = q_ref.shape[-1]
    q = q_ref[0].reshape(group * w, d) * scale
    keys = jnp.concatenate([kp_ref[0, 0], k0_ref[0, 0], kn_ref[0, 0], kc_ref[0, 0]], axis=0)
    vals = jnp.concatenate([vp_ref[0, 0], v0_ref[0, 0], vn_ref[0, 0], vc_ref[0, 0]], axis=0)
    s = _dot(q, keys, NT)
    nk = keys.shape[0]
    qi = lax.broadcasted_iota(jnp.int32, (group * w, nk), 0) % w
    kj = lax.broadcasted_iota(jnp.int32, (group * w, nk), 1)
    rel = qi + w - kj
    band_ok = (jnp.abs(rel) <= w) & ((kj >= w) | (n > 0)) & ((kj < 2 * w) | (n < n_blocks - 1))
    s = jnp.where((kj >= 3 * w) | band_ok, s, NEG_INF)
    sink = jnp.concatenate([jnp.broadcast_to(sink_ref[0, hh], (w, 1)) for hh in range(group)], axis=0)
    o = _softmax_pv(s, vals, sink)
    o_ref[0] = o.reshape(group, w, o.shape[-1])


def attn_window(q, k, v, sink, n_ctx):
    b, h, t, d = q.shape
    hk = k.shape[1]
    g = h // hk
    off = n_ctx // WINDOW
    nb = (t - n_ctx) // WINDOW
    sink_arr = sink.astype(F32).reshape(hk, g, 1, 1)

    def blk(f):
        return pl.BlockSpec((1, 1, WINDOW, d), f)

    prev = blk(lambda i, j, n: (i, j, jnp.maximum(n - 1, 0) + off, 0))
    own = blk(lambda i, j, n: (i, j, n + off, 0))
    nxt = blk(lambda i, j, n: (i, j, jnp.minimum(n + 1, nb - 1) + off, 0))
    ctx = pl.BlockSpec((1, 1, n_ctx, d), lambda i, j, n: (i, j, 0, 0))
    return pl.pallas_call(
        functools.partial(_attn_window_kernel, scale=d ** -0.5, group=g, n_blocks=nb),
        grid=(b, hk, nb),
        in_specs=[pl.BlockSpec((1, g, WINDOW, d), lambda i, j, n: (i, j, n + off, 0)),
                  prev, own, nxt, ctx, prev, own, nxt, ctx,
                  pl.BlockSpec((1, g, 1, 1), lambda i, j, n: (j, 0, 0, 0))],
        out_specs=pl.BlockSpec((1, g, WINDOW, d), lambda i, j, n: (i, j, n, 0)),
        out_shape=jax.ShapeDtypeStruct((b, h, t - n_ctx, d), F32),
        compiler_params=pltpu.CompilerParams(
            dimension_semantics=("arbitrary", "arbitrary", "arbitrary"),
            vmem_limit_bytes=V7X_VMEM_LIMIT_BYTES),
        name="attn_window",
    )(q, k, k, k, k, v, v, v, v, sink_arr)


def _rope_tables(row, col, head_dim, width):
    q = head_dim // 4
    inv = jnp.power(ROPE_THETA, -jnp.arange(q, dtype=F32) / q)
    ang_r = row.astype(F32)[:, None] * inv[None, :]
    ang_c = col.astype(F32)[:, None] * inv[None, :]
    cos = jnp.concatenate([jnp.cos(ang_r), jnp.cos(ang_r), jnp.cos(ang_c), jnp.cos(ang_c)], axis=-1)
    sin = jnp.concatenate([-jnp.sin(ang_r), jnp.sin(ang_r), -jnp.sin(ang_c), jnp.sin(ang_c)], axis=-1)
    reps = width // head_dim
    return jnp.tile(cos, (1, reps)), jnp.tile(sin, (1, reps))


def _rope(x, cos, sin, head_dim):
    q = head_dim // 4
    w = x.shape[-1]
    lane = lax.broadcasted_iota(jnp.int32, x.shape, 1)
    swapped = jnp.where(lane % (2 * q) < q, pltpu.roll(x, w - q, 1), pltpu.roll(x, q, 1))
    return x * cos + swapped * sin


def _head_rms(x, ones_bd, head_dim, gain):
    return x * lax.rsqrt(_seg_sum(x * x, ones_bd) * (1.0 / head_dim) + EPS) * gain


def _row_rms(x, n, gain):
    return x * lax.rsqrt(jnp.sum(x * x, axis=-1, keepdims=True) * (1.0 / n) + EPS) * gain


def _ab_prep_kernel(p_ref, c64_ref, s64_ref, c32_ref, s32_ref, gqa_ref, gka_ref, gcq_ref, gckv_ref, gqn_ref, gqr_ref,
                    gkn_ref, gkr_ref, wuq_ref, wuk_ref, wuv_ref,
                    qa_ref, ka_ref, qn_ref, qr_ref, kn_ref, vb_ref, kr_ref):
    p = p_ref[0]
    offs = np.cumsum((0,) + AB_SIZES)
    qa, ka, cq, ckv = (p[:, offs[n]:offs[n + 1]] for n in (0, 1, 3, 4))
    kr = p[:, offs[5]:offs[5] + LANE]
    c64, s64, c32, s32 = c64_ref[...], s64_ref[...], c32_ref[...], s32_ref[...]
    ones_q = _block_ones(A_HEADS * A_HEAD_DIM, A_HEAD_DIM)
    ones_k = _block_ones(A_KV_HEADS * A_HEAD_DIM, A_HEAD_DIM)
    ones_r = _block_ones(B_HEADS * B_ROPE, B_ROPE)
    ka_w = A_KV_HEADS * A_HEAD_DIM
    qa_ref[0] = _rope(_head_rms(qa, ones_q, A_HEAD_DIM, gqa_ref[...]), c64, s64, A_HEAD_DIM)
    ka_ref[0] = _rope(_head_rms(ka, ones_k, A_HEAD_DIM, gka_ref[...]), c64[:, :ka_w], s64[:, :ka_w], A_HEAD_DIM)
    qb = jnp.dot(_row_rms(cq, B_Q_RANK, gcq_ref[...]).astype(BF16), wuq_ref[...], preferred_element_type=F32)
    n_nope = B_HEADS * B_NOPE
    qn_ref[0] = _head_rms(qb[:, :n_nope], ones_q, B_NOPE, gqn_ref[...])
    qr_ref[0] = _rope(_head_rms(qb[:, n_nope:], ones_r, B_ROPE, gqr_ref[...]), c32, s32, B_ROPE)
    ckv_n = _row_rms(ckv, B_KV_RANK, gckv_ref[...]).astype(BF16)
    kn_ref[0] = _head_rms(jnp.dot(ckv_n, wuk_ref[...], preferred_element_type=F32), ones_q, B_NOPE, gkn_ref[...])
    vb_ref[0] = jnp.dot(ckv_n, wuv_ref[...], preferred_element_type=F32)
    kr_ref[0] = _rope(_row_rms(kr, B_ROPE, gkr_ref[...]), c32[:, :LANE], s32[:, :LANE], B_ROPE)


def ab_prep(p, row, col, a_q_norm, a_k_norm, b_cq_norm, b_ckv_norm, b_w_uq, b_w_uk, b_w_uv,
            b_qn_norm, b_qr_norm, b_kn_norm, b_kr_norm, interpret=False):
    b, t, pw = p.shape
    assert pw >= sum(AB_SIZES[:5]) + LANE and A_HEAD_DIM == B_NOPE
    c64, s64 = _rope_tables(row, col, A_HEAD_DIM, A_HEADS * A_HEAD_DIM)
    c32, s32 = _rope_tables(row, col, B_ROPE, B_HEADS * B_ROPE)
    wq = b_w_uq.reshape(B_Q_RANK, B_HEADS, B_NOPE + B_ROPE)
    wq = jnp.concatenate([wq[..., :B_NOPE].reshape(B_Q_RANK, -1), wq[..., B_NOPE:].reshape(B_Q_RANK, -1)], axis=1)

    def row_c(g, reps, width=None):
        x = jnp.tile(g, reps).reshape(1, -1)
        if width is not None:
            x = jnp.pad(x, ((0, 0), (0, width - x.shape[1])))
        return x

    consts = [row_c(a_q_norm, A_HEADS), row_c(a_k_norm, A_KV_HEADS), row_c(b_cq_norm, 1), row_c(b_ckv_norm, 1),
              row_c(b_qn_norm, B_HEADS), row_c(b_qr_norm, B_HEADS), row_c(b_kn_norm, B_HEADS),
              row_c(b_kr_norm, 1, LANE), wq.astype(BF16), b_w_uk.astype(BF16), b_w_uv.astype(BF16)]
    tabs = [c64, s64, c32, s32]
    widths = (A_HEADS * A_HEAD_DIM, A_KV_HEADS * A_HEAD_DIM, B_HEADS * B_NOPE, B_HEADS * B_ROPE, B_HEADS * B_NOPE,
              B_HEADS * B_V_DIM, LANE)
    return pl.pallas_call(
        _ab_prep_kernel,
        grid=(b, t // TOK_TM),
        in_specs=[pl.BlockSpec((1, TOK_TM, pw), lambda bi, i: (bi, i, 0))]
        + [pl.BlockSpec((TOK_TM, x.shape[1]), lambda bi, i: (i, 0)) for x in tabs]
        + [pl.BlockSpec(x.shape, lambda bi, i: (0, 0)) for x in consts],
        out_specs=[pl.BlockSpec((1, TOK_TM, w), lambda bi, i: (bi, i, 0)) for w in widths],
        out_shape=[jax.ShapeDtypeStruct((b, t, w), F32) for w in widths],
        compiler_params=pltpu.CompilerParams(
            dimension_semantics=("arbitrary", "arbitrary"),
            vmem_limit_bytes=V7X_VMEM_LIMIT_BYTES),
        name="ab_prep",
        interpret=interpret,
    )(p, *tabs, *consts)


def mixer_ab(p, n_ctx, row, col, a_q_norm, a_k_norm, a_sink, b_cq_norm, b_ckv_norm, b_w_uq, b_w_uk,
             b_w_uv, b_qn_norm, b_qr_norm, b_kn_norm, b_kr_norm, ctx_out):
    b, t = p.shape[:2]
    qa, ka, qn, qr, kn, vb, kr = ab_prep(p, row, col, a_q_norm, a_k_norm, b_cq_norm, b_ckv_norm, b_w_uq, b_w_uk,
                                         b_w_uv, b_qn_norm, b_qr_norm, b_kn_norm, b_kr_norm)
    va_off = AB_SIZES[0] + AB_SIZES[1]
    va = p[..., va_off:va_off + AB_SIZES[2]]

    def hm(z, heads):
        return _heads_major(z.reshape(b, t, heads, -1))

    qa_t, ka_t, va_t = hm(qa, A_HEADS), hm(ka, A_KV_HEADS), hm(va, A_KV_HEADS)
    qn_t, kn_t, vb_t = hm(qn, B_HEADS), hm(kn, B_HEADS), hm(vb, B_HEADS)
    rope_part = (hm(qr, B_HEADS), hm(kr[..., :B_ROPE], 1))
    b_scale = (B_NOPE + B_ROPE) ** -0.5

    o_a = attn_window(qa_t, ka_t, va_t, a_sink, n_ctx)
    o_b = attn_full(qn_t, kn_t, vb_t, b_scale, n_ctx, t - n_ctx, t, second=rope_part)
    if ctx_out:
        o_a_c = attn_full(qa_t, ka_t, va_t, A_HEAD_DIM ** -0.5, 0, n_ctx, n_ctx, sink=a_sink)
        o_b_c = attn_full(qn_t, kn_t, vb_t, b_scale, 0, n_ctx, n_ctx, second=rope_part)
    else:
        o_a_c = jnp.zeros((b, n_ctx, A_HEADS * A_HEAD_DIM), F32)
        o_b_c = jnp.zeros((b, n_ctx, B_HEADS * B_V_DIM), F32)
    o_a = jnp.concatenate([o_a_c, _heads_major(o_a).reshape(b, t - n_ctx, -1)], axis=1)
    o_b = jnp.concatenate([o_b_c, o_b], axis=1)
    return jnp.concatenate([o_a, o_b], axis=-1)


RW_CHUNK = 64
HEAD_PAIR = 2 * C_HEAD


def _unit_lower_inverse(xs, eye):
    ps = [eye + x for x in xs]
    xps = list(xs)
    for _ in range(int(np.log2(RW_CHUNK)) - 1):
        xps = [_dot(xp, xp, NN, passes=3) for xp in xps]
        ps = [p + _dot(p, xp, NN, passes=3) for p, xp in zip(ps, xps)]
    return ps


def _chunk_maps(nc, nctx):
    def fwd(s):
        return s

    def bwd(s):
        return jnp.where(s < nctx, nctx - 1 - s, nc + nctx - 1 - s)

    return fwd, bwd


def _rwkv_kernel(rf_ref, vf_ref, kkf_ref, lwf_ref, af_ref, kf_ref,
                 rb_ref, vb_ref, kkb_ref, lwb_ref, ab_ref, kb_ref, yf_ref, yb_ref, st_ref):
    @pl.when(pl.program_id(1) == 0)
    def _():
        st_ref[...] = jnp.zeros_like(st_ref)

    c_len = RW_CHUNK
    ii = lax.broadcasted_iota(jnp.int32, (c_len, c_len), 0)
    jj = lax.broadcasted_iota(jnp.int32, (c_len, c_len), 1)
    r2 = lax.broadcasted_iota(jnp.int32, (HEAD_PAIR, HEAD_PAIR), 0)
    c2 = lax.broadcasted_iota(jnp.int32, (HEAD_PAIR, HEAD_PAIR), 1)
    dlt = (r2 % c_len) - (c2 % c_len)
    eye = jnp.where(r2 == c2, 1.0, 0.0).astype(F32)
    lane = lax.broadcasted_iota(jnp.int32, (c_len, HEAD_PAIR), 1)
    m0 = lane < C_HEAD

    def stack_heads(x):
        return jnp.concatenate([jnp.where(m0, x, 0.0), jnp.where(m0, 0.0, x)], axis=0)

    n_pairs = C_DIM // HEAD_PAIR
    ch = []
    for d, refs in enumerate(((rf_ref, vf_ref, kkf_ref, lwf_ref, af_ref, kf_ref),
                              (rb_ref, vb_ref, kkb_ref, lwb_ref, ab_ref, kb_ref))):
        sign = 1 - 2 * d
        r_all, v_all, kk_all, lw_all, a_all, kd_all = (ref[0] for ref in refs)
        tri = jnp.where((ii - jj) * sign >= 0, 1.0, 0.0).astype(BF16)
        cum_all = _dot_ones(tri, lw_all, NN)
        tot_all = jnp.sum(lw_all, axis=0, keepdims=True)
        for p in range(n_pairs):
            sl = slice(p * HEAD_PAIR, (p + 1) * HEAD_PAIR)
            ch.append(dict(strict=dlt * sign > 0, incl=dlt * sign >= 0, st=st_ref[d, p],
                           lw=lw_all[:, sl], cum=cum_all[:, sl], tot=tot_all[:, sl], kk=kk_all[:, sl],
                           kd=kd_all[:, sl], bb=kk_all[:, sl] * a_all[:, sl], r=r_all[:, sl], v=v_all[:, sl]))
    for c in ch:
        e_neg = jnp.exp(-c['cum'])
        e_end = jnp.exp(c['tot'] - c['cum'])
        c['abar'] = stack_heads(-c['kk'] * jnp.exp(c['cum'] - c['lw']))
        c['rbar'] = stack_heads(c['r'] * jnp.exp(c['cum']))
        c['ktil'] = stack_heads(c['kd'] * e_neg)
        c['btil'] = stack_heads(c['bb'] * e_neg)
        c['khat'] = stack_heads(c['kd'] * e_end)
        c['bhat'] = stack_heads(c['bb'] * e_end)
        c['vs'] = stack_heads(c['v'])
    g = [_dot(jnp.concatenate([c['abar'], c['rbar']], axis=0), jnp.concatenate([c['ktil'], c['btil']], axis=0), NT)
         for c in ch]
    a_ak = [jnp.where(c['strict'], x[:HEAD_PAIR, :HEAD_PAIR], 0.0) for c, x in zip(ch, g)]
    a_ab = [jnp.where(c['strict'], x[:HEAD_PAIR, HEAD_PAIR:], 0.0) for c, x in zip(ch, g)]
    a_rk = [jnp.where(c['incl'], x[HEAD_PAIR:, :HEAD_PAIR], 0.0) for c, x in zip(ch, g)]
    a_rb = [jnp.where(c['incl'], x[HEAD_PAIR:, HEAD_PAIR:], 0.0) for c, x in zip(ch, g)]
    tinv = _unit_lower_inverse(a_ab, eye)
    n = range(len(ch))
    wm = [_dot(tinv[i], ch[i]['abar'], NN) for i in n]
    akv = [_dot(a_ak[i], ch[i]['vs'], NN) for i in n]
    u0 = [_dot(tinv[i], akv[i], NN) for i in n]
    u = [_dot(wm[i], ch[i]['st'], NT) + u0[i] for i in n]
    ys = [_dot(ch[i]['rbar'], ch[i]['st'], NT) + _dot(a_rk[i], ch[i]['vs'], NN) + _dot(a_rb[i], u[i], NN) for i in n]
    st_new = [ch[i]['st'] * jnp.exp(ch[i]['tot']) + _dot(ch[i]['vs'], ch[i]['khat'], TN) + _dot(u[i], ch[i]['bhat'], TN)
              for i in n]
    for d, y_ref in enumerate((yf_ref, yb_ref)):
        y_ref[0] = jnp.concatenate([ys[d * n_pairs + p][:c_len] + ys[d * n_pairs + p][c_len:] for p in range(n_pairs)],
                                   axis=1)
        for p in range(n_pairs):
            st_ref[d, p] = st_new[d * n_pairs + p]


def rwkv_chunked(r, v, kk, lw_f, a_f, k_f, lw_b, a_b, k_b, n_ctx, interpret=False):
    b, t, cd = r.shape
    nc = t // RW_CHUNK
    fwd, bwd = _chunk_maps(nc, n_ctx // RW_CHUNK)
    sf = pl.BlockSpec((1, RW_CHUNK, cd), lambda i, s: (i, fwd(s), 0))
    sb = pl.BlockSpec((1, RW_CHUNK, cd), lambda i, s: (i, bwd(s), 0))
    out = jax.ShapeDtypeStruct((b, t, cd), F32)
    return pl.pallas_call(
        _rwkv_kernel,
        grid=(b, nc),
        in_specs=[sf] * 6 + [sb] * 6,
        out_specs=[sf, sb],
        out_shape=[out, out],
        scratch_shapes=[pltpu.VMEM((2, cd // HEAD_PAIR, HEAD_PAIR, HEAD_PAIR), F32)],
        compiler_params=pltpu.CompilerParams(
            dimension_semantics=("arbitrary", "arbitrary"),
            vmem_limit_bytes=V7X_VMEM_LIMIT_BYTES),
        name="rwkv7_chunked",
        interpret=interpret,
    )(r, v, kk, lw_f, a_f, k_f, r, v, kk, lw_b, a_b, k_b)


def _gdn_kernel(qf_ref, kf_ref, vf_ref, bf_ref, gf_ref, gtf_ref,
                qb_ref, kb_ref, vb_ref, bb_ref, gb_ref, gtb_ref, of_ref, ob_ref, st_ref):
    @pl.when(pl.program_id(1) == 0)
    def _():
        st_ref[...] = jnp.zeros_like(st_ref)

    c_len = D_CHUNK
    n2 = 2 * c_len
    r2 = lax.broadcasted_iota(jnp.int32, (n2, n2), 0)
    c2 = lax.broadcasted_iota(jnp.int32, (n2, n2), 1)
    same = (r2 // c_len) == (c2 // c_len)
    dlt = (r2 % c_len) - (c2 % c_len)
    ones_bd = jnp.where(same, 1.0, 0.0).astype(BF16)
    eye = jnp.where(r2 == c2, 1.0, 0.0).astype(F32)

    def rows(blk, p):
        return jnp.concatenate([blk[:, h * D_HEAD_DIM:(h + 1) * D_HEAD_DIM] for h in (2 * p, 2 * p + 1)], axis=0)

    def col(blk, p):
        c = jnp.concatenate([blk[:, h:h + 1] for h in (2 * p, 2 * p + 1)], axis=0)
        return jnp.broadcast_to(c, (n2, n2))

    n_pairs = D_HEADS // 2
    ch = []
    for d, refs in enumerate(((qf_ref, kf_ref, vf_ref, bf_ref, gf_ref, gtf_ref),
                              (qb_ref, kb_ref, vb_ref, bb_ref, gb_ref, gtb_ref))):
        sign = 1 - 2 * d
        q_blk, k_blk, v_blk = refs[0][0], refs[1][0], refs[2][0]
        b_blk, g_blk, gt_blk = refs[3][0, 0], refs[4][0, 0], refs[5][0, 0]
        incl = same & (dlt * sign >= 0)
        tri = jnp.where(incl, 1.0, 0.0).astype(BF16)
        for p in range(n_pairs):
            g_row = jnp.concatenate([gt_blk[h:h + 1, :] for h in (2 * p, 2 * p + 1)], axis=1)
            ch.append(dict(d=d, p=p, strict=same & (dlt * sign > 0), incl=incl, tri=tri,
                           q=rows(q_blk, p), k=rows(k_blk, p), v=rows(v_blk, p), beta=col(b_blk, p),
                           g_colb=col(g_blk, p), g_rowb=jnp.broadcast_to(g_row, (n2, n2)),
                           st=[st_ref[d, 2 * p], st_ref[d, 2 * p + 1]]))
    n = range(len(ch))
    gc_col = [_dot_ones(c['tri'], c['g_colb'], NN) for c in ch]
    gc_row = [_dot_ones(c['tri'], c['g_rowb'], NT, ones_first=False) for c in ch]
    g_end = [_dot_ones(ones_bd, c['g_colb'], NN) for c in ch]
    decay = [jnp.where(ch[i]['incl'], jnp.exp(jnp.where(ch[i]['incl'], gc_col[i] - gc_row[i], 0.0)), 0.0) for i in n]
    kb = [c['k'] * c['beta'] for c in ch]
    vb = [c['v'] * c['beta'] for c in ch]
    g_mat = [_dot(jnp.concatenate([kb[i], ch[i]['q']], axis=0), ch[i]['k'], NT) for i in n]
    l_mat = [jnp.where(ch[i]['strict'], g_mat[i][:n2] * decay[i], 0.0) for i in n]
    a_intra = [g_mat[i][n2:] * decay[i] for i in n]
    tinv = _unit_lower_inverse([-l for l in l_mat], eye)
    u = [_dot(tinv[i], vb[i], NN) for i in n]
    wk = [_dot(tinv[i], kb[i] * jnp.exp(gc_col[i]), NN) for i in n]
    q_e = [ch[i]['q'] * jnp.exp(gc_col[i]) for i in n]
    k_e = [ch[i]['k'] * jnp.exp(g_end[i] - gc_col[i]) for i in n]
    halves = (slice(0, c_len), slice(c_len, n2))
    v_new = [[u[i][hs] - _dot(wk[i][hs], ch[i]['st'][j], NN) for j, hs in enumerate(halves)] for i in n]
    o_st = [[_dot(q_e[i][hs], ch[i]['st'][j], NN) for j, hs in enumerate(halves)] for i in n]
    v_new_s = [jnp.concatenate(v_new[i], axis=0) for i in n]
    o_s = [jnp.concatenate(o_st[i], axis=0) + _dot(a_intra[i], v_new_s[i], NN) for i in n]
    st_new = [[ch[i]['st'][j] * jnp.exp(g_end[i][hs][0:1, :]) + _dot(k_e[i][hs], v_new[i][j], TN)
               for j, hs in enumerate(halves)] for i in n]
    for d, o_ref in enumerate((of_ref, ob_ref)):
        o_ref[0] = jnp.concatenate([o_s[d * n_pairs + p][hs] for p in range(n_pairs) for hs in halves], axis=1)
        for p in range(n_pairs):
            for j in range(2):
                st_ref[d, 2 * p + j] = st_new[d * n_pairs + p][j]


def gdn_chunked(qkv, beta_f, g_f, beta_b, g_b, n_ctx, interpret=False):
    b, t, _ = qkv.shape
    cd = D_DIM
    nc = t // D_CHUNK
    fwd, bwd = _chunk_maps(nc, n_ctx // D_CHUNK)

    def chunks(z):
        return z.reshape(b, nc, D_CHUNK, D_HEADS)

    def specs(cmap):
        big = [pl.BlockSpec((1, D_CHUNK, cd), functools.partial(lambda part, i, s: (i, cmap(s), part), part))
               for part in range(3)]
        small = pl.BlockSpec((1, 1, D_CHUNK, D_HEADS), lambda i, s: (i, cmap(s), 0, 0))
        small_t = pl.BlockSpec((1, 1, D_HEADS, D_CHUNK), lambda i, s: (i, cmap(s), 0, 0))
        return big, small, small_t

    bf, sf, stf = specs(fwd)
    bb, sb, stb = specs(bwd)
    out = jax.ShapeDtypeStruct((b, t, cd), F32)
    return pl.pallas_call(
        _gdn_kernel,
        grid=(b, nc),
        in_specs=bf + [sf, sf, stf] + bb + [sb, sb, stb],
        out_specs=[bf[0], bb[0]],
        out_shape=[out, out],
        scratch_shapes=[pltpu.VMEM((2, D_HEADS, D_HEAD_DIM, D_HEAD_DIM), F32)],
        compiler_params=pltpu.CompilerParams(
            dimension_semantics=("arbitrary", "arbitrary"),
            vmem_limit_bytes=V7X_VMEM_LIMIT_BYTES),
        name="gdn_chunked",
        interpret=interpret,
    )(qkv, qkv, qkv, chunks(beta_f), chunks(g_f), jnp.swapaxes(chunks(g_f), -1, -2),
      qkv, qkv, qkv, chunks(beta_b), chunks(g_b), jnp.swapaxes(chunks(g_b), -1, -2))


TOK_TM = 256
TOK_HALO = 8


def _block_ones(width, seg):
    r = lax.broadcasted_iota(jnp.int32, (width, width), 0)
    c = lax.broadcasted_iota(jnp.int32, (width, width), 1)
    return jnp.where((r // seg) == (c // seg), 1.0, 0.0).astype(BF16)


def _seg_sum(x, ones_bd):
    hi = x.astype(BF16)
    lo = (x - hi.astype(F32)).astype(BF16)
    return jnp.dot(hi, ones_bd, preferred_element_type=F32) + jnp.dot(lo, ones_bd, preferred_element_type=F32)


def _softplus(x):
    return jnp.maximum(x, 0.0) + jnp.log(1.0 + jnp.exp(-jnp.abs(x)))


def _seq_edges(i, n_tiles, ctx_tiles):
    return (i == 0) | (i == ctx_tiles), (i == ctx_tiles - 1) | (i == n_tiles - 1)


def _rwkv_prep_kernel(pp_ref, p_ref, pn_ref, mup_ref, mun_ref, w0_ref, a0_ref, kkw_ref, kaw_ref,
                      w2f_ref, w2b_ref, a2f_ref, a2b_ref, g2_ref,
                      r_ref, v_ref, kk_ref, g_ref, lwf_ref, af_ref, kf_ref, lwb_ref, ab_ref, kb_ref,
                      *, n_tiles, ctx_tiles):
    i = pl.program_id(1)
    tm, halo = TOK_TM, TOK_HALO
    rows = tm + 2 * halo
    pe = jnp.concatenate([pp_ref[0], p_ref[0], pn_ref[0]], axis=0)
    ridx = lax.broadcasted_iota(jnp.int32, (rows, 1), 0)
    seq_first, seq_last = _seq_edges(i, n_tiles, ctx_tiles)
    outside = ((ridx == halo - 1) & seq_first) | ((ridx == halo + tm) & seq_last)
    pe = jnp.where(outside, 0.0, pe)
    prev = pltpu.roll(pe, 1, 0)
    nxt = pltpu.roll(pe, rows - 1, 0)
    xs = (pe + mup_ref[...] * (prev - pe) + mun_ref[...] * (nxt - pe))[halo:halo + tm]
    offs = np.cumsum((0,) + C_SIZES)
    r, k, v, wl_f, wl_b, al_f, al_b, gl = (xs[:, offs[n]:offs[n + 1]] for n in range(len(C_SIZES)))

    def lora(x, w_ref):
        return jnp.dot(x.astype(BF16), w_ref[...], preferred_element_type=F32)

    ones = _block_ones(C_DIM, C_HEAD)
    kq = k * kkw_ref[...]
    r_ref[0] = r
    v_ref[0] = v
    kk_ref[0] = kq * lax.rsqrt(_seg_sum(kq * kq, ones) + 1e-6)
    g_ref[0] = lora(jax.nn.sigmoid(gl), g2_ref)
    for d, (wl, al, w2_ref, a2_ref, lw_ref, a_ref, kd_ref) in enumerate((
            (wl_f, al_f, w2f_ref, a2f_ref, lwf_ref, af_ref, kf_ref),
            (wl_b, al_b, w2b_ref, a2b_ref, lwb_ref, ab_ref, kb_ref))):
        w = -_softplus(-(w0_ref[d:d + 1] + lora(jnp.tanh(wl), w2_ref))) - 0.5
        lw_ref[0] = -jnp.exp(w)
        a = jax.nn.sigmoid(a0_ref[d:d + 1] + lora(al, a2_ref))
        a_ref[0] = a
        kd_ref[0] = k * (1.0 + (a - 1.0) * kaw_ref[...])


def rwkv_prep(p, n_ctx, c_mu_prev, c_mu_next, c_w0, c_w2, c_a0, c_a2, c_g2, c_k_k, c_k_a, interpret=False):
    b, t, _ = p.shape
    nt = t // TOK_TM
    hb = TOK_TM // TOK_HALO
    assert IN_C % LANE == 0

    def full(x):
        x2 = x.reshape(-1, x.shape[-1])
        return x2, pl.BlockSpec(x2.shape, lambda bi, i: (0, 0))

    consts = [full(x) for x in (c_mu_prev, c_mu_next, c_w0, c_a0, c_k_k, c_k_a,
                                c_w2[0].astype(BF16), c_w2[1].astype(BF16), c_a2[0].astype(BF16),
                                c_a2[1].astype(BF16), c_g2.astype(BF16))]
    tile = pl.BlockSpec((1, TOK_TM, C_DIM), lambda bi, i: (bi, i, 0))
    out = jax.ShapeDtypeStruct((b, t, C_DIM), F32)
    r, v, kk, g, lw_f, a_f, k_f, lw_b, a_b, k_b = pl.pallas_call(
        functools.partial(_rwkv_prep_kernel, n_tiles=nt, ctx_tiles=n_ctx // TOK_TM),
        grid=(b, nt),
        in_specs=[pl.BlockSpec((1, TOK_HALO, IN_C), lambda bi, i: (bi, jnp.maximum(i * hb - 1, 0), 0)),
                  pl.BlockSpec((1, TOK_TM, IN_C), lambda bi, i: (bi, i, 0)),
                  pl.BlockSpec((1, TOK_HALO, IN_C), lambda bi, i: (bi, jnp.minimum((i + 1) * hb, nt * hb - 1), 0))]
        + [s for _, s in consts],
        out_specs=[tile] * 10,
        out_shape=[out] * 10,
        compiler_params=pltpu.CompilerParams(
            dimension_semantics=("arbitrary", "arbitrary"),
            vmem_limit_bytes=V7X_VMEM_LIMIT_BYTES),
        name="rwkv_prep",
        interpret=interpret,
    )(p, p, p, *[x for x, _ in consts])
    return r, v, kk, g, (lw_f, lw_b), (a_f, a_b), (k_f, k_b)


def _cd_out_kernel(yf_ref, yb_ref, r_ref, kf_ref, kb_ref, v_ref, g_ref, of_ref, ob_ref, z_ref, x_ref, gate_ref,
                   crk_ref, lnw_ref, lnb_ref, onorm_ref, wout_ref, o_ref):
    ones_c = _block_ones(C_DIM, C_HEAD)
    ones_d = _block_ones(D_DIM, D_HEAD_DIM)
    y = yf_ref[0] + yb_ref[0]
    mean = _seg_sum(y, ones_c) * (1.0 / C_HEAD)
    dev = y - mean
    var = _seg_sum(dev * dev, ones_c) * (1.0 / C_HEAD)
    yn = dev * lax.rsqrt(var + C_GN_EPS) * lnw_ref[...] + lnb_ref[...]
    bonus = _seg_sum(r_ref[0] * (kf_ref[0] + kb_ref[0]) * crk_ref[...], ones_c) * v_ref[0]
    out_c = (yn + bonus) * g_ref[0]
    o = of_ref[0] + ob_ref[0]
    ms = _seg_sum(o * o, ones_d) * (1.0 / D_HEAD_DIM)
    z = z_ref[0]
    out_d = o * lax.rsqrt(ms + EPS) * onorm_ref[...] * (z * jax.nn.sigmoid(z))
    y_cat = jnp.concatenate([out_c, out_d], axis=1).astype(BF16)
    o_ref[0] = x_ref[0] + gate_ref[0, 0] * jnp.dot(y_cat, wout_ref[...], preferred_element_type=F32)


def cd_out(xa, gate, y_f, y_b, r, k_f, k_b, v, g, o_f, o_b, z, c_r_k, c_ln_w, c_ln_b, d_o_norm, w_out, n_ctx,
           latent_only, interpret=False):
    b, t, d = xa.shape
    nt = t // TOK_TM
    ctx_tiles = n_ctx // TOK_TM
    t0 = ctx_tiles if latent_only else 0
    tile = pl.BlockSpec((1, TOK_TM, C_DIM), lambda bi, i: (bi, i + t0, 0))
    xtile = pl.BlockSpec((1, TOK_TM, d), lambda bi, i: (bi, i + t0, 0))

    def row(x):
        x2 = x.reshape(1, -1)
        return x2, pl.BlockSpec(x2.shape, lambda bi, i: (0, 0))

    consts = [row(c_r_k), row(c_ln_w), row(c_ln_b), row(jnp.tile(d_o_norm, D_HEADS))]
    return pl.pallas_call(
        _cd_out_kernel,
        grid=(b, nt - t0),
        in_specs=[tile] * 10 + [xtile,
                                pl.BlockSpec((1, 1, 1, d),
                                             lambda bi, i: (bi, jnp.where(i + t0 >= ctx_tiles, 1, 0), 0, 0))]
        + [s for _, s in consts] + [pl.BlockSpec(w_out.shape, lambda bi, i: (0, 0))],
        out_specs=pl.BlockSpec((1, TOK_TM, d), lambda bi, i: (bi, i, 0)),
        out_shape=jax.ShapeDtypeStruct((b, t - t0 * TOK_TM, d), F32),
        compiler_params=pltpu.CompilerParams(
            dimension_semantics=("arbitrary", "arbitrary"),
            vmem_limit_bytes=V7X_VMEM_LIMIT_BYTES),
        name="cd_out",
        interpret=interpret,
    )(y_f, y_b, r, k_f, k_b, v, g, o_f, o_b, z, xa, gate.reshape(b, 2, 1, d), *[x for x, _ in consts],
      w_out.astype(BF16))


def _gdn_qkv_kernel(p_ref, w_ref, o_ref, *, n_ctx):
    j = pl.program_id(1)
    x = p_ref[0]
    t = x.shape[0]
    idx = lax.broadcasted_iota(jnp.int32, (t, 1), 0)
    w = w_ref[...]
    y = x * w[D_CONV // 2:D_CONV // 2 + 1]
    for tap in range(D_CONV):
        off = tap - D_CONV // 2
        if off == 0:
            continue
        src = idx + off
        same_seq = (src >= 0) & (src < t) & ((src >= n_ctx) == (idx >= n_ctx))
        y = y + jnp.where(same_seq, pltpu.roll(x, (-off) % t, 0), 0.0) * w[tap:tap + 1]
    y = y * jax.nn.sigmoid(y)
    inv = lax.rsqrt(jnp.sum(y * y, axis=-1, keepdims=True) + 1e-6)
    factor = jnp.where(j < D_HEADS, inv * D_HEAD_DIM ** -0.5, jnp.where(j < 2 * D_HEADS, inv, 1.0))
    o_ref[0] = y * factor


def gdn_qkv(p, d_conv_w, n_ctx, interpret=False):
    b, t, _ = p.shape
    col0 = IN_C // D_HEAD_DIM
    assert IN_C % D_HEAD_DIM == 0
    n_blk = 3 * D_DIM // D_HEAD_DIM
    return pl.pallas_call(
        functools.partial(_gdn_qkv_kernel, n_ctx=n_ctx),
        grid=(b, n_blk),
        in_specs=[pl.BlockSpec((1, t, D_HEAD_DIM), lambda i, j: (i, 0, col0 + j)),
                  pl.BlockSpec((D_CONV, D_HEAD_DIM), lambda i, j: (0, j))],
        out_specs=pl.BlockSpec((1, t, D_HEAD_DIM), lambda i, j: (i, 0, j)),
        out_shape=jax.ShapeDtypeStruct((b, t, 3 * D_DIM), F32),
        compiler_params=pltpu.CompilerParams(
            dimension_semantics=("arbitrary", "arbitrary"),
            vmem_limit_bytes=V7X_VMEM_LIMIT_BYTES),
        name="gdn_qkv",
        interpret=interpret,
    )(p, d_conv_w)


def gdn_prep(p, n_ctx, d_conv_w, d_A_log, d_dt_bias):
    qkv = gdn_qkv(p, d_conv_w, n_ctx)
    offs = IN_C + np.cumsum((0,) + D_SIZES)
    z, bf, bb, af, ab = (p[..., offs[n]:offs[n + 1]] for n in range(1, len(D_SIZES)))
    betas = (jax.nn.sigmoid(bf), jax.nn.sigmoid(bb))
    gs = tuple(-jnp.exp(d_A_log[i]) * jax.nn.softplus(al + d_dt_bias[i]) for i, al in enumerate((af, ab)))
    return qkv, z, betas, gs


def mixer_cd(xa, gate, p, n_ctx, c_mu_prev, c_mu_next, c_w0, c_w2, c_a0, c_a2, c_g2, c_k_k, c_k_a, c_r_k,
             c_ln_w, c_ln_b, d_conv_w, d_A_log, d_dt_bias, d_o_norm, w_out, latent_only):
    r, v, kk, g, lw, a, ks = rwkv_prep(p, n_ctx, c_mu_prev, c_mu_next, c_w0, c_w2, c_a0, c_a2, c_g2, c_k_k, c_k_a)
    y_f, y_b = rwkv_chunked(r, v, kk, lw[0], a[0], ks[0], lw[1], a[1], ks[1], n_ctx)
    qkv, z, beta, gd = gdn_prep(p, n_ctx, d_conv_w, d_A_log, d_dt_bias)
    o_f, o_b = gdn_chunked(qkv, beta[0], gd[0], beta[1], gd[1], n_ctx)
    return cd_out(xa, gate, y_f, y_b, r, ks[0], ks[1], v, g, o_f, o_b, z, c_r_k, c_ln_w, c_ln_b, d_o_norm, w_out,
                  n_ctx, latent_only)


FFN_TM = 256
FFN_HALO = 8
FFN_FC = 256
FFN_VMEM_LIMIT_BYTES = 56 * 1024 * 1024


def _modulated(x, shift, scale):
    y = x * lax.rsqrt(jnp.mean(x * x, axis=-1, keepdims=True) + EPS)
    return y * (1.0 + scale) + shift


def _ffn_kernel(xp_ref, x_ref, xn_ref, shift_ref, scale_ref, gate_ref, wup_ref, cw_ref, cb_ref, wdn_ref, o_ref, acc_ref,
                *, n_tiles, ctx_tiles):
    i = pl.program_id(1)
    tm, halo = FFN_TM, FFN_HALO
    rows = tm + 2 * halo
    x = x_ref[0]
    xe = jnp.concatenate([xp_ref[0], x, xn_ref[0]], axis=0)
    h = _modulated(xe, shift_ref[0, 0], scale_ref[0, 0])
    r = lax.broadcasted_iota(jnp.int32, (rows, 1), 0)
    seq_first, seq_last = _seq_edges(i, n_tiles, ctx_tiles)
    outside = ((r == halo - 1) & seq_first) | ((r == halo + tm) & seq_last)
    h = jnp.where(outside, 0.0, h).astype(BF16)
    acc_ref[...] = jnp.zeros_like(acc_ref)

    def conv(u, c0):
        w = cw_ref[:, c0:c0 + FFN_FC]
        y = (pltpu.roll(u, 1, 0) * w[0:1] + u * w[1:2] + pltpu.roll(u, rows - 1, 0) * w[2:3]
             + cb_ref[:, c0:c0 + FFN_FC])
        return y[halo:halo + tm]

    def up(c):
        cv, cg = c * FFN_FC, D_FF + c * FFN_FC
        return (jnp.dot(h, wup_ref[:, cv:cv + FFN_FC], preferred_element_type=F32),
                jnp.dot(h, wup_ref[:, cg:cg + FFN_FC], preferred_element_type=F32))

    def down(act, c):
        acc_ref[...] += jnp.dot(act, wdn_ref[c * FFN_FC:(c + 1) * FFN_FC, :], preferred_element_type=F32)

    n_chunks = D_FF // FFN_FC
    u_next = up(0)
    act_prev = None
    for c in range(n_chunks):
        u_val, u_gat = u_next
        if c + 1 < n_chunks:
            u_next = up(c + 1)
        if act_prev is not None:
            down(act_prev, c - 1)
        val = conv(u_val, c * FFN_FC)
        gat = conv(u_gat, D_FF + c * FFN_FC)
        act_prev = (gat * jax.nn.sigmoid(gat) * val).astype(BF16)
    down(act_prev, n_chunks - 1)
    o_ref[0] = x + gate_ref[0, 0] * acc_ref[...]


def ffn_fused(x, shift, scale, gate, w_up, conv_w, conv_b, w_down, n_ctx, interpret=False):
    b, t, d = x.shape
    nt = t // FFN_TM
    ctx_tiles = n_ctx // FFN_TM
    hb = FFN_TM // FFN_HALO
    f2 = w_up.shape[1]
    mod = pl.BlockSpec((1, 1, 1, d), lambda bi, i: (bi, jnp.where(i >= ctx_tiles, 1, 0), 0, 0))

    def full(shp):
        return pl.BlockSpec(shp, lambda bi, i: (0,) * len(shp))

    def mods(m):
        return m.reshape(b, 2, 1, d)

    return pl.pallas_call(
        functools.partial(_ffn_kernel, n_tiles=nt, ctx_tiles=ctx_tiles),
        grid=(b, nt),
        in_specs=[pl.BlockSpec((1, FFN_HALO, d), lambda bi, i: (bi, jnp.maximum(i * hb - 1, 0), 0)),
                  pl.BlockSpec((1, FFN_TM, d), lambda bi, i: (bi, i, 0)),
                  pl.BlockSpec((1, FFN_HALO, d), lambda bi, i: (bi, jnp.minimum((i + 1) * hb, nt * hb - 1), 0)),
                  mod, mod, mod,
                  full((d, f2)), full((FFN_CONV, f2)), full((1, f2)), full((f2 // 2, d))],
        out_specs=pl.BlockSpec((1, FFN_TM, d), lambda bi, i: (bi, i, 0)),
        out_shape=jax.ShapeDtypeStruct((b, t, d), F32),
        scratch_shapes=[pltpu.VMEM((FFN_TM, d), F32)],
        compiler_params=pltpu.CompilerParams(
            dimension_semantics=("arbitrary", "arbitrary"),
            vmem_limit_bytes=FFN_VMEM_LIMIT_BYTES),
        name="conv_ffn",
        interpret=interpret,
    )(x, x, x, mods(shift), mods(scale), mods(gate), w_up.astype(BF16), conv_w, conv_b.reshape(1, f2),
      w_down.astype(BF16))


def kernel(x, c, ctx, c_ctx, ada_w, ada_b, ffn_w_up, ffn_conv_w, ffn_conv_b, ffn_w_down,
           ab_w_in, ab_w_out, a_q_norm, a_k_norm, a_sink, b_cq_norm, b_ckv_norm, b_w_uq, b_w_uk, b_w_uv,
           b_qn_norm, b_qr_norm, b_kn_norm, b_kr_norm, cd_w_in, cd_w_out, c_mu_prev, c_mu_next, c_w0, c_w2,
           c_a0, c_a2, c_g2, c_k_k, c_k_a, c_r_k, c_ln_w, c_ln_b, d_conv_w, d_A_log, d_dt_bias, d_o_norm):
    bsz, seq = x.shape[:2]
    n_ctx = ctx.shape[1]
    rows = seq // GRID_W
    zeros = jnp.zeros((n_ctx,), jnp.int32)
    row = jnp.concatenate([zeros, jnp.repeat(jnp.arange(rows, dtype=jnp.int32), GRID_W)])
    col = jnp.concatenate([zeros, jnp.tile(jnp.arange(GRID_W, dtype=jnp.int32), rows)])
    is_ctx = (jnp.arange(n_ctx + seq) < n_ctx)[None, :, None]
    silu_c = jax.nn.silu(c)
    silu_cc = jax.nn.silu(c_ctx)
    xa = jnp.concatenate([ctx, x], axis=1)
    for l in range(DEPTH):
        last = l == DEPTH - 1
        i = l // 2
        mod_l = jnp.split(silu_c @ ada_w[l] + ada_b[l], N_MOD, axis=-1)
        mod_c = jnp.split(silu_cc @ ada_w[l] + ada_b[l], N_MOD, axis=-1)
        mods = [jnp.stack([jnp.broadcast_to(mc, ml.shape), ml], axis=1) for mc, ml in zip(mod_c, mod_l)]

        def per_token(m):
            return jnp.where(is_ctx, m[:, 0:1, :], m[:, 1:2, :])

        w_in = ab_w_in[i] if l % 2 == 0 else cd_w_in[i]
        p = mod_mm(xa, mods[0], mods[1], w_in, n_ctx)
        if l % 2 == 0:
            y = mixer_ab(p, n_ctx, row, col, a_q_norm[i], a_k_norm[i],
                         a_sink[i], b_cq_norm[i], b_ckv_norm[i], b_w_uq[i], b_w_uk[i], b_w_uv[i],
                         b_qn_norm[i], b_qr_norm[i], b_kn_norm[i], b_kr_norm[i], not last)
            xa = xa + per_token(mods[2]) * _mm(y, ab_w_out[i])
            if last:
                xa = xa[:, n_ctx:]
        else:
            xa = mixer_cd(xa, mods[2], p, n_ctx, c_mu_prev[i], c_mu_next[i], c_w0[i],
                          c_w2[i], c_a0[i], c_a2[i], c_g2[i], c_k_k[i], c_k_a[i], c_r_k[i], c_ln_w[i],
                          c_ln_b[i], d_conv_w[i], d_A_log[i], d_dt_bias[i], d_o_norm[i], cd_w_out[i], last)
        xa = ffn_fused(xa, mods[3], mods[4], mods[5], ffn_w_up[l], ffn_conv_w[l], ffn_conv_b[l], ffn_w_down[l],
                       0 if last else n_ctx)
    return xa
```

```python
import functools

import jax
import jax.numpy as jnp
from jax import lax
import numpy as np
from jax.experimental import pallas as pl
from jax.experimental.pallas import tpu as pltpu

D_MODEL = 1024
DEPTH = 2
GRID_W = 64
N_MOD = 6
EPS = 1e-6
ROPE_THETA = 10000.0
NEG_INF = -1e30

A_HEADS = 8
A_KV_HEADS = 2
A_HEAD_DIM = 64
WINDOW = 128
B_HEADS = 8
B_Q_RANK = 256
B_KV_RANK = 256
B_NOPE = 64
B_ROPE = 32
B_V_DIM = 64
C_HEADS = 8
C_HEAD = 64
C_DIM = C_HEADS * C_HEAD
C_DECAY_LORA = 64
C_AAA_LORA = 64
C_GATE_LORA = 128
C_GN_EPS = 64e-5
D_HEADS = 4
D_HEAD_DIM = 128
D_DIM = D_HEADS * D_HEAD_DIM
D_CONV = 5
D_CHUNK = 64
D_FF = 2816
FFN_CONV = 3

AB_SIZES = (A_HEADS * A_HEAD_DIM, A_KV_HEADS * A_HEAD_DIM, A_KV_HEADS * A_HEAD_DIM, B_Q_RANK, B_KV_RANK, B_ROPE)
C_SIZES = (C_DIM, C_DIM, C_DIM, C_DECAY_LORA, C_DECAY_LORA, C_AAA_LORA, C_AAA_LORA, C_GATE_LORA)
IN_C = sum(C_SIZES)
D_SIZES = (3 * D_DIM, D_DIM, D_HEADS, D_HEADS, D_HEADS, D_HEADS)

F32 = jnp.float32
BF16 = jnp.bfloat16

V7X_VMEM_LIMIT_BYTES = 48 * 1024 * 1024
LANE = 128
MXU_N = 256

NN = ((1,), (0,))
NT = ((1,), (1,))
TN = ((0,), (0,))


def _mm_kernel(a_ref, b_ref, o_ref):
    o_ref[...] = jnp.dot(a_ref[...].astype(BF16), b_ref[...], preferred_element_type=F32)


def _pick_tile(n, candidates):
    for c in candidates:
        if n % c == 0:
            return c
    raise ValueError(f"no tile for {n}")


def _mm(a, w):
    lead = a.shape[:-1]
    k = a.shape[-1]
    n = w.shape[-1]
    a2 = a.reshape(-1, k)
    m = a2.shape[0]
    n_pad = -(-n // MXU_N) * MXU_N
    wb = w.astype(BF16)
    if n_pad != n:
        wb = jnp.pad(wb, ((0, 0), (0, n_pad - n)))
    tm = _pick_tile(m, (1024, 512, 256, 128, 8))
    tn = _pick_tile(n_pad, (1024, 768, 512, 256))
    out = pl.pallas_call(
        _mm_kernel,
        grid=(m // tm, n_pad // tn),
        in_specs=[pl.BlockSpec((tm, k), lambda i, j: (i, 0)),
                  pl.BlockSpec((k, tn), lambda i, j: (0, j))],
        out_specs=pl.BlockSpec((tm, tn), lambda i, j: (i, j)),
        out_shape=jax.ShapeDtypeStruct((m, n_pad), F32),
        compiler_params=pltpu.CompilerParams(
            dimension_semantics=("arbitrary", "arbitrary"),
            vmem_limit_bytes=V7X_VMEM_LIMIT_BYTES),
        name="mm",
    )(a2, wb)
    if n_pad != n:
        out = out[:, :n]
    return out.reshape(*lead, out.shape[-1])


MOD_TM = 768


def _mod_mm_kernel(x_ref, shift_ref, scale_ref, w_ref, o_ref, h_ref, *, n_ctx):
    @pl.when(pl.program_id(2) == 0)
    def _():
        x = x_ref[0]
        tok = pl.program_id(1) * MOD_TM + lax.broadcasted_iota(jnp.int32, (MOD_TM, 1), 0)
        is_ctx = tok < n_ctx
        shift = jnp.where(is_ctx, shift_ref[0, 0:1], shift_ref[0, 1:2])
        scale = jnp.where(is_ctx, scale_ref[0, 0:1], scale_ref[0, 1:2])
        y = x * lax.rsqrt(jnp.mean(x * x, axis=-1, keepdims=True) + EPS)
        h_ref[...] = (y * (1.0 + scale) + shift).astype(BF16)

    o_ref[0] = jnp.dot(h_ref[...], w_ref[...], preferred_element_type=F32)


def mod_mm(x, shift, scale, w, n_ctx):
    b, t, d = x.shape
    n = w.shape[-1]
    n_pad = -(-n // MXU_N) * MXU_N
    wb = jnp.pad(w.astype(BF16), ((0, 0), (0, n_pad - n)))
    tn = _pick_tile(n_pad, (1024, 768, 512, 256))
    mod = pl.BlockSpec((1, 2, d), lambda bi, i, j: (bi, 0, 0))
    return pl.pallas_call(
        functools.partial(_mod_mm_kernel, n_ctx=n_ctx),
        grid=(b, t // MOD_TM, n_pad // tn),
        in_specs=[pl.BlockSpec((1, MOD_TM, d), lambda bi, i, j: (bi, i, 0)), mod, mod,
                  pl.BlockSpec((d, tn), lambda bi, i, j: (0, j))],
        out_specs=pl.BlockSpec((1, MOD_TM, tn), lambda bi, i, j: (bi, i, j)),
        out_shape=jax.ShapeDtypeStruct((b, t, n_pad), F32),
        scratch_shapes=[pltpu.VMEM((MOD_TM, d), BF16)],
        compiler_params=pltpu.CompilerParams(
            dimension_semantics=("arbitrary", "arbitrary", "arbitrary"),
            vmem_limit_bytes=V7X_VMEM_LIMIT_BYTES),
        name="mod_mm",
    )(x, shift, scale, wb)


def _dot(a, b, dims, passes=1):
    def dg(x, y):
        return lax.dot_general(x, y, (dims, ((), ())), preferred_element_type=F32)

    ah, bh = a.astype(BF16), b.astype(BF16)
    if passes == 1:
        return dg(ah, bh)
    al = (a - ah.astype(F32)).astype(BF16)
    bl = (b - bh.astype(F32)).astype(BF16)
    return dg(ah, bh) + (dg(ah, bl) + dg(al, bh))


def _split3(x):
    hi = x.astype(BF16)
    r1 = x - hi.astype(F32)
    mid = r1.astype(BF16)
    lo = (r1 - mid.astype(F32)).astype(BF16)
    return hi, mid, lo


def _dot_ones(ones, x, dims, ones_first=True):
    def dg(piece):
        a, b = (ones, piece) if ones_first else (piece, ones)
        return lax.dot_general(a, b, (dims, ((), ())), preferred_element_type=F32)

    hi, mid, lo = _split3(x)
    return dg(hi) + (dg(mid) + dg(lo))


def _heads_major(z):
    return jnp.swapaxes(z, 1, 2)


def _softmax_pv(s, v, sink):
    m = jnp.max(s, axis=-1, keepdims=True)
    if sink is not None:
        m = jnp.maximum(m, sink)
    p = jnp.exp(s - m)
    den = jnp.sum(p, axis=-1, keepdims=True)
    if sink is not None:
        den = den + jnp.exp(sink - m)
    return _dot(p, v, NN) / den


ATTN_HEADS_PER_STEP = 2


def _attn_full_kernel(q_ref, k_ref, q2_ref, k2_ref, v_ref, sink_ref, o_ref, *, scale, use_sink, use_second,
                      kv_heads):
    outs = []
    for hh in range(ATTN_HEADS_PER_STEP):
        kh = hh if kv_heads == ATTN_HEADS_PER_STEP else 0
        s = _dot(q_ref[0, hh] * scale, k_ref[0, kh], NT)
        if use_second:
            s = s + _dot(q2_ref[0, hh] * scale, k2_ref[0, 0], NT)
        sink = sink_ref[hh] if use_sink else None
        outs.append(_softmax_pv(s, v_ref[0, kh], sink))
    o_ref[0] = jnp.concatenate(outs, axis=1)


def attn_full(q, k, v, scale, q_start, n_q, n_keys, sink=None, second=None):
    b, h, _, d = q.shape
    hk = k.shape[1]
    dv = v.shape[-1]
    g = h // hk
    hp = ATTN_HEADS_PER_STEP
    assert hp * dv == LANE and h % hp == 0 and (g == 1 or g % hp == 0)
    kv_heads = hp if g == 1 else 1
    tq = min(n_q, 256)
    q0 = q_start // tq
    use_sink = sink is not None
    use_second = second is not None
    sink_arr = (sink if use_sink else jnp.zeros((h,), F32)).astype(F32).reshape(h, 1, 1)
    q2, k2 = second if use_second else (q, k)
    d2 = q2.shape[-1]

    def kv_spec(width):
        return pl.BlockSpec((1, kv_heads, n_keys, width), lambda i, j, t: (i, (j * hp) // (g * kv_heads), 0, 0))

    k2_spec = pl.BlockSpec((1, 1, n_keys, d2), lambda i, j, t: (i, 0, 0, 0)) if use_second else kv_spec(d2)
    return pl.pallas_call(
        functools.partial(_attn_full_kernel, scale=scale, use_sink=use_sink, use_second=use_second,
                          kv_heads=kv_heads),
        grid=(b, h // hp, n_q // tq),
        in_specs=[pl.BlockSpec((1, hp, tq, d), lambda i, j, t: (i, j, t + q0, 0)),
                  kv_spec(d),
                  pl.BlockSpec((1, hp, tq, d2), lambda i, j, t: (i, j, t + q0, 0)),
                  k2_spec,
                  kv_spec(dv),
                  pl.BlockSpec((hp, 1, 1), lambda i, j, t: (j, 0, 0))],
        out_specs=pl.BlockSpec((1, tq, hp * dv), lambda i, j, t: (i, t, j)),
        out_shape=jax.ShapeDtypeStruct((b, n_q, h * dv), F32),
        compiler_params=pltpu.CompilerParams(
            dimension_semantics=("arbitrary", "arbitrary", "arbitrary"),
            vmem_limit_bytes=V7X_VMEM_LIMIT_BYTES),
        name="attn_full",
    )(q, k, q2, k2, v, sink_arr)


def _attn_window_kernel(q_ref, kp_ref, k0_ref, kn_ref, kc_ref, vp_ref, v0_ref, vn_ref, vc_ref, sink_ref, o_ref,
                        *, scale, group, n_blocks):
    n = pl.program_id(2)
    w = WINDOW
    d = q_ref.shape[-1]
    q = q_ref[0].reshape(group * w, d) * scale
    keys = jnp.concatenate([kp_ref[0, 0], k0_ref[0, 0], kn_ref[0, 0], kc_ref[0, 0]], axis=0)
    vals = jnp.concatenate([vp_ref[0, 0], v0_ref[0, 0], vn_ref[0, 0], vc_ref[0, 0]], axis=0)
    s = _dot(q, keys, NT)
    nk = keys.shape[0]
    qi = lax.broadcasted_iota(jnp.int32, (group * w, nk), 0) % w
    kj = lax.broadcasted_iota(jnp.int32, (group * w, nk), 1)
    rel = qi + w - kj
    band_ok = (jnp.abs(rel) <= w) & ((kj >= w) | (n > 0)) & ((kj < 2 * w) | (n < n_blocks - 1))
    s = jnp.where((kj >= 3 * w) | band_ok, s, NEG_INF)
    sink = jnp.concatenate([jnp.broadcast_to(sink_ref[0, hh], (w, 1)) for hh in range(group)], axis=0)
    o = _softmax_pv(s, vals, sink)
    o_ref[0] = jnp.concatenate([o[hh * w:(hh + 1) * w] for hh in range(group)], axis=1)


def attn_window(q, k, v, sink, n_ctx):
    b, h, t, d = q.shape
    hk = k.shape[1]
    g = h // hk
    off = n_ctx // WINDOW
    nb = (t - n_ctx) // WINDOW
    sink_arr = sink.astype(F32).reshape(hk, g, 1, 1)

    def blk(f):
        return pl.BlockSpec((1, 1, WINDOW, d), f)

    prev = blk(lambda i, j, n: (i, j, jnp.maximum(n - 1, 0) + off, 0))
    own = blk(lambda i, j, n: (i, j, n + off, 0))
    nxt = blk(lambda i, j, n: (i, j, jnp.minimum(n + 1, nb - 1) + off, 0))
    ctx = pl.BlockSpec((1, 1, n_ctx, d), lambda i, j, n: (i, j, 0, 0))
    return pl.pallas_call(
        functools.partial(_attn_window_kernel, scale=d ** -0.5, group=g, n_blocks=nb),
        grid=(b, hk, nb),
        in_specs=[pl.BlockSpec((1, g, WINDOW, d), lambda i, j, n: (i, j, n + off, 0)),
                  prev, own, nxt, ctx, prev, own, nxt, ctx,
                  pl.BlockSpec((1, g, 1, 1), lambda i, j, n: (j, 0, 0, 0))],
        out_specs=pl.BlockSpec((1, WINDOW, g * d), lambda i, j, n: (i, n, j)),
        out_shape=jax.ShapeDtypeStruct((b, t - n_ctx, h * d), F32),
        compiler_params=pltpu.CompilerParams(
            dimension_semantics=("arbitrary", "arbitrary", "arbitrary"),
            vmem_limit_bytes=V7X_VMEM_LIMIT_BYTES),
        name="attn_window",
    )(q, k, k, k, k, v, v, v, v, sink_arr)


def _rope_tables(row, col, head_dim, width):
    q = head_dim // 4
    inv = jnp.power(ROPE_THETA, -jnp.arange(q, dtype=F32) / q)
    ang_r = row.astype(F32)[:, None] * inv[None, :]
    ang_c = col.astype(F32)[:, None] * inv[None, :]
    cos = jnp.concatenate([jnp.cos(ang_r), jnp.cos(ang_r), jnp.cos(ang_c), jnp.cos(ang_c)], axis=-1)
    sin = jnp.concatenate([-jnp.sin(ang_r), jnp.sin(ang_r), -jnp.sin(ang_c), jnp.sin(ang_c)], axis=-1)
    reps = width // head_dim
    return jnp.tile(cos, (1, reps)), jnp.tile(sin, (1, reps))


def _rope(x, cos, sin, head_dim):
    q = head_dim // 4
    w = x.shape[-1]
    lane = lax.broadcasted_iota(jnp.int32, x.shape, 1)
    swapped = jnp.where(lane % (2 * q) < q, pltpu.roll(x, w - q, 1), pltpu.roll(x, q, 1))
    return x * cos + swapped * sin


def _head_rms(x, ones_bd, head_dim, gain):
    return x * lax.rsqrt(_seg_sum(x * x, ones_bd) * (1.0 / head_dim) + EPS) * gain


def _row_rms(x, n, gain):
    return x * lax.rsqrt(jnp.sum(x * x, axis=-1, keepdims=True) * (1.0 / n) + EPS) * gain


def _ab_prep_kernel(p_ref, c64_ref, s64_ref, c32_ref, s32_ref, gqa_ref, gka_ref, gcq_ref, gckv_ref, gqn_ref, gqr_ref,
                    gkn_ref, gkr_ref, wuq_ref, wuk_ref, wuv_ref,
                    qa_ref, ka_ref, qn_ref, qr_ref, kn_ref, vb_ref, kr_ref):
    p = p_ref[0]
    offs = np.cumsum((0,) + AB_SIZES)
    qa, ka, cq, ckv = (p[:, offs[n]:offs[n + 1]] for n in (0, 1, 3, 4))
    kr = p[:, offs[5]:offs[5] + LANE]
    c64, s64, c32, s32 = c64_ref[...], s64_ref[...], c32_ref[...], s32_ref[...]
    ones_q = _block_ones(A_HEADS * A_HEAD_DIM, A_HEAD_DIM)
    ones_k = _block_ones(A_KV_HEADS * A_HEAD_DIM, A_HEAD_DIM)
    ones_r = _block_ones(B_HEADS * B_ROPE, B_ROPE)
    ka_w = A_KV_HEADS * A_HEAD_DIM
    qa_ref[0] = _rope(_head_rms(qa, ones_q, A_HEAD_DIM, gqa_ref[...]), c64, s64, A_HEAD_DIM)
    ka_ref[0] = _rope(_head_rms(ka, ones_k, A_HEAD_DIM, gka_ref[...]), c64[:, :ka_w], s64[:, :ka_w], A_HEAD_DIM)
    qb = jnp.dot(_row_rms(cq, B_Q_RANK, gcq_ref[...]).astype(BF16), wuq_ref[...], preferred_element_type=F32)
    n_nope = B_HEADS * B_NOPE
    qn_ref[0] = _head_rms(qb[:, :n_nope], ones_q, B_NOPE, gqn_ref[...])
    qr_ref[0] = _rope(_head_rms(qb[:, n_nope:], ones_r, B_ROPE, gqr_ref[...]), c32, s32, B_ROPE)
    ckv_n = _row_rms(ckv, B_KV_RANK, gckv_ref[...]).astype(BF16)
    kn_ref[0] = _head_rms(jnp.dot(ckv_n, wuk_ref[...], preferred_element_type=F32), ones_q, B_NOPE, gkn_ref[...])
    vb_ref[0] = jnp.dot(ckv_n, wuv_ref[...], preferred_element_type=F32)
    kr_ref[0] = _rope(_row_rms(kr, B_ROPE, gkr_ref[...]), c32[:, :LANE], s32[:, :LANE], B_ROPE)


def ab_prep(p, row, col, a_q_norm, a_k_norm, b_cq_norm, b_ckv_norm, b_w_uq, b_w_uk, b_w_uv,
            b_qn_norm, b_qr_norm, b_kn_norm, b_kr_norm, interpret=False):
    b, t, pw = p.shape
    assert pw >= sum(AB_SIZES[:5]) + LANE and A_HEAD_DIM == B_NOPE
    c64, s64 = _rope_tables(row, col, A_HEAD_DIM, A_HEADS * A_HEAD_DIM)
    c32, s32 = _rope_tables(row, col, B_ROPE, B_HEADS * B_ROPE)
    wq = b_w_uq.reshape(B_Q_RANK, B_HEADS, B_NOPE + B_ROPE)
    wq = jnp.concatenate([wq[..., :B_NOPE].reshape(B_Q_RANK, -1), wq[..., B_NOPE:].reshape(B_Q_RANK, -1)], axis=1)

    def row_c(g, reps, width=None):
        x = jnp.tile(g, reps).reshape(1, -1)
        if width is not None:
            x = jnp.pad(x, ((0, 0), (0, width - x.shape[1])))
        return x

    consts = [row_c(a_q_norm, A_HEADS), row_c(a_k_norm, A_KV_HEADS), row_c(b_cq_norm, 1), row_c(b_ckv_norm, 1),
              row_c(b_qn_norm, B_HEADS), row_c(b_qr_norm, B_HEADS), row_c(b_kn_norm, B_HEADS),
              row_c(b_kr_norm, 1, LANE), wq.astype(BF16), b_w_uk.astype(BF16), b_w_uv.astype(BF16)]
    tabs = [c64, s64, c32, s32]
    widths = (A_HEADS * A_HEAD_DIM, A_KV_HEADS * A_HEAD_DIM, B_HEADS * B_NOPE, B_HEADS * B_ROPE, B_HEADS * B_NOPE,
              B_HEADS * B_V_DIM, LANE)
    return pl.pallas_call(
        _ab_prep_kernel,
        grid=(b, t // TOK_TM),
        in_specs=[pl.BlockSpec((1, TOK_TM, pw), lambda bi, i: (bi, i, 0))]
        + [pl.BlockSpec((TOK_TM, x.shape[1]), lambda bi, i: (i, 0)) for x in tabs]
        + [pl.BlockSpec(x.shape, lambda bi, i: (0, 0)) for x in consts],
        out_specs=[pl.BlockSpec((1, TOK_TM, w), lambda bi, i: (bi, i, 0)) for w in widths],
        out_shape=[jax.ShapeDtypeStruct((b, t, w), F32) for w in widths],
        compiler_params=pltpu.CompilerParams(
            dimension_semantics=("arbitrary", "arbitrary"),
            vmem_limit_bytes=V7X_VMEM_LIMIT_BYTES),
        name="ab_prep",
        interpret=interpret,
    )(p, *tabs, *consts)


def mixer_ab(p, n_ctx, row, col, a_q_norm, a_k_norm, a_sink, b_cq_norm, b_ckv_norm, b_w_uq, b_w_uk,
             b_w_uv, b_qn_norm, b_qr_norm, b_kn_norm, b_kr_norm, ctx_out):
    b, t = p.shape[:2]
    qa, ka, qn, qr, kn, vb, kr = ab_prep(p, row, col, a_q_norm, a_k_norm, b_cq_norm, b_ckv_norm, b_w_uq, b_w_uk,
                                         b_w_uv, b_qn_norm, b_qr_norm, b_kn_norm, b_kr_norm)
    va_off = AB_SIZES[0] + AB_SIZES[1]
    va = p[..., va_off:va_off + AB_SIZES[2]]

    def hm(z, heads):
        return _heads_major(z.reshape(b, t, heads, -1))

    qa_t, ka_t, va_t = hm(qa, A_HEADS), hm(ka, A_KV_HEADS), hm(va, A_KV_HEADS)
    qn_t, kn_t, vb_t = hm(qn, B_HEADS), hm(kn, B_HEADS), hm(vb, B_HEADS)
    rope_part = (hm(qr, B_HEADS), hm(kr[..., :B_ROPE], 1))
    b_scale = (B_NOPE + B_ROPE) ** -0.5

    o_a = attn_window(qa_t, ka_t, va_t, a_sink, n_ctx)
    o_b = attn_full(qn_t, kn_t, vb_t, b_scale, n_ctx, t - n_ctx, t, second=rope_part)
    if ctx_out:
        o_a_c = attn_full(qa_t, ka_t, va_t, A_HEAD_DIM ** -0.5, 0, n_ctx, n_ctx, sink=a_sink)
        o_b_c = attn_full(qn_t, kn_t, vb_t, b_scale, 0, n_ctx, n_ctx, second=rope_part)
    else:
        o_a_c = jnp.zeros((b, n_ctx, A_HEADS * A_HEAD_DIM), F32)
        o_b_c = jnp.zeros((b, n_ctx, B_HEADS * B_V_DIM), F32)
    o_a = jnp.concatenate([o_a_c, o_a], axis=1)
    o_b = jnp.concatenate([o_b_c, o_b], axis=1)
    return jnp.concatenate([o_a, o_b], axis=-1)


RW_CHUNK = 64
HEAD_PAIR = 2 * C_HEAD


def _unit_lower_inverse(xs, eye):
    ps = [eye + x for x in xs]
    xps = list(xs)
    for _ in range(int(np.log2(RW_CHUNK)) - 1):
        xps = [_dot(xp, xp, NN, passes=3) for xp in xps]
        ps = [p + _dot(p, xp, NN, passes=3) for p, xp in zip(ps, xps)]
    return ps


def _chunk_maps(nc, nctx):
    def fwd(s):
        return s

    def bwd(s):
        return jnp.where(s < nctx, nctx - 1 - s, nc + nctx - 1 - s)

    return fwd, bwd


def _rwkv_kernel(rf_ref, vf_ref, kkf_ref, lwf_ref, af_ref, kf_ref,
                 rb_ref, vb_ref, kkb_ref, lwb_ref, ab_ref, kb_ref, yf_ref, yb_ref, st_ref):
    @pl.when(pl.program_id(1) == 0)
    def _():
        st_ref[...] = jnp.zeros_like(st_ref)

    c_len = RW_CHUNK
    ii = lax.broadcasted_iota(jnp.int32, (c_len, c_len), 0)
    jj = lax.broadcasted_iota(jnp.int32, (c_len, c_len), 1)
    r2 = lax.broadcasted_iota(jnp.int32, (HEAD_PAIR, HEAD_PAIR), 0)
    c2 = lax.broadcasted_iota(jnp.int32, (HEAD_PAIR, HEAD_PAIR), 1)
    dlt = (r2 % c_len) - (c2 % c_len)
    eye = jnp.where(r2 == c2, 1.0, 0.0).astype(F32)
    lane = lax.broadcasted_iota(jnp.int32, (c_len, HEAD_PAIR), 1)
    m0 = lane < C_HEAD

    def stack_heads(x):
        return jnp.concatenate([jnp.where(m0, x, 0.0), jnp.where(m0, 0.0, x)], axis=0)

    n_pairs = C_DIM // HEAD_PAIR
    ch = []
    for d, refs in enumerate(((rf_ref, vf_ref, kkf_ref, lwf_ref, af_ref, kf_ref),
                              (rb_ref, vb_ref, kkb_ref, lwb_ref, ab_ref, kb_ref))):
        sign = 1 - 2 * d
        r_all, v_all, kk_all, lw_all, a_all, kd_all = (ref[0] for ref in refs)
        tri = jnp.where((ii - jj) * sign >= 0, 1.0, 0.0).astype(BF16)
        cum_all = _dot_ones(tri, lw_all, NN)
        tot_all = jnp.sum(lw_all, axis=0, keepdims=True)
        for p in range(n_pairs):
            sl = slice(p * HEAD_PAIR, (p + 1) * HEAD_PAIR)
            ch.append(dict(strict=dlt * sign > 0, incl=dlt * sign >= 0, st=st_ref[d, p],
                           lw=lw_all[:, sl], cum=cum_all[:, sl], tot=tot_all[:, sl], kk=kk_all[:, sl],
                           kd=kd_all[:, sl], bb=kk_all[:, sl] * a_all[:, sl], r=r_all[:, sl], v=v_all[:, sl]))
    for c in ch:
        e_neg = jnp.exp(-c['cum'])
        e_end = jnp.exp(c['tot'] - c['cum'])
        c['abar'] = stack_heads(-c['kk'] * jnp.exp(c['cum'] - c['lw']))
        c['rbar'] = stack_heads(c['r'] * jnp.exp(c['cum']))
        c['ktil'] = stack_heads(c['kd'] * e_neg)
        c['btil'] = stack_heads(c['bb'] * e_neg)
        c['khat'] = stack_heads(c['kd'] * e_end)
        c['bhat'] = stack_heads(c['bb'] * e_end)
        c['vs'] = stack_heads(c['v'])
    g = [_dot(jnp.concatenate([c['abar'], c['rbar']], axis=0), jnp.concatenate([c['ktil'], c['btil']], axis=0), NT)
         for c in ch]
    a_ak = [jnp.where(c['strict'], x[:HEAD_PAIR, :HEAD_PAIR], 0.0) for c, x in zip(ch, g)]
    a_ab = [jnp.where(c['strict'], x[:HEAD_PAIR, HEAD_PAIR:], 0.0) for c, x in zip(ch, g)]
    a_rk = [jnp.where(c['incl'], x[HEAD_PAIR:, :HEAD_PAIR], 0.0) for c, x in zip(ch, g)]
    a_rb = [jnp.where(c['incl'], x[HEAD_PAIR:, HEAD_PAIR:], 0.0) for c, x in zip(ch, g)]
    tinv = _unit_lower_inverse(a_ab, eye)
    n = range(len(ch))
    wm = [_dot(tinv[i], ch[i]['abar'], NN) for i in n]
    akv = [_dot(a_ak[i], ch[i]['vs'], NN) for i in n]
    u0 = [_dot(tinv[i], akv[i], NN) for i in n]
    u = [_dot(wm[i], ch[i]['st'], NT) + u0[i] for i in n]
    ys = [_dot(ch[i]['rbar'], ch[i]['st'], NT) + _dot(a_rk[i], ch[i]['vs'], NN) + _dot(a_rb[i], u[i], NN) for i in n]
    st_new = [ch[i]['st'] * jnp.exp(ch[i]['tot']) + _dot(ch[i]['vs'], ch[i]['khat'], TN) + _dot(u[i], ch[i]['bhat'], TN)
              for i in n]
    for d, y_ref in enumerate((yf_ref, yb_ref)):
        y_ref[0] = jnp.concatenate([ys[d * n_pairs + p][:c_len] + ys[d * n_pairs + p][c_len:] for p in range(n_pairs)],
                                   axis=1)
        for p in range(n_pairs):
            st_ref[d, p] = st_new[d * n_pairs + p]


def rwkv_chunked(r, v, kk, lw_f, a_f, k_f, lw_b, a_b, k_b, n_ctx, interpret=False):
    b, t, cd = r.shape
    nc = t // RW_CHUNK
    fwd, bwd = _chunk_maps(nc, n_ctx // RW_CHUNK)
    sf = pl.BlockSpec((1, RW_CHUNK, cd), lambda i, s: (i, fwd(s), 0))
    sb = pl.BlockSpec((1, RW_CHUNK, cd), lambda i, s: (i, bwd(s), 0))
    out = jax.ShapeDtypeStruct((b, t, cd), F32)
    return pl.pallas_call(
        _rwkv_kernel,
        grid=(b, nc),
        in_specs=[sf] * 6 + [sb] * 6,
        out_specs=[sf, sb],
        out_shape=[out, out],
        scratch_shapes=[pltpu.VMEM((2, cd // HEAD_PAIR, HEAD_PAIR, HEAD_PAIR), F32)],
        compiler_params=pltpu.CompilerParams(
            dimension_semantics=("arbitrary", "arbitrary"),
            vmem_limit_bytes=V7X_VMEM_LIMIT_BYTES),
        name="rwkv7_chunked",
        interpret=interpret,
    )(r, v, kk, lw_f, a_f, k_f, r, v, kk, lw_b, a_b, k_b)


def _gdn_kernel(qf_ref, kf_ref, vf_ref, bf_ref, gf_ref, gtf_ref,
                qb_ref, kb_ref, vb_ref, bb_ref, gb_ref, gtb_ref, of_ref, ob_ref, st_ref):
    @pl.when(pl.program_id(1) == 0)
    def _():
        st_ref[...] = jnp.zeros_like(st_ref)

    c_len = D_CHUNK
    n2 = 2 * c_len
    r2 = lax.broadcasted_iota(jnp.int32, (n2, n2), 0)
    c2 = lax.broadcasted_iota(jnp.int32, (n2, n2), 1)
    same = (r2 // c_len) == (c2 // c_len)
    dlt = (r2 % c_len) - (c2 % c_len)
    ones_bd = jnp.where(same, 1.0, 0.0).astype(BF16)
    eye = jnp.where(r2 == c2, 1.0, 0.0).astype(F32)

    def rows(blk, p):
        return jnp.concatenate([blk[:, h * D_HEAD_DIM:(h + 1) * D_HEAD_DIM] for h in (2 * p, 2 * p + 1)], axis=0)

    def col(blk, p):
        c = jnp.concatenate([blk[:, h:h + 1] for h in (2 * p, 2 * p + 1)], axis=0)
        return jnp.broadcast_to(c, (n2, n2))

    n_pairs = D_HEADS // 2
    ch = []
    for d, refs in enumerate(((qf_ref, kf_ref, vf_ref, bf_ref, gf_ref, gtf_ref),
                              (qb_ref, kb_ref, vb_ref, bb_ref, gb_ref, gtb_ref))):
        sign = 1 - 2 * d
        q_blk, k_blk, v_blk = refs[0][0], refs[1][0], refs[2][0]
        b_blk, g_blk, gt_blk = refs[3][0, 0], refs[4][0, 0], refs[5][0, 0]
        incl = same & (dlt * sign >= 0)
        tri = jnp.where(incl, 1.0, 0.0).astype(BF16)
        for p in range(n_pairs):
            g_row = jnp.concatenate([gt_blk[h:h + 1, :] for h in (2 * p, 2 * p + 1)], axis=1)
            ch.append(dict(d=d, p=p, strict=same & (dlt * sign > 0), incl=incl, tri=tri,
                           q=rows(q_blk, p), k=rows(k_blk, p), v=rows(v_blk, p), beta=col(b_blk, p),
                           g_colb=col(g_blk, p), g_rowb=jnp.broadcast_to(g_row, (n2, n2)),
                           st=[st_ref[d, 2 * p], st_ref[d, 2 * p + 1]]))
    n = range(len(ch))
    gc_col = [_dot_ones(c['tri'], c['g_colb'], NN) for c in ch]
    gc_row = [_dot_ones(c['tri'], c['g_rowb'], NT, ones_first=False) for c in ch]
    g_end = [_dot_ones(ones_bd, c['g_colb'], NN) for c in ch]
    decay = [jnp.where(ch[i]['incl'], jnp.exp(jnp.where(ch[i]['incl'], gc_col[i] - gc_row[i], 0.0)), 0.0) for i in n]
    kb = [c['k'] * c['beta'] for c in ch]
    vb = [c['v'] * c['beta'] for c in ch]
    g_mat = [_dot(jnp.concatenate([kb[i], ch[i]['q']], axis=0), ch[i]['k'], NT) for i in n]
    l_mat = [jnp.where(ch[i]['strict'], g_mat[i][:n2] * decay[i], 0.0) for i in n]
    a_intra = [g_mat[i][n2:] * decay[i] for i in n]
    tinv = _unit_lower_inverse([-l for l in l_mat], eye)
    u = [_dot(tinv[i], vb[i], NN) for i in n]
    wk = [_dot(tinv[i], kb[i] * jnp.exp(gc_col[i]), NN) for i in n]
    q_e = [ch[i]['q'] * jnp.exp(gc_col[i]) for i in n]
    k_e = [ch[i]['k'] * jnp.exp(g_end[i] - gc_col[i]) for i in n]
    halves = (slice(0, c_len), slice(c_len, n2))
    v_new = [[u[i][hs] - _dot(wk[i][hs], ch[i]['st'][j], NN) for j, hs in enumerate(halves)] for i in n]
    o_st = [[_dot(q_e[i][hs], ch[i]['st'][j], NN) for j, hs in enumerate(halves)] for i in n]
    v_new_s = [jnp.concatenate(v_new[i], axis=0) for i in n]
    o_s = [jnp.concatenate(o_st[i], axis=0) + _dot(a_intra[i], v_new_s[i], NN) for i in n]
    st_new = [[ch[i]['st'][j] * jnp.exp(g_end[i][hs][0:1, :]) + _dot(k_e[i][hs], v_new[i][j], TN)
               for j, hs in enumerate(halves)] for i in n]
    for d, o_ref in enumerate((of_ref, ob_ref)):
        o_ref[0] = jnp.concatenate([o_s[d * n_pairs + p][hs] for p in range(n_pairs) for hs in halves], axis=1)
        for p in range(n_pairs):
            for j in range(2):
                st_ref[d, 2 * p + j] = st_new[d * n_pairs + p][j]


def gdn_chunked(qkv, beta_f, g_f, beta_b, g_b, n_ctx, interpret=False):
    b, t, _ = qkv.shape
    cd = D_DIM
    nc = t // D_CHUNK
    fwd, bwd = _chunk_maps(nc, n_ctx // D_CHUNK)

    def chunks(z):
        return z.reshape(b, nc, D_CHUNK, D_HEADS)

    def specs(cmap):
        big = [pl.BlockSpec((1, D_CHUNK, cd), functools.partial(lambda part, i, s: (i, cmap(s), part), part))
               for part in range(3)]
        small = pl.BlockSpec((1, 1, D_CHUNK, D_HEADS), lambda i, s: (i, cmap(s), 0, 0))
        small_t = pl.BlockSpec((1, 1, D_HEADS, D_CHUNK), lambda i, s: (i, cmap(s), 0, 0))
        return big, small, small_t

    bf, sf, stf = specs(fwd)
    bb, sb, stb = specs(bwd)
    out = jax.ShapeDtypeStruct((b, t, cd), F32)
    return pl.pallas_call(
        _gdn_kernel,
        grid=(b, nc),
        in_specs=bf + [sf, sf, stf] + bb + [sb, sb, stb],
        out_specs=[bf[0], bb[0]],
        out_shape=[out, out],
        scratch_shapes=[pltpu.VMEM((2, D_HEADS, D_HEAD_DIM, D_HEAD_DIM), F32)],
        compiler_params=pltpu.CompilerParams(
            dimension_semantics=("arbitrary", "arbitrary"),
            vmem_limit_bytes=V7X_VMEM_LIMIT_BYTES),
        name="gdn_chunked",
        interpret=interpret,
    )(qkv, qkv, qkv, chunks(beta_f), chunks(g_f), jnp.swapaxes(chunks(g_f), -1, -2),
      qkv, qkv, qkv, chunks(beta_b), chunks(g_b), jnp.swapaxes(chunks(g_b), -1, -2))


TOK_TM = 256
TOK_HALO = 8


def _block_ones(width, seg):
    r = lax.broadcasted_iota(jnp.int32, (width, width), 0)
    c = lax.broadcasted_iota(jnp.int32, (width, width), 1)
    return jnp.where((r // seg) == (c // seg), 1.0, 0.0).astype(BF16)


def _seg_sum(x, ones_bd):
    hi = x.astype(BF16)
    lo = (x - hi.astype(F32)).astype(BF16)
    return jnp.dot(hi, ones_bd, preferred_element_type=F32) + jnp.dot(lo, ones_bd, preferred_element_type=F32)


def _softplus(x):
    return jnp.maximum(x, 0.0) + jnp.log(1.0 + jnp.exp(-jnp.abs(x)))


def _seq_edges(i, n_tiles, ctx_tiles):
    return (i == 0) | (i == ctx_tiles), (i == ctx_tiles - 1) | (i == n_tiles - 1)


def _rwkv_prep_kernel(pp_ref, p_ref, pn_ref, mup_ref, mun_ref, w0_ref, a0_ref, kkw_ref, kaw_ref,
                      w2f_ref, w2b_ref, a2f_ref, a2b_ref, g2_ref,
                      r_ref, v_ref, kk_ref, g_ref, lwf_ref, af_ref, kf_ref, lwb_ref, ab_ref, kb_ref,
                      *, n_tiles, ctx_tiles):
    i = pl.program_id(1)
    tm, halo = TOK_TM, TOK_HALO
    rows = tm + 2 * halo
    pe = jnp.concatenate([pp_ref[0], p_ref[0], pn_ref[0]], axis=0)
    ridx = lax.broadcasted_iota(jnp.int32, (rows, 1), 0)
    seq_first, seq_last = _seq_edges(i, n_tiles, ctx_tiles)
    outside = ((ridx == halo - 1) & seq_first) | ((ridx == halo + tm) & seq_last)
    pe = jnp.where(outside, 0.0, pe)
    prev = pltpu.roll(pe, 1, 0)
    nxt = pltpu.roll(pe, rows - 1, 0)
    xs = (pe + mup_ref[...] * (prev - pe) + mun_ref[...] * (nxt - pe))[halo:halo + tm]
    offs = np.cumsum((0,) + C_SIZES)
    r, k, v, wl_f, wl_b, al_f, al_b, gl = (xs[:, offs[n]:offs[n + 1]] for n in range(len(C_SIZES)))

    def lora(x, w_ref):
        return jnp.dot(x.astype(BF16), w_ref[...], preferred_element_type=F32)

    ones = _block_ones(C_DIM, C_HEAD)
    kq = k * kkw_ref[...]
    r_ref[0] = r
    v_ref[0] = v
    kk_ref[0] = kq * lax.rsqrt(_seg_sum(kq * kq, ones) + 1e-6)
    g_ref[0] = lora(jax.nn.sigmoid(gl), g2_ref)
    for d, (wl, al, w2_ref, a2_ref, lw_ref, a_ref, kd_ref) in enumerate((
            (wl_f, al_f, w2f_ref, a2f_ref, lwf_ref, af_ref, kf_ref),
            (wl_b, al_b, w2b_ref, a2b_ref, lwb_ref, ab_ref, kb_ref))):
        w = -_softplus(-(w0_ref[d:d + 1] + lora(jnp.tanh(wl), w2_ref))) - 0.5
        lw_ref[0] = -jnp.exp(w)
        a = jax.nn.sigmoid(a0_ref[d:d + 1] + lora(al, a2_ref))
        a_ref[0] = a
        kd_ref[0] = k * (1.0 + (a - 1.0) * kaw_ref[...])


def rwkv_prep(p, n_ctx, c_mu_prev, c_mu_next, c_w0, c_w2, c_a0, c_a2, c_g2, c_k_k, c_k_a, interpret=False):
    b, t, _ = p.shape
    nt = t // TOK_TM
    hb = TOK_TM // TOK_HALO
    assert IN_C % LANE == 0

    def full(x):
        x2 = x.reshape(-1, x.shape[-1])
        return x2, pl.BlockSpec(x2.shape, lambda bi, i: (0, 0))

    consts = [full(x) for x in (c_mu_prev, c_mu_next, c_w0, c_a0, c_k_k, c_k_a,
                                c_w2[0].astype(BF16), c_w2[1].astype(BF16), c_a2[0].astype(BF16),
                                c_a2[1].astype(BF16), c_g2.astype(BF16))]
    tile = pl.BlockSpec((1, TOK_TM, C_DIM), lambda bi, i: (bi, i, 0))
    out = jax.ShapeDtypeStruct((b, t, C_DIM), F32)
    r, v, kk, g, lw_f, a_f, k_f, lw_b, a_b, k_b = pl.pallas_call(
        functools.partial(_rwkv_prep_kernel, n_tiles=nt, ctx_tiles=n_ctx // TOK_TM),
        grid=(b, nt),
        in_specs=[pl.BlockSpec((1, TOK_HALO, IN_C), lambda bi, i: (bi, jnp.maximum(i * hb - 1, 0), 0)),
                  pl.BlockSpec((1, TOK_TM, IN_C), lambda bi, i: (bi, i, 0)),
                  pl.BlockSpec((1, TOK_HALO, IN_C), lambda bi, i: (bi, jnp.minimum((i + 1) * hb, nt * hb - 1), 0))]
        + [s for _, s in consts],
        out_specs=[tile] * 10,
        out_shape=[out] * 10,
        compiler_params=pltpu.CompilerParams(
            dimension_semantics=("arbitrary", "arbitrary"),
            vmem_limit_bytes=V7X_VMEM_LIMIT_BYTES),
        name="rwkv_prep",
        interpret=interpret,
    )(p, p, p, *[x for x, _ in consts])
    return r, v, kk, g, (lw_f, lw_b), (a_f, a_b), (k_f, k_b)


def _cd_out_kernel(yf_ref, yb_ref, r_ref, kf_ref, kb_ref, v_ref, g_ref, of_ref, ob_ref, z_ref, x_ref, gate_ref,
                   crk_ref, lnw_ref, lnb_ref, onorm_ref, wout_ref, o_ref):
    ones_c = _block_ones(C_DIM, C_HEAD)
    ones_d = _block_ones(D_DIM, D_HEAD_DIM)
    y = yf_ref[0] + yb_ref[0]
    mean = _seg_sum(y, ones_c) * (1.0 / C_HEAD)
    dev = y - mean
    var = _seg_sum(dev * dev, ones_c) * (1.0 / C_HEAD)
    yn = dev * lax.rsqrt(var + C_GN_EPS) * lnw_ref[...] + lnb_ref[...]
    bonus = _seg_sum(r_ref[0] * (kf_ref[0] + kb_ref[0]) * crk_ref[...], ones_c) * v_ref[0]
    out_c = (yn + bonus) * g_ref[0]
    o = of_ref[0] + ob_ref[0]
    ms = _seg_sum(o * o, ones_d) * (1.0 / D_HEAD_DIM)
    z = z_ref[0]
    out_d = o * lax.rsqrt(ms + EPS) * onorm_ref[...] * (z * jax.nn.sigmoid(z))
    y_cat = jnp.concatenate([out_c, out_d], axis=1).astype(BF16)
    o_ref[0] = x_ref[0] + gate_ref[0, 0] * jnp.dot(y_cat, wout_ref[...], preferred_element_type=F32)


def cd_out(xa, gate, y_f, y_b, r, k_f, k_b, v, g, o_f, o_b, z, c_r_k, c_ln_w, c_ln_b, d_o_norm, w_out, n_ctx,
           latent_only, interpret=False):
    b, t, d = xa.shape
    nt = t // TOK_TM
    ctx_tiles = n_ctx // TOK_TM
    t0 = ctx_tiles if latent_only else 0
    tile = pl.BlockSpec((1, TOK_TM, C_DIM), lambda bi, i: (bi, i + t0, 0))
    xtile = pl.BlockSpec((1, TOK_TM, d), lambda bi, i: (bi, i + t0, 0))

    def row(x):
        x2 = x.reshape(1, -1)
        return x2, pl.BlockSpec(x2.shape, lambda bi, i: (0, 0))

    consts = [row(c_r_k), row(c_ln_w), row(c_ln_b), row(jnp.tile(d_o_norm, D_HEADS))]
    return pl.pallas_call(
        _cd_out_kernel,
        grid=(b, nt - t0),
        in_specs=[tile] * 10 + [xtile,
                                pl.BlockSpec((1, 1, 1, d),
                                             lambda bi, i: (bi, jnp.where(i + t0 >= ctx_tiles, 1, 0), 0, 0))]
        + [s for _, s in consts] + [pl.BlockSpec(w_out.shape, lambda bi, i: (0, 0))],
        out_specs=pl.BlockSpec((1, TOK_TM, d), lambda bi, i: (bi, i, 0)),
        out_shape=jax.ShapeDtypeStruct((b, t - t0 * TOK_TM, d), F32),
        compiler_params=pltpu.CompilerParams(
            dimension_semantics=("arbitrary", "arbitrary"),
            vmem_limit_bytes=V7X_VMEM_LIMIT_BYTES),
        name="cd_out",
        interpret=interpret,
    )(y_f, y_b, r, k_f, k_b, v, g, o_f, o_b, z, xa, gate.reshape(b, 2, 1, d), *[x for x, _ in consts],
      w_out.astype(BF16))


def _gdn_qkv_kernel(p_ref, w_ref, o_ref, *, n_ctx):
    j = pl.program_id(1)
    x = p_ref[0]
    t = x.shape[0]
    idx = lax.broadcasted_iota(jnp.int32, (t, 1), 0)
    w = w_ref[...]
    y = x * w[D_CONV // 2:D_CONV // 2 + 1]
    for tap in range(D_CONV):
        off = tap - D_CONV // 2
        if off == 0:
            continue
        src = idx + off
        same_seq = (src >= 0) & (src < t) & ((src >= n_ctx) == (idx >= n_ctx))
        y = y + jnp.where(same_seq, pltpu.roll(x, (-off) % t, 0), 0.0) * w[tap:tap + 1]
    y = y * jax.nn.sigmoid(y)
    inv = lax.rsqrt(jnp.sum(y * y, axis=-1, keepdims=True) + 1e-6)
    factor = jnp.where(j < D_HEADS, inv * D_HEAD_DIM ** -0.5, jnp.where(j < 2 * D_HEADS, inv, 1.0))
    o_ref[0] = y * factor


def gdn_qkv(p, d_conv_w, n_ctx, interpret=False):
    b, t, _ = p.shape
    col0 = IN_C // D_HEAD_DIM
    assert IN_C % D_HEAD_DIM == 0
    n_blk = 3 * D_DIM // D_HEAD_DIM
    return pl.pallas_call(
        functools.partial(_gdn_qkv_kernel, n_ctx=n_ctx),
        grid=(b, n_blk),
        in_specs=[pl.BlockSpec((1, t, D_HEAD_DIM), lambda i, j: (i, 0, col0 + j)),
                  pl.BlockSpec((D_CONV, D_HEAD_DIM), lambda i, j: (0, j))],
        out_specs=pl.BlockSpec((1, t, D_HEAD_DIM), lambda i, j: (i, 0, j)),
        out_shape=jax.ShapeDtypeStruct((b, t, 3 * D_DIM), F32),
        compiler_params=pltpu.CompilerParams(
            dimension_semantics=("arbitrary", "arbitrary"),
            vmem_limit_bytes=V7X_VMEM_LIMIT_BYTES),
        name="gdn_qkv",
        interpret=interpret,
    )(p, d_conv_w)


def gdn_prep(p, n_ctx, d_conv_w, d_A_log, d_dt_bias):
    qkv = gdn_qkv(p, d_conv_w, n_ctx)
    offs = IN_C + np.cumsum((0,) + D_SIZES)
    z, bf, bb, af, ab = (p[..., offs[n]:offs[n + 1]] for n in range(1, len(D_SIZES)))
    betas = (jax.nn.sigmoid(bf), jax.nn.sigmoid(bb))
    gs = tuple(-jnp.exp(d_A_log[i]) * jax.nn.softplus(al + d_dt_bias[i]) for i, al in enumerate((af, ab)))
    return qkv, z, betas, gs


def mixer_cd(xa, gate, p, n_ctx, c_mu_prev, c_mu_next, c_w0, c_w2, c_a0, c_a2, c_g2, c_k_k, c_k_a, c_r_k,
             c_ln_w, c_ln_b, d_conv_w, d_A_log, d_dt_bias, d_o_norm, w_out, latent_only):
    r, v, kk, g, lw, a, ks = rwkv_prep(p, n_ctx, c_mu_prev, c_mu_next, c_w0, c_w2, c_a0, c_a2, c_g2, c_k_k, c_k_a)
    y_f, y_b = rwkv_chunked(r, v, kk, lw[0], a[0], ks[0], lw[1], a[1], ks[1], n_ctx)
    qkv, z, beta, gd = gdn_prep(p, n_ctx, d_conv_w, d_A_log, d_dt_bias)
    o_f, o_b = gdn_chunked(qkv, beta[0], gd[0], beta[1], gd[1], n_ctx)
    return cd_out(xa, gate, y_f, y_b, r, ks[0], ks[1], v, g, o_f, o_b, z, c_r_k, c_ln_w, c_ln_b, d_o_norm, w_out,
                  n_ctx, latent_only)


FFN_TM = 256
FFN_HALO = 8
FFN_FC = 256
FFN_VMEM_LIMIT_BYTES = 56 * 1024 * 1024


def _modulated(x, shift, scale):
    y = x * lax.rsqrt(jnp.mean(x * x, axis=-1, keepdims=True) + EPS)
    return y * (1.0 + scale) + shift


def _ffn_kernel(xp_ref, x_ref, xn_ref, shift_ref, scale_ref, gate_ref, wup_ref, cw_ref, cb_ref, wdn_ref, o_ref, acc_ref,
                *, n_tiles, ctx_tiles):
    i = pl.program_id(1)
    tm, halo = FFN_TM, FFN_HALO
    rows = tm + 2 * halo
    x = x_ref[0]
    xe = jnp.concatenate([xp_ref[0], x, xn_ref[0]], axis=0)
    h = _modulated(xe, shift_ref[0, 0], scale_ref[0, 0])
    r = lax.broadcasted_iota(jnp.int32, (rows, 1), 0)
    seq_first, seq_last = _seq_edges(i, n_tiles, ctx_tiles)
    outside = ((r == halo - 1) & seq_first) | ((r == halo + tm) & seq_last)
    h = jnp.where(outside, 0.0, h).astype(BF16)
    acc_ref[...] = jnp.zeros_like(acc_ref)

    def conv(u, c0):
        w = cw_ref[:, c0:c0 + FFN_FC]
        y = (pltpu.roll(u, 1, 0) * w[0:1] + u * w[1:2] + pltpu.roll(u, rows - 1, 0) * w[2:3]
             + cb_ref[:, c0:c0 + FFN_FC])
        return y[halo:halo + tm]

    def up(c):
        cv, cg = c * FFN_FC, D_FF + c * FFN_FC
        return (jnp.dot(h, wup_ref[:, cv:cv + FFN_FC], preferred_element_type=F32),
                jnp.dot(h, wup_ref[:, cg:cg + FFN_FC], preferred_element_type=F32))

    def down(act, c):
        acc_ref[...] += jnp.dot(act, wdn_ref[c * FFN_FC:(c + 1) * FFN_FC, :], preferred_element_type=F32)

    n_chunks = D_FF // FFN_FC
    u_next = up(0)
    act_prev = None
    for c in range(n_chunks):
        u_val, u_gat = u_next
        if c + 1 < n_chunks:
            u_next = up(c + 1)
        if act_prev is not None:
            down(act_prev, c - 1)
        val = conv(u_val, c * FFN_FC)
        gat = conv(u_gat, D_FF + c * FFN_FC)
        act_prev = (gat * jax.nn.sigmoid(gat) * val).astype(BF16)
    down(act_prev, n_chunks - 1)
    o_ref[0] = x + gate_ref[0, 0] * acc_ref[...]


def ffn_fused(x, shift, scale, gate, w_up, conv_w, conv_b, w_down, n_ctx, interpret=False):
    b, t, d = x.shape
    nt = t // FFN_TM
    ctx_tiles = n_ctx // FFN_TM
    hb = FFN_TM // FFN_HALO
    f2 = w_up.shape[1]
    mod = pl.BlockSpec((1, 1, 1, d), lambda bi, i: (bi, jnp.where(i >= ctx_tiles, 1, 0), 0, 0))

    def full(shp):
        return pl.BlockSpec(shp, lambda bi, i: (0,) * len(shp))

    def mods(m):
        return m.reshape(b, 2, 1, d)

    return pl.pallas_call(
        functools.partial(_ffn_kernel, n_tiles=nt, ctx_tiles=ctx_tiles),
        grid=(b, nt),
        in_specs=[pl.BlockSpec((1, FFN_HALO, d), lambda bi, i: (bi, jnp.maximum(i * hb - 1, 0), 0)),
                  pl.BlockSpec((1, FFN_TM, d), lambda bi, i: (bi, i, 0)),
                  pl.BlockSpec((1, FFN_HALO, d), lambda bi, i: (bi, jnp.minimum((i + 1) * hb, nt * hb - 1), 0)),
                  mod, mod, mod,
                  full((d, f2)), full((FFN_CONV, f2)), full((1, f2)), full((f2 // 2, d))],
        out_specs=pl.BlockSpec((1, FFN_TM, d), lambda bi, i: (bi, i, 0)),
        out_shape=jax.ShapeDtypeStruct((b, t, d), F32),
        scratch_shapes=[pltpu.VMEM((FFN_TM, d), F32)],
        compiler_params=pltpu.CompilerParams(
            dimension_semantics=("arbitrary", "arbitrary"),
            vmem_limit_bytes=FFN_VMEM_LIMIT_BYTES),
        name="conv_ffn",
        interpret=interpret,
    )(x, x, x, mods(shift), mods(scale), mods(gate), w_up.astype(BF16), conv_w, conv_b.reshape(1, f2),
      w_down.astype(BF16))


def kernel(x, c, ctx, c_ctx, ada_w, ada_b, ffn_w_up, ffn_conv_w, ffn_conv_b, ffn_w_down,
           ab_w_in, ab_w_out, a_q_norm, a_k_norm, a_sink, b_cq_norm, b_ckv_norm, b_w_uq, b_w_uk, b_w_uv,
           b_qn_norm, b_qr_norm, b_kn_norm, b_kr_norm, cd_w_in, cd_w_out, c_mu_prev, c_mu_next, c_w0, c_w2,
           c_a0, c_a2, c_g2, c_k_k, c_k_a, c_r_k, c_ln_w, c_ln_b, d_conv_w, d_A_log, d_dt_bias, d_o_norm):
    bsz, seq = x.shape[:2]
    n_ctx = ctx.shape[1]
    rows = seq // GRID_W
    zeros = jnp.zeros((n_ctx,), jnp.int32)
    row = jnp.concatenate([zeros, jnp.repeat(jnp.arange(rows, dtype=jnp.int32), GRID_W)])
    col = jnp.concatenate([zeros, jnp.tile(jnp.arange(GRID_W, dtype=jnp.int32), rows)])
    is_ctx = (jnp.arange(n_ctx + seq) < n_ctx)[None, :, None]
    silu_c = jax.nn.silu(c)
    silu_cc = jax.nn.silu(c_ctx)
    xa = jnp.concatenate([ctx, x], axis=1)
    for l in range(DEPTH):
        last = l == DEPTH - 1
        i = l // 2
        mod_l = jnp.split(silu_c @ ada_w[l] + ada_b[l], N_MOD, axis=-1)
        mod_c = jnp.split(silu_cc @ ada_w[l] + ada_b[l], N_MOD, axis=-1)
        mods = [jnp.stack([jnp.broadcast_to(mc, ml.shape), ml], axis=1) for mc, ml in zip(mod_c, mod_l)]

        def per_token(m):
            return jnp.where(is_ctx, m[:, 0:1, :], m[:, 1:2, :])

        w_in = ab_w_in[i] if l % 2 == 0 else cd_w_in[i]
        p = mod_mm(xa, mods[0], mods[1], w_in, n_ctx)
        if l % 2 == 0:
            y = mixer_ab(p, n_ctx, row, col, a_q_norm[i], a_k_norm[i],
                         a_sink[i], b_cq_norm[i], b_ckv_norm[i], b_w_uq[i], b_w_uk[i], b_w_uv[i],
                         b_qn_norm[i], b_qr_norm[i], b_kn_norm[i], b_kr_norm[i], not last)
            xa = xa + per_token(mods[2]) * _mm(y, ab_w_out[i])
            if last:
                xa = xa[:, n_ctx:]
        else:
            xa = mixer_cd(xa, mods[2], p, n_ctx, c_mu_prev[i], c_mu_next[i], c_w0[i],
                          c_w2[i], c_a0[i], c_a2[i], c_g2[i], c_k_k[i], c_k_a[i], c_r_k[i], c_ln_w[i],
                          c_ln_b[i], d_conv_w[i], d_A_log[i], d_dt_bias[i], d_o_norm[i], cd_w_out[i], last)
        xa = ffn_fused(xa, mods[3], mods[4], mods[5], ffn_w_up[l], ffn_conv_w[l], ffn_conv_b[l], ffn_w_down[l],
                       0 if last else n_ctx)
    return xa
```
